```python
import jax, jax.numpy as jnp
from jax import lax
import numpy as np

D_MODEL = 2048
BATCH = 4
SEQ = 4096
DEPTH = 2

CHUNK = 64
N_MIXERS = 2
RMS_EPS = 1e-5

GLA_HEADS = 4
GLA_KEY_DIM = D_MODEL // 2
GLA_VAL_DIM = D_MODEL
GLA_HEAD_K = GLA_KEY_DIM // GLA_HEADS
GLA_HEAD_V = GLA_VAL_DIM // GLA_HEADS
GLA_GATE_RANK = 16
GLA_GATE_NORMALIZER = 16.0
GLA_IN = 2 * GLA_KEY_DIM + 2 * GLA_VAL_DIM + GLA_GATE_RANK

SSD_INNER = 2 * D_MODEL
SSD_HEAD_DIM = 64
SSD_HEADS = SSD_INNER // SSD_HEAD_DIM
SSD_GROUPS = 8
SSD_HEADS_PER_GROUP = SSD_HEADS // SSD_GROUPS
SSD_STATE = 128
SSD_CONV = 4
SSD_CONV_DIM = SSD_INNER + 2 * SSD_GROUPS * SSD_STATE
SSD_IN = SSD_INNER + SSD_CONV_DIM + SSD_HEADS

D_FF = 4 * D_MODEL

kernel_name = "hybrid_gla_ssd_sqrelu_trunk"


def rmsnorm(x, w):
    xf = x.astype(jnp.float32)
    y = xf * lax.rsqrt(jnp.mean(xf * xf, axis=-1, keepdims=True) + RMS_EPS)
    return (y * w.astype(jnp.float32)).astype(x.dtype)


def to_chunks(t):
    b, tl = t.shape[:2]
    return jnp.moveaxis(t.reshape(b, tl // CHUNK, CHUNK, *t.shape[2:]), 1, 0)


def from_chunks(t):
    nc, b, c = t.shape[:3]
    return jnp.moveaxis(t, 0, 1).reshape(b, nc * c, *t.shape[3:])


def gla_chunk_scan(q, k, v, gk):
    b_sz, _, h, dk = q.shape
    dv = v.shape[-1]
    causal = jnp.tril(jnp.ones((CHUNK, CHUNK), dtype=bool))

    def step(S, inp):
        qc, kc, vc, gc = inp
        bcum = jnp.cumsum(gc, axis=1)
        b_last = bcum[:, -1]
        o_inter = jnp.einsum('bihd,bhde->bihe', qc * jnp.exp(bcum), S)
        diff = bcum[:, :, None] - bcum[:, None, :]
        decay = jnp.exp(jnp.where(causal[None, :, :, None, None], diff, -jnp.inf))
        att = jnp.einsum('bihd,bjhd,bijhd->bhij', qc, kc, decay)
        o_intra = jnp.einsum('bhij,bjhe->bihe', att, vc)
        k_dec = kc * jnp.exp(b_last[:, None] - bcum)
        S = jnp.exp(b_last)[..., None] * S + jnp.einsum('bjhd,bjhe->bhde', k_dec, vc)
        return S, o_inter + o_intra

    S0 = jnp.zeros((b_sz, h, dk, dv), jnp.float32)
    _, o = lax.scan(step, S0, (to_chunks(q), to_chunks(k), to_chunks(v), to_chunks(gk)))
    return from_chunks(o)


def gla_mixer(u, w_in, w_gk_up, b_gk_up, o_norm_w, w_out):
    b_sz, tl, _ = u.shape
    proj = u @ w_in
    q, k, v, g, gr = jnp.split(
        proj, [GLA_KEY_DIM, 2 * GLA_KEY_DIM, 2 * GLA_KEY_DIM + GLA_VAL_DIM,
               2 * GLA_KEY_DIM + 2 * GLA_VAL_DIM], axis=-1)
    gk = jax.nn.log_sigmoid((gr @ w_gk_up + b_gk_up).astype(jnp.float32)) / GLA_GATE_NORMALIZER
    q = q.reshape(b_sz, tl, GLA_HEADS, GLA_HEAD_K).astype(jnp.float32) * (GLA_HEAD_K ** -0.5)
    k = k.reshape(b_sz, tl, GLA_HEADS, GLA_HEAD_K).astype(jnp.float32)
    v = v.reshape(b_sz, tl, GLA_HEADS, GLA_HEAD_V).astype(jnp.float32)
    gk = gk.reshape(b_sz, tl, GLA_HEADS, GLA_HEAD_K)
    o = gla_chunk_scan(q, k, v, gk)
    o = rmsnorm(o, o_norm_w).reshape(b_sz, tl, GLA_VAL_DIM)
    o = o * jax.nn.silu(g.astype(jnp.float32))
    return o.astype(u.dtype) @ w_out


def causal_depthwise_conv(u, w, b):
    out = lax.conv_general_dilated(
        u, w[:, None, :].astype(u.dtype), window_strides=(1,),
        padding=[(SSD_CONV - 1, 0)], dimension_numbers=('NWC', 'WIO', 'NWC'),
        feature_group_count=u.shape[-1])
    return out + b


def ssd_chunk_scan(xs, dt, A, Bm, Cm):
    b_sz = xs.shape[0]
    causal = jnp.tril(jnp.ones((CHUNK, CHUNK), dtype=bool))

    def step(S, inp):
        xc, dtc, bc, cc = inp
        a = jnp.cumsum(dtc * A, axis=1)
        a_last = a[:, -1]
        diff = a[:, :, None] - a[:, None]
        L = jnp.exp(jnp.where(causal[None, :, :, None, None], diff, -jnp.inf))
        cb = jnp.einsum('bign,bjgn->bijg', cc, bc)
        scores = cb[..., None] * L * dtc[:, None]
        y_intra = jnp.einsum('bijgh,bjghp->bighp', scores, xc)
        y_inter = jnp.einsum('bign,bghpn->bighp', cc, S) * jnp.exp(a)[..., None]
        w_state = jnp.exp(a_last[:, None] - a) * dtc
        S = (jnp.exp(a_last)[..., None, None] * S
             + jnp.einsum('bjgh,bjghp,bjgn->bghpn', w_state, xc, bc))
        return S, y_intra + y_inter

    S0 = jnp.zeros((b_sz, SSD_GROUPS, SSD_HEADS_PER_GROUP, SSD_HEAD_DIM, SSD_STATE), jnp.float32)
    _, y = lax.scan(step, S0, (to_chunks(xs), to_chunks(dt), to_chunks(Bm), to_chunks(Cm)))
    return from_chunks(y)


def ssd_mixer(u, w_in, conv_w, conv_b, dt_bias, a_log, d_skip, gnorm_w, w_out):
    b_sz, tl, _ = u.shape
    proj = u @ w_in
    z, xbc, dt = jnp.split(proj, [SSD_INNER, SSD_INNER + SSD_CONV_DIM], axis=-1)
    xbc = jax.nn.silu(causal_depthwise_conv(xbc, conv_w, conv_b))
    xs, Bm, Cm = jnp.split(xbc, [SSD_INNER, SSD_INNER + SSD_GROUPS * SSD_STATE], axis=-1)
    xs = xs.reshape(b_sz, tl, SSD_GROUPS, SSD_HEADS_PER_GROUP, SSD_HEAD_DIM).astype(jnp.float32)
    Bm = Bm.reshape(b_sz, tl, SSD_GROUPS, SSD_STATE).astype(jnp.float32)
    Cm = Cm.reshape(b_sz, tl, SSD_GROUPS, SSD_STATE).astype(jnp.float32)
    dt = jax.nn.softplus(dt.astype(jnp.float32) + dt_bias.astype(jnp.float32))
    dt = dt.reshape(b_sz, tl, SSD_GROUPS, SSD_HEADS_PER_GROUP)
    A = -jnp.exp(a_log.astype(jnp.float32)).reshape(SSD_GROUPS, SSD_HEADS_PER_GROUP)
    y = ssd_chunk_scan(xs, dt, A, Bm, Cm)
    y = y + d_skip.astype(jnp.float32).reshape(SSD_GROUPS, SSD_HEADS_PER_GROUP)[..., None] * xs
    y = y.reshape(b_sz, tl, SSD_INNER) * jax.nn.silu(z.astype(jnp.float32))
    y = rmsnorm(y.reshape(b_sz, tl, SSD_GROUPS, SSD_INNER // SSD_GROUPS),
                gnorm_w.reshape(SSD_GROUPS, SSD_INNER // SSD_GROUPS))
    return y.reshape(b_sz, tl, SSD_INNER).astype(u.dtype) @ w_out


def sq_relu_mlp(u, w_fc1, w_fc2):
    hdn = jax.nn.relu(u @ w_fc1)
    return (hdn * hdn) @ w_fc2


def setup_inputs(seed: int = 0) -> dict:
    key = jax.random.key(seed)
    ks = jax.random.split(key, 24)
    n_gla = (DEPTH + 1) // 2
    n_ssd = DEPTH // 2
    f32 = jnp.float32

    def nrm(k, shape, scale):
        return jax.random.normal(k, shape, f32) * scale

    def gain(k, shape):
        return 1.0 + 0.02 * jax.random.normal(k, shape, f32)

    x = jax.random.normal(ks[0], (BATCH, SEQ, D_MODEL), f32)
    mixer_norm_w = gain(ks[1], (DEPTH, D_MODEL))
    gla_w_in = nrm(ks[2], (n_gla, D_MODEL, GLA_IN), D_MODEL ** -0.5)
    gla_w_gk_up = nrm(ks[3], (n_gla, GLA_GATE_RANK, GLA_KEY_DIM), GLA_GATE_RANK ** -0.5)
    gla_b_gk_up = 2.0 + 0.1 * jax.random.normal(ks[4], (n_gla, GLA_KEY_DIM), f32)
    gla_o_norm_w = gain(ks[5], (n_gla, GLA_HEAD_V))
    gla_w_out = nrm(ks[6], (n_gla, GLA_VAL_DIM, D_MODEL), GLA_VAL_DIM ** -0.5)
    ssd_w_in = nrm(ks[7], (n_ssd, D_MODEL, SSD_IN), D_MODEL ** -0.5)
    ssd_conv_w = nrm(ks[8], (n_ssd, SSD_CONV, SSD_CONV_DIM), SSD_CONV ** -0.5)
    ssd_conv_b = nrm(ks[9], (n_ssd, SSD_CONV_DIM), 0.02)
    dt0 = jnp.exp(jax.random.uniform(ks[10], (n_ssd, SSD_HEADS), f32,
                                     np.log(1e-3).astype(np.float32), np.log(1e-1).astype(np.float32)))
    ssd_dt_bias = dt0 + jnp.log(-jnp.expm1(-dt0))
    ssd_a_log = jnp.log(jax.random.uniform(ks[11], (n_ssd, SSD_HEADS), f32, 1.0, 16.0))
    ssd_d_skip = gain(ks[12], (n_ssd, SSD_HEADS))
    ssd_gnorm_w = gain(ks[13], (n_ssd, SSD_INNER))
    ssd_w_out = nrm(ks[14], (n_ssd, SSD_INNER, D_MODEL), SSD_INNER ** -0.5)
    mlp_norm_w = gain(ks[15], (DEPTH, D_MODEL))
    mlp_w_fc1 = nrm(ks[16], (DEPTH, D_MODEL, D_FF), D_MODEL ** -0.5)
    mlp_w_fc2 = nrm(ks[17], (DEPTH, D_FF, D_MODEL), D_FF ** -0.5)
    final_norm_w = gain(ks[18], (D_MODEL,))
    return {"x": x, "mixer_norm_w": mixer_norm_w,
            "gla_w_in": gla_w_in, "gla_w_gk_up": gla_w_gk_up, "gla_b_gk_up": gla_b_gk_up,
            "gla_o_norm_w": gla_o_norm_w, "gla_w_out": gla_w_out,
            "ssd_w_in": ssd_w_in, "ssd_conv_w": ssd_conv_w, "ssd_conv_b": ssd_conv_b,
            "ssd_dt_bias": ssd_dt_bias, "ssd_a_log": ssd_a_log, "ssd_d_skip": ssd_d_skip,
            "ssd_gnorm_w": ssd_gnorm_w, "ssd_w_out": ssd_w_out,
            "mlp_norm_w": mlp_norm_w, "mlp_w_fc1": mlp_w_fc1, "mlp_w_fc2": mlp_w_fc2,
            "final_norm_w": final_norm_w}


def reference(x, mixer_norm_w, gla_w_in, gla_w_gk_up, gla_b_gk_up, gla_o_norm_w, gla_w_out,
              ssd_w_in, ssd_conv_w, ssd_conv_b, ssd_dt_bias, ssd_a_log, ssd_d_skip,
              ssd_gnorm_w, ssd_w_out, mlp_norm_w, mlp_w_fc1, mlp_w_fc2, final_norm_w):
    h = x
    for i in range(DEPTH):
        j = i // N_MIXERS
        u = rmsnorm(h, mixer_norm_w[i])
        if i % N_MIXERS == 0:
            h = h + gla_mixer(u, gla_w_in[j], gla_w_gk_up[j], gla_b_gk_up[j],
                              gla_o_norm_w[j], gla_w_out[j])
        else:
            h = h + ssd_mixer(u, ssd_w_in[j], ssd_conv_w[j], ssd_conv_b[j], ssd_dt_bias[j],
                              ssd_a_log[j], ssd_d_skip[j], ssd_gnorm_w[j], ssd_w_out[j])
        u = rmsnorm(h, mlp_norm_w[i])
        h = h + sq_relu_mlp(u, mlp_w_fc1[i], mlp_w_fc2[i])
    return rmsnorm(h, final_norm_w)
```

```python
import functools
import math

import jax
import jax.numpy as jnp
from jax import lax
from jax.experimental import pallas as pl
from jax.experimental.pallas import tpu as pltpu

_F32 = jnp.float32
_MXU_DTYPE = jnp.bfloat16
_RMS_EPS = 1e-5
_GLA_GATE_NORMALIZER = 16.0
_SSD_GROUPS = 8
_CHUNK = 128
_LANES = 128
_GLA_DIAG = 16
_VMEM_LIMIT_BYTES = 56 * 1024 * 1024

_NT = (((1,), (1,)), ((), ()))
_TN = (((0,), (0,)), ((), ()))


def _params(sem):
    return pltpu.CompilerParams(dimension_semantics=sem, vmem_limit_bytes=_VMEM_LIMIT_BYTES)


def _rmsnorm_kernel(x_ref, w_ref, o_ref):
    x = x_ref[...].astype(_F32)
    ms = jnp.mean(x * x, axis=-1, keepdims=True)
    o_ref[...] = ((x * lax.rsqrt(ms + _RMS_EPS)) * w_ref[...]).astype(o_ref.dtype)


def _rmsnorm(x, w, out_dtype, rows=512):
    m, d = x.shape
    rows = min(rows, m)
    return pl.pallas_call(
        _rmsnorm_kernel,
        grid=(m // rows,),
        in_specs=[pl.BlockSpec((rows, d), lambda i: (i, 0)),
                  pl.BlockSpec((1, d), lambda i: (0, 0))],
        out_specs=pl.BlockSpec((rows, d), lambda i: (i, 0)),
        out_shape=jax.ShapeDtypeStruct((m, d), out_dtype),
        compiler_params=_params(("parallel",)),
        name="rmsnorm",
    )(x, w.reshape(1, d).astype(_F32))


def _matmul_kernel(*refs, nk, act, has_res):
    a_ref, w_ref = refs[0], refs[1]
    r_ref = refs[2] if has_res else None
    o_ref = refs[3] if has_res else refs[2]
    acc_ref = refs[-1] if nk > 1 else None

    part = jnp.dot(a_ref[...], w_ref[...], preferred_element_type=_F32)

    def finish(acc):
        if act == "relu2":
            acc = jnp.square(jnp.maximum(acc, 0.0))
        if has_res:
            acc = r_ref[...] + acc
        o_ref[...] = acc.astype(o_ref.dtype)

    if nk == 1:
        finish(part)
    else:
        k = pl.program_id(2)

        @pl.when(k == 0)
        def _():
            acc_ref[...] = part

        @pl.when(k > 0)
        def _():
            acc_ref[...] += part

        @pl.when(k == nk - 1)
        def _():
            finish(acc_ref[...])


def _divisor_tile(n, pref, align):
    if n <= pref:
        return n
    t = (pref // align) * align
    while n % t:
        t -= align
    return t


def _matmul_tiles(m, k, n):
    if k <= 2048:
        pm, pn = 1024, 1024
    elif k <= 4096:
        pm, pn = 1024, 512
    else:
        pm, pn = 512, 512
    return _divisor_tile(m, pm, 8), _divisor_tile(n, pn, _LANES), k


def _matmul(a, w, *, act=None, res=None, out_dtype):
    m, k = a.shape
    n = w.shape[1]
    tm, tn, tk = _matmul_tiles(m, k, n)
    nk = k // tk
    in_specs = [pl.BlockSpec((tm, tk), lambda i, j, kk: (i, kk)),
                pl.BlockSpec((tk, tn), lambda i, j, kk: (kk, j))]
    args = [a, w]
    if res is not None:
        in_specs.append(pl.BlockSpec((tm, tn), lambda i, j, kk: (i, j)))
        args.append(res)
    return pl.pallas_call(
        functools.partial(_matmul_kernel, nk=nk, act=act, has_res=res is not None),
        grid=(m // tm, n // tn, nk),
        in_specs=in_specs,
        out_specs=pl.BlockSpec((tm, tn), lambda i, j, kk: (i, j)),
        out_shape=jax.ShapeDtypeStruct((m, n), out_dtype),
        scratch_shapes=[pltpu.VMEM((tm, tn), _F32)] if nk > 1 else [],
        compiler_params=_params(("parallel", "parallel", "arbitrary")),
        name="matmul",
    )(*args)


def _softplus(x):
    return jnp.maximum(x, 0.0) + jnp.log1p(jnp.exp(-jnp.abs(x)))


def _chunk_cumsum(x, chunk):
    rows = x.shape[0]
    r = lax.broadcasted_iota(jnp.int32, (chunk, chunk), 0)
    c = lax.broadcasted_iota(jnp.int32, (chunk, chunk), 1)
    tri = jnp.where(c <= r, 1.0, 0.0).astype(_MXU_DTYPE)
    out = []
    for i in range(rows // chunk):
        rem = x[i * chunk:(i + 1) * chunk, :]
        acc = None
        for _ in range(3):
            piece = rem.astype(_MXU_DTYPE)
            d = jnp.dot(tri, piece, preferred_element_type=_F32)
            acc = d if acc is None else acc + d
            rem = rem - piece.astype(_F32)
        out.append(acc)
    return jnp.concatenate(out, axis=0) if len(out) > 1 else out[0]


def _row_bcast(ref, row, n):
    return jnp.broadcast_to(ref[pl.ds(row, 1), :], (n, ref.shape[1]))


def _gla_gate_kernel(u_ref, wgr_ref, wup_ref, b_ref, o_ref, *, chunk):
    gr = jnp.dot(u_ref[...], wgr_ref[...], preferred_element_type=_F32)
    pre = jnp.dot(gr.astype(_MXU_DTYPE), wup_ref[...], preferred_element_type=_F32) + b_ref[...]
    gk = -_softplus(-pre) / _GLA_GATE_NORMALIZER
    o_ref[...] = _chunk_cumsum(gk, chunk)


def _gla_gate(u, w_gr, w_up, b_up, rows=512):
    m, d = u.shape
    kd = w_up.shape[1]
    rows = min(rows, m)
    return pl.pallas_call(
        functools.partial(_gla_gate_kernel, chunk=_CHUNK),
        grid=(m // rows,),
        in_specs=[pl.BlockSpec((rows, d), lambda i: (i, 0)),
                  pl.BlockSpec(w_gr.shape, lambda i: (0, 0)),
                  pl.BlockSpec(w_up.shape, lambda i: (0, 0)),
                  pl.BlockSpec((1, kd), lambda i: (0, 0))],
        out_specs=pl.BlockSpec((rows, kd), lambda i: (i, 0)),
        out_shape=jax.ShapeDtypeStruct((m, kd), _F32),
        compiler_params=_params(("parallel",)),
        name="gla_gate",
    )(u, w_gr, w_up, b_up)


def _gla_scan_kernel(q_ref, k_ref, v_ref, g_ref, b_ref, onw_ref, o_ref, st_ref, qf_ref, kf_ref,
                     *, scale):
    c, dk = q_ref.shape

    @pl.when(pl.program_id(2) == 0)
    def _():
        st_ref[...] = jnp.zeros_like(st_ref)

    qf_ref[...] = q_ref[...].astype(_F32) * scale
    kf_ref[...] = k_ref[...].astype(_F32)
    q = qf_ref[...]
    k = kf_ref[...]
    v = v_ref[...]
    bc = b_ref[...]
    st = st_ref[...]
    b_last = _row_bcast(b_ref, c - 1, c)

    o = lax.dot_general((q * jnp.exp(bc)).astype(_MXU_DTYPE), st.astype(_MXU_DTYPE), _NT,
                        preferred_element_type=_F32)
    k_dec = (k * jnp.exp(b_last - bc)).astype(_MXU_DTYPE)

    row = lax.broadcasted_iota(jnp.int32, (c, c), 0)
    col = lax.broadcasted_iota(jnp.int32, (c, c), 1)

    att = jnp.zeros((c, c), _F32)
    s = _GLA_DIAG
    while s < c:
        prev = [jnp.zeros((s, dk), _F32)] + [_row_bcast(b_ref, m * s - 1, s) for m in range(1, c // s)]
        end = [_row_bcast(b_ref, (m + 1) * s - 1, s) for m in range(c // s)]
        qs = (q * jnp.exp(bc - jnp.concatenate(prev, axis=0))).astype(_MXU_DTYPE)
        ks = (k * jnp.exp(jnp.concatenate(end, axis=0) - bc)).astype(_MXU_DTYPE)
        a_s = lax.dot_general(qs, ks, _NT, preferred_element_type=_F32)
        sh = int(math.log2(s))
        rb = row >> sh
        cb = col >> sh
        att = att + jnp.where((rb == cb + 1) & ((rb & 1) == 1), a_s, 0.0)
        s *= 2

    nd = _GLA_DIAG
    lane = lax.broadcasted_iota(jnp.int32, (nd, c), 1)
    pieces = []
    for m in range(c // nd):
        qb = qf_ref[m * nd:(m + 1) * nd, :]
        bb = b_ref[m * nd:(m + 1) * nd, :]
        acc = jnp.zeros((nd, c), _F32)
        for j in range(nd):
            kj = _row_bcast(kf_ref, m * nd + j, nd)
            bj = _row_bcast(b_ref, m * nd + j, nd)
            term = qb * kj * jnp.exp(jnp.minimum(bb - bj, 0.0))
            acc = jnp.where(lane == m * nd + j, jnp.sum(term, axis=1, keepdims=True), acc)
        pieces.append(acc)
    diag = jnp.concatenate(pieces, axis=0)
    sh = int(math.log2(nd))
    att = att + jnp.where(((row >> sh) == (col >> sh)) & (col <= row), diag, 0.0)

    o = o + jnp.dot(att.astype(_MXU_DTYPE), v, preferred_element_type=_F32)
    st_ref[...] = st * jnp.exp(b_last[:1, :]) + lax.dot_general(
        v, k_dec, _TN, preferred_element_type=_F32)

    ms = jnp.mean(o * o, axis=-1, keepdims=True)
    on = (o * lax.rsqrt(ms + _RMS_EPS)) * onw_ref[...]
    g = g_ref[...].astype(_F32)
    o_ref[...] = (on * (g * jax.nn.sigmoid(g))).astype(o_ref.dtype)


def _gla_scan(proj, bcum, o_norm_w, batch, seq, heads, dk, dv):
    m = proj.shape[0]
    c = _CHUNK
    assert c == _LANES and dk % _LANES == 0 and dv % _LANES == 0 and (2 * heads * dk) % dv == 0
    nt = seq // c
    kd, vd = heads * dk, heads * dv
    k_blk, v_blk, g_blk = kd // dk, (2 * kd) // dv, (2 * kd + vd) // dv

    def rows(b, h, t):
        return b * nt + t

    return pl.pallas_call(
        functools.partial(_gla_scan_kernel, scale=dk ** -0.5),
        grid=(batch, heads, nt),
        in_specs=[pl.BlockSpec((c, dk), lambda b, h, t: (rows(b, h, t), h)),
                  pl.BlockSpec((c, dk), lambda b, h, t: (rows(b, h, t), k_blk + h)),
                  pl.BlockSpec((c, dv), lambda b, h, t: (rows(b, h, t), v_blk + h)),
                  pl.BlockSpec((c, dv), lambda b, h, t: (rows(b, h, t), g_blk + h)),
                  pl.BlockSpec((c, dk), lambda b, h, t: (rows(b, h, t), h)),
                  pl.BlockSpec((1, dv), lambda b, h, t: (0, 0))],
        out_specs=pl.BlockSpec((c, dv), lambda b, h, t: (rows(b, h, t), h)),
        out_shape=jax.ShapeDtypeStruct((m, vd), _MXU_DTYPE),
        scratch_shapes=[pltpu.VMEM((dv, dk), _F32),
                        pltpu.VMEM((c, dk), _F32),
                        pltpu.VMEM((c, dk), _F32)],
        compiler_params=_params(("parallel", "parallel", "arbitrary")),
        name="gla_scan",
    )(proj, proj, proj, proj, bcum, o_norm_w.reshape(1, dv).astype(_F32))


def _gla_mixer(h, u, batch, seq, w_in, w_gk_up, b_gk_up, o_norm_w, w_out):
    rank, kd = w_gk_up.shape
    vd = (w_in.shape[1] - rank - 2 * kd) // 2
    dv = o_norm_w.shape[0]
    heads = vd // dv
    n_main = 2 * kd + 2 * vd
    w_main = w_in[:, :n_main].astype(_MXU_DTYPE)
    w_gr = jnp.pad(w_in[:, n_main:], ((0, 0), (0, _LANES - rank))).astype(_MXU_DTYPE)
    w_up = jnp.pad(w_gk_up, ((0, _LANES - rank), (0, 0))).astype(_MXU_DTYPE)
    proj = _matmul(u, w_main, out_dtype=_MXU_DTYPE)
    bcum = _gla_gate(u, w_gr, w_up, b_gk_up.reshape(1, kd).astype(_F32))
    o = _gla_scan(proj, bcum, o_norm_w, batch, seq, heads, kd // heads, dv)
    return _matmul(o, w_out.astype(_MXU_DTYPE), res=h, out_dtype=_F32)


def _ssd_gate_kernel(u_ref, wdt_ref, bias_ref, alog_ref, acol_ref, dcol_ref, arow_ref, drow_ref,
                     *, chunk, heads, groups):
    raw = jnp.dot(u_ref[...], wdt_ref[...], preferred_element_type=_F32)
    dt = _softplus(raw + bias_ref[...])
    a = _chunk_cumsum(dt * (-jnp.exp(alog_ref[...])), chunk)
    hg = heads // groups
    for g in range(groups):
        acol_ref[g] = a[:, g * hg:(g + 1) * hg]
        dcol_ref[g] = dt[:, g * hg:(g + 1) * hg]
    for i in range(a.shape[0] // chunk):
        arow_ref[i] = a[i * chunk:(i + 1) * chunk, :].T[:heads, :]
        drow_ref[i] = dt[i * chunk:(i + 1) * chunk, :].T[:heads, :]


def _ssd_gate(u, w_dt, dt_bias, a_log, heads, rows=512):
    m, d = u.shape
    c = _CHUNK
    rows = min(rows, m)
    g = _SSD_GROUPS
    hg = heads // g
    col = jax.ShapeDtypeStruct((g, m, hg), _F32)
    row = jax.ShapeDtypeStruct((m // c, heads, c), _F32)
    col_spec = pl.BlockSpec((g, rows, hg), lambda i: (0, i, 0))
    row_spec = pl.BlockSpec((rows // c, heads, c), lambda i: (i, 0, 0))
    return pl.pallas_call(
        functools.partial(_ssd_gate_kernel, chunk=c, heads=heads, groups=g),
        grid=(m // rows,),
        in_specs=[pl.BlockSpec((rows, d), lambda i: (i, 0)),
                  pl.BlockSpec(w_dt.shape, lambda i: (0, 0)),
                  pl.BlockSpec((1, _LANES), lambda i: (0, 0)),
                  pl.BlockSpec((1, _LANES), lambda i: (0, 0))],
        out_specs=[col_spec, col_spec, row_spec, row_spec],
        out_shape=[col, col, row, row],
        compiler_params=_params(("parallel",)),
        name="ssd_gate",
    )(u, w_dt, dt_bias, a_log)


def _silu(x):
    return x * jax.nn.sigmoid(x)


def _causal_conv(ext_ref, raw_ref, w_ref, b_ref, first):
    c = raw_ref.shape[0]
    kw = w_ref.shape[0]

    @pl.when(first)
    def _():
        ext_ref[0:8, :] = jnp.zeros((8, ext_ref.shape[1]), _F32)

    ext_ref[8:8 + c, :] = raw_ref[...].astype(_F32)
    out = b_ref[...]
    for i in range(kw):
        out = out + w_ref[i:i + 1, :] * ext_ref[pl.ds(8 - (kw - 1) + i, c), :]
    tail = ext_ref[c:c + 8, :]
    ext_ref[0:8, :] = tail
    return out


def _expand_heads(colarr, n_heads, head_dim):
    c = colarr.shape[0]
    per = _LANES // head_dim
    lane = lax.broadcasted_iota(jnp.int32, (c, _LANES), 1)
    pieces = []
    for p in range(n_heads // per):
        out = jnp.broadcast_to(colarr[:, p * per:p * per + 1], (c, _LANES))
        for i in range(1, per):
            nxt = jnp.broadcast_to(colarr[:, p * per + i:p * per + i + 1], (c, _LANES))
            out = jnp.where(lane >= i * head_dim, nxt, out)
        pieces.append(out)
    return jnp.concatenate(pieces, axis=1)


def _ssd_scan_kernel(z_ref, x_ref, bm_ref, cm_ref, wx_ref, wb_ref, wc_ref, cbx_ref, cbb_ref, cbc_ref,
                     acol_ref, dcol_ref, arow_ref, drow_ref, dskip_ref, gnw_ref, o_ref,
                     st_ref, ex_ref, eb_ref, ec_ref, *, head_dim):
    c = z_ref.shape[0]
    hg = acol_ref.shape[1]
    first = pl.program_id(2) == 0

    @pl.when(first)
    def _():
        st_ref[...] = jnp.zeros_like(st_ref)

    xs = _silu(_causal_conv(ex_ref, x_ref, wx_ref, cbx_ref, first))
    bm = _silu(_causal_conv(eb_ref, bm_ref, wb_ref, cbb_ref, first))
    cm = _silu(_causal_conv(ec_ref, cm_ref, wc_ref, cbc_ref, first))
    bmx = bm.astype(_MXU_DTYPE)
    cmx = cm.astype(_MXU_DTYPE)

    acol = acol_ref[...]
    dcol = dcol_ref[...]
    a_last = acol[c - 1:c, :]
    w_state = jnp.exp(a_last - acol) * dcol

    per = _LANES // head_dim
    width = per * c
    cb = lax.dot_general(cmx, jnp.concatenate([bmx] * per, axis=0), _NT,
                         preferred_element_type=_F32)
    ri = lax.broadcasted_iota(jnp.int32, (c, width), 0)
    ci = lax.broadcasted_iota(jnp.int32, (c, width), 1) & (c - 1)
    causal = ci <= ri
    lane = lax.broadcasted_iota(jnp.int32, (c, _LANES), 1)
    y_pieces = []
    for p in range(hg // per):
        ac = jnp.concatenate(
            [jnp.broadcast_to(acol[:, p * per + i:p * per + i + 1], (c, c)) for i in range(per)], axis=1)
        ar = arow_ref[0, :, p * width:(p + 1) * width]
        dr = drow_ref[0, :, p * width:(p + 1) * width]
        decay = jnp.where(causal, jnp.exp(jnp.minimum(ac - ar, 0.0)), 0.0)
        sc = (cb * decay * dr).astype(_MXU_DTYPE)
        xp = xs[:, p * _LANES:(p + 1) * _LANES]
        bd = jnp.concatenate(
            [jnp.where((lane >= i * head_dim) & (lane < (i + 1) * head_dim), xp, 0.0) for i in range(per)],
            axis=0).astype(_MXU_DTYPE)
        y_pieces.append(jnp.dot(sc, bd, preferred_element_type=_F32))
    y = jnp.concatenate(y_pieces, axis=1)

    st = st_ref[...]
    ea = jnp.exp(_expand_heads(acol, hg, head_dim))
    y = y + jnp.dot(cmx, st.astype(_MXU_DTYPE), preferred_element_type=_F32) * ea
    xw = (xs * _expand_heads(w_state, hg, head_dim)).astype(_MXU_DTYPE)
    st_ref[...] = st * ea[c - 1:c, :] + lax.dot_general(bmx, xw, _TN, preferred_element_type=_F32)

    y = y + dskip_ref[...] * xs
    y = y * _silu(z_ref[...].astype(_F32))
    ms = jnp.mean(y * y, axis=-1, keepdims=True)
    o_ref[...] = ((y * lax.rsqrt(ms + _RMS_EPS)) * gnw_ref[...]).astype(o_ref.dtype)


def _ssd_scan(proj, conv_w, conv_b, acol, dcol, arow, drow, dskip, gnorm_w, batch, seq, inner, heads, n_state):
    m = proj.shape[0]
    c = _CHUNK
    g = _SSD_GROUPS
    hg = heads // g
    p = inner // heads
    gw = hg * p
    assert c == _LANES and gw % _LANES == 0 and n_state % _LANES == 0 and _LANES % p == 0
    assert inner % gw == 0 and inner % n_state == 0 and (g * n_state) % n_state == 0
    nt = seq // c
    x_blk = inner // gw
    b_blk = (2 * inner) // n_state
    c_blk = (2 * inner + g * n_state) // n_state
    cw_b = inner // n_state
    cw_c = (inner + g * n_state) // n_state

    def rows(b, t):
        return b * nt + t

    kw = conv_w.shape[0]
    in_specs = [
        pl.BlockSpec((c, gw), lambda b, gi, t: (rows(b, t), gi)),
        pl.BlockSpec((c, gw), lambda b, gi, t: (rows(b, t), x_blk + gi)),
        pl.BlockSpec((c, n_state), lambda b, gi, t: (rows(b, t), b_blk + gi)),
        pl.BlockSpec((c, n_state), lambda b, gi, t: (rows(b, t), c_blk + gi)),
        pl.BlockSpec((kw, gw), lambda b, gi, t: (0, gi)),
        pl.BlockSpec((kw, n_state), lambda b, gi, t: (0, cw_b + gi)),
        pl.BlockSpec((kw, n_state), lambda b, gi, t: (0, cw_c + gi)),
        pl.BlockSpec((1, gw), lambda b, gi, t: (0, gi)),
        pl.BlockSpec((1, n_state), lambda b, gi, t: (0, cw_b + gi)),
        pl.BlockSpec((1, n_state), lambda b, gi, t: (0, cw_c + gi)),
        pl.BlockSpec((None, c, hg), lambda b, gi, t: (gi, rows(b, t), 0)),
        pl.BlockSpec((None, c, hg), lambda b, gi, t: (gi, rows(b, t), 0)),
        pl.BlockSpec((1, 1, hg * c), lambda b, gi, t: (rows(b, t), 0, gi)),
        pl.BlockSpec((1, 1, hg * c), lambda b, gi, t: (rows(b, t), 0, gi)),
        pl.BlockSpec((1, gw), lambda b, gi, t: (0, gi)),
        pl.BlockSpec((1, gw), lambda b, gi, t: (0, gi)),
    ]
    return pl.pallas_call(
        functools.partial(_ssd_scan_kernel, head_dim=p),
        grid=(batch, g, nt),
        in_specs=in_specs,
        out_specs=pl.BlockSpec((c, gw), lambda b, gi, t: (rows(b, t), gi)),
        out_shape=jax.ShapeDtypeStruct((m, inner), _MXU_DTYPE),
        scratch_shapes=[pltpu.VMEM((n_state, gw), _F32),
                        pltpu.VMEM((c + 8, gw), _F32),
                        pltpu.VMEM((c + 8, n_state), _F32),
                        pltpu.VMEM((c + 8, n_state), _F32)],
        compiler_params=_params(("parallel", "parallel", "arbitrary")),
        name="ssd_scan",
    )(proj, proj, proj, proj, conv_w, conv_w, conv_w, conv_b, conv_b, conv_b,
      acol, dcol, arow, drow, dskip, gnorm_w)


def _ssd_mixer(h, u, batch, seq, w_in, conv_w, conv_b, dt_bias, a_log, d_skip, gnorm_w, w_out):
    heads = dt_bias.shape[0]
    conv_dim = conv_w.shape[1]
    inner = w_in.shape[1] - conv_dim - heads
    n_state = (conv_dim - inner) // (2 * _SSD_GROUPS)
    n_main = inner + conv_dim
    m = u.shape[0]
    w_main = w_in[:, :n_main].astype(_MXU_DTYPE)
    w_dt = jnp.pad(w_in[:, n_main:], ((0, 0), (0, _LANES - heads))).astype(_MXU_DTYPE)
    pad1 = lambda v: jnp.pad(v.astype(_F32), (0, _LANES - heads)).reshape(1, _LANES)
    proj = _matmul(u, w_main, out_dtype=_MXU_DTYPE)
    acol, dcol, arow, drow = _ssd_gate(u, w_dt, pad1(dt_bias), pad1(a_log), heads)
    arow = arow.reshape(m // _CHUNK, 1, heads * _CHUNK)
    drow = drow.reshape(m // _CHUNK, 1, heads * _CHUNK)
    dskip = jnp.repeat(d_skip.astype(_F32), inner // heads).reshape(1, inner)
    y = _ssd_scan(proj, conv_w.astype(_F32), conv_b.reshape(1, conv_dim).astype(_F32), acol, dcol, arow, drow,
                  dskip, gnorm_w.reshape(1, inner).astype(_F32), batch, seq, inner, heads, n_state)
    return _matmul(y, w_out.astype(_MXU_DTYPE), res=h, out_dtype=_F32)


def kernel(x, mixer_norm_w, gla_w_in, gla_w_gk_up, gla_b_gk_up, gla_o_norm_w, gla_w_out, ssd_w_in, ssd_conv_w, ssd_conv_b, ssd_dt_bias, ssd_a_log, ssd_d_skip, ssd_gnorm_w, ssd_w_out, mlp_norm_w, mlp_w_fc1, mlp_w_fc2, final_norm_w):
    batch, seq, d = x.shape
    h = x.reshape(batch * seq, d)
    for i in range(mixer_norm_w.shape[0]):
        j = i // 2
        u = _rmsnorm(h, mixer_norm_w[i], _MXU_DTYPE)
        if i % 2 == 0:
            h = _gla_mixer(h, u, batch, seq, gla_w_in[j], gla_w_gk_up[j], gla_b_gk_up[j],
                           gla_o_norm_w[j], gla_w_out[j])
        else:
            h = _ssd_mixer(h, u, batch, seq, ssd_w_in[j], ssd_conv_w[j], ssd_conv_b[j], ssd_dt_bias[j],
                           ssd_a_log[j], ssd_d_skip[j], ssd_gnorm_w[j], ssd_w_out[j])
        u = _rmsnorm(h, mlp_norm_w[i], _MXU_DTYPE)
        hidden = _matmul(u, mlp_w_fc1[i].astype(_MXU_DTYPE), act="relu2", out_dtype=_MXU_DTYPE)
        h = _matmul(hidden, mlp_w_fc2[i].astype(_MXU_DTYPE), res=h, out_dtype=_F32)
    return _rmsnorm(h, final_norm_w, _F32).reshape(batch, seq, d)
```

```python
import functools
import math

import jax
import jax.numpy as jnp
from jax import lax
from jax.experimental import pallas as pl
from jax.experimental.pallas import tpu as pltpu

_F32 = jnp.float32
_MXU_DTYPE = jnp.bfloat16
_RMS_EPS = 1e-5
_GLA_GATE_NORMALIZER = 16.0
_SSD_GROUPS = 8
_CHUNK = 128
_LANES = 128
_GLA_DIAG = 8
_GLA_HEADS_PER_STEP = 2
_SSD_GROUPS_PER_STEP = 2
_LOG2E = math.log2(math.e)
_NEG_BIG = -1e30
_VMEM_LIMIT_BYTES = 56 * 1024 * 1024

_NT = (((1,), (1,)), ((), ()))
_TN = (((0,), (0,)), ((), ()))


def _params(sem):
    return pltpu.CompilerParams(dimension_semantics=sem, vmem_limit_bytes=_VMEM_LIMIT_BYTES)


def _rmsnorm_kernel(x_ref, w_ref, o_ref):
    x = x_ref[...].astype(_F32)
    ms = jnp.mean(x * x, axis=-1, keepdims=True)
    o_ref[...] = ((x * lax.rsqrt(ms + _RMS_EPS)) * w_ref[...]).astype(o_ref.dtype)


def _rmsnorm(x, w, out_dtype, rows=512):
    m, d = x.shape
    rows = min(rows, m)
    return pl.pallas_call(
        _rmsnorm_kernel,
        grid=(m // rows,),
        in_specs=[pl.BlockSpec((rows, d), lambda i: (i, 0)),
                  pl.BlockSpec((1, d), lambda i: (0, 0))],
        out_specs=pl.BlockSpec((rows, d), lambda i: (i, 0)),
        out_shape=jax.ShapeDtypeStruct((m, d), out_dtype),
        compiler_params=_params(("parallel",)),
        name="rmsnorm",
    )(x, w.reshape(1, d).astype(_F32))


def _matmul_kernel(*refs, nk, act, has_res):
    a_ref, w_ref = refs[0], refs[1]
    r_ref = refs[2] if has_res else None
    o_ref = refs[3] if has_res else refs[2]
    acc_ref = refs[-1] if nk > 1 else None

    part = jnp.dot(a_ref[...], w_ref[...], preferred_element_type=_F32)

    def finish(acc):
        if act == "relu2":
            acc = jnp.square(jnp.maximum(acc, 0.0))
        if has_res:
            acc = r_ref[...] + acc
        o_ref[...] = acc.astype(o_ref.dtype)

    if nk == 1:
        finish(part)
    else:
        k = pl.program_id(2)

        @pl.when(k == 0)
        def _():
            acc_ref[...] = part

        @pl.when(k > 0)
        def _():
            acc_ref[...] += part

        @pl.when(k == nk - 1)
        def _():
            finish(acc_ref[...])


def _divisor_tile(n, pref, align):
    if n <= pref:
        return n
    t = (pref // align) * align
    while n % t:
        t -= align
    return t


def _matmul_tiles(m, k, n):
    if k <= 2048:
        pm, pn = 1024, 1024
    elif k <= 4096:
        pm, pn = 1024, 512
    else:
        pm, pn = 512, 512
    return _divisor_tile(m, pm, 8), _divisor_tile(n, pn, _LANES), k


def _matmul(a, w, *, act=None, res=None, out_dtype):
    m, k = a.shape
    n = w.shape[1]
    tm, tn, tk = _matmul_tiles(m, k, n)
    nk = k // tk
    in_specs = [pl.BlockSpec((tm, tk), lambda i, j, kk: (i, kk)),
                pl.BlockSpec((tk, tn), lambda i, j, kk: (kk, j))]
    args = [a, w]
    if res is not None:
        in_specs.append(pl.BlockSpec((tm, tn), lambda i, j, kk: (i, j)))
        args.append(res)
    return pl.pallas_call(
        functools.partial(_matmul_kernel, nk=nk, act=act, has_res=res is not None),
        grid=(m // tm, n // tn, nk),
        in_specs=in_specs,
        out_specs=pl.BlockSpec((tm, tn), lambda i, j, kk: (i, j)),
        out_shape=jax.ShapeDtypeStruct((m, n), out_dtype),
        scratch_shapes=[pltpu.VMEM((tm, tn), _F32)] if nk > 1 else [],
        compiler_params=_params(("parallel", "parallel", "arbitrary")),
        name="matmul",
    )(*args)


def _softplus(x):
    return jnp.maximum(x, 0.0) + jnp.log1p(jnp.exp(-jnp.abs(x)))


def _chunk_cumsum(x, chunk):
    rows = x.shape[0]
    r = lax.broadcasted_iota(jnp.int32, (chunk, chunk), 0)
    c = lax.broadcasted_iota(jnp.int32, (chunk, chunk), 1)
    tri = jnp.where(c <= r, 1.0, 0.0).astype(_MXU_DTYPE)
    out = []
    for i in range(rows // chunk):
        rem = x[i * chunk:(i + 1) * chunk, :]
        acc = None
        for _ in range(3):
            piece = rem.astype(_MXU_DTYPE)
            d = jnp.dot(tri, piece, preferred_element_type=_F32)
            acc = d if acc is None else acc + d
            rem = rem - piece.astype(_F32)
        out.append(acc)
    return jnp.concatenate(out, axis=0) if len(out) > 1 else out[0]


def _row_bcast(ref, row, n):
    return jnp.broadcast_to(ref[pl.ds(row, 1), :], (n, ref.shape[1]))


def _gla_gate_kernel(u_ref, wgr_ref, wup_ref, b_ref, o_ref, *, chunk):
    gr = jnp.dot(u_ref[...], wgr_ref[...], preferred_element_type=_F32)
    pre = jnp.dot(gr.astype(_MXU_DTYPE), wup_ref[...], preferred_element_type=_F32) + b_ref[...]
    gk = -_softplus(-pre) / _GLA_GATE_NORMALIZER
    o_ref[...] = _chunk_cumsum(gk * _LOG2E, chunk)


def _gla_gate(u, w_gr, w_up, b_up, rows=512):
    m, d = u.shape
    kd = w_up.shape[1]
    rows = min(rows, m)
    return pl.pallas_call(
        functools.partial(_gla_gate_kernel, chunk=_CHUNK),
        grid=(m // rows,),
        in_specs=[pl.BlockSpec((rows, d), lambda i: (i, 0)),
                  pl.BlockSpec(w_gr.shape, lambda i: (0, 0)),
                  pl.BlockSpec(w_up.shape, lambda i: (0, 0)),
                  pl.BlockSpec((1, kd), lambda i: (0, 0))],
        out_specs=pl.BlockSpec((rows, kd), lambda i: (i, 0)),
        out_shape=jax.ShapeDtypeStruct((m, kd), _F32),
        compiler_params=_params(("parallel",)),
        name="gla_gate",
    )(u, w_gr, w_up, b_up)


def _gla_levels(c, nd):
    i = lax.broadcasted_iota(jnp.int32, (c, c), 0)
    j = lax.broadcasted_iota(jnp.int32, (c, c), 1)
    x = i ^ j
    lvl = jnp.zeros((c, c), jnp.int32)
    s, level = nd, 1
    while s < c:
        lvl = jnp.where(x >= s, level, lvl)
        s, level = 2 * s, level + 1
    return jnp.where(j > i, -1, lvl)


def _gla_scan_head(q_ref, k_ref, v_ref, g_ref, b_ref, onw_ref, lvl_ref, o_ref, st_ref, qf_ref, kf_ref, scale):
    c, dk = q_ref.shape
    qf_ref[...] = q_ref[...].astype(_F32) * scale
    kf_ref[...] = k_ref[...].astype(_F32)
    q = qf_ref[...]
    k = kf_ref[...]
    v = v_ref[...]
    bc = b_ref[...]
    st = st_ref[...]
    b_last = jnp.concatenate([_row_bcast(b_ref, c - 1, 8)] * (c // 8), axis=0)

    o = lax.dot_general((q * jnp.exp2(bc)).astype(_MXU_DTYPE), st.astype(_MXU_DTYPE), _NT,
                        preferred_element_type=_F32)
    k_dec = (k * jnp.exp2(b_last - bc)).astype(_MXU_DTYPE)

    nd = _GLA_DIAG
    lane = lax.broadcasted_iota(jnp.int32, (nd, c), 1)
    pieces = []
    for m in range(c // nd):
        qb = qf_ref[m * nd:(m + 1) * nd, :]
        bb = b_ref[m * nd:(m + 1) * nd, :]
        acc = jnp.zeros((nd, c), _F32)
        for j in range(nd):
            kj = _row_bcast(kf_ref, m * nd + j, nd)
            bj = _row_bcast(b_ref, m * nd + j, nd)
            term = qb * kj * jnp.exp2(jnp.minimum(bb - bj, 0.0))
            acc = jnp.where(lane == m * nd + j, jnp.sum(term, axis=1, keepdims=True), acc)
        pieces.append(acc)
    lvl = lvl_ref[...]
    att = jnp.where(lvl == 0, jnp.concatenate(pieces, axis=0), 0.0)

    s, level = nd, 1
    while s < c:
        prev = [jnp.zeros((s, dk), _F32)]
        end = []
        for m in range(c // s):
            if m:
                prev += [_row_bcast(b_ref, m * s - 1, 8)] * (s // 8)
            end += [_row_bcast(b_ref, (m + 1) * s - 1, 8)] * (s // 8)
        qs = (q * jnp.exp2(bc - jnp.concatenate(prev, axis=0))).astype(_MXU_DTYPE)
        ks = (k * jnp.exp2(jnp.concatenate(end, axis=0) - bc)).astype(_MXU_DTYPE)
        att = jnp.where(lvl == level, lax.dot_general(qs, ks, _NT, preferred_element_type=_F32), att)
        s, level = 2 * s, level + 1

    o = o + jnp.dot(att.astype(_MXU_DTYPE), v, preferred_element_type=_F32)
    st_ref[...] = st * jnp.exp2(b_last[:1, :]) + lax.dot_general(
        v, k_dec, _TN, preferred_element_type=_F32)

    ms = jnp.mean(o * o, axis=-1, keepdims=True)
    on = (o * lax.rsqrt(ms + _RMS_EPS)) * onw_ref[...]
    g = g_ref[...].astype(_F32)
    o_ref[...] = (on * (g * jax.nn.sigmoid(g))).astype(o_ref.dtype)


def _gla_scan_kernel(q_ref, k_ref, v_ref, g_ref, b_ref, onw_ref, lvl_ref, o_ref, st_ref, qf_ref, kf_ref,
                     *, scale, dk, dv):
    @pl.when(pl.program_id(2) == 0)
    def _():
        st_ref[...] = jnp.zeros_like(st_ref)

    for h in range(st_ref.shape[0]):
        ks, vs = pl.ds(h * dk, dk), pl.ds(h * dv, dv)
        _gla_scan_head(q_ref.at[:, ks], k_ref.at[:, ks], v_ref.at[:, vs], g_ref.at[:, vs], b_ref.at[:, ks],
                       onw_ref, lvl_ref, o_ref.at[:, vs], st_ref.at[h], qf_ref.at[h], kf_ref.at[h], scale)


def _gla_scan(proj, bcum, o_norm_w, batch, seq, heads, dk, dv):
    m = proj.shape[0]
    c = _CHUNK
    hb = _GLA_HEADS_PER_STEP if heads % _GLA_HEADS_PER_STEP == 0 else 1
    assert c == _LANES and dk % _LANES == 0 and dv % _LANES == 0 and (2 * heads * dk) % (hb * dv) == 0
    nt = seq // c
    kd, vd = heads * dk, heads * dv
    wk, wv = hb * dk, hb * dv
    k_blk, v_blk, g_blk = kd // wk, (2 * kd) // wv, (2 * kd + vd) // wv

    def rows(b, h, t):
        return b * nt + t

    return pl.pallas_call(
        functools.partial(_gla_scan_kernel, scale=dk ** -0.5, dk=dk, dv=dv),
        grid=(batch, heads // hb, nt),
        in_specs=[pl.BlockSpec((c, wk), lambda b, h, t: (rows(b, h, t), h)),
                  pl.BlockSpec((c, wk), lambda b, h, t: (rows(b, h, t), k_blk + h)),
                  pl.BlockSpec((c, wv), lambda b, h, t: (rows(b, h, t), v_blk + h)),
                  pl.BlockSpec((c, wv), lambda b, h, t: (rows(b, h, t), g_blk + h)),
                  pl.BlockSpec((c, wk), lambda b, h, t: (rows(b, h, t), h)),
                  pl.BlockSpec((1, dv), lambda b, h, t: (0, 0)),
                  pl.BlockSpec((c, c), lambda b, h, t: (0, 0))],
        out_specs=pl.BlockSpec((c, wv), lambda b, h, t: (rows(b, h, t), h)),
        out_shape=jax.ShapeDtypeStruct((m, vd), _MXU_DTYPE),
        scratch_shapes=[pltpu.VMEM((hb, dv, dk), _F32),
                        pltpu.VMEM((hb, c, dk), _F32),
                        pltpu.VMEM((hb, c, dk), _F32)],
        compiler_params=_params(("parallel", "parallel", "arbitrary")),
        name="gla_scan",
    )(proj, proj, proj, proj, bcum, o_norm_w.reshape(1, dv).astype(_F32), _gla_levels(c, _GLA_DIAG))


def _gla_mixer(h, u, batch, seq, w_in, w_gk_up, b_gk_up, o_norm_w, w_out):
    rank, kd = w_gk_up.shape
    vd = (w_in.shape[1] - rank - 2 * kd) // 2
    dv = o_norm_w.shape[0]
    heads = vd // dv
    n_main = 2 * kd + 2 * vd
    w_main = w_in[:, :n_main].astype(_MXU_DTYPE)
    w_gr = jnp.pad(w_in[:, n_main:], ((0, 0), (0, _LANES - rank))).astype(_MXU_DTYPE)
    w_up = jnp.pad(w_gk_up, ((0, _LANES - rank), (0, 0))).astype(_MXU_DTYPE)
    proj = _matmul(u, w_main, out_dtype=_MXU_DTYPE)
    bcum = _gla_gate(u, w_gr, w_up, b_gk_up.reshape(1, kd).astype(_F32))
    o = _gla_scan(proj, bcum, o_norm_w, batch, seq, heads, kd // heads, dv)
    return _matmul(o, w_out.astype(_MXU_DTYPE), res=h, out_dtype=_F32)


def _ssd_gate_kernel(u_ref, wdt_ref, bias_ref, alog_ref, acol_ref, dcol_ref, arow_ref, drow_ref,
                     *, chunk, heads, groups):
    raw = jnp.dot(u_ref[...], wdt_ref[...], preferred_element_type=_F32)
    dt = _softplus(raw + bias_ref[...])
    a = _chunk_cumsum(dt * (-jnp.exp(alog_ref[...])) * _LOG2E, chunk)
    hg = heads // groups
    for g in range(groups):
        acol_ref[g] = a[:, g * hg:(g + 1) * hg]
        dcol_ref[g] = dt[:, g * hg:(g + 1) * hg]
    for i in range(a.shape[0] // chunk):
        arow_ref[i] = a[i * chunk:(i + 1) * chunk, :].T[:heads, :]
        drow_ref[i] = dt[i * chunk:(i + 1) * chunk, :].T[:heads, :]


def _ssd_gate(u, w_dt, dt_bias, a_log, heads, rows=512):
    m, d = u.shape
    c = _CHUNK
    rows = min(rows, m)
    g = _SSD_GROUPS
    hg = heads // g
    col = jax.ShapeDtypeStruct((g, m, hg), _F32)
    row = jax.ShapeDtypeStruct((m // c, heads, c), _F32)
    col_spec = pl.BlockSpec((g, rows, hg), lambda i: (0, i, 0))
    row_spec = pl.BlockSpec((rows // c, heads, c), lambda i: (i, 0, 0))
    return pl.pallas_call(
        functools.partial(_ssd_gate_kernel, chunk=c, heads=heads, groups=g),
        grid=(m // rows,),
        in_specs=[pl.BlockSpec((rows, d), lambda i: (i, 0)),
                  pl.BlockSpec(w_dt.shape, lambda i: (0, 0)),
                  pl.BlockSpec((1, _LANES), lambda i: (0, 0)),
                  pl.BlockSpec((1, _LANES), lambda i: (0, 0))],
        out_specs=[col_spec, col_spec, row_spec, row_spec],
        out_shape=[col, col, row, row],
        compiler_params=_params(("parallel",)),
        name="ssd_gate",
    )(u, w_dt, dt_bias, a_log)


def _silu(x):
    return x * jax.nn.sigmoid(x)


def _causal_conv(tail_ref, raw_ref, w_ref, b_ref, first):
    c = raw_ref.shape[0]
    kw = w_ref.shape[0]

    @pl.when(first)
    def _():
        tail_ref[...] = jnp.zeros_like(tail_ref)

    cur = raw_ref[...].astype(_F32)
    ext = jnp.concatenate([tail_ref[...], cur], axis=0)
    out = b_ref[...] + w_ref[kw - 1:kw, :] * cur
    for s in range(1, kw):
        out = out + w_ref[kw - 1 - s:kw - s, :] * pltpu.roll(ext, s, axis=0)[8:, :]
    tail_ref[...] = cur[c - 8:, :]
    return out


def _expand_heads(colarr, n_heads, head_dim):
    c = colarr.shape[0]
    per = _LANES // head_dim
    lane = lax.broadcasted_iota(jnp.int32, (c, _LANES), 1)
    pieces = []
    for p in range(n_heads // per):
        out = jnp.broadcast_to(colarr[:, p * per:p * per + 1], (c, _LANES))
        for i in range(1, per):
            nxt = jnp.broadcast_to(colarr[:, p * per + i:p * per + i + 1], (c, _LANES))
            out = jnp.where(lane >= i * head_dim, nxt, out)
        pieces.append(out)
    return jnp.concatenate(pieces, axis=1)


def _ssd_scan_group(z_ref, x_ref, bm_ref, cm_ref, wx_ref, wb_ref, wc_ref, cbx_ref, cbb_ref, cbc_ref,
                    acol_ref, dcol_ref, arow_ref, drow_ref, dskip_ref, gnw_ref, o_ref,
                    st_ref, tx_ref, tb_ref, tc_ref, head_dim, first):
    c = z_ref.shape[0]
    hg = acol_ref.shape[1]
    xs = _silu(_causal_conv(tx_ref, x_ref, wx_ref, cbx_ref, first))
    bm = _silu(_causal_conv(tb_ref, bm_ref, wb_ref, cbb_ref, first))
    cm = _silu(_causal_conv(tc_ref, cm_ref, wc_ref, cbc_ref, first))
    bmx = bm.astype(_MXU_DTYPE)
    cmx = cm.astype(_MXU_DTYPE)

    acol = acol_ref[...]
    dcol = dcol_ref[...]
    a_last = acol[c - 1:c, :]
    w_state = jnp.exp2(a_last - acol) * dcol

    per = _LANES // head_dim
    width = per * c
    cb = lax.dot_general(cmx, jnp.concatenate([bmx] * per, axis=0), _NT,
                         preferred_element_type=_F32)
    ri = lax.broadcasted_iota(jnp.int32, (c, width), 0)
    ci = lax.broadcasted_iota(jnp.int32, (c, width), 1) & (c - 1)
    causal = ci <= ri
    lane = lax.broadcasted_iota(jnp.int32, (c, _LANES), 1)
    y_pieces = []
    for p in range(hg // per):
        ac = jnp.concatenate(
            [jnp.broadcast_to(acol[:, p * per + i:p * per + i + 1], (c, c)) for i in range(per)], axis=1)
        ar = arow_ref[:, p * width:(p + 1) * width]
        dr = drow_ref[:, p * width:(p + 1) * width]
        decay = jnp.exp2(jnp.where(causal, ac - ar, _NEG_BIG))
        sc = (cb * decay * dr).astype(_MXU_DTYPE)
        xp = xs[:, p * _LANES:(p + 1) * _LANES]
        bd = jnp.concatenate(
            [jnp.where((lane >= i * head_dim) & (lane < (i + 1) * head_dim), xp, 0.0) for i in range(per)],
            axis=0).astype(_MXU_DTYPE)
        y_pieces.append(jnp.dot(sc, bd, preferred_element_type=_F32))
    y = jnp.concatenate(y_pieces, axis=1)

    st = st_ref[...]
    ea = jnp.exp2(_expand_heads(acol, hg, head_dim))
    y = y + jnp.dot(cmx, st.astype(_MXU_DTYPE), preferred_element_type=_F32) * ea
    xw = (xs * _expand_heads(w_state, hg, head_dim)).astype(_MXU_DTYPE)
    st_ref[...] = st * ea[c - 1:c, :] + lax.dot_general(bmx, xw, _TN, preferred_element_type=_F32)

    y = y + dskip_ref[...] * xs
    y = y * _silu(z_ref[...].astype(_F32))
    ms = jnp.mean(y * y, axis=-1, keepdims=True)
    o_ref[...] = ((y * lax.rsqrt(ms + _RMS_EPS)) * gnw_ref[...]).astype(o_ref.dtype)


def _ssd_scan_kernel(z_ref, x_ref, bm_ref, cm_ref, wx_ref, wb_ref, wc_ref, cbx_ref, cbb_ref, cbc_ref,
                     acol_ref, dcol_ref, arow_ref, drow_ref, dskip_ref, gnw_ref, o_ref,
                     st_ref, tx_ref, tb_ref, tc_ref, *, head_dim):
    first = pl.program_id(2) == 0

    @pl.when(first)
    def _():
        st_ref[...] = jnp.zeros_like(st_ref)

    gb, n, gw = st_ref.shape
    c = z_ref.shape[0]
    hg = acol_ref.shape[2]
    for g in range(gb):
        xs_, ns_, rs_ = pl.ds(g * gw, gw), pl.ds(g * n, n), pl.ds(g * hg * c, hg * c)
        _ssd_scan_group(z_ref.at[:, xs_], x_ref.at[:, xs_], bm_ref.at[:, ns_], cm_ref.at[:, ns_],
                        wx_ref.at[:, xs_], wb_ref.at[:, ns_], wc_ref.at[:, ns_],
                        cbx_ref.at[:, xs_], cbb_ref.at[:, ns_], cbc_ref.at[:, ns_],
                        acol_ref.at[g], dcol_ref.at[g], arow_ref.at[0, :, rs_], drow_ref.at[0, :, rs_],
                        dskip_ref.at[:, xs_], gnw_ref.at[:, xs_], o_ref.at[:, xs_],
                        st_ref.at[g], tx_ref.at[:, xs_], tb_ref.at[:, ns_], tc_ref.at[:, ns_], head_dim, first)


def _ssd_scan(proj, conv_w, conv_b, acol, dcol, arow, drow, dskip, gnorm_w, batch, seq, inner, heads, n_state):
    m = proj.shape[0]
    c = _CHUNK
    g = _SSD_GROUPS
    gb = _SSD_GROUPS_PER_STEP if g % _SSD_GROUPS_PER_STEP == 0 else 1
    hg = heads // g
    p = inner // heads
    gw = hg * p
    sw, sn = gb * gw, gb * n_state
    assert c == _LANES and gw % _LANES == 0 and n_state % _LANES == 0 and _LANES % p == 0
    assert inner % sw == 0 and inner % sn == 0 and (g * n_state) % sn == 0
    nt = seq // c
    x_blk = inner // sw
    b_blk = (2 * inner) // sn
    c_blk = (2 * inner + g * n_state) // sn
    cw_b = inner // sn
    cw_c = (inner + g * n_state) // sn

    def rows(b, t):
        return b * nt + t

    kw = conv_w.shape[0]
    in_specs = [
        pl.BlockSpec((c, sw), lambda b, gi, t: (rows(b, t), gi)),
        pl.BlockSpec((c, sw), lambda b, gi, t: (rows(b, t), x_blk + gi)),
        pl.BlockSpec((c, sn), lambda b, gi, t: (rows(b, t), b_blk + gi)),
        pl.BlockSpec((c, sn), lambda b, gi, t: (rows(b, t), c_blk + gi)),
        pl.BlockSpec((kw, sw), lambda b, gi, t: (0, gi)),
        pl.BlockSpec((kw, sn), lambda b, gi, t: (0, cw_b + gi)),
        pl.BlockSpec((kw, sn), lambda b, gi, t: (0, cw_c + gi)),
        pl.BlockSpec((1, sw), lambda b, gi, t: (0, gi)),
        pl.BlockSpec((1, sn), lambda b, gi, t: (0, cw_b + gi)),
        pl.BlockSpec((1, sn), lambda b, gi, t: (0, cw_c + gi)),
        pl.BlockSpec((gb, c, hg), lambda b, gi, t: (gi, rows(b, t), 0)),
        pl.BlockSpec((gb, c, hg), lambda b, gi, t: (gi, rows(b, t), 0)),
        pl.BlockSpec((1, 1, gb * hg * c), lambda b, gi, t: (rows(b, t), 0, gi)),
        pl.BlockSpec((1, 1, gb * hg * c), lambda b, gi, t: (rows(b, t), 0, gi)),
        pl.BlockSpec((1, sw), lambda b, gi, t: (0, gi)),
        pl.BlockSpec((1, sw), lambda b, gi, t: (0, gi)),
    ]
    return pl.pallas_call(
        functools.partial(_ssd_scan_kernel, head_dim=p),
        grid=(batch, g // gb, nt),
        in_specs=in_specs,
        out_specs=pl.BlockSpec((c, sw), lambda b, gi, t: (rows(b, t), gi)),
        out_shape=jax.ShapeDtypeStruct((m, inner), _MXU_DTYPE),
        scratch_shapes=[pltpu.VMEM((gb, n_state, gw), _F32),
                        pltpu.VMEM((8, sw), _F32),
                        pltpu.VMEM((8, sn), _F32),
                        pltpu.VMEM((8, sn), _F32)],
        compiler_params=_params(("parallel", "parallel", "arbitrary")),
        name="ssd_scan",
    )(proj, proj, proj, proj, conv_w, conv_w, conv_w, conv_b, conv_b, conv_b,
      acol, dcol, arow, drow, dskip, gnorm_w)


def _ssd_mixer(h, u, batch, seq, w_in, conv_w, conv_b, dt_bias, a_log, d_skip, gnorm_w, w_out):
    heads = dt_bias.shape[0]
    conv_dim = conv_w.shape[1]
    inner = w_in.shape[1] - conv_dim - heads
    n_state = (conv_dim - inner) // (2 * _SSD_GROUPS)
    n_main = inner + conv_dim
    m = u.shape[0]
    w_main = w_in[:, :n_main].astype(_MXU_DTYPE)
    w_dt = jnp.pad(w_in[:, n_main:], ((0, 0), (0, _LANES - heads))).astype(_MXU_DTYPE)
    pad1 = lambda v: jnp.pad(v.astype(_F32), (0, _LANES - heads)).reshape(1, _LANES)
    proj = _matmul(u, w_main, out_dtype=_MXU_DTYPE)
    acol, dcol, arow, drow = _ssd_gate(u, w_dt, pad1(dt_bias), pad1(a_log), heads)
    arow = arow.reshape(m // _CHUNK, 1, heads * _CHUNK)
    drow = drow.reshape(m // _CHUNK, 1, heads * _CHUNK)
    dskip = jnp.repeat(d_skip.astype(_F32), inner // heads).reshape(1, inner)
    y = _ssd_scan(proj, conv_w.astype(_F32), conv_b.reshape(1, conv_dim).astype(_F32), acol, dcol, arow, drow,
                  dskip, gnorm_w.reshape(1, inner).astype(_F32), batch, seq, inner, heads, n_state)
    return _matmul(y, w_out.astype(_MXU_DTYPE), res=h, out_dtype=_F32)


def kernel(x, mixer_norm_w, gla_w_in, gla_w_gk_up, gla_b_gk_up, gla_o_norm_w, gla_w_out, ssd_w_in, ssd_conv_w, ssd_conv_b, ssd_dt_bias, ssd_a_log, ssd_d_skip, ssd_gnorm_w, ssd_w_out, mlp_norm_w, mlp_w_fc1, mlp_w_fc2, final_norm_w):
    batch, seq, d = x.shape
    h = x.reshape(batch * seq, d)
    for i in range(mixer_norm_w.shape[0]):
        j = i // 2
        u = _rmsnorm(h, mixer_norm_w[i], _MXU_DTYPE)
        if i % 2 == 0:
            h = _gla_mixer(h, u, batch, seq, gla_w_in[j], gla_w_gk_up[j], gla_b_gk_up[j],
                           gla_o_norm_w[j], gla_w_out[j])
        else:
            h = _ssd_mixer(h, u, batch, seq, ssd_w_in[j], ssd_conv_w[j], ssd_conv_b[j], ssd_dt_bias[j],
                           ssd_a_log[j], ssd_d_skip[j], ssd_gnorm_w[j], ssd_w_out[j])
        u = _rmsnorm(h, mlp_norm_w[i], _MXU_DTYPE)
        hidden = _matmul(u, mlp_w_fc1[i].astype(_MXU_DTYPE), act="relu2", out_dtype=_MXU_DTYPE)
        h = _matmul(hidden, mlp_w_fc2[i].astype(_MXU_DTYPE), res=h, out_dtype=_F32)
    return _rmsnorm(h, final_norm_w, _F32).reshape(batch, seq, d)
```

```python
import functools
import math

import jax
import jax.numpy as jnp
from jax import lax
from jax.experimental import pallas as pl
from jax.experimental.pallas import tpu as pltpu

_F32 = jnp.float32
_MXU_DTYPE = jnp.bfloat16
_RMS_EPS = 1e-5
_GLA_GATE_NORMALIZER = 16.0
_SSD_GROUPS = 8
_CHUNK = 128
_LANES = 128
_GLA_DIAG = 8
_GLA_HEADS_PER_STEP = 2
_SSD_GROUPS_PER_STEP = 2
_LOG2E = math.log2(math.e)
_NEG_BIG = -1e30
_VMEM_LIMIT_BYTES = 56 * 1024 * 1024
_MATMUL_VMEM_BUDGET = 50 * 1024 * 1024

_NT = (((1,), (1,)), ((), ()))
_TN = (((0,), (0,)), ((), ()))


def _params(sem):
    return pltpu.CompilerParams(dimension_semantics=sem, vmem_limit_bytes=_VMEM_LIMIT_BYTES)


def _rms(x, w):
    ms = jnp.mean(x * x, axis=-1, keepdims=True)
    return (x * lax.rsqrt(ms + _RMS_EPS)) * w


def _rmsnorm_kernel(x_ref, w_ref, o_ref):
    o_ref[...] = _rms(x_ref[...].astype(_F32), w_ref[...]).astype(o_ref.dtype)


def _rmsnorm(x, w, out_dtype, rows=512):
    m, d = x.shape
    rows = min(rows, m)
    return pl.pallas_call(
        _rmsnorm_kernel,
        grid=(m // rows,),
        in_specs=[pl.BlockSpec((rows, d), lambda i: (i, 0)),
                  pl.BlockSpec((1, d), lambda i: (0, 0))],
        out_specs=pl.BlockSpec((rows, d), lambda i: (i, 0)),
        out_shape=jax.ShapeDtypeStruct((m, d), out_dtype),
        compiler_params=_params(("parallel",)),
        name="rmsnorm",
    )(x, w.reshape(1, d).astype(_F32))


def _matmul_kernel(*refs, act, has_norm, has_res):
    it = iter(refs)
    a_ref = next(it)
    nw_ref = next(it) if has_norm else None
    w_ref = next(it)
    r_ref = next(it) if has_res else None
    o_ref = next(it)
    if has_norm:
        an_ref = next(it)

        @pl.when(pl.program_id(1) == 0)
        def _():
            an_ref[...] = _rms(a_ref[...], nw_ref[...]).astype(_MXU_DTYPE)

        a = an_ref[...]
    else:
        a = a_ref[...]
    acc = jnp.dot(a, w_ref[...].astype(_MXU_DTYPE), preferred_element_type=_F32)
    if act == "relu2":
        acc = jnp.square(jnp.maximum(acc, 0.0))
    if has_res:
        acc = r_ref[...] + acc
    o_ref[...] = acc.astype(o_ref.dtype)


def _divisor_tile(n, pref, align):
    if n <= pref:
        return n
    t = (pref // align) * align
    while n % t:
        t -= align
    return t


def _matmul_tiles(m, k, n, a_bytes, w_bytes, out_bytes, has_norm, has_res):
    mxu_bytes = jnp.dtype(_MXU_DTYPE).itemsize
    for pm, pn in ((1024, 1024), (1024, 512), (512, 512), (512, 256), (256, 256), (128, 128)):
        tm, tn = _divisor_tile(m, pm, 8), _divisor_tile(n, pn, _LANES)
        need = 2 * tm * k * a_bytes + 2 * k * tn * w_bytes + 2 * tm * tn * out_bytes + tm * tn * 4
        need += 2 * tm * tn * 4 if has_res else 0
        need += tm * k * mxu_bytes if has_norm else 0
        need += k * tn * mxu_bytes if w_bytes != mxu_bytes else 0
        if need <= _MATMUL_VMEM_BUDGET:
            break
    return tm, tn


def _matmul(a, w, *, n=None, norm_w=None, act=None, res=None, out_dtype):
    m, k = a.shape
    n = w.shape[1] if n is None else n
    tm, tn = _matmul_tiles(m, k, n, a.dtype.itemsize, w.dtype.itemsize, jnp.dtype(out_dtype).itemsize,
                           norm_w is not None, res is not None)
    in_specs = [pl.BlockSpec((tm, k), lambda i, j: (i, 0))]
    args = [a]
    if norm_w is not None:
        in_specs.append(pl.BlockSpec((1, k), lambda i, j: (0, 0)))
        args.append(norm_w.reshape(1, k).astype(_F32))
    in_specs.append(pl.BlockSpec((k, tn), lambda i, j: (0, j)))
    args.append(w)
    if res is not None:
        in_specs.append(pl.BlockSpec((tm, tn), lambda i, j: (i, j)))
        args.append(res)
    return pl.pallas_call(
        functools.partial(_matmul_kernel, act=act, has_norm=norm_w is not None, has_res=res is not None),
        grid=(m // tm, n // tn),
        in_specs=in_specs,
        out_specs=pl.BlockSpec((tm, tn), lambda i, j: (i, j)),
        out_shape=jax.ShapeDtypeStruct((m, n), out_dtype),
        scratch_shapes=[pltpu.VMEM((tm, k), _MXU_DTYPE)] if norm_w is not None else [],
        compiler_params=_params(("parallel", "arbitrary")),
        name="matmul",
    )(*args)


def _softplus(x):
    return jnp.maximum(x, 0.0) + jnp.log1p(jnp.exp(-jnp.abs(x)))


def _chunk_cumsum(x, chunk):
    rows = x.shape[0]
    r = lax.broadcasted_iota(jnp.int32, (chunk, chunk), 0)
    c = lax.broadcasted_iota(jnp.int32, (chunk, chunk), 1)
    tri = jnp.where(c <= r, 1.0, 0.0).astype(_MXU_DTYPE)
    out = []
    for i in range(rows // chunk):
        rem = x[i * chunk:(i + 1) * chunk, :]
        acc = None
        for _ in range(3):
            piece = rem.astype(_MXU_DTYPE)
            d = jnp.dot(tri, piece, preferred_element_type=_F32)
            acc = d if acc is None else acc + d
            rem = rem - piece.astype(_F32)
        out.append(acc)
    return jnp.concatenate(out, axis=0) if len(out) > 1 else out[0]


def _row_bcast(ref, row, n):
    return jnp.broadcast_to(ref[pl.ds(row, 1), :], (n, ref.shape[1]))


def _gla_gate_kernel(h_ref, nw_ref, wgr_ref, wup_ref, b_ref, o_ref, *, chunk):
    u = _rms(h_ref[...], nw_ref[...]).astype(_MXU_DTYPE)
    gr = jnp.dot(u, wgr_ref[...], preferred_element_type=_F32)
    pre = jnp.dot(gr.astype(_MXU_DTYPE), wup_ref[...], preferred_element_type=_F32) + b_ref[...]
    gk = -_softplus(-pre) / _GLA_GATE_NORMALIZER
    o_ref[...] = _chunk_cumsum(gk * _LOG2E, chunk)


def _gla_gate(h, norm_w, w_gr, w_up, b_up, rows=512):
    m, d = h.shape
    kd = w_up.shape[1]
    rows = min(rows, m)
    return pl.pallas_call(
        functools.partial(_gla_gate_kernel, chunk=_CHUNK),
        grid=(m // rows,),
        in_specs=[pl.BlockSpec((rows, d), lambda i: (i, 0)),
                  pl.BlockSpec((1, d), lambda i: (0, 0)),
                  pl.BlockSpec(w_gr.shape, lambda i: (0, 0)),
                  pl.BlockSpec(w_up.shape, lambda i: (0, 0)),
                  pl.BlockSpec((1, kd), lambda i: (0, 0))],
        out_specs=pl.BlockSpec((rows, kd), lambda i: (i, 0)),
        out_shape=jax.ShapeDtypeStruct((m, kd), _F32),
        compiler_params=_params(("parallel",)),
        name="gla_gate",
    )(h, norm_w.reshape(1, d).astype(_F32), w_gr, w_up, b_up)


def _gla_levels(c, nd):
    i = lax.broadcasted_iota(jnp.int32, (c, c), 0)
    j = lax.broadcasted_iota(jnp.int32, (c, c), 1)
    x = i ^ j
    lvl = jnp.zeros((c, c), jnp.int32)
    s, level = nd, 1
    while s < c:
        lvl = jnp.where(x >= s, level, lvl)
        s, level = 2 * s, level + 1
    return jnp.where(j > i, -1, lvl)


def _gla_scan_head(q_ref, k_ref, v_ref, g_ref, b_ref, onw_ref, lvl_ref, o_ref, st_ref, qf_ref, kf_ref, scale):
    c, dk = q_ref.shape
    qf_ref[...] = q_ref[...].astype(_F32) * scale
    kf_ref[...] = k_ref[...].astype(_F32)
    q = qf_ref[...]
    k = kf_ref[...]
    v = v_ref[...]
    bc = b_ref[...]
    st = st_ref[...]
    b_last = jnp.concatenate([_row_bcast(b_ref, c - 1, 8)] * (c // 8), axis=0)

    o = lax.dot_general((q * jnp.exp2(bc)).astype(_MXU_DTYPE), st.astype(_MXU_DTYPE), _NT,
                        preferred_element_type=_F32)
    k_dec = (k * jnp.exp2(b_last - bc)).astype(_MXU_DTYPE)

    nd = _GLA_DIAG
    lane = lax.broadcasted_iota(jnp.int32, (nd, c), 1)
    pieces = []
    for m in range(c // nd):
        qb = qf_ref[m * nd:(m + 1) * nd, :]
        bb = b_ref[m * nd:(m + 1) * nd, :]
        acc = jnp.zeros((nd, c), _F32)
        for j in range(nd):
            kj = _row_bcast(kf_ref, m * nd + j, nd)
            bj = _row_bcast(b_ref, m * nd + j, nd)
            term = qb * kj * jnp.exp2(jnp.minimum(bb - bj, 0.0))
            acc = jnp.where(lane == m * nd + j, jnp.sum(term, axis=1, keepdims=True), acc)
        pieces.append(acc)
    lvl = lvl_ref[...]
    att = jnp.where(lvl == 0, jnp.concatenate(pieces, axis=0), 0.0)

    s, level = nd, 1
    while s < c:
        prev = [jnp.zeros((s, dk), _F32)]
        end = []
        for m in range(c // s):
            if m:
                prev += [_row_bcast(b_ref, m * s - 1, 8)] * (s // 8)
            end += [_row_bcast(b_ref, (m + 1) * s - 1, 8)] * (s // 8)
        qs = (q * jnp.exp2(bc - jnp.concatenate(prev, axis=0))).astype(_MXU_DTYPE)
        ks = (k * jnp.exp2(jnp.concatenate(end, axis=0) - bc)).astype(_MXU_DTYPE)
        att = jnp.where(lvl == level, lax.dot_general(qs, ks, _NT, preferred_element_type=_F32), att)
        s, level = 2 * s, level + 1

    o = o + jnp.dot(att.astype(_MXU_DTYPE), v, preferred_element_type=_F32)
    st_ref[...] = st * jnp.exp2(b_last[:1, :]) + lax.dot_general(
        v, k_dec, _TN, preferred_element_type=_F32)

    ms = jnp.mean(o * o, axis=-1, keepdims=True)
    on = (o * lax.rsqrt(ms + _RMS_EPS)) * onw_ref[...]
    g = g_ref[...].astype(_F32)
    o_ref[...] = (on * (g * jax.nn.sigmoid(g))).astype(o_ref.dtype)


def _gla_scan_kernel(q_ref, k_ref, v_ref, g_ref, b_ref, onw_ref, lvl_ref, o_ref, st_ref, qf_ref, kf_ref,
                     *, scale, dk, dv):
    @pl.when(pl.program_id(2) == 0)
    def _():
        st_ref[...] = jnp.zeros_like(st_ref)

    for h in range(st_ref.shape[0]):
        ks, vs = pl.ds(h * dk, dk), pl.ds(h * dv, dv)
        _gla_scan_head(q_ref.at[:, ks], k_ref.at[:, ks], v_ref.at[:, vs], g_ref.at[:, vs], b_ref.at[:, ks],
                       onw_ref, lvl_ref, o_ref.at[:, vs], st_ref.at[h], qf_ref.at[h], kf_ref.at[h], scale)


def _gla_scan(proj, bcum, o_norm_w, batch, seq, heads, dk, dv):
    m = proj.shape[0]
    c = _CHUNK
    hb = _GLA_HEADS_PER_STEP if heads % _GLA_HEADS_PER_STEP == 0 else 1
    assert c == _LANES and dk % _LANES == 0 and dv % _LANES == 0 and (2 * heads * dk) % (hb * dv) == 0
    nt = seq // c
    kd, vd = heads * dk, heads * dv
    wk, wv = hb * dk, hb * dv
    k_blk, v_blk, g_blk = kd // wk, (2 * kd) // wv, (2 * kd + vd) // wv

    def rows(b, h, t):
        return b * nt + t

    return pl.pallas_call(
        functools.partial(_gla_scan_kernel, scale=dk ** -0.5, dk=dk, dv=dv),
        grid=(batch, heads // hb, nt),
        in_specs=[pl.BlockSpec((c, wk), lambda b, h, t: (rows(b, h, t), h)),
                  pl.BlockSpec((c, wk), lambda b, h, t: (rows(b, h, t), k_blk + h)),
                  pl.BlockSpec((c, wv), lambda b, h, t: (rows(b, h, t), v_blk + h)),
                  pl.BlockSpec((c, wv), lambda b, h, t: (rows(b, h, t), g_blk + h)),
                  pl.BlockSpec((c, wk), lambda b, h, t: (rows(b, h, t), h)),
                  pl.BlockSpec((1, dv), lambda b, h, t: (0, 0)),
                  pl.BlockSpec((c, c), lambda b, h, t: (0, 0))],
        out_specs=pl.BlockSpec((c, wv), lambda b, h, t: (rows(b, h, t), h)),
        out_shape=jax.ShapeDtypeStruct((m, vd), _MXU_DTYPE),
        scratch_shapes=[pltpu.VMEM((hb, dv, dk), _F32),
                        pltpu.VMEM((hb, c, dk), _F32),
                        pltpu.VMEM((hb, c, dk), _F32)],
        compiler_params=_params(("parallel", "parallel", "arbitrary")),
        name="gla_scan",
    )(proj, proj, proj, proj, bcum, o_norm_w.reshape(1, dv).astype(_F32), _gla_levels(c, _GLA_DIAG))


def _gla_mixer(h, norm_w, batch, seq, w_in, w_gk_up, b_gk_up, o_norm_w, w_out):
    rank, kd = w_gk_up.shape
    vd = (w_in.shape[1] - rank - 2 * kd) // 2
    dv = o_norm_w.shape[0]
    heads = vd // dv
    n_main = 2 * kd + 2 * vd
    w_gr = jnp.pad(w_in[:, n_main:], ((0, 0), (0, _LANES - rank))).astype(_MXU_DTYPE)
    w_up = jnp.pad(w_gk_up, ((0, _LANES - rank), (0, 0))).astype(_MXU_DTYPE)
    proj = _matmul(h, w_in, n=n_main, norm_w=norm_w, out_dtype=_MXU_DTYPE)
    bcum = _gla_gate(h, norm_w, w_gr, w_up, b_gk_up.reshape(1, kd).astype(_F32))
    o = _gla_scan(proj, bcum, o_norm_w, batch, seq, heads, kd // heads, dv)
    return _matmul(o, w_out, res=h, out_dtype=_F32)


def _ssd_gate_kernel(h_ref, nw_ref, wdt_ref, bias_ref, alog_ref, acol_ref, dcol_ref, arow_ref, drow_ref,
                     *, chunk, heads, groups):
    u = _rms(h_ref[...], nw_ref[...]).astype(_MXU_DTYPE)
    raw = jnp.dot(u, wdt_ref[...], preferred_element_type=_F32)
    dt = _softplus(raw + bias_ref[...])
    a = _chunk_cumsum(dt * (-jnp.exp(alog_ref[...])) * _LOG2E, chunk)
    hg = heads // groups
    for g in range(groups):
        acol_ref[g] = a[:, g * hg:(g + 1) * hg]
        dcol_ref[g] = dt[:, g * hg:(g + 1) * hg]
    for i in range(a.shape[0] // chunk):
        arow_ref[i] = a[i * chunk:(i + 1) * chunk, :].T[:heads, :]
        drow_ref[i] = dt[i * chunk:(i + 1) * chunk, :].T[:heads, :]


def _ssd_gate(h, norm_w, w_dt, dt_bias, a_log, heads, rows=512):
    m, d = h.shape
    c = _CHUNK
    rows = min(rows, m)
    g = _SSD_GROUPS
    hg = heads // g
    col = jax.ShapeDtypeStruct((g, m, hg), _F32)
    row = jax.ShapeDtypeStruct((m // c, heads, c), _F32)
    col_spec = pl.BlockSpec((g, rows, hg), lambda i: (0, i, 0))
    row_spec = pl.BlockSpec((rows // c, heads, c), lambda i: (i, 0, 0))
    return pl.pallas_call(
        functools.partial(_ssd_gate_kernel, chunk=c, heads=heads, groups=g),
        grid=(m // rows,),
        in_specs=[pl.BlockSpec((rows, d), lambda i: (i, 0)),
                  pl.BlockSpec((1, d), lambda i: (0, 0)),
                  pl.BlockSpec(w_dt.shape, lambda i: (0, 0)),
                  pl.BlockSpec((1, _LANES), lambda i: (0, 0)),
                  pl.BlockSpec((1, _LANES), lambda i: (0, 0))],
        out_specs=[col_spec, col_spec, row_spec, row_spec],
        out_shape=[col, col, row, row],
        compiler_params=_params(("parallel",)),
        name="ssd_gate",
    )(h, norm_w.reshape(1, d).astype(_F32), w_dt, dt_bias, a_log)


def _silu(x):
    return x * jax.nn.sigmoid(x)


def _causal_conv(tail_ref, raw_ref, w_ref, b_ref, first):
    c = raw_ref.shape[0]
    kw = w_ref.shape[0]

    @pl.when(first)
    def _():
        tail_ref[...] = jnp.zeros_like(tail_ref)

    cur = raw_ref[...].astype(_F32)
    ext = jnp.concatenate([tail_ref[...], cur], axis=0)
    out = b_ref[...] + w_ref[kw - 1:kw, :] * cur
    for s in range(1, kw):
        out = out + w_ref[kw - 1 - s:kw - s, :] * pltpu.roll(ext, s, axis=0)[8:, :]
    tail_ref[...] = cur[c - 8:, :]
    return out


def _expand_heads(colarr, n_heads, head_dim):
    c = colarr.shape[0]
    per = _LANES // head_dim
    lane = lax.broadcasted_iota(jnp.int32, (c, _LANES), 1)
    pieces = []
    for p in range(n_heads // per):
        out = jnp.broadcast_to(colarr[:, p * per:p * per + 1], (c, _LANES))
        for i in range(1, per):
            nxt = jnp.broadcast_to(colarr[:, p * per + i:p * per + i + 1], (c, _LANES))
            out = jnp.where(lane >= i * head_dim, nxt, out)
        pieces.append(out)
    return jnp.concatenate(pieces, axis=1)


def _ssd_scan_group(z_ref, x_ref, bm_ref, cm_ref, wx_ref, wb_ref, wc_ref, cbx_ref, cbb_ref, cbc_ref,
                    acol_ref, dcol_ref, arow_ref, drow_ref, dskip_ref, gnw_ref, o_ref,
                    st_ref, tx_ref, tb_ref, tc_ref, head_dim, first):
    c = z_ref.shape[0]
    hg = acol_ref.shape[1]
    xs = _silu(_causal_conv(tx_ref, x_ref, wx_ref, cbx_ref, first))
    bm = _silu(_causal_conv(tb_ref, bm_ref, wb_ref, cbb_ref, first))
    cm = _silu(_causal_conv(tc_ref, cm_ref, wc_ref, cbc_ref, first))
    bmx = bm.astype(_MXU_DTYPE)
    cmx = cm.astype(_MXU_DTYPE)

    acol = acol_ref[...]
    dcol = dcol_ref[...]
    a_last = acol[c - 1:c, :]
    w_state = jnp.exp2(a_last - acol) * dcol

    per = _LANES // head_dim
    width = per * c
    cb = lax.dot_general(cmx, jnp.concatenate([bmx] * per, axis=0), _NT,
                         preferred_element_type=_F32)
    ri = lax.broadcasted_iota(jnp.int32, (c, width), 0)
    ci = lax.broadcasted_iota(jnp.int32, (c, width), 1) & (c - 1)
    causal = ci <= ri
    lane = lax.broadcasted_iota(jnp.int32, (c, _LANES), 1)
    y_pieces = []
    for p in range(hg // per):
        ac = jnp.concatenate(
            [jnp.broadcast_to(acol[:, p * per + i:p * per + i + 1], (c, c)) for i in range(per)], axis=1)
        ar = arow_ref[:, p * width:(p + 1) * width]
        dr = drow_ref[:, p * width:(p + 1) * width]
        decay = jnp.exp2(jnp.where(causal, ac - ar, _NEG_BIG))
        sc = (cb * decay * dr).astype(_MXU_DTYPE)
        xp = xs[:, p * _LANES:(p + 1) * _LANES]
        bd = jnp.concatenate(
            [jnp.where((lane >= i * head_dim) & (lane < (i + 1) * head_dim), xp, 0.0) for i in range(per)],
            axis=0).astype(_MXU_DTYPE)
        y_pieces.append(jnp.dot(sc, bd, preferred_element_type=_F32))
    y = jnp.concatenate(y_pieces, axis=1)

    st = st_ref[...]
    ea = jnp.exp2(_expand_heads(acol, hg, head_dim))
    y = y + jnp.dot(cmx, st.astype(_MXU_DTYPE), preferred_element_type=_F32) * ea
    xw = (xs * _expand_heads(w_state, hg, head_dim)).astype(_MXU_DTYPE)
    st_ref[...] = st * ea[c - 1:c, :] + lax.dot_general(bmx, xw, _TN, preferred_element_type=_F32)

    y = y + dskip_ref[...] * xs
    y = y * _silu(z_ref[...].astype(_F32))
    ms = jnp.mean(y * y, axis=-1, keepdims=True)
    o_ref[...] = ((y * lax.rsqrt(ms + _RMS_EPS)) * gnw_ref[...]).astype(o_ref.dtype)


def _ssd_scan_kernel(z_ref, x_ref, bm_ref, cm_ref, wx_ref, wb_ref, wc_ref, cbx_ref, cbb_ref, cbc_ref,
                     acol_ref, dcol_ref, arow_ref, drow_ref, dskip_ref, gnw_ref, o_ref,
                     st_ref, tx_ref, tb_ref, tc_ref, *, head_dim):
    first = pl.program_id(2) == 0

    @pl.when(first)
    def _():
        st_ref[...] = jnp.zeros_like(st_ref)

    gb, n, gw = st_ref.shape
    c = z_ref.shape[0]
    hg = acol_ref.shape[2]
    for g in range(gb):
        xs_, ns_, rs_ = pl.ds(g * gw, gw), pl.ds(g * n, n), pl.ds(g * hg * c, hg * c)
        _ssd_scan_group(z_ref.at[:, xs_], x_ref.at[:, xs_], bm_ref.at[:, ns_], cm_ref.at[:, ns_],
                        wx_ref.at[:, xs_], wb_ref.at[:, ns_], wc_ref.at[:, ns_],
                        cbx_ref.at[:, xs_], cbb_ref.at[:, ns_], cbc_ref.at[:, ns_],
                        acol_ref.at[g], dcol_ref.at[g], arow_ref.at[0, :, rs_], drow_ref.at[0, :, rs_],
                        dskip_ref.at[:, xs_], gnw_ref.at[:, xs_], o_ref.at[:, xs_],
                        st_ref.at[g], tx_ref.at[:, xs_], tb_ref.at[:, ns_], tc_ref.at[:, ns_], head_dim, first)


def _ssd_scan(proj, conv_w, conv_b, acol, dcol, arow, drow, dskip, gnorm_w, batch, seq, inner, heads, n_state):
    m = proj.shape[0]
    c = _CHUNK
    g = _SSD_GROUPS
    gb = _SSD_GROUPS_PER_STEP if g % _SSD_GROUPS_PER_STEP == 0 else 1
    hg = heads // g
    p = inner // heads
    gw = hg * p
    sw, sn = gb * gw, gb * n_state
    assert c == _LANES and gw % _LANES == 0 and n_state % _LANES == 0 and _LANES % p == 0
    assert inner % sw == 0 and inner % sn == 0 and (g * n_state) % sn == 0
    nt = seq // c
    x_blk = inner // sw
    b_blk = (2 * inner) // sn
    c_blk = (2 * inner + g * n_state) // sn
    cw_b = inner // sn
    cw_c = (inner + g * n_state) // sn

    def rows(b, t):
        return b * nt + t

    kw = conv_w.shape[0]
    in_specs = [
        pl.BlockSpec((c, sw), lambda b, gi, t: (rows(b, t), gi)),
        pl.BlockSpec((c, sw), lambda b, gi, t: (rows(b, t), x_blk + gi)),
        pl.BlockSpec((c, sn), lambda b, gi, t: (rows(b, t), b_blk + gi)),
        pl.BlockSpec((c, sn), lambda b, gi, t: (rows(b, t), c_blk + gi)),
        pl.BlockSpec((kw, sw), lambda b, gi, t: (0, gi)),
        pl.BlockSpec((kw, sn), lambda b, gi, t: (0, cw_b + gi)),
        pl.BlockSpec((kw, sn), lambda b, gi, t: (0, cw_c + gi)),
        pl.BlockSpec((1, sw), lambda b, gi, t: (0, gi)),
        pl.BlockSpec((1, sn), lambda b, gi, t: (0, cw_b + gi)),
        pl.BlockSpec((1, sn), lambda b, gi, t: (0, cw_c + gi)),
        pl.BlockSpec((gb, c, hg), lambda b, gi, t: (gi, rows(b, t), 0)),
        pl.BlockSpec((gb, c, hg), lambda b, gi, t: (gi, rows(b, t), 0)),
        pl.BlockSpec((1, 1, gb * hg * c), lambda b, gi, t: (rows(b, t), 0, gi)),
        pl.BlockSpec((1, 1, gb * hg * c), lambda b, gi, t: (rows(b, t), 0, gi)),
        pl.BlockSpec((1, sw), lambda b, gi, t: (0, gi)),
        pl.BlockSpec((1, sw), lambda b, gi, t: (0, gi)),
    ]
    return pl.pallas_call(
        functools.partial(_ssd_scan_kernel, head_dim=p),
        grid=(batch, g // gb, nt),
        in_specs=in_specs,
        out_specs=pl.BlockSpec((c, sw), lambda b, gi, t: (rows(b, t), gi)),
        out_shape=jax.ShapeDtypeStruct((m, inner), _MXU_DTYPE),
        scratch_shapes=[pltpu.VMEM((gb, n_state, gw), _F32),
                        pltpu.VMEM((8, sw), _F32),
                        pltpu.VMEM((8, sn), _F32),
                        pltpu.VMEM((8, sn), _F32)],
        compiler_params=_params(("parallel", "parallel", "arbitrary")),
        name="ssd_scan",
    )(proj, proj, proj, proj, conv_w, conv_w, conv_w, conv_b, conv_b, conv_b,
      acol, dcol, arow, drow, dskip, gnorm_w)


def _ssd_mixer(h, norm_w, batch, seq, w_in, conv_w, conv_b, dt_bias, a_log, d_skip, gnorm_w, w_out):
    heads = dt_bias.shape[0]
    conv_dim = conv_w.shape[1]
    inner = w_in.shape[1] - conv_dim - heads
    n_state = (conv_dim - inner) // (2 * _SSD_GROUPS)
    n_main = inner + conv_dim
    m = h.shape[0]
    w_dt = jnp.pad(w_in[:, n_main:], ((0, 0), (0, _LANES - heads))).astype(_MXU_DTYPE)
    pad1 = lambda v: jnp.pad(v.astype(_F32), (0, _LANES - heads)).reshape(1, _LANES)
    proj = _matmul(h, w_in, n=n_main, norm_w=norm_w, out_dtype=_MXU_DTYPE)
    acol, dcol, arow, drow = _ssd_gate(h, norm_w, w_dt, pad1(dt_bias), pad1(a_log), heads)
    arow = arow.reshape(m // _CHUNK, 1, heads * _CHUNK)
    drow = drow.reshape(m // _CHUNK, 1, heads * _CHUNK)
    dskip = jnp.repeat(d_skip.astype(_F32), inner // heads).reshape(1, inner)
    y = _ssd_scan(proj, conv_w.astype(_F32), conv_b.reshape(1, conv_dim).astype(_F32), acol, dcol, arow, drow,
                  dskip, gnorm_w.reshape(1, inner).astype(_F32), batch, seq, inner, heads, n_state)
    return _matmul(y, w_out, res=h, out_dtype=_F32)


def kernel(x, mixer_norm_w, gla_w_in, gla_w_gk_up, gla_b_gk_up, gla_o_norm_w, gla_w_out, ssd_w_in, ssd_conv_w, ssd_conv_b, ssd_dt_bias, ssd_a_log, ssd_d_skip, ssd_gnorm_w, ssd_w_out, mlp_norm_w, mlp_w_fc1, mlp_w_fc2, final_norm_w):
    batch, seq, d = x.shape
    h = x.reshape(batch * seq, d)
    for i in range(mixer_norm_w.shape[0]):
        j = i // 2
        if i % 2 == 0:
            h = _gla_mixer(h, mixer_norm_w[i], batch, seq, gla_w_in[j], gla_w_gk_up[j], gla_b_gk_up[j],
                           gla_o_norm_w[j], gla_w_out[j])
        else:
            h = _ssd_mixer(h, mixer_norm_w[i], batch, seq, ssd_w_in[j], ssd_conv_w[j], ssd_conv_b[j],
                           ssd_dt_bias[j], ssd_a_log[j], ssd_d_skip[j], ssd_gnorm_w[j], ssd_w_out[j])
        hidden = _matmul(h, mlp_w_fc1[i], norm_w=mlp_norm_w[i], act="relu2", out_dtype=_MXU_DTYPE)
        h = _matmul(hidden, mlp_w_fc2[i].astype(_MXU_DTYPE), res=h, out_dtype=_F32)
    return _rmsnorm(h, final_norm_w, _F32).reshape(batch, seq, d)
```

```python
import functools
import math

import jax
import jax.numpy as jnp
from jax import lax
from jax.experimental import pallas as pl
from jax.experimental.pallas import tpu as pltpu

_F32 = jnp.float32
_MXU_DTYPE = jnp.bfloat16
_RMS_EPS = 1e-5
_GLA_GATE_NORMALIZER = 16.0
_SSD_GROUPS = 8
_CHUNK = 128
_LANES = 128
_GLA_DIAG = 8
_GLA_HEADS_PER_STEP = 2
_SSD_GROUPS_PER_STEP = 2
_LOG2E = math.log2(math.e)
_NEG_BIG = -1e30
_HALO = 16
_MXU_WIDTH = 256
_VMEM_LIMIT_BYTES = 56 * 1024 * 1024
_MATMUL_VMEM_BUDGET = 50 * 1024 * 1024

_NT = (((1,), (1,)), ((), ()))
_TN = (((0,), (0,)), ((), ()))


def _params(sem, flags=None):
    return pltpu.CompilerParams(dimension_semantics=sem, vmem_limit_bytes=_VMEM_LIMIT_BYTES, flags=flags)


def _rms(x, w):
    ms = jnp.mean(x * x, axis=-1, keepdims=True)
    return (x * lax.rsqrt(ms + _RMS_EPS)) * w


def _rmsnorm_kernel(x_ref, w_ref, o_ref):
    o_ref[...] = _rms(x_ref[...].astype(_F32), w_ref[...]).astype(o_ref.dtype)


def _rmsnorm(x, w, out_dtype, rows=512):
    m, d = x.shape
    rows = min(rows, m)
    return pl.pallas_call(
        _rmsnorm_kernel,
        grid=(m // rows,),
        in_specs=[pl.BlockSpec((rows, d), lambda i: (i, 0)),
                  pl.BlockSpec((1, d), lambda i: (0, 0))],
        out_specs=pl.BlockSpec((rows, d), lambda i: (i, 0)),
        out_shape=jax.ShapeDtypeStruct((m, d), out_dtype),
        compiler_params=_params(("parallel",)),
        name="rmsnorm",
    )(x, w.reshape(1, d).astype(_F32))


def _silu(x):
    half = 0.5 * x
    return half + half * jnp.tanh(half)


def _matmul_kernel(*refs, act, has_norm, has_res):
    it = iter(refs)
    a_ref = next(it)
    nw_ref = next(it) if has_norm else None
    w_ref = next(it)
    r_ref = next(it) if has_res else None
    o_ref = next(it)
    if has_norm:
        an_ref = next(it)

        @pl.when(pl.program_id(1) == 0)
        def _():
            an_ref[...] = _rms(a_ref[...], nw_ref[...]).astype(_MXU_DTYPE)

        a = an_ref[...]
    else:
        a = a_ref[...]
    acc = jnp.dot(a, w_ref[...].astype(_MXU_DTYPE), preferred_element_type=_F32)
    if act == "relu2":
        acc = jnp.square(jnp.maximum(acc, 0.0))
    elif act == "silu":
        acc = _silu(acc)
    if has_res:
        acc = r_ref[...] + acc
    o_ref[...] = acc.astype(o_ref.dtype)


def _divisor_tile(n, pref, align):
    if n <= pref:
        return n
    t = (pref // align) * align
    while n % t:
        t -= align
    return t


def _matmul_tiles(m, k, n, a_bytes, w_bytes, out_bytes, has_norm, has_res, conv_taps=0, row_cap=None):
    mxu_bytes = jnp.dtype(_MXU_DTYPE).itemsize
    for pm, pn in ((1024, 1024), (1024, 512), (512, 512), (512, 256), (256, 256), (128, 128)):
        pm = pm if row_cap is None else min(pm, row_cap)
        tm, tn = _divisor_tile(m, pm, 8), _divisor_tile(n, pn, _LANES)
        need = 2 * tm * k * a_bytes + 2 * k * tn * w_bytes + 2 * tm * tn * out_bytes
        need += tm * tn * 4 if not conv_taps else (1 + conv_taps) * tm * _MXU_WIDTH * 4
        need += 2 * tm * tn * 4 if has_res else 0
        need += tm * k * mxu_bytes if has_norm else 0
        need += k * tn * mxu_bytes if w_bytes != mxu_bytes else 0
        if need <= _MATMUL_VMEM_BUDGET:
            break
    return tm, tn


def _matmul(a, w, *, col0=0, n=None, norm_w=None, act=None, res=None, out_dtype):
    m, k = a.shape
    n = w.shape[1] - col0 if n is None else n
    tm, tn = _matmul_tiles(m, k, math.gcd(n, col0), a.dtype.itemsize, w.dtype.itemsize,
                           jnp.dtype(out_dtype).itemsize, norm_w is not None, res is not None)
    in_specs = [pl.BlockSpec((tm, k), lambda i, j: (i, 0))]
    args = [a]
    if norm_w is not None:
        in_specs.append(pl.BlockSpec((1, k), lambda i, j: (0, 0)))
        args.append(norm_w.reshape(1, k).astype(_F32))
    in_specs.append(pl.BlockSpec((k, tn), lambda i, j: (0, col0 // tn + j)))
    args.append(w)
    if res is not None:
        in_specs.append(pl.BlockSpec((tm, tn), lambda i, j: (i, j)))
        args.append(res)
    return pl.pallas_call(
        functools.partial(_matmul_kernel, act=act, has_norm=norm_w is not None, has_res=res is not None),
        grid=(m // tm, n // tn),
        in_specs=in_specs,
        out_specs=pl.BlockSpec((tm, tn), lambda i, j: (i, j)),
        out_shape=jax.ShapeDtypeStruct((m, n), out_dtype),
        scratch_shapes=[pltpu.VMEM((tm, k), _MXU_DTYPE)] if norm_w is not None else [],
        compiler_params=_params(("parallel", "arbitrary")),
        name="matmul",
    )(*args)


def _proj_conv_kernel(a_ref, halo_ref, nw_ref, w_ref, cw_ref, cb_ref, o_ref, an_ref, *, tiles_per_seq):
    kw = cw_ref.shape[0]

    @pl.when(pl.program_id(1) == 0)
    def _():
        keep = jnp.where(pl.program_id(0) % tiles_per_seq == 0, 0.0, 1.0)
        an_ref[0:_HALO, :] = (_rms(halo_ref[...], nw_ref[...]) * keep).astype(_MXU_DTYPE)
        an_ref[_HALO:, :] = _rms(a_ref[...], nw_ref[...]).astype(_MXU_DTYPE)

    wb = w_ref[...].astype(_MXU_DTYPE)
    for c0 in range(0, o_ref.shape[1], _MXU_WIDTH):
        cols = pl.ds(c0, _MXU_WIDTH)
        acc = jnp.dot(an_ref[...], wb[:, c0:c0 + _MXU_WIDTH], preferred_element_type=_F32)
        out = cb_ref[:, cols] + cw_ref[kw - 1:kw, cols] * acc[_HALO:, :]
        for s in range(1, kw):
            out = out + cw_ref[kw - 1 - s:kw - s, cols] * pltpu.roll(acc, s, axis=0)[_HALO:, :]
        o_ref[:, cols] = _silu(out).astype(o_ref.dtype)


def _proj_conv(a, norm_w, w, col0, conv_w, conv_b, seq, out_dtype):
    m, k = a.shape
    kw, n = conv_w.shape
    tm, tn = _matmul_tiles(m, k, math.gcd(n, col0), a.dtype.itemsize, w.dtype.itemsize,
                           jnp.dtype(out_dtype).itemsize, True, False, conv_taps=kw, row_cap=seq)
    assert seq % tm == 0 and tm % _HALO == 0 and tn % _MXU_WIDTH == 0 and kw - 1 <= _HALO
    return pl.pallas_call(
        functools.partial(_proj_conv_kernel, tiles_per_seq=seq // tm),
        grid=(m // tm, n // tn),
        in_specs=[pl.BlockSpec((tm, k), lambda i, j: (i, 0)),
                  pl.BlockSpec((_HALO, k), lambda i, j: (jnp.maximum(i * (tm // _HALO) - 1, 0), 0)),
                  pl.BlockSpec((1, k), lambda i, j: (0, 0)),
                  pl.BlockSpec((k, tn), lambda i, j: (0, col0 // tn + j)),
                  pl.BlockSpec((kw, tn), lambda i, j: (0, j)),
                  pl.BlockSpec((1, tn), lambda i, j: (0, j))],
        out_specs=pl.BlockSpec((tm, tn), lambda i, j: (i, j)),
        out_shape=jax.ShapeDtypeStruct((m, n), out_dtype),
        scratch_shapes=[pltpu.VMEM((tm + _HALO, k), _MXU_DTYPE)],
        compiler_params=_params(("parallel", "arbitrary")),
        name="proj_conv",
    )(a, a, norm_w.reshape(1, k).astype(_F32), w, conv_w, conv_b)


def _softplus(x):
    return jnp.maximum(x, 0.0) + jnp.log1p(jnp.exp(-jnp.abs(x)))


def _chunk_cumsum(x, chunk):
    rows = x.shape[0]
    r = lax.broadcasted_iota(jnp.int32, (chunk, chunk), 0)
    c = lax.broadcasted_iota(jnp.int32, (chunk, chunk), 1)
    tri = jnp.where(c <= r, 1.0, 0.0).astype(_MXU_DTYPE)
    out = []
    for i in range(rows // chunk):
        rem = x[i * chunk:(i + 1) * chunk, :]
        acc = None
        for _ in range(3):
            piece = rem.astype(_MXU_DTYPE)
            d = jnp.dot(tri, piece, preferred_element_type=_F32)
            acc = d if acc is None else acc + d
            rem = rem - piece.astype(_F32)
        out.append(acc)
    return jnp.concatenate(out, axis=0) if len(out) > 1 else out[0]


def _row_bcast(ref, row, n):
    return jnp.broadcast_to(ref[pl.ds(row, 1), :], (n, ref.shape[1]))


def _gla_gate_kernel(h_ref, nw_ref, wgr_ref, wup_ref, b_ref, o_ref, *, chunk):
    u = _rms(h_ref[...], nw_ref[...]).astype(_MXU_DTYPE)
    gr = jnp.dot(u, wgr_ref[...], preferred_element_type=_F32)
    pre = jnp.dot(gr.astype(_MXU_DTYPE), wup_ref[...], preferred_element_type=_F32) + b_ref[...]
    gk = -_softplus(-pre) / _GLA_GATE_NORMALIZER
    o_ref[...] = _chunk_cumsum(gk * _LOG2E, chunk)


def _gla_gate(h, norm_w, w_gr, w_up, b_up, rows=512):
    m, d = h.shape
    kd = w_up.shape[1]
    rows = min(rows, m)
    return pl.pallas_call(
        functools.partial(_gla_gate_kernel, chunk=_CHUNK),
        grid=(m // rows,),
        in_specs=[pl.BlockSpec((rows, d), lambda i: (i, 0)),
                  pl.BlockSpec((1, d), lambda i: (0, 0)),
                  pl.BlockSpec(w_gr.shape, lambda i: (0, 0)),
                  pl.BlockSpec(w_up.shape, lambda i: (0, 0)),
                  pl.BlockSpec((1, kd), lambda i: (0, 0))],
        out_specs=pl.BlockSpec((rows, kd), lambda i: (i, 0)),
        out_shape=jax.ShapeDtypeStruct((m, kd), _F32),
        compiler_params=_params(("parallel",)),
        name="gla_gate",
    )(h, norm_w.reshape(1, d).astype(_F32), w_gr, w_up, b_up)


def _gla_levels(c, nd):
    i = lax.broadcasted_iota(jnp.int32, (c, c), 0)
    j = lax.broadcasted_iota(jnp.int32, (c, c), 1)
    x = i ^ j
    lvl = jnp.zeros((c, c), jnp.int32)
    s, level = nd, 1
    while s < c:
        lvl = jnp.where(x >= s, level, lvl)
        s, level = 2 * s, level + 1
    return jnp.where(j > i, -1, lvl)


def _gla_scan_head(q_ref, k_ref, v_ref, gs_ref, b_ref, onw_ref, lvl_ref, o_ref, st_ref, qf_ref, kf_ref, scale):
    c, dk = q_ref.shape
    qf_ref[...] = q_ref[...].astype(_F32) * scale
    kf_ref[...] = k_ref[...].astype(_F32)
    q = qf_ref[...]
    k = kf_ref[...]
    v = v_ref[...]
    bc = b_ref[...]
    st = st_ref[...]
    b_last = jnp.concatenate([_row_bcast(b_ref, c - 1, 8)] * (c // 8), axis=0)

    o = lax.dot_general((q * jnp.exp2(bc)).astype(_MXU_DTYPE), st.astype(_MXU_DTYPE), _NT,
                        preferred_element_type=_F32)
    k_dec = (k * jnp.exp2(b_last - bc)).astype(_MXU_DTYPE)

    nd = _GLA_DIAG
    lane = lax.broadcasted_iota(jnp.int32, (nd, c), 1)
    pieces = []
    for m in range(c // nd):
        qb = qf_ref[m * nd:(m + 1) * nd, :]
        bb = b_ref[m * nd:(m + 1) * nd, :]
        acc = jnp.zeros((nd, c), _F32)
        for j in range(nd):
            kj = _row_bcast(kf_ref, m * nd + j, nd)
            bj = _row_bcast(b_ref, m * nd + j, nd)
            term = qb * kj * jnp.exp2(jnp.minimum(bb - bj, 0.0))
            acc = jnp.where(lane == m * nd + j, jnp.sum(term, axis=1, keepdims=True), acc)
        pieces.append(acc)
    lvl = lvl_ref[...]
    att = jnp.where(lvl == 0, jnp.concatenate(pieces, axis=0), 0.0)

    s, level = nd, 1
    while s < c:
        prev = [jnp.zeros((s, dk), _F32)]
        end = []
        for m in range(c // s):
            if m:
                prev += [_row_bcast(b_ref, m * s - 1, 8)] * (s // 8)
            end += [_row_bcast(b_ref, (m + 1) * s - 1, 8)] * (s // 8)
        qs = (q * jnp.exp2(bc - jnp.concatenate(prev, axis=0))).astype(_MXU_DTYPE)
        ks = (k * jnp.exp2(jnp.concatenate(end, axis=0) - bc)).astype(_MXU_DTYPE)
        att = jnp.where(lvl == level, lax.dot_general(qs, ks, _NT, preferred_element_type=_F32), att)
        s, level = 2 * s, level + 1

    o = o + jnp.dot(att.astype(_MXU_DTYPE), v, preferred_element_type=_F32)
    st_ref[...] = st * jnp.exp2(b_last[:1, :]) + lax.dot_general(
        v, k_dec, _TN, preferred_element_type=_F32)

    ms = jnp.mean(o * o, axis=-1, keepdims=True)
    on = (o * lax.rsqrt(ms + _RMS_EPS)) * onw_ref[...]
    o_ref[...] = (on * gs_ref[...].astype(_F32)).astype(o_ref.dtype)


def _gla_scan_kernel(q_ref, k_ref, v_ref, gs_ref, b_ref, onw_ref, lvl_ref, o_ref, st_ref, qf_ref, kf_ref,
                     *, scale, dk, dv):
    @pl.when(pl.program_id(2) == 0)
    def _():
        st_ref[...] = jnp.zeros_like(st_ref)

    for h in range(st_ref.shape[0]):
        ks, vs = pl.ds(h * dk, dk), pl.ds(h * dv, dv)
        _gla_scan_head(q_ref.at[:, ks], k_ref.at[:, ks], v_ref.at[:, vs], gs_ref.at[:, vs], b_ref.at[:, ks],
                       onw_ref, lvl_ref, o_ref.at[:, vs], st_ref.at[h], qf_ref.at[h], kf_ref.at[h], scale)


def _gla_scan(qkv, gs, bcum, o_norm_w, batch, seq, heads, dk, dv):
    m = qkv.shape[0]
    c = _CHUNK
    hb = _GLA_HEADS_PER_STEP if heads % _GLA_HEADS_PER_STEP == 0 else 1
    assert c == _LANES and dk % _LANES == 0 and dv % _LANES == 0 and (2 * heads * dk) % (hb * dv) == 0
    nt = seq // c
    kd, vd = heads * dk, heads * dv
    wk, wv = hb * dk, hb * dv
    k_blk, v_blk = kd // wk, (2 * kd) // wv

    def rows(b, h, t):
        return b * nt + t

    return pl.pallas_call(
        functools.partial(_gla_scan_kernel, scale=dk ** -0.5, dk=dk, dv=dv),
        grid=(batch, heads // hb, nt),
        in_specs=[pl.BlockSpec((c, wk), lambda b, h, t: (rows(b, h, t), h)),
                  pl.BlockSpec((c, wk), lambda b, h, t: (rows(b, h, t), k_blk + h)),
                  pl.BlockSpec((c, wv), lambda b, h, t: (rows(b, h, t), v_blk + h)),
                  pl.BlockSpec((c, wv), lambda b, h, t: (rows(b, h, t), h)),
                  pl.BlockSpec((c, wk), lambda b, h, t: (rows(b, h, t), h)),
                  pl.BlockSpec((1, dv), lambda b, h, t: (0, 0)),
                  pl.BlockSpec((c, c), lambda b, h, t: (0, 0))],
        out_specs=pl.BlockSpec((c, wv), lambda b, h, t: (rows(b, h, t), h)),
        out_shape=jax.ShapeDtypeStruct((m, vd), _MXU_DTYPE),
        scratch_shapes=[pltpu.VMEM((hb, dv, dk), _F32),
                        pltpu.VMEM((hb, c, dk), _F32),
                        pltpu.VMEM((hb, c, dk), _F32)],
        compiler_params=_params(("parallel", "parallel", "arbitrary")),
        name="gla_scan",
    )(qkv, qkv, qkv, gs, bcum, o_norm_w.reshape(1, dv).astype(_F32), _gla_levels(c, _GLA_DIAG))


def _gla_mixer(h, norm_w, batch, seq, w_in, w_gk_up, b_gk_up, o_norm_w, w_out):
    rank, kd = w_gk_up.shape
    vd = (w_in.shape[1] - rank - 2 * kd) // 2
    dv = o_norm_w.shape[0]
    heads = vd // dv
    n_main = 2 * kd + 2 * vd
    w_gr = jnp.pad(w_in[:, n_main:], ((0, 0), (0, _LANES - rank))).astype(_MXU_DTYPE)
    w_up = jnp.pad(w_gk_up, ((0, _LANES - rank), (0, 0))).astype(_MXU_DTYPE)
    qkv = _matmul(h, w_in, n=2 * kd + vd, norm_w=norm_w, out_dtype=_MXU_DTYPE)
    gs = _matmul(h, w_in, col0=2 * kd + vd, n=vd, norm_w=norm_w, act="silu", out_dtype=_MXU_DTYPE)
    bcum = _gla_gate(h, norm_w, w_gr, w_up, b_gk_up.reshape(1, kd).astype(_F32))
    o = _gla_scan(qkv, gs, bcum, o_norm_w, batch, seq, heads, kd // heads, dv)
    return _matmul(o, w_out.astype(_MXU_DTYPE), res=h, out_dtype=_F32)


def _ssd_gate_kernel(h_ref, nw_ref, wdt_ref, bias_ref, alog_ref, acol_ref, dcol_ref, arow_ref, drow_ref,
                     *, chunk, heads, groups):
    u = _rms(h_ref[...], nw_ref[...]).astype(_MXU_DTYPE)
    raw = jnp.dot(u, wdt_ref[...], preferred_element_type=_F32)
    dt = _softplus(raw + bias_ref[...])
    a = _chunk_cumsum(dt * (-jnp.exp(alog_ref[...])) * _LOG2E, chunk)
    hg = heads // groups
    for g in range(groups):
        acol_ref[g] = a[:, g * hg:(g + 1) * hg]
        dcol_ref[g] = dt[:, g * hg:(g + 1) * hg]
    for i in range(a.shape[0] // chunk):
        arow_ref[i] = a[i * chunk:(i + 1) * chunk, :].T[:heads, :]
        drow_ref[i] = dt[i * chunk:(i + 1) * chunk, :].T[:heads, :]


def _ssd_gate(h, norm_w, w_dt, dt_bias, a_log, heads, rows=512):
    m, d = h.shape
    c = _CHUNK
    rows = min(rows, m)
    g = _SSD_GROUPS
    hg = heads // g
    col = jax.ShapeDtypeStruct((g, m, hg), _F32)
    row = jax.ShapeDtypeStruct((m // c, heads, c), _F32)
    col_spec = pl.BlockSpec((g, rows, hg), lambda i: (0, i, 0))
    row_spec = pl.BlockSpec((rows // c, heads, c), lambda i: (i, 0, 0))
    return pl.pallas_call(
        functools.partial(_ssd_gate_kernel, chunk=c, heads=heads, groups=g),
        grid=(m // rows,),
        in_specs=[pl.BlockSpec((rows, d), lambda i: (i, 0)),
                  pl.BlockSpec((1, d), lambda i: (0, 0)),
                  pl.BlockSpec(w_dt.shape, lambda i: (0, 0)),
                  pl.BlockSpec((1, _LANES), lambda i: (0, 0)),
                  pl.BlockSpec((1, _LANES), lambda i: (0, 0))],
        out_specs=[col_spec, col_spec, row_spec, row_spec],
        out_shape=[col, col, row, row],
        compiler_params=_params(("parallel",)),
        name="ssd_gate",
    )(h, norm_w.reshape(1, d).astype(_F32), w_dt, dt_bias, a_log)


def _expand_heads(colarr, n_heads, head_dim):
    c = colarr.shape[0]
    per = _LANES // head_dim
    lane = lax.broadcasted_iota(jnp.int32, (c, _LANES), 1)
    pieces = []
    for p in range(n_heads // per):
        out = jnp.broadcast_to(colarr[:, p * per:p * per + 1], (c, _LANES))
        for i in range(1, per):
            nxt = jnp.broadcast_to(colarr[:, p * per + i:p * per + i + 1], (c, _LANES))
            out = jnp.where(lane >= i * head_dim, nxt, out)
        pieces.append(out)
    return jnp.concatenate(pieces, axis=1)


def _ssd_scan_group(zs_ref, x_ref, bm_ref, cm_ref, acol_ref, dcol_ref, arow_ref, drow_ref, dskip_ref, gnw_ref,
                    o_ref, st_ref, head_dim):
    c = zs_ref.shape[0]
    hg = acol_ref.shape[1]
    xs = x_ref[...].astype(_F32)
    bmx = bm_ref[...]
    cmx = cm_ref[...]

    acol = acol_ref[...]
    dcol = dcol_ref[...]
    a_last = acol[c - 1:c, :]
    w_state = jnp.exp2(a_last - acol) * dcol

    per = _LANES // head_dim
    width = per * c
    cb = lax.dot_general(cmx, jnp.concatenate([bmx] * per, axis=0), _NT,
                         preferred_element_type=_F32)
    ri = lax.broadcasted_iota(jnp.int32, (c, width), 0)
    ci = lax.broadcasted_iota(jnp.int32, (c, width), 1) & (c - 1)
    causal = ci <= ri
    lane = lax.broadcasted_iota(jnp.int32, (c, _LANES), 1)
    y_pieces = []
    for p in range(hg // per):
        ac = jnp.concatenate(
            [jnp.broadcast_to(acol[:, p * per + i:p * per + i + 1], (c, c)) for i in range(per)], axis=1)
        ar = arow_ref[:, p * width:(p + 1) * width]
        dr = drow_ref[:, p * width:(p + 1) * width]
        decay = jnp.exp2(jnp.where(causal, ac - ar, _NEG_BIG))
        sc = (cb * decay * dr).astype(_MXU_DTYPE)
        xp = x_ref[:, p * _LANES:(p + 1) * _LANES]
        bd = jnp.concatenate(
            [jnp.where((lane >= i * head_dim) & (lane < (i + 1) * head_dim), xp, jnp.zeros_like(xp))
             for i in range(per)], axis=0)
        y_pieces.append(jnp.dot(sc, bd, preferred_element_type=_F32))
    y = jnp.concatenate(y_pieces, axis=1)

    st = st_ref[...]
    ea = jnp.exp2(_expand_heads(acol, hg, head_dim))
    y = y + jnp.dot(cmx, st.astype(_MXU_DTYPE), preferred_element_type=_F32) * ea
    xw = (xs * _expand_heads(w_state, hg, head_dim)).astype(_MXU_DTYPE)
    st_ref[...] = st * ea[c - 1:c, :] + lax.dot_general(bmx, xw, _TN, preferred_element_type=_F32)

    y = y + dskip_ref[...] * xs
    y = y * zs_ref[...].astype(_F32)
    ms = jnp.mean(y * y, axis=-1, keepdims=True)
    o_ref[...] = ((y * lax.rsqrt(ms + _RMS_EPS)) * gnw_ref[...]).astype(o_ref.dtype)


def _ssd_scan_kernel(zs_ref, x_ref, bm_ref, cm_ref, acol_ref, dcol_ref, arow_ref, drow_ref, dskip_ref, gnw_ref,
                     o_ref, st_ref, *, head_dim):
    @pl.when(pl.program_id(2) == 0)
    def _():
        st_ref[...] = jnp.zeros_like(st_ref)

    gb, n, gw = st_ref.shape
    c = zs_ref.shape[0]
    hg = acol_ref.shape[2]
    for g in range(gb):
        xs_, ns_, rs_ = pl.ds(g * gw, gw), pl.ds(g * n, n), pl.ds(g * hg * c, hg * c)
        _ssd_scan_group(zs_ref.at[:, xs_], x_ref.at[:, xs_], bm_ref.at[:, ns_], cm_ref.at[:, ns_],
                        acol_ref.at[g], dcol_ref.at[g], arow_ref.at[0, :, rs_], drow_ref.at[0, :, rs_],
                        dskip_ref.at[:, xs_], gnw_ref.at[:, xs_], o_ref.at[:, xs_], st_ref.at[g], head_dim)


def _ssd_scan(zs, xbc, acol, dcol, arow, drow, dskip, gnorm_w, batch, seq, inner, heads, n_state):
    m = zs.shape[0]
    c = _CHUNK
    g = _SSD_GROUPS
    gb = _SSD_GROUPS_PER_STEP if g % _SSD_GROUPS_PER_STEP == 0 else 1
    hg = heads // g
    p = inner // heads
    gw = hg * p
    sw, sn = gb * gw, gb * n_state
    assert c == _LANES and gw % _LANES == 0 and n_state % _LANES == 0 and _LANES % p == 0
    assert inner % sn == 0 and (g * n_state) % sn == 0
    nt = seq // c
    b_blk = inner // sn
    c_blk = (inner + g * n_state) // sn

    def rows(b, t):
        return b * nt + t

    in_specs = [
        pl.BlockSpec((c, sw), lambda b, gi, t: (rows(b, t), gi)),
        pl.BlockSpec((c, sw), lambda b, gi, t: (rows(b, t), gi)),
        pl.BlockSpec((c, sn), lambda b, gi, t: (rows(b, t), b_blk + gi)),
        pl.BlockSpec((c, sn), lambda b, gi, t: (rows(b, t), c_blk + gi)),
        pl.BlockSpec((gb, c, hg), lambda b, gi, t: (gi, rows(b, t), 0)),
        pl.BlockSpec((gb, c, hg), lambda b, gi, t: (gi, rows(b, t), 0)),
        pl.BlockSpec((1, 1, gb * hg * c), lambda b, gi, t: (rows(b, t), 0, gi)),
        pl.BlockSpec((1, 1, gb * hg * c), lambda b, gi, t: (rows(b, t), 0, gi)),
        pl.BlockSpec((1, sw), lambda b, gi, t: (0, gi)),
        pl.BlockSpec((1, sw), lambda b, gi, t: (0, gi)),
    ]
    return pl.pallas_call(
        functools.partial(_ssd_scan_kernel, head_dim=p),
        grid=(batch, g // gb, nt),
        in_specs=in_specs,
        out_specs=pl.BlockSpec((c, sw), lambda b, gi, t: (rows(b, t), gi)),
        out_shape=jax.ShapeDtypeStruct((m, inner), _MXU_DTYPE),
        scratch_shapes=[pltpu.VMEM((gb, n_state, gw), _F32)],
        compiler_params=_params(("parallel", "parallel", "arbitrary")),
        name="ssd_scan",
    )(zs, xbc, xbc, xbc, acol, dcol, arow, drow, dskip, gnorm_w)


def _ssd_mixer(h, norm_w, batch, seq, w_in, conv_w, conv_b, dt_bias, a_log, d_skip, gnorm_w, w_out):
    heads = dt_bias.shape[0]
    conv_dim = conv_w.shape[1]
    inner = w_in.shape[1] - conv_dim - heads
    n_state = (conv_dim - inner) // (2 * _SSD_GROUPS)
    n_main = inner + conv_dim
    m = h.shape[0]
    w_dt = jnp.pad(w_in[:, n_main:], ((0, 0), (0, _LANES - heads))).astype(_MXU_DTYPE)
    pad1 = lambda v: jnp.pad(v.astype(_F32), (0, _LANES - heads)).reshape(1, _LANES)
    zs = _matmul(h, w_in, n=inner, norm_w=norm_w, act="silu", out_dtype=_MXU_DTYPE)
    xbc = _proj_conv(h, norm_w, w_in, inner, conv_w.astype(_F32), conv_b.reshape(1, conv_dim).astype(_F32),
                     seq, _MXU_DTYPE)
    acol, dcol, arow, drow = _ssd_gate(h, norm_w, w_dt, pad1(dt_bias), pad1(a_log), heads)
    arow = arow.reshape(m // _CHUNK, 1, heads * _CHUNK)
    drow = drow.reshape(m // _CHUNK, 1, heads * _CHUNK)
    dskip = jnp.repeat(d_skip.astype(_F32), inner // heads).reshape(1, inner)
    y = _ssd_scan(zs, xbc, acol, dcol, arow, drow, dskip, gnorm_w.reshape(1, inner).astype(_F32),
                  batch, seq, inner, heads, n_state)
    return _matmul(y, w_out.astype(_MXU_DTYPE), res=h, out_dtype=_F32)


def kernel(x, mixer_norm_w, gla_w_in, gla_w_gk_up, gla_b_gk_up, gla_o_norm_w, gla_w_out, ssd_w_in, ssd_conv_w, ssd_conv_b, ssd_dt_bias, ssd_a_log, ssd_d_skip, ssd_gnorm_w, ssd_w_out, mlp_norm_w, mlp_w_fc1, mlp_w_fc2, final_norm_w):
    batch, seq, d = x.shape
    h = x.reshape(batch * seq, d)
    for i in range(mixer_norm_w.shape[0]):
        j = i // 2
        if i % 2 == 0:
            h = _gla_mixer(h, mixer_norm_w[i], batch, seq, gla_w_in[j], gla_w_gk_up[j], gla_b_gk_up[j],
                           gla_o_norm_w[j], gla_w_out[j])
        else:
            h = _ssd_mixer(h, mixer_norm_w[i], batch, seq, ssd_w_in[j], ssd_conv_w[j], ssd_conv_b[j],
                           ssd_dt_bias[j], ssd_a_log[j], ssd_d_skip[j], ssd_gnorm_w[j], ssd_w_out[j])
        hidden = _matmul(h, mlp_w_fc1[i], norm_w=mlp_norm_w[i], act="relu2", out_dtype=_MXU_DTYPE)
        h = _matmul(hidden, mlp_w_fc2[i].astype(_MXU_DTYPE), res=h, out_dtype=_F32)
    return _rmsnorm(h, final_norm_w, _F32).reshape(batch, seq, d)
```

```python
import functools
import math

import jax
import jax.numpy as jnp
from jax import lax
from jax.experimental import pallas as pl
from jax.experimental.pallas import tpu as pltpu

_F32 = jnp.float32
_MXU_DTYPE = jnp.bfloat16
_RMS_EPS = 1e-5
_GLA_GATE_NORMALIZER = 16.0
_SSD_GROUPS = 8
_CHUNK = 128
_LANES = 128
_GLA_DIAG = 8
_GLA_HEADS_PER_STEP = 2
_SSD_GROUPS_PER_STEP = 2
_LOG2E = math.log2(math.e)
_NEG_BIG = -1e30
_HALO = 16
_MXU_WIDTH = 256
_VMEM_LIMIT_BYTES = 56 * 1024 * 1024
_MATMUL_VMEM_BUDGET = 50 * 1024 * 1024

_NT = (((1,), (1,)), ((), ()))
_TN = (((0,), (0,)), ((), ()))


def _params(sem, flags=None):
    return pltpu.CompilerParams(dimension_semantics=sem, vmem_limit_bytes=_VMEM_LIMIT_BYTES, flags=flags)


def _rms(x, w):
    ms = jnp.mean(x * x, axis=-1, keepdims=True)
    return (x * lax.rsqrt(ms + _RMS_EPS)) * w


def _rmsnorm_kernel(x_ref, w_ref, o_ref):
    o_ref[...] = _rms(x_ref[...].astype(_F32), w_ref[...]).astype(o_ref.dtype)


def _rmsnorm(x, w, out_dtype, rows=512):
    m, d = x.shape
    rows = min(rows, m)
    return pl.pallas_call(
        _rmsnorm_kernel,
        grid=(m // rows,),
        in_specs=[pl.BlockSpec((rows, d), lambda i: (i, 0)),
                  pl.BlockSpec((1, d), lambda i: (0, 0))],
        out_specs=pl.BlockSpec((rows, d), lambda i: (i, 0)),
        out_shape=jax.ShapeDtypeStruct((m, d), out_dtype),
        compiler_params=_params(("parallel",)),
        name="rmsnorm",
    )(x, w.reshape(1, d).astype(_F32))


def _silu(x):
    half = 0.5 * x
    return half + half * jnp.tanh(half)


def _matmul_kernel(*refs, act, act_tile0, has_norm, has_res):
    it = iter(refs)
    a_ref = next(it)
    nw_ref = next(it) if has_norm else None
    w_ref = next(it)
    r_ref = next(it) if has_res else None
    o_ref = next(it)
    if has_norm:
        an_ref = next(it)

        @pl.when(pl.program_id(1) == 0)
        def _():
            an_ref[...] = _rms(a_ref[...], nw_ref[...]).astype(_MXU_DTYPE)

        a = an_ref[...]
    else:
        a = a_ref[...]
    acc = jnp.dot(a, w_ref[...].astype(_MXU_DTYPE), preferred_element_type=_F32)
    if act == "relu2":
        act_acc = jnp.square(jnp.maximum(acc, 0.0))
    elif act == "silu":
        act_acc = _silu(acc)
    if act is not None:
        acc = act_acc if act_tile0 == 0 else jnp.where(pl.program_id(1) >= act_tile0, act_acc, acc)
    if has_res:
        acc = r_ref[...] + acc
    o_ref[...] = acc.astype(o_ref.dtype)


def _divisor_tile(n, pref, align):
    if n <= pref:
        return n
    t = (pref // align) * align
    while n % t:
        t -= align
    return t


def _matmul_tiles(m, k, n, a_bytes, w_bytes, out_bytes, has_norm, has_res, conv_taps=0, row_cap=None):
    mxu_bytes = jnp.dtype(_MXU_DTYPE).itemsize
    for pm, pn in ((1024, 1024), (1024, 512), (512, 512), (512, 256), (256, 256), (128, 128)):
        pm = pm if row_cap is None else min(pm, row_cap)
        tm, tn = _divisor_tile(m, pm, 8), _divisor_tile(n, pn, _LANES)
        need = 2 * tm * k * a_bytes + 2 * k * tn * w_bytes + 2 * tm * tn * out_bytes
        need += tm * tn * 4 if not conv_taps else (1 + conv_taps) * tm * _MXU_WIDTH * 4
        need += 2 * tm * tn * 4 if has_res else 0
        need += tm * k * mxu_bytes if has_norm else 0
        need += k * tn * mxu_bytes if w_bytes != mxu_bytes else 0
        if need <= _MATMUL_VMEM_BUDGET:
            break
    return tm, tn


def _matmul(a, w, layer, *, n=None, norm_w=None, act=None, act_col0=0, res=None, out_dtype):
    m, k = a.shape
    n = w.shape[2] if n is None else n
    tm, tn = _matmul_tiles(m, k, math.gcd(n, act_col0), a.dtype.itemsize, w.dtype.itemsize,
                           jnp.dtype(out_dtype).itemsize, norm_w is not None, res is not None)
    in_specs = [pl.BlockSpec((tm, k), lambda i, j: (i, 0))]
    args = [a]
    if norm_w is not None:
        in_specs.append(pl.BlockSpec((1, k), lambda i, j: (0, 0)))
        args.append(norm_w.reshape(1, k).astype(_F32))
    in_specs.append(pl.BlockSpec((None, k, tn), lambda i, j: (layer, 0, j)))
    args.append(w)
    if res is not None:
        in_specs.append(pl.BlockSpec((tm, tn), lambda i, j: (i, j)))
        args.append(res)
    return pl.pallas_call(
        functools.partial(_matmul_kernel, act=act, act_tile0=act_col0 // tn, has_norm=norm_w is not None,
                          has_res=res is not None),
        grid=(m // tm, n // tn),
        in_specs=in_specs,
        out_specs=pl.BlockSpec((tm, tn), lambda i, j: (i, j)),
        out_shape=jax.ShapeDtypeStruct((m, n), out_dtype),
        scratch_shapes=[pltpu.VMEM((tm, k), _MXU_DTYPE)] if norm_w is not None else [],
        compiler_params=_params(("parallel", "arbitrary")),
        name="matmul",
    )(*args)


def _proj_conv_kernel(a_ref, halo_ref, nw_ref, w_ref, cw_ref, cb_ref, o_ref, an_ref, *, tiles_per_seq):
    kw = cw_ref.shape[0]

    @pl.when(pl.program_id(1) == 0)
    def _():
        keep = jnp.where(pl.program_id(0) % tiles_per_seq == 0, 0.0, 1.0)
        an_ref[0:_HALO, :] = (_rms(halo_ref[...], nw_ref[...]) * keep).astype(_MXU_DTYPE)
        an_ref[_HALO:, :] = _rms(a_ref[...], nw_ref[...]).astype(_MXU_DTYPE)

    wb = w_ref[...].astype(_MXU_DTYPE)
    for c0 in range(0, o_ref.shape[1], _MXU_WIDTH):
        cols = pl.ds(c0, _MXU_WIDTH)
        acc = jnp.dot(an_ref[...], wb[:, c0:c0 + _MXU_WIDTH], preferred_element_type=_F32)
        out = cb_ref[:, cols] + cw_ref[kw - 1:kw, cols] * acc[_HALO:, :]
        for s in range(1, kw):
            out = out + cw_ref[kw - 1 - s:kw - s, cols] * pltpu.roll(acc, s, axis=0)[_HALO:, :]
        o_ref[:, cols] = _silu(out).astype(o_ref.dtype)


def _proj_conv(a, norm_w, w, layer, col0, conv_w, conv_b, seq, out_dtype):
    m, k = a.shape
    kw, n = conv_w.shape
    tm, tn = _matmul_tiles(m, k, math.gcd(n, col0), a.dtype.itemsize, w.dtype.itemsize,
                           jnp.dtype(out_dtype).itemsize, True, False, conv_taps=kw, row_cap=seq)
    assert seq % tm == 0 and tm % _HALO == 0 and tn % _MXU_WIDTH == 0 and kw - 1 <= _HALO
    return pl.pallas_call(
        functools.partial(_proj_conv_kernel, tiles_per_seq=seq // tm),
        grid=(m // tm, n // tn),
        in_specs=[pl.BlockSpec((tm, k), lambda i, j: (i, 0)),
                  pl.BlockSpec((_HALO, k), lambda i, j: (jnp.maximum(i * (tm // _HALO) - 1, 0), 0)),
                  pl.BlockSpec((1, k), lambda i, j: (0, 0)),
                  pl.BlockSpec((None, k, tn), lambda i, j: (layer, 0, col0 // tn + j)),
                  pl.BlockSpec((kw, tn), lambda i, j: (0, j)),
                  pl.BlockSpec((1, tn), lambda i, j: (0, j))],
        out_specs=pl.BlockSpec((tm, tn), lambda i, j: (i, j)),
        out_shape=jax.ShapeDtypeStruct((m, n), out_dtype),
        scratch_shapes=[pltpu.VMEM((tm + _HALO, k), _MXU_DTYPE)],
        compiler_params=_params(("parallel", "arbitrary")),
        name="proj_conv",
    )(a, a, norm_w.reshape(1, k).astype(_F32), w, conv_w, conv_b)


def _softplus(x):
    return jnp.maximum(x, 0.0) + jnp.log1p(jnp.exp(-jnp.abs(x)))


def _chunk_cumsum(x, chunk):
    rows = x.shape[0]
    r = lax.broadcasted_iota(jnp.int32, (chunk, chunk), 0)
    c = lax.broadcasted_iota(jnp.int32, (chunk, chunk), 1)
    tri = jnp.where(c <= r, 1.0, 0.0).astype(_MXU_DTYPE)
    out = []
    for i in range(rows // chunk):
        rem = x[i * chunk:(i + 1) * chunk, :]
        acc = None
        for _ in range(3):
            piece = rem.astype(_MXU_DTYPE)
            d = jnp.dot(tri, piece, preferred_element_type=_F32)
            acc = d if acc is None else acc + d
            rem = rem - piece.astype(_F32)
        out.append(acc)
    return jnp.concatenate(out, axis=0) if len(out) > 1 else out[0]


def _row_bcast(ref, row, n):
    return jnp.broadcast_to(ref[pl.ds(row, 1), :], (n, ref.shape[1]))


def _tail_columns(w_ref, n_valid):
    lane = lax.broadcasted_iota(jnp.int32, w_ref.shape, 1)
    return jnp.where(lane < n_valid, w_ref[...], 0.0).astype(_MXU_DTYPE)


def _gla_gate_kernel(h_ref, nw_ref, wgr_ref, wup_ref, b_ref, o_ref, *, chunk, rank):
    u = _rms(h_ref[...], nw_ref[...]).astype(_MXU_DTYPE)
    gr = jnp.dot(u, _tail_columns(wgr_ref, rank), preferred_element_type=_F32)
    pre = jnp.dot(gr.astype(_MXU_DTYPE), wup_ref[...], preferred_element_type=_F32) + b_ref[...]
    gk = -_softplus(-pre) / _GLA_GATE_NORMALIZER
    o_ref[...] = _chunk_cumsum(gk * _LOG2E, chunk)


def _gla_gate(h, norm_w, w_in, layer, col0, rank, w_up, b_up, rows=512):
    m, d = h.shape
    kd = w_up.shape[1]
    rows = min(rows, m)
    assert col0 % _LANES == 0 and col0 + rank == w_in.shape[2] and rank <= _LANES
    return pl.pallas_call(
        functools.partial(_gla_gate_kernel, chunk=_CHUNK, rank=rank),
        grid=(m // rows,),
        in_specs=[pl.BlockSpec((rows, d), lambda i: (i, 0)),
                  pl.BlockSpec((1, d), lambda i: (0, 0)),
                  pl.BlockSpec((None, d, _LANES), lambda i: (layer, 0, col0 // _LANES)),
                  pl.BlockSpec(w_up.shape, lambda i: (0, 0)),
                  pl.BlockSpec((1, kd), lambda i: (0, 0))],
        out_specs=pl.BlockSpec((rows, kd), lambda i: (i, 0)),
        out_shape=jax.ShapeDtypeStruct((m, kd), _F32),
        compiler_params=_params(("parallel",)),
        name="gla_gate",
    )(h, norm_w.reshape(1, d).astype(_F32), w_in, w_up, b_up)


def _gla_levels(c, nd):
    i = lax.broadcasted_iota(jnp.int32, (c, c), 0)
    j = lax.broadcasted_iota(jnp.int32, (c, c), 1)
    x = i ^ j
    lvl = jnp.zeros((c, c), jnp.int32)
    s, level = nd, 1
    while s < c:
        lvl = jnp.where(x >= s, level, lvl)
        s, level = 2 * s, level + 1
    return jnp.where(j > i, -1, lvl)


def _gla_scan_head(q_ref, k_ref, v_ref, gs_ref, b_ref, onw_ref, lvl_ref, o_ref, st_ref, qf_ref, kf_ref, scale):
    c, dk = q_ref.shape
    qf_ref[...] = q_ref[...].astype(_F32) * scale
    kf_ref[...] = k_ref[...].astype(_F32)
    q = qf_ref[...]
    k = kf_ref[...]
    v = v_ref[...]
    bc = b_ref[...]
    st = st_ref[...]
    b_last = jnp.concatenate([_row_bcast(b_ref, c - 1, 8)] * (c // 8), axis=0)

    o = lax.dot_general((q * jnp.exp2(bc)).astype(_MXU_DTYPE), st.astype(_MXU_DTYPE), _NT,
                        preferred_element_type=_F32)
    k_dec = (k * jnp.exp2(b_last - bc)).astype(_MXU_DTYPE)

    nd = _GLA_DIAG
    lane = lax.broadcasted_iota(jnp.int32, (nd, c), 1)
    pieces = []
    for m in range(c // nd):
        qb = qf_ref[m * nd:(m + 1) * nd, :]
        bb = b_ref[m * nd:(m + 1) * nd, :]
        acc = jnp.zeros((nd, c), _F32)
        for j in range(nd):
            kj = _row_bcast(kf_ref, m * nd + j, nd)
            bj = _row_bcast(b_ref, m * nd + j, nd)
            term = qb * kj * jnp.exp2(jnp.minimum(bb - bj, 0.0))
            acc = jnp.where(lane == m * nd + j, jnp.sum(term, axis=1, keepdims=True), acc)
        pieces.append(acc)
    lvl = lvl_ref[...]
    att = jnp.where(lvl == 0, jnp.concatenate(pieces, axis=0), 0.0)

    s, level = nd, 1
    while s < c:
        prev = [jnp.zeros((s, dk), _F32)]
        end = []
        for m in range(c // s):
            if m:
                prev += [_row_bcast(b_ref, m * s - 1, 8)] * (s // 8)
            end += [_row_bcast(b_ref, (m + 1) * s - 1, 8)] * (s // 8)
        qs = (q * jnp.exp2(bc - jnp.concatenate(prev, axis=0))).astype(_MXU_DTYPE)
        ks = (k * jnp.exp2(jnp.concatenate(end, axis=0) - bc)).astype(_MXU_DTYPE)
        att = jnp.where(lvl == level, lax.dot_general(qs, ks, _NT, preferred_element_type=_F32), att)
        s, level = 2 * s, level + 1

    o = o + jnp.dot(att.astype(_MXU_DTYPE), v, preferred_element_type=_F32)
    st_ref[...] = st * jnp.exp2(b_last[:1, :]) + lax.dot_general(
        v, k_dec, _TN, preferred_element_type=_F32)

    ms = jnp.mean(o * o, axis=-1, keepdims=True)
    on = (o * lax.rsqrt(ms + _RMS_EPS)) * onw_ref[...]
    o_ref[...] = (on * gs_ref[...].astype(_F32)).astype(o_ref.dtype)


def _gla_scan_kernel(q_ref, k_ref, v_ref, gs_ref, b_ref, onw_ref, lvl_ref, o_ref, st_ref, qf_ref, kf_ref,
                     *, scale, dk, dv):
    @pl.when(pl.program_id(2) == 0)
    def _():
        st_ref[...] = jnp.zeros_like(st_ref)

    for h in range(st_ref.shape[0]):
        ks, vs = pl.ds(h * dk, dk), pl.ds(h * dv, dv)
        _gla_scan_head(q_ref.at[:, ks], k_ref.at[:, ks], v_ref.at[:, vs], gs_ref.at[:, vs], b_ref.at[:, ks],
                       onw_ref, lvl_ref, o_ref.at[:, vs], st_ref.at[h], qf_ref.at[h], kf_ref.at[h], scale)


def _gla_scan(proj, bcum, o_norm_w, batch, seq, heads, dk, dv):
    m = proj.shape[0]
    c = _CHUNK
    hb = _GLA_HEADS_PER_STEP if heads % _GLA_HEADS_PER_STEP == 0 else 1
    assert c == _LANES and dk % _LANES == 0 and dv % _LANES == 0 and (2 * heads * dk) % (hb * dv) == 0
    nt = seq // c
    kd, vd = heads * dk, heads * dv
    wk, wv = hb * dk, hb * dv
    k_blk, v_blk, g_blk = kd // wk, (2 * kd) // wv, (2 * kd + vd) // wv

    def rows(b, h, t):
        return b * nt + t

    return pl.pallas_call(
        functools.partial(_gla_scan_kernel, scale=dk ** -0.5, dk=dk, dv=dv),
        grid=(batch, heads // hb, nt),
        in_specs=[pl.BlockSpec((c, wk), lambda b, h, t: (rows(b, h, t), h)),
                  pl.BlockSpec((c, wk), lambda b, h, t: (rows(b, h, t), k_blk + h)),
                  pl.BlockSpec((c, wv), lambda b, h, t: (rows(b, h, t), v_blk + h)),
                  pl.BlockSpec((c, wv), lambda b, h, t: (rows(b, h, t), g_blk + h)),
                  pl.BlockSpec((c, wk), lambda b, h, t: (rows(b, h, t), h)),
                  pl.BlockSpec((1, dv), lambda b, h, t: (0, 0)),
                  pl.BlockSpec((c, c), lambda b, h, t: (0, 0))],
        out_specs=pl.BlockSpec((c, wv), lambda b, h, t: (rows(b, h, t), h)),
        out_shape=jax.ShapeDtypeStruct((m, vd), _MXU_DTYPE),
        scratch_shapes=[pltpu.VMEM((hb, dv, dk), _F32),
                        pltpu.VMEM((hb, c, dk), _F32),
                        pltpu.VMEM((hb, c, dk), _F32)],
        compiler_params=_params(("parallel", "parallel", "arbitrary")),
        name="gla_scan",
    )(proj, proj, proj, proj, bcum, o_norm_w.reshape(1, dv).astype(_F32), _gla_levels(c, _GLA_DIAG))


def _gla_mixer(h, norm_w, batch, seq, layer, w_in, w_gk_up, b_gk_up, o_norm_w, w_out):
    rank, kd = w_gk_up.shape
    vd = (w_in.shape[2] - rank - 2 * kd) // 2
    dv = o_norm_w.shape[0]
    heads = vd // dv
    n_main = 2 * kd + 2 * vd
    w_up = jnp.pad(w_gk_up, ((0, _LANES - rank), (0, 0))).astype(_MXU_DTYPE)
    proj = _matmul(h, w_in, layer, n=n_main, norm_w=norm_w, act="silu", act_col0=2 * kd + vd,
                   out_dtype=_MXU_DTYPE)
    bcum = _gla_gate(h, norm_w, w_in, layer, n_main, rank, w_up, b_gk_up.reshape(1, kd).astype(_F32))
    o = _gla_scan(proj, bcum, o_norm_w, batch, seq, heads, kd // heads, dv)
    return _matmul(o, w_out.astype(_MXU_DTYPE), layer, res=h, out_dtype=_F32)


def _ssd_gate_kernel(h_ref, nw_ref, wdt_ref, bias_ref, alog_ref, acol_ref, dcol_ref, arow_ref, drow_ref,
                     *, chunk, heads, groups):
    u = _rms(h_ref[...], nw_ref[...]).astype(_MXU_DTYPE)
    raw = jnp.dot(u, _tail_columns(wdt_ref, heads), preferred_element_type=_F32)
    dt = _softplus(raw + bias_ref[...])
    a = _chunk_cumsum(dt * (-jnp.exp(alog_ref[...])) * _LOG2E, chunk)
    hg = heads // groups
    for g in range(groups):
        acol_ref[g] = a[:, g * hg:(g + 1) * hg]
        dcol_ref[g] = dt[:, g * hg:(g + 1) * hg]
    for i in range(a.shape[0] // chunk):
        arow_ref[i] = a[i * chunk:(i + 1) * chunk, :].T[:heads, :]
        drow_ref[i] = dt[i * chunk:(i + 1) * chunk, :].T[:heads, :]


def _ssd_gate(h, norm_w, w_in, layer, col0, dt_bias, a_log, heads, rows=512):
    m, d = h.shape
    c = _CHUNK
    rows = min(rows, m)
    g = _SSD_GROUPS
    hg = heads // g
    assert col0 % _LANES == 0 and col0 + heads == w_in.shape[2] and heads <= _LANES
    col = jax.ShapeDtypeStruct((g, m, hg), _F32)
    row = jax.ShapeDtypeStruct((m // c, heads, c), _F32)
    col_spec = pl.BlockSpec((g, rows, hg), lambda i: (0, i, 0))
    row_spec = pl.BlockSpec((rows // c, heads, c), lambda i: (i, 0, 0))
    return pl.pallas_call(
        functools.partial(_ssd_gate_kernel, chunk=c, heads=heads, groups=g),
        grid=(m // rows,),
        in_specs=[pl.BlockSpec((rows, d), lambda i: (i, 0)),
                  pl.BlockSpec((1, d), lambda i: (0, 0)),
                  pl.BlockSpec((None, d, _LANES), lambda i: (layer, 0, col0 // _LANES)),
                  pl.BlockSpec((1, _LANES), lambda i: (0, 0)),
                  pl.BlockSpec((1, _LANES), lambda i: (0, 0))],
        out_specs=[col_spec, col_spec, row_spec, row_spec],
        out_shape=[col, col, row, row],
        compiler_params=_params(("parallel",)),
        name="ssd_gate",
    )(h, norm_w.reshape(1, d).astype(_F32), w_in, dt_bias, a_log)


def _expand_heads(colarr, n_heads, head_dim):
    c = colarr.shape[0]
    per = _LANES // head_dim
    lane = lax.broadcasted_iota(jnp.int32, (c, _LANES), 1)
    pieces = []
    for p in range(n_heads // per):
        out = jnp.broadcast_to(colarr[:, p * per:p * per + 1], (c, _LANES))
        for i in range(1, per):
            nxt = jnp.broadcast_to(colarr[:, p * per + i:p * per + i + 1], (c, _LANES))
            out = jnp.where(lane >= i * head_dim, nxt, out)
        pieces.append(out)
    return jnp.concatenate(pieces, axis=1)


def _ssd_scan_group(zs_ref, x_ref, bm_ref, cm_ref, acol_ref, dcol_ref, arow_ref, drow_ref, dskip_ref, gnw_ref,
                    o_ref, st_ref, head_dim):
    c = zs_ref.shape[0]
    hg = acol_ref.shape[1]
    xs = x_ref[...].astype(_F32)
    bmx = bm_ref[...]
    cmx = cm_ref[...]

    acol = acol_ref[...]
    dcol = dcol_ref[...]
    a_last = acol[c - 1:c, :]
    w_state = jnp.exp2(a_last - acol) * dcol

    per = _LANES // head_dim
    width = per * c
    cb = lax.dot_general(cmx, jnp.concatenate([bmx] * per, axis=0), _NT,
                         preferred_element_type=_F32)
    ri = lax.broadcasted_iota(jnp.int32, (c, width), 0)
    ci = lax.broadcasted_iota(jnp.int32, (c, width), 1) & (c - 1)
    causal = ci <= ri
    lane = lax.broadcasted_iota(jnp.int32, (c, _LANES), 1)
    y_pieces = []
    for p in range(hg // per):
        ac = jnp.concatenate(
            [jnp.broadcast_to(acol[:, p * per + i:p * per + i + 1], (c, c)) for i in range(per)], axis=1)
        ar = arow_ref[:, p * width:(p + 1) * width]
        dr = drow_ref[:, p * width:(p + 1) * width]
        decay = jnp.exp2(jnp.where(causal, ac - ar, _NEG_BIG))
        sc = (cb * decay * dr).astype(_MXU_DTYPE)
        xp = x_ref[:, p * _LANES:(p + 1) * _LANES]
        bd = jnp.concatenate(
            [jnp.where((lane >= i * head_dim) & (lane < (i + 1) * head_dim), xp, jnp.zeros_like(xp))
             for i in range(per)], axis=0)
        y_pieces.append(jnp.dot(sc, bd, preferred_element_type=_F32))
    y = jnp.concatenate(y_pieces, axis=1)

    st = st_ref[...]
    ea = jnp.exp2(_expand_heads(acol, hg, head_dim))
    y = y + jnp.dot(cmx, st.astype(_MXU_DTYPE), preferred_element_type=_F32) * ea
    xw = (xs * _expand_heads(w_state, hg, head_dim)).astype(_MXU_DTYPE)
    st_ref[...] = st * ea[c - 1:c, :] + lax.dot_general(bmx, xw, _TN, preferred_element_type=_F32)

    y = y + dskip_ref[...] * xs
    y = y * zs_ref[...].astype(_F32)
    ms = jnp.mean(y * y, axis=-1, keepdims=True)
    o_ref[...] = ((y * lax.rsqrt(ms + _RMS_EPS)) * gnw_ref[...]).astype(o_ref.dtype)


def _ssd_scan_kernel(zs_ref, x_ref, bm_ref, cm_ref, acol_ref, dcol_ref, arow_ref, drow_ref, dskip_ref, gnw_ref,
                     o_ref, st_ref, *, head_dim):
    @pl.when(pl.program_id(2) == 0)
    def _():
        st_ref[...] = jnp.zeros_like(st_ref)

    gb, n, gw = st_ref.shape
    c = zs_ref.shape[0]
    hg = acol_ref.shape[2]
    for g in range(gb):
        xs_, ns_, rs_ = pl.ds(g * gw, gw), pl.ds(g * n, n), pl.ds(g * hg * c, hg * c)
        _ssd_scan_group(zs_ref.at[:, xs_], x_ref.at[:, xs_], bm_ref.at[:, ns_], cm_ref.at[:, ns_],
                        acol_ref.at[g], dcol_ref.at[g], arow_ref.at[0, :, rs_], drow_ref.at[0, :, rs_],
                        dskip_ref.at[:, xs_], gnw_ref.at[:, xs_], o_ref.at[:, xs_], st_ref.at[g], head_dim)


def _ssd_scan(zs, xbc, acol, dcol, arow, drow, dskip, gnorm_w, batch, seq, inner, heads, n_state):
    m = zs.shape[0]
    c = _CHUNK
    g = _SSD_GROUPS
    gb = _SSD_GROUPS_PER_STEP if g % _SSD_GROUPS_PER_STEP == 0 else 1
    hg = heads // g
    p = inner // heads
    gw = hg * p
    sw, sn = gb * gw, gb * n_state
    assert c == _LANES and gw % _LANES == 0 and n_state % _LANES == 0 and _LANES % p == 0
    assert inner % sn == 0 and (g * n_state) % sn == 0
    nt = seq // c
    b_blk = inner // sn
    c_blk = (inner + g * n_state) // sn

    def rows(b, t):
        return b * nt + t

    in_specs = [
        pl.BlockSpec((c, sw), lambda b, gi, t: (rows(b, t), gi)),
        pl.BlockSpec((c, sw), lambda b, gi, t: (rows(b, t), gi)),
        pl.BlockSpec((c, sn), lambda b, gi, t: (rows(b, t), b_blk + gi)),
        pl.BlockSpec((c, sn), lambda b, gi, t: (rows(b, t), c_blk + gi)),
        pl.BlockSpec((gb, c, hg), lambda b, gi, t: (gi, rows(b, t), 0)),
        pl.BlockSpec((gb, c, hg), lambda b, gi, t: (gi, rows(b, t), 0)),
        pl.BlockSpec((1, 1, gb * hg * c), lambda b, gi, t: (rows(b, t), 0, gi)),
        pl.BlockSpec((1, 1, gb * hg * c), lambda b, gi, t: (rows(b, t), 0, gi)),
        pl.BlockSpec((1, sw), lambda b, gi, t: (0, gi)),
        pl.BlockSpec((1, sw), lambda b, gi, t: (0, gi)),
    ]
    return pl.pallas_call(
        functools.partial(_ssd_scan_kernel, head_dim=p),
        grid=(batch, g // gb, nt),
        in_specs=in_specs,
        out_specs=pl.BlockSpec((c, sw), lambda b, gi, t: (rows(b, t), gi)),
        out_shape=jax.ShapeDtypeStruct((m, inner), _MXU_DTYPE),
        scratch_shapes=[pltpu.VMEM((gb, n_state, gw), _F32)],
        compiler_params=_params(("parallel", "parallel", "arbitrary")),
        name="ssd_scan",
    )(zs, xbc, xbc, xbc, acol, dcol, arow, drow, dskip, gnorm_w)


def _ssd_mixer(h, norm_w, batch, seq, layer, w_in, conv_w, conv_b, dt_bias, a_log, d_skip, gnorm_w, w_out):
    heads = dt_bias.shape[0]
    conv_dim = conv_w.shape[1]
    inner = w_in.shape[2] - conv_dim - heads
    n_state = (conv_dim - inner) // (2 * _SSD_GROUPS)
    n_main = inner + conv_dim
    m = h.shape[0]
    pad1 = lambda v: jnp.pad(v.astype(_F32), (0, _LANES - heads)).reshape(1, _LANES)
    zs = _matmul(h, w_in, layer, n=inner, norm_w=norm_w, act="silu", out_dtype=_MXU_DTYPE)
    xbc = _proj_conv(h, norm_w, w_in, layer, inner, conv_w.astype(_F32),
                     conv_b.reshape(1, conv_dim).astype(_F32), seq, _MXU_DTYPE)
    acol, dcol, arow, drow = _ssd_gate(h, norm_w, w_in, layer, n_main, pad1(dt_bias), pad1(a_log), heads)
    arow = arow.reshape(m // _CHUNK, 1, heads * _CHUNK)
    drow = drow.reshape(m // _CHUNK, 1, heads * _CHUNK)
    dskip = jnp.repeat(d_skip.astype(_F32), inner // heads).reshape(1, inner)
    y = _ssd_scan(zs, xbc, acol, dcol, arow, drow, dskip, gnorm_w.reshape(1, inner).astype(_F32),
                  batch, seq, inner, heads, n_state)
    return _matmul(y, w_out.astype(_MXU_DTYPE), layer, res=h, out_dtype=_F32)


def kernel(x, mixer_norm_w, gla_w_in, gla_w_gk_up, gla_b_gk_up, gla_o_norm_w, gla_w_out, ssd_w_in, ssd_conv_w, ssd_conv_b, ssd_dt_bias, ssd_a_log, ssd_d_skip, ssd_gnorm_w, ssd_w_out, mlp_norm_w, mlp_w_fc1, mlp_w_fc2, final_norm_w):
    batch, seq, d = x.shape
    h = x.reshape(batch * seq, d)
    w_fc2 = mlp_w_fc2.astype(_MXU_DTYPE)
    for i in range(mixer_norm_w.shape[0]):
        j = i // 2
        if i % 2 == 0:
            h = _gla_mixer(h, mixer_norm_w[i], batch, seq, j, gla_w_in, gla_w_gk_up[j], gla_b_gk_up[j],
                           gla_o_norm_w[j], gla_w_out)
        else:
            h = _ssd_mixer(h, mixer_norm_w[i], batch, seq, j, ssd_w_in, ssd_conv_w[j], ssd_conv_b[j],
                           ssd_dt_bias[j], ssd_a_log[j], ssd_d_skip[j], ssd_gnorm_w[j], ssd_w_out)
        hidden = _matmul(h, mlp_w_fc1, i, norm_w=mlp_norm_w[i], act="relu2", out_dtype=_MXU_DTYPE)
        h = _matmul(hidden, w_fc2, i, res=h, out_dtype=_F32)
    return _rmsnorm(h, final_norm_w, _F32).reshape(batch, seq, d)
```

```python
import functools
import math

import jax
import jax.numpy as jnp
from jax import lax
from jax.experimental import pallas as pl
from jax.experimental.pallas import tpu as pltpu

_F32 = jnp.float32
_MXU_DTYPE = jnp.bfloat16
_RMS_EPS = 1e-5
_GLA_GATE_NORMALIZER = 16.0
_SSD_GROUPS = 8
_CHUNK = 128
_LANES = 128
_GLA_DIAG = 8
_GLA_HEADS_PER_STEP = 2
_SSD_GROUPS_PER_STEP = 2
_LOG2E = math.log2(math.e)
_NEG_BIG = -1e30
_HALO = 16
_MXU_WIDTH = 256
_VMEM_LIMIT_BYTES = 56 * 1024 * 1024
_MATMUL_VMEM_BUDGET = 50 * 1024 * 1024

_NN = (((1,), (0,)), ((), ()))
_NT = (((1,), (1,)), ((), ()))
_TN = (((0,), (0,)), ((), ()))


def _params(sem, flags=None):
    return pltpu.CompilerParams(dimension_semantics=sem, vmem_limit_bytes=_VMEM_LIMIT_BYTES, flags=flags)


def _rms(x, w):
    ms = jnp.mean(x * x, axis=-1, keepdims=True)
    return (x * lax.rsqrt(ms + _RMS_EPS)) * w


def _rmsnorm_kernel(x_ref, w_ref, o_ref):
    o_ref[...] = _rms(x_ref[...].astype(_F32), w_ref[...]).astype(o_ref.dtype)


def _rmsnorm(x, w, out_dtype, rows=512):
    m, d = x.shape
    rows = min(rows, m)
    return pl.pallas_call(
        _rmsnorm_kernel,
        grid=(m // rows,),
        in_specs=[pl.BlockSpec((rows, d), lambda i: (i, 0)),
                  pl.BlockSpec((1, d), lambda i: (0, 0))],
        out_specs=pl.BlockSpec((rows, d), lambda i: (i, 0)),
        out_shape=jax.ShapeDtypeStruct((m, d), out_dtype),
        compiler_params=_params(("parallel",)),
        name="rmsnorm",
    )(x, w.reshape(1, d).astype(_F32))


def _silu(x):
    half = 0.5 * x
    return half + half * jnp.tanh(half)


def _matmul_kernel(*refs, act, act_tile0, has_norm, has_res, wt):
    it = iter(refs)
    a_ref = next(it)
    nw_ref = next(it) if has_norm else None
    w_ref = next(it)
    r_ref = next(it) if has_res else None
    o_ref = next(it)
    if has_norm:
        an_ref = next(it)

        @pl.when(pl.program_id(1) == 0)
        def _():
            an_ref[...] = _rms(a_ref[...], nw_ref[...]).astype(_MXU_DTYPE)

        a = an_ref[...]
    else:
        a = a_ref[...]
    w = w_ref[...].astype(_MXU_DTYPE)
    acc = lax.dot_general(a, w, _NT if wt else _NN, preferred_element_type=_F32)
    if act == "relu2":
        act_acc = jnp.square(jnp.maximum(acc, 0.0))
    elif act == "silu":
        act_acc = _silu(acc)
    if act is not None:
        acc = act_acc if act_tile0 == 0 else jnp.where(pl.program_id(1) >= act_tile0, act_acc, acc)
    if has_res:
        acc = r_ref[...] + acc
    o_ref[...] = acc.astype(o_ref.dtype)


def _divisor_tile(n, pref, align):
    if n <= pref:
        return n
    t = (pref // align) * align
    while n % t:
        t -= align
    return t


def _matmul_tiles(m, k, n, a_bytes, w_bytes, out_bytes, has_norm, has_res, conv_taps=0, row_cap=None):
    mxu_bytes = jnp.dtype(_MXU_DTYPE).itemsize
    for pm, pn in ((1024, 1024), (1024, 512), (512, 512), (512, 256), (256, 256), (128, 128)):
        pm = pm if row_cap is None else min(pm, row_cap)
        tm, tn = _divisor_tile(m, pm, 8), _divisor_tile(n, pn, _LANES)
        need = 2 * tm * k * a_bytes + 2 * k * tn * w_bytes + 2 * tm * tn * out_bytes
        need += tm * tn * 4 if not conv_taps else (1 + conv_taps) * tm * _MXU_WIDTH * 4
        need += 2 * tm * tn * 4 if has_res else 0
        need += tm * k * mxu_bytes if has_norm else 0
        need += k * tn * mxu_bytes if w_bytes != mxu_bytes else 0
        if need <= _MATMUL_VMEM_BUDGET:
            break
    return tm, tn


def _matmul(a, w, layer, *, wt=False, n=None, norm_w=None, act=None, act_col0=0, res=None, out_dtype):
    m, k = a.shape
    n = w.shape[1 if wt else 2] if n is None else n
    tm, tn = _matmul_tiles(m, k, math.gcd(n, act_col0), a.dtype.itemsize, w.dtype.itemsize,
                           jnp.dtype(out_dtype).itemsize, norm_w is not None, res is not None)
    in_specs = [pl.BlockSpec((tm, k), lambda i, j: (i, 0))]
    args = [a]
    if norm_w is not None:
        in_specs.append(pl.BlockSpec((1, k), lambda i, j: (0, 0)))
        args.append(norm_w.reshape(1, k).astype(_F32))
    in_specs.append(pl.BlockSpec((None, tn, k), lambda i, j: (layer, j, 0)) if wt else
                    pl.BlockSpec((None, k, tn), lambda i, j: (layer, 0, j)))
    args.append(w)
    if res is not None:
        in_specs.append(pl.BlockSpec((tm, tn), lambda i, j: (i, j)))
        args.append(res)
    return pl.pallas_call(
        functools.partial(_matmul_kernel, act=act, act_tile0=act_col0 // tn, has_norm=norm_w is not None,
                          has_res=res is not None, wt=wt),
        grid=(m // tm, n // tn),
        in_specs=in_specs,
        out_specs=pl.BlockSpec((tm, tn), lambda i, j: (i, j)),
        out_shape=jax.ShapeDtypeStruct((m, n), out_dtype),
        scratch_shapes=[pltpu.VMEM((tm, k), _MXU_DTYPE)] if norm_w is not None else [],
        compiler_params=_params(("parallel", "arbitrary")),
        name="matmul",
    )(*args)


def _proj_conv_kernel(a_ref, halo_ref, nw_ref, w_ref, cw_ref, cb_ref, o_ref, an_ref, *, tiles_per_seq):
    kw = cw_ref.shape[0]

    @pl.when(pl.program_id(1) == 0)
    def _():
        keep = jnp.where(pl.program_id(0) % tiles_per_seq == 0, 0.0, 1.0)
        an_ref[0:_HALO, :] = (_rms(halo_ref[...], nw_ref[...]) * keep).astype(_MXU_DTYPE)
        an_ref[_HALO:, :] = _rms(a_ref[...], nw_ref[...]).astype(_MXU_DTYPE)

    wb = w_ref[...].astype(_MXU_DTYPE)
    for c0 in range(0, o_ref.shape[1], _MXU_WIDTH):
        cols = pl.ds(c0, _MXU_WIDTH)
        acc = lax.dot_general(an_ref[...], wb[c0:c0 + _MXU_WIDTH, :], _NT, preferred_element_type=_F32)
        out = cb_ref[:, cols] + cw_ref[kw - 1:kw, cols] * acc[_HALO:, :]
        for s in range(1, kw):
            out = out + cw_ref[kw - 1 - s:kw - s, cols] * pltpu.roll(acc, s, axis=0)[_HALO:, :]
        o_ref[:, cols] = _silu(out).astype(o_ref.dtype)


def _proj_conv(a, norm_w, w, layer, col0, conv_w, conv_b, seq, out_dtype):
    m, k = a.shape
    kw, n = conv_w.shape
    tm, tn = _matmul_tiles(m, k, math.gcd(n, col0), a.dtype.itemsize, w.dtype.itemsize,
                           jnp.dtype(out_dtype).itemsize, True, False, conv_taps=kw, row_cap=seq)
    assert seq % tm == 0 and tm % _HALO == 0 and tn % _MXU_WIDTH == 0 and kw - 1 <= _HALO
    return pl.pallas_call(
        functools.partial(_proj_conv_kernel, tiles_per_seq=seq // tm),
        grid=(m // tm, n // tn),
        in_specs=[pl.BlockSpec((tm, k), lambda i, j: (i, 0)),
                  pl.BlockSpec((_HALO, k), lambda i, j: (jnp.maximum(i * (tm // _HALO) - 1, 0), 0)),
                  pl.BlockSpec((1, k), lambda i, j: (0, 0)),
                  pl.BlockSpec((None, tn, k), lambda i, j: (layer, col0 // tn + j, 0)),
                  pl.BlockSpec((kw, tn), lambda i, j: (0, j)),
                  pl.BlockSpec((1, tn), lambda i, j: (0, j))],
        out_specs=pl.BlockSpec((tm, tn), lambda i, j: (i, j)),
        out_shape=jax.ShapeDtypeStruct((m, n), out_dtype),
        scratch_shapes=[pltpu.VMEM((tm + _HALO, k), _MXU_DTYPE)],
        compiler_params=_params(("parallel", "arbitrary")),
        name="proj_conv",
    )(a, a, norm_w.reshape(1, k).astype(_F32), w, conv_w, conv_b)


def _softplus(x):
    return jnp.maximum(x, 0.0) + jnp.log1p(jnp.exp(-jnp.abs(x)))


def _chunk_cumsum(x, chunk):
    rows = x.shape[0]
    r = lax.broadcasted_iota(jnp.int32, (chunk, chunk), 0)
    c = lax.broadcasted_iota(jnp.int32, (chunk, chunk), 1)
    tri = jnp.where(c <= r, 1.0, 0.0).astype(_MXU_DTYPE)
    out = []
    for i in range(rows // chunk):
        rem = x[i * chunk:(i + 1) * chunk, :]
        acc = None
        for _ in range(3):
            piece = rem.astype(_MXU_DTYPE)
            d = jnp.dot(tri, piece, preferred_element_type=_F32)
            acc = d if acc is None else acc + d
            rem = rem - piece.astype(_F32)
        out.append(acc)
    return jnp.concatenate(out, axis=0) if len(out) > 1 else out[0]


def _row_bcast(ref, row, n):
    return jnp.broadcast_to(ref[pl.ds(row, 1), :], (n, ref.shape[1]))


def _tail_rows(w_ref, n_valid):
    row = lax.broadcasted_iota(jnp.int32, w_ref.shape, 0)
    return jnp.where(row < n_valid, w_ref[...], 0.0).astype(_MXU_DTYPE)


def _gla_gate_kernel(h_ref, nw_ref, wgr_ref, wup_ref, b_ref, o_ref, *, chunk, rank):
    u = _rms(h_ref[...], nw_ref[...]).astype(_MXU_DTYPE)
    gr = lax.dot_general(u, _tail_rows(wgr_ref, rank), _NT, preferred_element_type=_F32)
    pre = jnp.dot(gr.astype(_MXU_DTYPE), wup_ref[...], preferred_element_type=_F32) + b_ref[...]
    gk = -_softplus(-pre) / _GLA_GATE_NORMALIZER
    o_ref[...] = _chunk_cumsum(gk * _LOG2E, chunk)


def _gla_gate(h, norm_w, wt_in, layer, col0, rank, w_up, b_up, rows=512):
    m, d = h.shape
    kd = w_up.shape[1]
    rows = min(rows, m)
    assert col0 % _LANES == 0 and col0 + rank == wt_in.shape[1] and rank <= _LANES
    return pl.pallas_call(
        functools.partial(_gla_gate_kernel, chunk=_CHUNK, rank=rank),
        grid=(m // rows,),
        in_specs=[pl.BlockSpec((rows, d), lambda i: (i, 0)),
                  pl.BlockSpec((1, d), lambda i: (0, 0)),
                  pl.BlockSpec((None, _LANES, d), lambda i: (layer, col0 // _LANES, 0)),
                  pl.BlockSpec(w_up.shape, lambda i: (0, 0)),
                  pl.BlockSpec((1, kd), lambda i: (0, 0))],
        out_specs=pl.BlockSpec((rows, kd), lambda i: (i, 0)),
        out_shape=jax.ShapeDtypeStruct((m, kd), _F32),
        compiler_params=_params(("parallel",)),
        name="gla_gate",
    )(h, norm_w.reshape(1, d).astype(_F32), wt_in, w_up, b_up)


def _gla_levels(c, nd):
    i = lax.broadcasted_iota(jnp.int32, (c, c), 0)
    j = lax.broadcasted_iota(jnp.int32, (c, c), 1)
    x = i ^ j
    lvl = jnp.zeros((c, c), jnp.int32)
    s, level = nd, 1
    while s < c:
        lvl = jnp.where(x >= s, level, lvl)
        s, level = 2 * s, level + 1
    return jnp.where(j > i, -1, lvl)


def _gla_scan_head(q_ref, k_ref, v_ref, gs_ref, b_ref, onw_ref, lvl_ref, o_ref, st_ref, qf_ref, kf_ref, scale):
    c, dk = q_ref.shape
    qf_ref[...] = q_ref[...].astype(_F32) * scale
    kf_ref[...] = k_ref[...].astype(_F32)
    q = qf_ref[...]
    k = kf_ref[...]
    v = v_ref[...]
    bc = b_ref[...]
    st = st_ref[...]
    b_last = jnp.concatenate([_row_bcast(b_ref, c - 1, 8)] * (c // 8), axis=0)

    o = lax.dot_general((q * jnp.exp2(bc)).astype(_MXU_DTYPE), st.astype(_MXU_DTYPE), _NT,
                        preferred_element_type=_F32)
    k_dec = (k * jnp.exp2(b_last - bc)).astype(_MXU_DTYPE)

    nd = _GLA_DIAG
    lane = lax.broadcasted_iota(jnp.int32, (nd, c), 1)
    pieces = []
    for m in range(c // nd):
        qb = qf_ref[m * nd:(m + 1) * nd, :]
        bb = b_ref[m * nd:(m + 1) * nd, :]
        acc = jnp.zeros((nd, c), _F32)
        for j in range(nd):
            kj = _row_bcast(kf_ref, m * nd + j, nd)
            bj = _row_bcast(b_ref, m * nd + j, nd)
            term = qb * kj * jnp.exp2(jnp.minimum(bb - bj, 0.0))
            acc = jnp.where(lane == m * nd + j, jnp.sum(term, axis=1, keepdims=True), acc)
        pieces.append(acc)
    lvl = lvl_ref[...]
    att = jnp.where(lvl == 0, jnp.concatenate(pieces, axis=0), 0.0)

    s, level = nd, 1
    while s < c:
        prev = [jnp.zeros((s, dk), _F32)]
        end = []
        for m in range(c // s):
            if m:
                prev += [_row_bcast(b_ref, m * s - 1, 8)] * (s // 8)
            end += [_row_bcast(b_ref, (m + 1) * s - 1, 8)] * (s // 8)
        qs = (q * jnp.exp2(bc - jnp.concatenate(prev, axis=0))).astype(_MXU_DTYPE)
        ks = (k * jnp.exp2(jnp.concatenate(end, axis=0) - bc)).astype(_MXU_DTYPE)
        att = jnp.where(lvl == level, lax.dot_general(qs, ks, _NT, preferred_element_type=_F32), att)
        s, level = 2 * s, level + 1

    o = o + jnp.dot(att.astype(_MXU_DTYPE), v, preferred_element_type=_F32)
    st_ref[...] = st * jnp.exp2(b_last[:1, :]) + lax.dot_general(
        v, k_dec, _TN, preferred_element_type=_F32)

    ms = jnp.mean(o * o, axis=-1, keepdims=True)
    on = (o * lax.rsqrt(ms + _RMS_EPS)) * onw_ref[...]
    o_ref[...] = (on * gs_ref[...].astype(_F32)).astype(o_ref.dtype)


def _gla_scan_kernel(q_ref, k_ref, v_ref, gs_ref, b_ref, onw_ref, lvl_ref, o_ref, st_ref, qf_ref, kf_ref,
                     *, scale, dk, dv):
    @pl.when(pl.program_id(2) == 0)
    def _():
        st_ref[...] = jnp.zeros_like(st_ref)

    for h in range(st_ref.shape[0]):
        ks, vs = pl.ds(h * dk, dk), pl.ds(h * dv, dv)
        _gla_scan_head(q_ref.at[:, ks], k_ref.at[:, ks], v_ref.at[:, vs], gs_ref.at[:, vs], b_ref.at[:, ks],
                       onw_ref, lvl_ref, o_ref.at[:, vs], st_ref.at[h], qf_ref.at[h], kf_ref.at[h], scale)


def _gla_scan(proj, bcum, o_norm_w, batch, seq, heads, dk, dv):
    m = proj.shape[0]
    c = _CHUNK
    hb = _GLA_HEADS_PER_STEP if heads % _GLA_HEADS_PER_STEP == 0 else 1
    assert c == _LANES and dk % _LANES == 0 and dv % _LANES == 0 and (2 * heads * dk) % (hb * dv) == 0
    nt = seq // c
    kd, vd = heads * dk, heads * dv
    wk, wv = hb * dk, hb * dv
    k_blk, v_blk, g_blk = kd // wk, (2 * kd) // wv, (2 * kd + vd) // wv

    def rows(b, h, t):
        return b * nt + t

    return pl.pallas_call(
        functools.partial(_gla_scan_kernel, scale=dk ** -0.5, dk=dk, dv=dv),
        grid=(batch, heads // hb, nt),
        in_specs=[pl.BlockSpec((c, wk), lambda b, h, t: (rows(b, h, t), h)),
                  pl.BlockSpec((c, wk), lambda b, h, t: (rows(b, h, t), k_blk + h)),
                  pl.BlockSpec((c, wv), lambda b, h, t: (rows(b, h, t), v_blk + h)),
                  pl.BlockSpec((c, wv), lambda b, h, t: (rows(b, h, t), g_blk + h)),
                  pl.BlockSpec((c, wk), lambda b, h, t: (rows(b, h, t), h)),
                  pl.BlockSpec((1, dv), lambda b, h, t: (0, 0)),
                  pl.BlockSpec((c, c), lambda b, h, t: (0, 0))],
        out_specs=pl.BlockSpec((c, wv), lambda b, h, t: (rows(b, h, t), h)),
        out_shape=jax.ShapeDtypeStruct((m, vd), _MXU_DTYPE),
        scratch_shapes=[pltpu.VMEM((hb, dv, dk), _F32),
                        pltpu.VMEM((hb, c, dk), _F32),
                        pltpu.VMEM((hb, c, dk), _F32)],
        compiler_params=_params(("parallel", "parallel", "arbitrary")),
        name="gla_scan",
    )(proj, proj, proj, proj, bcum, o_norm_w.reshape(1, dv).astype(_F32), _gla_levels(c, _GLA_DIAG))


def _gla_mixer(h, norm_w, batch, seq, layer, wt_in, w_gk_up, b_gk_up, o_norm_w, w_out):
    rank, kd = w_gk_up.shape
    vd = (wt_in.shape[1] - rank - 2 * kd) // 2
    dv = o_norm_w.shape[0]
    heads = vd // dv
    n_main = 2 * kd + 2 * vd
    w_up = jnp.pad(w_gk_up, ((0, _LANES - rank), (0, 0))).astype(_MXU_DTYPE)
    proj = _matmul(h, wt_in, layer, wt=True, n=n_main, norm_w=norm_w, act="silu", act_col0=2 * kd + vd,
                   out_dtype=_MXU_DTYPE)
    bcum = _gla_gate(h, norm_w, wt_in, layer, n_main, rank, w_up, b_gk_up.reshape(1, kd).astype(_F32))
    o = _gla_scan(proj, bcum, o_norm_w, batch, seq, heads, kd // heads, dv)
    return _matmul(o, w_out.astype(_MXU_DTYPE), layer, res=h, out_dtype=_F32)


def _ssd_gate_kernel(h_ref, nw_ref, wdt_ref, bias_ref, alog_ref, acol_ref, dcol_ref, arow_ref, drow_ref,
                     *, chunk, heads, groups):
    u = _rms(h_ref[...], nw_ref[...]).astype(_MXU_DTYPE)
    raw = lax.dot_general(u, _tail_rows(wdt_ref, heads), _NT, preferred_element_type=_F32)
    dt = _softplus(raw + bias_ref[...])
    a = _chunk_cumsum(dt * (-jnp.exp(alog_ref[...])) * _LOG2E, chunk)
    hg = heads // groups
    for g in range(groups):
        acol_ref[g] = a[:, g * hg:(g + 1) * hg]
        dcol_ref[g] = dt[:, g * hg:(g + 1) * hg]
    for i in range(a.shape[0] // chunk):
        arow_ref[i] = a[i * chunk:(i + 1) * chunk, :].T[:heads, :]
        drow_ref[i] = dt[i * chunk:(i + 1) * chunk, :].T[:heads, :]


def _ssd_gate(h, norm_w, wt_in, layer, col0, dt_bias, a_log, heads, rows=512):
    m, d = h.shape
    c = _CHUNK
    rows = min(rows, m)
    g = _SSD_GROUPS
    hg = heads // g
    assert col0 % _LANES == 0 and col0 + heads == wt_in.shape[1] and heads <= _LANES
    col = jax.ShapeDtypeStruct((g, m, hg), _F32)
    row = jax.ShapeDtypeStruct((m // c, heads, c), _F32)
    col_spec = pl.BlockSpec((g, rows, hg), lambda i: (0, i, 0))
    row_spec = pl.BlockSpec((rows // c, heads, c), lambda i: (i, 0, 0))
    return pl.pallas_call(
        functools.partial(_ssd_gate_kernel, chunk=c, heads=heads, groups=g),
        grid=(m // rows,),
        in_specs=[pl.BlockSpec((rows, d), lambda i: (i, 0)),
                  pl.BlockSpec((1, d), lambda i: (0, 0)),
                  pl.BlockSpec((None, _LANES, d), lambda i: (layer, col0 // _LANES, 0)),
                  pl.BlockSpec((1, _LANES), lambda i: (0, 0)),
                  pl.BlockSpec((1, _LANES), lambda i: (0, 0))],
        out_specs=[col_spec, col_spec, row_spec, row_spec],
        out_shape=[col, col, row, row],
        compiler_params=_params(("parallel",)),
        name="ssd_gate",
    )(h, norm_w.reshape(1, d).astype(_F32), wt_in, dt_bias, a_log)


def _expand_heads(colarr, n_heads, head_dim):
    c = colarr.shape[0]
    per = _LANES // head_dim
    lane = lax.broadcasted_iota(jnp.int32, (c, _LANES), 1)
    pieces = []
    for p in range(n_heads // per):
        out = jnp.broadcast_to(colarr[:, p * per:p * per + 1], (c, _LANES))
        for i in range(1, per):
            nxt = jnp.broadcast_to(colarr[:, p * per + i:p * per + i + 1], (c, _LANES))
            out = jnp.where(lane >= i * head_dim, nxt, out)
        pieces.append(out)
    return jnp.concatenate(pieces, axis=1)


def _ssd_scan_group(zs_ref, x_ref, bm_ref, cm_ref, acol_ref, dcol_ref, arow_ref, drow_ref, dskip_ref, gnw_ref,
                    o_ref, st_ref, head_dim):
    c = zs_ref.shape[0]
    hg = acol_ref.shape[1]
    xs = x_ref[...].astype(_F32)
    bmx = bm_ref[...]
    cmx = cm_ref[...]

    acol = acol_ref[...]
    dcol = dcol_ref[...]
    a_last = acol[c - 1:c, :]
    w_state = jnp.exp2(a_last - acol) * dcol

    per = _LANES // head_dim
    width = per * c
    cb = lax.dot_general(cmx, jnp.concatenate([bmx] * per, axis=0), _NT,
                         preferred_element_type=_F32)
    ri = lax.broadcasted_iota(jnp.int32, (c, width), 0)
    ci = lax.broadcasted_iota(jnp.int32, (c, width), 1) & (c - 1)
    causal = ci <= ri
    lane = lax.broadcasted_iota(jnp.int32, (c, _LANES), 1)
    y_pieces = []
    for p in range(hg // per):
        ac = jnp.concatenate(
            [jnp.broadcast_to(acol[:, p * per + i:p * per + i + 1], (c, c)) for i in range(per)], axis=1)
        ar = arow_ref[:, p * width:(p + 1) * width]
        dr = drow_ref[:, p * width:(p + 1) * width]
        decay = jnp.exp2(jnp.where(causal, ac - ar, _NEG_BIG))
        sc = (cb * decay * dr).astype(_MXU_DTYPE)
        xp = x_ref[:, p * _LANES:(p + 1) * _LANES]
        bd = jnp.concatenate(
            [jnp.where((lane >= i * head_dim) & (lane < (i + 1) * head_dim), xp, jnp.zeros_like(xp))
             for i in range(per)], axis=0)
        y_pieces.append(jnp.dot(sc, bd, preferred_element_type=_F32))
    y = jnp.concatenate(y_pieces, axis=1)

    st = st_ref[...]
    ea = jnp.exp2(_expand_heads(acol, hg, head_dim))
    y = y + jnp.dot(cmx, st.astype(_MXU_DTYPE), preferred_element_type=_F32) * ea
    xw = (xs * _expand_heads(w_state, hg, head_dim)).astype(_MXU_DTYPE)
    st_ref[...] = st * ea[c - 1:c, :] + lax.dot_general(bmx, xw, _TN, preferred_element_type=_F32)

    y = y + dskip_ref[...] * xs
    y = y * zs_ref[...].astype(_F32)
    ms = jnp.mean(y * y, axis=-1, keepdims=True)
    o_ref[...] = ((y * lax.rsqrt(ms + _RMS_EPS)) * gnw_ref[...]).astype(o_ref.dtype)


def _ssd_scan_kernel(zs_ref, x_ref, bm_ref, cm_ref, acol_ref, dcol_ref, arow_ref, drow_ref, dskip_ref, gnw_ref,
                     o_ref, st_ref, *, head_dim):
    @pl.when(pl.program_id(2) == 0)
    def _():
        st_ref[...] = jnp.zeros_like(st_ref)

    gb, n, gw = st_ref.shape
    c = zs_ref.shape[0]
    hg = acol_ref.shape[2]
    for g in range(gb):
        xs_, ns_, rs_ = pl.ds(g * gw, gw), pl.ds(g * n, n), pl.ds(g * hg * c, hg * c)
        _ssd_scan_group(zs_ref.at[:, xs_], x_ref.at[:, xs_], bm_ref.at[:, ns_], cm_ref.at[:, ns_],
                        acol_ref.at[g], dcol_ref.at[g], arow_ref.at[0, :, rs_], drow_ref.at[0, :, rs_],
                        dskip_ref.at[:, xs_], gnw_ref.at[:, xs_], o_ref.at[:, xs_], st_ref.at[g], head_dim)


def _ssd_scan(zs, xbc, acol, dcol, arow, drow, dskip, gnorm_w, batch, seq, inner, heads, n_state):
    m = zs.shape[0]
    c = _CHUNK
    g = _SSD_GROUPS
    gb = _SSD_GROUPS_PER_STEP if g % _SSD_GROUPS_PER_STEP == 0 else 1
    hg = heads // g
    p = inner // heads
    gw = hg * p
    sw, sn = gb * gw, gb * n_state
    assert c == _LANES and gw % _LANES == 0 and n_state % _LANES == 0 and _LANES % p == 0
    assert inner % sn == 0 and (g * n_state) % sn == 0
    nt = seq // c
    b_blk = inner // sn
    c_blk = (inner + g * n_state) // sn

    def rows(b, t):
        return b * nt + t

    in_specs = [
        pl.BlockSpec((c, sw), lambda b, gi, t: (rows(b, t), gi)),
        pl.BlockSpec((c, sw), lambda b, gi, t: (rows(b, t), gi)),
        pl.BlockSpec((c, sn), lambda b, gi, t: (rows(b, t), b_blk + gi)),
        pl.BlockSpec((c, sn), lambda b, gi, t: (rows(b, t), c_blk + gi)),
        pl.BlockSpec((gb, c, hg), lambda b, gi, t: (gi, rows(b, t), 0)),
        pl.BlockSpec((gb, c, hg), lambda b, gi, t: (gi, rows(b, t), 0)),
        pl.BlockSpec((1, 1, gb * hg * c), lambda b, gi, t: (rows(b, t), 0, gi)),
        pl.BlockSpec((1, 1, gb * hg * c), lambda b, gi, t: (rows(b, t), 0, gi)),
        pl.BlockSpec((1, sw), lambda b, gi, t: (0, gi)),
        pl.BlockSpec((1, sw), lambda b, gi, t: (0, gi)),
    ]
    return pl.pallas_call(
        functools.partial(_ssd_scan_kernel, head_dim=p),
        grid=(batch, g // gb, nt),
        in_specs=in_specs,
        out_specs=pl.BlockSpec((c, sw), lambda b, gi, t: (rows(b, t), gi)),
        out_shape=jax.ShapeDtypeStruct((m, inner), _MXU_DTYPE),
        scratch_shapes=[pltpu.VMEM((gb, n_state, gw), _F32)],
        compiler_params=_params(("parallel", "parallel", "arbitrary")),
        name="ssd_scan",
    )(zs, xbc, xbc, xbc, acol, dcol, arow, drow, dskip, gnorm_w)


def _ssd_mixer(h, norm_w, batch, seq, layer, wt_in, conv_w, conv_b, dt_bias, a_log, d_skip, gnorm_w, w_out):
    heads = dt_bias.shape[0]
    conv_dim = conv_w.shape[1]
    inner = wt_in.shape[1] - conv_dim - heads
    n_state = (conv_dim - inner) // (2 * _SSD_GROUPS)
    n_main = inner + conv_dim
    m = h.shape[0]
    pad1 = lambda v: jnp.pad(v.astype(_F32), (0, _LANES - heads)).reshape(1, _LANES)
    zs = _matmul(h, wt_in, layer, wt=True, n=inner, norm_w=norm_w, act="silu", out_dtype=_MXU_DTYPE)
    xbc = _proj_conv(h, norm_w, wt_in, layer, inner, conv_w.astype(_F32),
                     conv_b.reshape(1, conv_dim).astype(_F32), seq, _MXU_DTYPE)
    acol, dcol, arow, drow = _ssd_gate(h, norm_w, wt_in, layer, n_main, pad1(dt_bias), pad1(a_log), heads)
    arow = arow.reshape(m // _CHUNK, 1, heads * _CHUNK)
    drow = drow.reshape(m // _CHUNK, 1, heads * _CHUNK)
    dskip = jnp.repeat(d_skip.astype(_F32), inner // heads).reshape(1, inner)
    y = _ssd_scan(zs, xbc, acol, dcol, arow, drow, dskip, gnorm_w.reshape(1, inner).astype(_F32),
                  batch, seq, inner, heads, n_state)
    return _matmul(y, w_out.astype(_MXU_DTYPE), layer, res=h, out_dtype=_F32)


def kernel(x, mixer_norm_w, gla_w_in, gla_w_gk_up, gla_b_gk_up, gla_o_norm_w, gla_w_out, ssd_w_in, ssd_conv_w, ssd_conv_b, ssd_dt_bias, ssd_a_log, ssd_d_skip, ssd_gnorm_w, ssd_w_out, mlp_norm_w, mlp_w_fc1, mlp_w_fc2, final_norm_w):
    batch, seq, d = x.shape
    h = x.reshape(batch * seq, d)
    w_fc2 = mlp_w_fc2.astype(_MXU_DTYPE)
    gla_wt_in = jnp.swapaxes(gla_w_in, 1, 2)
    ssd_wt_in = jnp.swapaxes(ssd_w_in, 1, 2)
    for i in range(mixer_norm_w.shape[0]):
        j = i // 2
        if i % 2 == 0:
            h = _gla_mixer(h, mixer_norm_w[i], batch, seq, j, gla_wt_in, gla_w_gk_up[j], gla_b_gk_up[j],
                           gla_o_norm_w[j], gla_w_out)
        else:
            h = _ssd_mixer(h, mixer_norm_w[i], batch, seq, j, ssd_wt_in, ssd_conv_w[j], ssd_conv_b[j],
                           ssd_dt_bias[j], ssd_a_log[j], ssd_d_skip[j], ssd_gnorm_w[j], ssd_w_out)
        hidden = _matmul(h, mlp_w_fc1, i, norm_w=mlp_norm_w[i], act="relu2", out_dtype=_MXU_DTYPE)
        h = _matmul(hidden, w_fc2, i, res=h, out_dtype=_F32)
    return _rmsnorm(h, final_norm_w, _F32).reshape(batch, seq, d)
```

```python
import functools
import math

import jax
import jax.numpy as jnp
from jax import lax
from jax.experimental import pallas as pl
from jax.experimental.pallas import tpu as pltpu

_F32 = jnp.float32
_MXU_DTYPE = jnp.bfloat16
_RMS_EPS = 1e-5
_GLA_GATE_NORMALIZER = 16.0
_SSD_GROUPS = 8
_CHUNK = 128
_LANES = 128
_GLA_DIAG = 8
_GLA_HEADS_PER_STEP = 2
_SSD_GROUPS_PER_STEP = 2
_LOG2E = math.log2(math.e)
_NEG_BIG = -1e30
_HALO = 16
_MXU_WIDTH = 256
_VMEM_LIMIT_BYTES = 56 * 1024 * 1024
_MATMUL_VMEM_BUDGET = 50 * 1024 * 1024

_NN = (((1,), (0,)), ((), ()))
_NT = (((1,), (1,)), ((), ()))
_TN = (((0,), (0,)), ((), ()))


def _params(sem, flags=None):
    return pltpu.CompilerParams(dimension_semantics=sem, vmem_limit_bytes=_VMEM_LIMIT_BYTES, flags=flags)


def _rms(x, w):
    ms = jnp.mean(x * x, axis=-1, keepdims=True)
    return (x * lax.rsqrt(ms + _RMS_EPS)) * w


def _rmsnorm_kernel(x_ref, w_ref, o_ref):
    o_ref[...] = _rms(x_ref[...].astype(_F32), w_ref[...]).astype(o_ref.dtype)


def _rmsnorm(x, w, out_dtype, rows=512):
    m, d = x.shape
    rows = min(rows, m)
    return pl.pallas_call(
        _rmsnorm_kernel,
        grid=(m // rows,),
        in_specs=[pl.BlockSpec((rows, d), lambda i: (i, 0)),
                  pl.BlockSpec((1, d), lambda i: (0, 0))],
        out_specs=pl.BlockSpec((rows, d), lambda i: (i, 0)),
        out_shape=jax.ShapeDtypeStruct((m, d), out_dtype),
        compiler_params=_params(("parallel",)),
        name="rmsnorm",
    )(x, w.reshape(1, d).astype(_F32))


def _silu(x):
    half = 0.5 * x
    return half + half * jnp.tanh(half)


def _matmul_kernel(*refs, act, act_tile0, has_norm, has_res, wt):
    it = iter(refs)
    a_ref = next(it)
    nw_ref = next(it) if has_norm else None
    w_ref = next(it)
    r_ref = next(it) if has_res else None
    o_ref = next(it)
    if has_norm:
        an_ref = next(it)

        @pl.when(pl.program_id(1) == 0)
        def _():
            an_ref[...] = _rms(a_ref[...], nw_ref[...]).astype(_MXU_DTYPE)

        a = an_ref[...]
    else:
        a = a_ref[...]
    w = w_ref[...].astype(_MXU_DTYPE)
    acc = lax.dot_general(a, w, _NT if wt else _NN, preferred_element_type=_F32)
    if act == "relu2":
        act_acc = jnp.square(jnp.maximum(acc, 0.0))
    elif act == "silu":
        act_acc = _silu(acc)
    if act is not None:
        acc = act_acc if act_tile0 == 0 else jnp.where(pl.program_id(1) >= act_tile0, act_acc, acc)
    if has_res:
        acc = r_ref[...] + acc
    o_ref[...] = acc.astype(o_ref.dtype)


def _divisor_tile(n, pref, align):
    if n <= pref:
        return n
    t = (pref // align) * align
    while n % t:
        t -= align
    return t


def _matmul_tiles(m, k, n, a_bytes, w_bytes, out_bytes, has_norm, has_res, conv_taps=0, row_cap=None):
    mxu_bytes = jnp.dtype(_MXU_DTYPE).itemsize
    for pm, pn in ((1024, 1024), (1024, 512), (1024, 256), (512, 256), (256, 256), (128, 128)):
        pm = pm if row_cap is None else min(pm, row_cap)
        tm, tn = _divisor_tile(m, pm, 8), _divisor_tile(n, pn, _LANES)
        need = 2 * tm * k * a_bytes + 2 * k * tn * w_bytes + 2 * tm * tn * out_bytes
        need += tm * tn * 4 if not conv_taps else (1 + conv_taps) * tm * _MXU_WIDTH * 4
        need += 2 * tm * tn * 4 if has_res else 0
        need += tm * k * mxu_bytes if has_norm else 0
        need += k * tn * mxu_bytes if w_bytes != mxu_bytes else 0
        if need <= _MATMUL_VMEM_BUDGET:
            break
    return tm, tn


def _matmul(a, w, layer, *, wt=False, n=None, norm_w=None, act=None, act_col0=0, res=None, out_dtype):
    m, k = a.shape
    n = w.shape[1 if wt else 2] if n is None else n
    tm, tn = _matmul_tiles(m, k, math.gcd(n, act_col0), a.dtype.itemsize, w.dtype.itemsize,
                           jnp.dtype(out_dtype).itemsize, norm_w is not None, res is not None)
    in_specs = [pl.BlockSpec((tm, k), lambda i, j: (i, 0))]
    args = [a]
    if norm_w is not None:
        in_specs.append(pl.BlockSpec((1, k), lambda i, j: (0, 0)))
        args.append(norm_w.reshape(1, k).astype(_F32))
    in_specs.append(pl.BlockSpec((None, tn, k), lambda i, j: (layer, j, 0)) if wt else
                    pl.BlockSpec((None, k, tn), lambda i, j: (layer, 0, j)))
    args.append(w)
    if res is not None:
        in_specs.append(pl.BlockSpec((tm, tn), lambda i, j: (i, j)))
        args.append(res)
    return pl.pallas_call(
        functools.partial(_matmul_kernel, act=act, act_tile0=act_col0 // tn, has_norm=norm_w is not None,
                          has_res=res is not None, wt=wt),
        grid=(m // tm, n // tn),
        in_specs=in_specs,
        out_specs=pl.BlockSpec((tm, tn), lambda i, j: (i, j)),
        out_shape=jax.ShapeDtypeStruct((m, n), out_dtype),
        scratch_shapes=[pltpu.VMEM((tm, k), _MXU_DTYPE)] if norm_w is not None else [],
        compiler_params=_params(("parallel", "arbitrary")),
        name="matmul",
    )(*args)


def _proj_conv_kernel(a_ref, halo_ref, nw_ref, w_ref, cw_ref, cb_ref, o_ref, an_ref, *, tiles_per_seq):
    kw = cw_ref.shape[0]

    @pl.when(pl.program_id(1) == 0)
    def _():
        keep = jnp.where(pl.program_id(0) % tiles_per_seq == 0, 0.0, 1.0)
        an_ref[0:_HALO, :] = (_rms(halo_ref[...], nw_ref[...]) * keep).astype(_MXU_DTYPE)
        an_ref[_HALO:, :] = _rms(a_ref[...], nw_ref[...]).astype(_MXU_DTYPE)

    wb = w_ref[...].astype(_MXU_DTYPE)
    for c0 in range(0, o_ref.shape[1], _MXU_WIDTH):
        cols = pl.ds(c0, _MXU_WIDTH)
        acc = lax.dot_general(an_ref[...], wb[c0:c0 + _MXU_WIDTH, :], _NT, preferred_element_type=_F32)
        out = cb_ref[:, cols] + cw_ref[kw - 1:kw, cols] * acc[_HALO:, :]
        for s in range(1, kw):
            out = out + cw_ref[kw - 1 - s:kw - s, cols] * pltpu.roll(acc, s, axis=0)[_HALO:, :]
        o_ref[:, cols] = _silu(out).astype(o_ref.dtype)


def _proj_conv(a, norm_w, w, layer, col0, conv_w, conv_b, seq, out_dtype):
    m, k = a.shape
    kw, n = conv_w.shape
    tm, tn = _matmul_tiles(m, k, math.gcd(n, col0), a.dtype.itemsize, w.dtype.itemsize,
                           jnp.dtype(out_dtype).itemsize, True, False, conv_taps=kw, row_cap=seq)
    assert seq % tm == 0 and tm % _HALO == 0 and tn % _MXU_WIDTH == 0 and kw - 1 <= _HALO
    return pl.pallas_call(
        functools.partial(_proj_conv_kernel, tiles_per_seq=seq // tm),
        grid=(m // tm, n // tn),
        in_specs=[pl.BlockSpec((tm, k), lambda i, j: (i, 0)),
                  pl.BlockSpec((_HALO, k), lambda i, j: (jnp.maximum(i * (tm // _HALO) - 1, 0), 0)),
                  pl.BlockSpec((1, k), lambda i, j: (0, 0)),
                  pl.BlockSpec((None, tn, k), lambda i, j: (layer, col0 // tn + j, 0)),
                  pl.BlockSpec((kw, tn), lambda i, j: (0, j)),
                  pl.BlockSpec((1, tn), lambda i, j: (0, j))],
        out_specs=pl.BlockSpec((tm, tn), lambda i, j: (i, j)),
        out_shape=jax.ShapeDtypeStruct((m, n), out_dtype),
        scratch_shapes=[pltpu.VMEM((tm + _HALO, k), _MXU_DTYPE)],
        compiler_params=_params(("parallel", "arbitrary")),
        name="proj_conv",
    )(a, a, norm_w.reshape(1, k).astype(_F32), w, conv_w, conv_b)


def _softplus(x):
    return jnp.maximum(x, 0.0) + jnp.log(1.0 + jnp.exp(-jnp.abs(x)))


def _chunk_cumsum(x, chunk):
    rows = x.shape[0]
    r = lax.broadcasted_iota(jnp.int32, (chunk, chunk), 0)
    c = lax.broadcasted_iota(jnp.int32, (chunk, chunk), 1)
    tri = jnp.where(c <= r, 1.0, 0.0).astype(_MXU_DTYPE)
    out = []
    for i in range(rows // chunk):
        rem = x[i * chunk:(i + 1) * chunk, :]
        acc = None
        for _ in range(3):
            piece = rem.astype(_MXU_DTYPE)
            d = jnp.dot(tri, piece, preferred_element_type=_F32)
            acc = d if acc is None else acc + d
            rem = rem - piece.astype(_F32)
        out.append(acc)
    return jnp.concatenate(out, axis=0) if len(out) > 1 else out[0]


def _row_bcast(ref, row, n):
    return jnp.broadcast_to(ref[pl.ds(row, 1), :], (n, ref.shape[1]))


def _tail_rows(w_ref, n_valid):
    row = lax.broadcasted_iota(jnp.int32, w_ref.shape, 0)
    return jnp.where(row < n_valid, w_ref[...], 0.0).astype(_MXU_DTYPE)


def _gla_gate_kernel(h_ref, nw_ref, wgr_ref, wup_ref, b_ref, o_ref, *, chunk, rank):
    u = _rms(h_ref[...], nw_ref[...]).astype(_MXU_DTYPE)
    gr = lax.dot_general(u, _tail_rows(wgr_ref, rank), _NT, preferred_element_type=_F32)
    pre = jnp.dot(gr.astype(_MXU_DTYPE), wup_ref[...], preferred_element_type=_F32) + b_ref[...]
    gk = -_softplus(-pre) / _GLA_GATE_NORMALIZER
    o_ref[...] = _chunk_cumsum(gk * _LOG2E, chunk)


def _gla_gate(h, norm_w, wt_in, layer, col0, rank, w_up, b_up, rows=512):
    m, d = h.shape
    kd = w_up.shape[1]
    rows = min(rows, m)
    assert col0 % _LANES == 0 and col0 + rank == wt_in.shape[1] and rank <= _LANES
    return pl.pallas_call(
        functools.partial(_gla_gate_kernel, chunk=_CHUNK, rank=rank),
        grid=(m // rows,),
        in_specs=[pl.BlockSpec((rows, d), lambda i: (i, 0)),
                  pl.BlockSpec((1, d), lambda i: (0, 0)),
                  pl.BlockSpec((None, _LANES, d), lambda i: (layer, col0 // _LANES, 0)),
                  pl.BlockSpec(w_up.shape, lambda i: (0, 0)),
                  pl.BlockSpec((1, kd), lambda i: (0, 0))],
        out_specs=pl.BlockSpec((rows, kd), lambda i: (i, 0)),
        out_shape=jax.ShapeDtypeStruct((m, kd), _F32),
        compiler_params=_params(("parallel",)),
        name="gla_gate",
    )(h, norm_w.reshape(1, d).astype(_F32), wt_in, w_up, b_up)


def _gla_levels(c, nd):
    i = lax.broadcasted_iota(jnp.int32, (c, c), 0)
    j = lax.broadcasted_iota(jnp.int32, (c, c), 1)
    x = i ^ j
    lvl = jnp.zeros((c, c), jnp.int32)
    s, level = nd, 1
    while s < c:
        lvl = jnp.where(x >= s, level, lvl)
        s, level = 2 * s, level + 1
    return jnp.where(j > i, -1, lvl)


def _gla_scan_head(q_ref, k_ref, v_ref, gs_ref, b_ref, onw_ref, lvl_ref, o_ref, st_ref, qf_ref, kf_ref, scale):
    c, dk = q_ref.shape
    qf_ref[...] = q_ref[...].astype(_F32) * scale
    kf_ref[...] = k_ref[...].astype(_F32)
    q = qf_ref[...]
    k = kf_ref[...]
    v = v_ref[...]
    bc = b_ref[...]
    st = st_ref[...]
    b_last = jnp.concatenate([_row_bcast(b_ref, c - 1, 8)] * (c // 8), axis=0)

    o = lax.dot_general((q * jnp.exp2(bc)).astype(_MXU_DTYPE), st.astype(_MXU_DTYPE), _NT,
                        preferred_element_type=_F32)
    k_dec = (k * jnp.exp2(b_last - bc)).astype(_MXU_DTYPE)

    nd = _GLA_DIAG
    lane = lax.broadcasted_iota(jnp.int32, (nd, c), 1)
    pieces = []
    for m in range(c // nd):
        qb = qf_ref[m * nd:(m + 1) * nd, :]
        bb = b_ref[m * nd:(m + 1) * nd, :]
        acc = jnp.zeros((nd, c), _F32)
        for j in range(nd):
            kj = _row_bcast(kf_ref, m * nd + j, nd)
            bj = _row_bcast(b_ref, m * nd + j, nd)
            term = qb * kj * jnp.exp2(jnp.minimum(bb - bj, 0.0))
            acc = jnp.where(lane == m * nd + j, jnp.sum(term, axis=1, keepdims=True), acc)
        pieces.append(acc)
    lvl = lvl_ref[...]
    att = jnp.where(lvl == 0, jnp.concatenate(pieces, axis=0), 0.0)

    s, level = nd, 1
    while s < c:
        prev = [jnp.zeros((s, dk), _F32)]
        end = []
        for m in range(c // s):
            if m:
                prev += [_row_bcast(b_ref, m * s - 1, 8)] * (s // 8)
            end += [_row_bcast(b_ref, (m + 1) * s - 1, 8)] * (s // 8)
        qs = (q * jnp.exp2(bc - jnp.concatenate(prev, axis=0))).astype(_MXU_DTYPE)
        ks = (k * jnp.exp2(jnp.concatenate(end, axis=0) - bc)).astype(_MXU_DTYPE)
        att = jnp.where(lvl == level, lax.dot_general(qs, ks, _NT, preferred_element_type=_F32), att)
        s, level = 2 * s, level + 1

    o = o + jnp.dot(att.astype(_MXU_DTYPE), v, preferred_element_type=_F32)
    st_ref[...] = st * jnp.exp2(b_last[:1, :]) + lax.dot_general(
        v, k_dec, _TN, preferred_element_type=_F32)

    ms = jnp.mean(o * o, axis=-1, keepdims=True)
    on = (o * lax.rsqrt(ms + _RMS_EPS)) * onw_ref[...]
    o_ref[...] = (on * gs_ref[...].astype(_F32)).astype(o_ref.dtype)


def _gla_scan_kernel(q_ref, k_ref, v_ref, gs_ref, b_ref, onw_ref, lvl_ref, o_ref, st_ref, qf_ref, kf_ref,
                     *, scale, dk, dv):
    @pl.when(pl.program_id(2) == 0)
    def _():
        st_ref[...] = jnp.zeros_like(st_ref)

    for h in range(st_ref.shape[0]):
        ks, vs = pl.ds(h * dk, dk), pl.ds(h * dv, dv)
        _gla_scan_head(q_ref.at[:, ks], k_ref.at[:, ks], v_ref.at[:, vs], gs_ref.at[:, vs], b_ref.at[:, ks],
                       onw_ref, lvl_ref, o_ref.at[:, vs], st_ref.at[h], qf_ref.at[h], kf_ref.at[h], scale)


def _gla_scan(proj, bcum, o_norm_w, batch, seq, heads, dk, dv):
    m = proj.shape[0]
    c = _CHUNK
    hb = _GLA_HEADS_PER_STEP if heads % _GLA_HEADS_PER_STEP == 0 else 1
    assert c == _LANES and dk % _LANES == 0 and dv % _LANES == 0 and (2 * heads * dk) % (hb * dv) == 0
    nt = seq // c
    kd, vd = heads * dk, heads * dv
    wk, wv = hb * dk, hb * dv
    k_blk, v_blk, g_blk = kd // wk, (2 * kd) // wv, (2 * kd + vd) // wv

    def rows(b, h, t):
        return b * nt + t

    return pl.pallas_call(
        functools.partial(_gla_scan_kernel, scale=dk ** -0.5, dk=dk, dv=dv),
        grid=(batch, heads // hb, nt),
        in_specs=[pl.BlockSpec((c, wk), lambda b, h, t: (rows(b, h, t), h)),
                  pl.BlockSpec((c, wk), lambda b, h, t: (rows(b, h, t), k_blk + h)),
                  pl.BlockSpec((c, wv), lambda b, h, t: (rows(b, h, t), v_blk + h)),
                  pl.BlockSpec((c, wv), lambda b, h, t: (rows(b, h, t), g_blk + h)),
                  pl.BlockSpec((c, wk), lambda b, h, t: (rows(b, h, t), h)),
                  pl.BlockSpec((1, dv), lambda b, h, t: (0, 0)),
                  pl.BlockSpec((c, c), lambda b, h, t: (0, 0))],
        out_specs=pl.BlockSpec((c, wv), lambda b, h, t: (rows(b, h, t), h)),
        out_shape=jax.ShapeDtypeStruct((m, vd), _MXU_DTYPE),
        scratch_shapes=[pltpu.VMEM((hb, dv, dk), _F32),
                        pltpu.VMEM((hb, c, dk), _F32),
                        pltpu.VMEM((hb, c, dk), _F32)],
        compiler_params=_params(("parallel", "parallel", "arbitrary")),
        name="gla_scan",
    )(proj, proj, proj, proj, bcum, o_norm_w.reshape(1, dv).astype(_F32), _gla_levels(c, _GLA_DIAG))


def _gla_mixer(h, norm_w, batch, seq, layer, wt_in, w_gk_up, b_gk_up, o_norm_w, w_out):
    rank, kd = w_gk_up.shape
    vd = (wt_in.shape[1] - rank - 2 * kd) // 2
    dv = o_norm_w.shape[0]
    heads = vd // dv
    n_main = 2 * kd + 2 * vd
    w_up = jnp.pad(w_gk_up, ((0, _LANES - rank), (0, 0))).astype(_MXU_DTYPE)
    proj = _matmul(h, wt_in, layer, wt=True, n=n_main, norm_w=norm_w, act="silu", act_col0=2 * kd + vd,
                   out_dtype=_MXU_DTYPE)
    bcum = _gla_gate(h, norm_w, wt_in, layer, n_main, rank, w_up, b_gk_up.reshape(1, kd).astype(_F32))
    o = _gla_scan(proj, bcum, o_norm_w, batch, seq, heads, kd // heads, dv)
    return _matmul(o, w_out.astype(_MXU_DTYPE), layer, res=h, out_dtype=_F32)


def _ssd_gate_kernel(h_ref, nw_ref, wdt_ref, bias_ref, alog_ref, acol_ref, dcol_ref, arow_ref, drow_ref,
                     *, chunk, heads, groups):
    u = _rms(h_ref[...], nw_ref[...]).astype(_MXU_DTYPE)
    raw = lax.dot_general(u, _tail_rows(wdt_ref, heads), _NT, preferred_element_type=_F32)
    dt = _softplus(raw + bias_ref[...])
    a = _chunk_cumsum(dt * (-jnp.exp(alog_ref[...])) * _LOG2E, chunk)
    hg = heads // groups
    for g in range(groups):
        acol_ref[g] = a[:, g * hg:(g + 1) * hg]
        dcol_ref[g] = dt[:, g * hg:(g + 1) * hg]
    for i in range(a.shape[0] // chunk):
        arow_ref[i] = a[i * chunk:(i + 1) * chunk, :].T[:heads, :]
        drow_ref[i] = dt[i * chunk:(i + 1) * chunk, :].T[:heads, :]


def _ssd_gate(h, norm_w, wt_in, layer, col0, dt_bias, a_log, heads, rows=512):
    m, d = h.shape
    c = _CHUNK
    rows = min(rows, m)
    g = _SSD_GROUPS
    hg = heads // g
    assert col0 % _LANES == 0 and col0 + heads == wt_in.shape[1] and heads <= _LANES
    col = jax.ShapeDtypeStruct((g, m, hg), _F32)
    row = jax.ShapeDtypeStruct((m // c, heads, c), _F32)
    col_spec = pl.BlockSpec((g, rows, hg), lambda i: (0, i, 0))
    row_spec = pl.BlockSpec((rows // c, heads, c), lambda i: (i, 0, 0))
    return pl.pallas_call(
        functools.partial(_ssd_gate_kernel, chunk=c, heads=heads, groups=g),
        grid=(m // rows,),
        in_specs=[pl.BlockSpec((rows, d), lambda i: (i, 0)),
                  pl.BlockSpec((1, d), lambda i: (0, 0)),
                  pl.BlockSpec((None, _LANES, d), lambda i: (layer, col0 // _LANES, 0)),
                  pl.BlockSpec((1, _LANES), lambda i: (0, 0)),
                  pl.BlockSpec((1, _LANES), lambda i: (0, 0))],
        out_specs=[col_spec, col_spec, row_spec, row_spec],
        out_shape=[col, col, row, row],
        compiler_params=_params(("parallel",)),
        name="ssd_gate",
    )(h, norm_w.reshape(1, d).astype(_F32), wt_in, dt_bias, a_log)


def _expand_heads(colarr, n_heads, head_dim):
    c = colarr.shape[0]
    per = _LANES // head_dim
    lane = lax.broadcasted_iota(jnp.int32, (c, _LANES), 1)
    pieces = []
    for p in range(n_heads // per):
        out = jnp.broadcast_to(colarr[:, p * per:p * per + 1], (c, _LANES))
        for i in range(1, per):
            nxt = jnp.broadcast_to(colarr[:, p * per + i:p * per + i + 1], (c, _LANES))
            out = jnp.where(lane >= i * head_dim, nxt, out)
        pieces.append(out)
    return jnp.concatenate(pieces, axis=1)


def _ssd_scan_group(zs_ref, x_ref, bm_ref, cm_ref, acol_ref, dcol_ref, arow_ref, drow_ref, dskip_ref, gnw_ref,
                    o_ref, st_ref, head_dim):
    c = zs_ref.shape[0]
    hg = acol_ref.shape[1]
    xs = x_ref[...].astype(_F32)
    bmx = bm_ref[...]
    cmx = cm_ref[...]

    acol = acol_ref[...]
    dcol = dcol_ref[...]
    a_last = acol[c - 1:c, :]
    w_state = jnp.exp2(a_last - acol) * dcol

    per = _LANES // head_dim
    width = per * c
    cb = lax.dot_general(cmx, jnp.concatenate([bmx] * per, axis=0), _NT,
                         preferred_element_type=_F32)
    ri = lax.broadcasted_iota(jnp.int32, (c, width), 0)
    ci = lax.broadcasted_iota(jnp.int32, (c, width), 1) & (c - 1)
    causal = ci <= ri
    lane = lax.broadcasted_iota(jnp.int32, (c, _LANES), 1)
    y_pieces = []
    for p in range(hg // per):
        ac = jnp.concatenate(
            [jnp.broadcast_to(acol[:, p * per + i:p * per + i + 1], (c, c)) for i in range(per)], axis=1)
        ar = arow_ref[:, p * width:(p + 1) * width]
        dr = drow_ref[:, p * width:(p + 1) * width]
        decay = jnp.exp2(jnp.where(causal, ac - ar, _NEG_BIG))
        sc = (cb * decay * dr).astype(_MXU_DTYPE)
        xp = x_ref[:, p * _LANES:(p + 1) * _LANES]
        bd = jnp.concatenate(
            [jnp.where((lane >= i * head_dim) & (lane < (i + 1) * head_dim), xp, jnp.zeros_like(xp))
             for i in range(per)], axis=0)
        y_pieces.append(jnp.dot(sc, bd, preferred_element_type=_F32))
    y = jnp.concatenate(y_pieces, axis=1)

    st = st_ref[...]
    ea = jnp.exp2(_expand_heads(acol, hg, head_dim))
    y = y + jnp.dot(cmx, st.astype(_MXU_DTYPE), preferred_element_type=_F32) * ea
    xw = (xs * _expand_heads(w_state, hg, head_dim)).astype(_MXU_DTYPE)
    st_ref[...] = st * ea[c - 1:c, :] + lax.dot_general(bmx, xw, _TN, preferred_element_type=_F32)

    y = y + dskip_ref[...] * xs
    y = y * zs_ref[...].astype(_F32)
    ms = jnp.mean(y * y, axis=-1, keepdims=True)
    o_ref[...] = ((y * lax.rsqrt(ms + _RMS_EPS)) * gnw_ref[...]).astype(o_ref.dtype)


def _ssd_scan_kernel(zs_ref, x_ref, bm_ref, cm_ref, acol_ref, dcol_ref, arow_ref, drow_ref, dskip_ref, gnw_ref,
                     o_ref, st_ref, *, head_dim):
    @pl.when(pl.program_id(2) == 0)
    def _():
        st_ref[...] = jnp.zeros_like(st_ref)

    gb, n, gw = st_ref.shape
    c = zs_ref.shape[0]
    hg = acol_ref.shape[2]
    for g in range(gb):
        xs_, ns_, rs_ = pl.ds(g * gw, gw), pl.ds(g * n, n), pl.ds(g * hg * c, hg * c)
        _ssd_scan_group(zs_ref.at[:, xs_], x_ref.at[:, xs_], bm_ref.at[:, ns_], cm_ref.at[:, ns_],
                        acol_ref.at[g], dcol_ref.at[g], arow_ref.at[0, :, rs_], drow_ref.at[0, :, rs_],
                        dskip_ref.at[:, xs_], gnw_ref.at[:, xs_], o_ref.at[:, xs_], st_ref.at[g], head_dim)


def _ssd_scan(zs, xbc, acol, dcol, arow, drow, dskip, gnorm_w, batch, seq, inner, heads, n_state):
    m = zs.shape[0]
    c = _CHUNK
    g = _SSD_GROUPS
    gb = _SSD_GROUPS_PER_STEP if g % _SSD_GROUPS_PER_STEP == 0 else 1
    hg = heads // g
    p = inner // heads
    gw = hg * p
    sw, sn = gb * gw, gb * n_state
    assert c == _LANES and gw % _LANES == 0 and n_state % _LANES == 0 and _LANES % p == 0
    assert inner % sn == 0 and (g * n_state) % sn == 0
    nt = seq // c
    b_blk = inner // sn
    c_blk = (inner + g * n_state) // sn

    def rows(b, t):
        return b * nt + t

    in_specs = [
        pl.BlockSpec((c, sw), lambda b, gi, t: (rows(b, t), gi)),
        pl.BlockSpec((c, sw), lambda b, gi, t: (rows(b, t), gi)),
        pl.BlockSpec((c, sn), lambda b, gi, t: (rows(b, t), b_blk + gi)),
        pl.BlockSpec((c, sn), lambda b, gi, t: (rows(b, t), c_blk + gi)),
        pl.BlockSpec((gb, c, hg), lambda b, gi, t: (gi, rows(b, t), 0)),
        pl.BlockSpec((gb, c, hg), lambda b, gi, t: (gi, rows(b, t), 0)),
        pl.BlockSpec((1, 1, gb * hg * c), lambda b, gi, t: (rows(b, t), 0, gi)),
        pl.BlockSpec((1, 1, gb * hg * c), lambda b, gi, t: (rows(b, t), 0, gi)),
        pl.BlockSpec((1, sw), lambda b, gi, t: (0, gi)),
        pl.BlockSpec((1, sw), lambda b, gi, t: (0, gi)),
    ]
    return pl.pallas_call(
        functools.partial(_ssd_scan_kernel, head_dim=p),
        grid=(batch, g // gb, nt),
        in_specs=in_specs,
        out_specs=pl.BlockSpec((c, sw), lambda b, gi, t: (rows(b, t), gi)),
        out_shape=jax.ShapeDtypeStruct((m, inner), _MXU_DTYPE),
        scratch_shapes=[pltpu.VMEM((gb, n_state, gw), _F32)],
        compiler_params=_params(("parallel", "parallel", "arbitrary")),
        name="ssd_scan",
    )(zs, xbc, xbc, xbc, acol, dcol, arow, drow, dskip, gnorm_w)


def _ssd_mixer(h, norm_w, batch, seq, layer, wt_in, conv_w, conv_b, dt_bias, a_log, d_skip, gnorm_w, w_out):
    heads = dt_bias.shape[0]
    conv_dim = conv_w.shape[1]
    inner = wt_in.shape[1] - conv_dim - heads
    n_state = (conv_dim - inner) // (2 * _SSD_GROUPS)
    n_main = inner + conv_dim
    m = h.shape[0]
    pad1 = lambda v: jnp.pad(v.astype(_F32), (0, _LANES - heads)).reshape(1, _LANES)
    zs = _matmul(h, wt_in, layer, wt=True, n=inner, norm_w=norm_w, act="silu", out_dtype=_MXU_DTYPE)
    xbc = _proj_conv(h, norm_w, wt_in, layer, inner, conv_w.astype(_F32),
                     conv_b.reshape(1, conv_dim).astype(_F32), seq, _MXU_DTYPE)
    acol, dcol, arow, drow = _ssd_gate(h, norm_w, wt_in, layer, n_main, pad1(dt_bias), pad1(a_log), heads)
    arow = arow.reshape(m // _CHUNK, 1, heads * _CHUNK)
    drow = drow.reshape(m // _CHUNK, 1, heads * _CHUNK)
    dskip = jnp.repeat(d_skip.astype(_F32), inner // heads).reshape(1, inner)
    y = _ssd_scan(zs, xbc, acol, dcol, arow, drow, dskip, gnorm_w.reshape(1, inner).astype(_F32),
                  batch, seq, inner, heads, n_state)
    return _matmul(y, w_out.astype(_MXU_DTYPE), layer, res=h, out_dtype=_F32)


def kernel(x, mixer_norm_w, gla_w_in, gla_w_gk_up, gla_b_gk_up, gla_o_norm_w, gla_w_out, ssd_w_in, ssd_conv_w, ssd_conv_b, ssd_dt_bias, ssd_a_log, ssd_d_skip, ssd_gnorm_w, ssd_w_out, mlp_norm_w, mlp_w_fc1, mlp_w_fc2, final_norm_w):
    batch, seq, d = x.shape
    h = x.reshape(batch * seq, d)
    w_fc2 = mlp_w_fc2.astype(_MXU_DTYPE)
    gla_wt_in = jnp.swapaxes(gla_w_in, 1, 2)
    ssd_wt_in = jnp.swapaxes(ssd_w_in, 1, 2)
    for i in range(mixer_norm_w.shape[0]):
        j = i // 2
        if i % 2 == 0:
            h = _gla_mixer(h, mixer_norm_w[i], batch, seq, j, gla_wt_in, gla_w_gk_up[j], gla_b_gk_up[j],
                           gla_o_norm_w[j], gla_w_out)
        else:
            h = _ssd_mixer(h, mixer_norm_w[i], batch, seq, j, ssd_wt_in, ssd_conv_w[j], ssd_conv_b[j],
                           ssd_dt_bias[j], ssd_a_log[j], ssd_d_skip[j], ssd_gnorm_w[j], ssd_w_out)
        hidden = _matmul(h, mlp_w_fc1, i, norm_w=mlp_norm_w[i], act="relu2", out_dtype=_MXU_DTYPE)
        h = _matmul(hidden, w_fc2, i, res=h, out_dtype=_F32)
    return _rmsnorm(h, final_norm_w, _F32).reshape(batch, seq, d)
```

```python
import functools
import math

import jax
import jax.numpy as jnp
from jax import lax
from jax.experimental import pallas as pl
from jax.experimental.pallas import tpu as pltpu

_F32 = jnp.float32
_MXU_DTYPE = jnp.bfloat16
_RMS_EPS = 1e-5
_GLA_GATE_NORMALIZER = 16.0
_SSD_GROUPS = 8
_CHUNK = 128
_LANES = 128
_GLA_DIAG = 8
_GLA_HEADS_PER_STEP = 4
_SSD_GROUPS_PER_STEP = 4
_LOG2E = math.log2(math.e)
_NEG_BIG = -1e30
_HALO = 16
_MXU_WIDTH = 256
_VMEM_LIMIT_BYTES = 56 * 1024 * 1024
_MATMUL_VMEM_BUDGET = 52 * 1024 * 1024

_NN = (((1,), (0,)), ((), ()))
_NT = (((1,), (1,)), ((), ()))
_TN = (((0,), (0,)), ((), ()))


def _params(sem, flags=None):
    return pltpu.CompilerParams(dimension_semantics=sem, vmem_limit_bytes=_VMEM_LIMIT_BYTES, flags=flags)


def _rms(x, w):
    ms = jnp.mean(x * x, axis=-1, keepdims=True)
    return (x * lax.rsqrt(ms + _RMS_EPS)) * w


def _rmsnorm_kernel(x_ref, w_ref, o_ref):
    o_ref[...] = _rms(x_ref[...].astype(_F32), w_ref[...]).astype(o_ref.dtype)


def _rmsnorm(x, w, out_dtype, rows=512):
    m, d = x.shape
    rows = min(rows, m)
    return pl.pallas_call(
        _rmsnorm_kernel,
        grid=(m // rows,),
        in_specs=[pl.BlockSpec((rows, d), lambda i: (i, 0)),
                  pl.BlockSpec((1, d), lambda i: (0, 0))],
        out_specs=pl.BlockSpec((rows, d), lambda i: (i, 0)),
        out_shape=jax.ShapeDtypeStruct((m, d), out_dtype),
        compiler_params=_params(("parallel",)),
        name="rmsnorm",
    )(x, w.reshape(1, d).astype(_F32))


def _silu(x):
    half = 0.5 * x
    return half + half * jnp.tanh(half)


def _matmul_kernel(*refs, act, act_tile0, has_norm, has_res, wt):
    it = iter(refs)
    a_ref = next(it)
    nw_ref = next(it) if has_norm else None
    w_ref = next(it)
    r_ref = next(it) if has_res else None
    o_ref = next(it)
    if has_norm:
        an_ref = next(it)

        @pl.when(pl.program_id(1) == 0)
        def _():
            an_ref[...] = _rms(a_ref[...], nw_ref[...]).astype(_MXU_DTYPE)

        a = an_ref[...]
    else:
        a = a_ref[...]
    w = w_ref[...].astype(_MXU_DTYPE)
    acc = lax.dot_general(a, w, _NT if wt else _NN, preferred_element_type=_F32)
    if act == "relu2":
        act_acc = jnp.square(jnp.maximum(acc, 0.0))
    elif act == "silu":
        act_acc = _silu(acc)
    if act is not None:
        acc = act_acc if act_tile0 == 0 else jnp.where(pl.program_id(1) >= act_tile0, act_acc, acc)
    if has_res:
        acc = r_ref[...] + acc
    o_ref[...] = acc.astype(o_ref.dtype)


def _divisor_tile(n, pref, align):
    if n <= pref:
        return n
    t = (pref // align) * align
    while n % t:
        t -= align
    return t


def _matmul_tiles(m, k, n, a_bytes, w_bytes, out_bytes, norm_copies, has_res, conv_taps=0, row_cap=None):
    mxu_bytes = jnp.dtype(_MXU_DTYPE).itemsize
    for pm, pn in ((1024, 1024), (1024, 512), (1024, 256), (512, 256), (256, 256), (128, 128)):
        pm = pm if row_cap is None else min(pm, row_cap)
        tm, tn = _divisor_tile(m, pm, 8), _divisor_tile(n, pn, _LANES)
        need = 2 * tm * k * a_bytes + 2 * k * tn * w_bytes + 2 * tm * tn * out_bytes
        need += tm * tn * 4 if not conv_taps else (1 + conv_taps) * tm * _MXU_WIDTH * 4
        need += 2 * tm * tn * 4 if has_res else 0
        need += norm_copies * tm * k * mxu_bytes
        need += k * tn * mxu_bytes if w_bytes != mxu_bytes else 0
        if need <= _MATMUL_VMEM_BUDGET:
            break
    return tm, tn


def _matmul(a, w, layer, *, wt=False, n=None, norm_w=None, emit_norm=False, act=None, act_col0=0, res=None,
            out_dtype):
    m, k = a.shape
    n = w.shape[1 if wt else 2] if n is None else n
    tm, tn = _matmul_tiles(m, k, math.gcd(n, act_col0), a.dtype.itemsize, w.dtype.itemsize,
                           jnp.dtype(out_dtype).itemsize, (norm_w is not None) + emit_norm, res is not None)
    in_specs = [pl.BlockSpec((tm, k), lambda i, j: (i, 0))]
    args = [a]
    if norm_w is not None:
        in_specs.append(pl.BlockSpec((1, k), lambda i, j: (0, 0)))
        args.append(norm_w.reshape(1, k).astype(_F32))
    in_specs.append(pl.BlockSpec((None, tn, k), lambda i, j: (layer, j, 0)) if wt else
                    pl.BlockSpec((None, k, tn), lambda i, j: (layer, 0, j)))
    args.append(w)
    if res is not None:
        in_specs.append(pl.BlockSpec((tm, tn), lambda i, j: (i, j)))
        args.append(res)
    out_specs = [pl.BlockSpec((tm, tn), lambda i, j: (i, j))]
    out_shape = [jax.ShapeDtypeStruct((m, n), out_dtype)]
    scratch = []
    if emit_norm:
        out_specs.append(pl.BlockSpec((tm, k), lambda i, j: (i, 0)))
        out_shape.append(jax.ShapeDtypeStruct((m, k), _MXU_DTYPE))
    elif norm_w is not None:
        scratch.append(pltpu.VMEM((tm, k), _MXU_DTYPE))
    outs = pl.pallas_call(
        functools.partial(_matmul_kernel, act=act, act_tile0=act_col0 // tn, has_norm=norm_w is not None,
                          has_res=res is not None, wt=wt),
        grid=(m // tm, n // tn),
        in_specs=in_specs,
        out_specs=out_specs,
        out_shape=out_shape,
        scratch_shapes=scratch,
        compiler_params=_params(("parallel", "arbitrary")),
        name="matmul",
    )(*args)
    return outs if emit_norm else outs[0]


def _proj_conv_kernel(u_ref, halo_ref, w_ref, cw_ref, cb_ref, o_ref, an_ref, *, tiles_per_seq):
    kw = cw_ref.shape[0]

    @pl.when(pl.program_id(1) == 0)
    def _():
        halo = halo_ref[...]
        an_ref[0:_HALO, :] = jnp.where(pl.program_id(0) % tiles_per_seq == 0, jnp.zeros_like(halo), halo)
        an_ref[_HALO:, :] = u_ref[...]

    wb = w_ref[...].astype(_MXU_DTYPE)
    for c0 in range(0, o_ref.shape[1], _MXU_WIDTH):
        cols = pl.ds(c0, _MXU_WIDTH)
        acc = lax.dot_general(an_ref[...], wb[c0:c0 + _MXU_WIDTH, :], _NT, preferred_element_type=_F32)
        out = cb_ref[:, cols] + cw_ref[kw - 1:kw, cols] * acc[_HALO:, :]
        for s in range(1, kw):
            out = out + cw_ref[kw - 1 - s:kw - s, cols] * pltpu.roll(acc, s, axis=0)[_HALO:, :]
        o_ref[:, cols] = _silu(out).astype(o_ref.dtype)


def _proj_conv(u, w, layer, col0, conv_w, conv_b, seq, out_dtype):
    m, k = u.shape
    kw, n = conv_w.shape
    tm, tn = _matmul_tiles(m, k, math.gcd(n, col0), u.dtype.itemsize, w.dtype.itemsize,
                           jnp.dtype(out_dtype).itemsize, 1, False, conv_taps=kw, row_cap=seq)
    assert seq % tm == 0 and tm % _HALO == 0 and tn % _MXU_WIDTH == 0 and kw - 1 <= _HALO
    return pl.pallas_call(
        functools.partial(_proj_conv_kernel, tiles_per_seq=seq // tm),
        grid=(m // tm, n // tn),
        in_specs=[pl.BlockSpec((tm, k), lambda i, j: (i, 0)),
                  pl.BlockSpec((_HALO, k), lambda i, j: (jnp.maximum(i * (tm // _HALO) - 1, 0), 0)),
                  pl.BlockSpec((None, tn, k), lambda i, j: (layer, col0 // tn + j, 0)),
                  pl.BlockSpec((kw, tn), lambda i, j: (0, j)),
                  pl.BlockSpec((1, tn), lambda i, j: (0, j))],
        out_specs=pl.BlockSpec((tm, tn), lambda i, j: (i, j)),
        out_shape=jax.ShapeDtypeStruct((m, n), out_dtype),
        scratch_shapes=[pltpu.VMEM((tm + _HALO, k), _MXU_DTYPE)],
        compiler_params=_params(("parallel", "arbitrary")),
        name="proj_conv",
    )(u, u, w, conv_w, conv_b)


def _softplus(x):
    return jnp.maximum(x, 0.0) + jnp.log(1.0 + jnp.exp(-jnp.abs(x)))


def _chunk_cumsum(x, chunk):
    rows = x.shape[0]
    r = lax.broadcasted_iota(jnp.int32, (chunk, chunk), 0)
    c = lax.broadcasted_iota(jnp.int32, (chunk, chunk), 1)
    tri = jnp.where(c <= r, 1.0, 0.0).astype(_MXU_DTYPE)
    out = []
    for i in range(rows // chunk):
        rem = x[i * chunk:(i + 1) * chunk, :]
        acc = None
        for _ in range(3):
            piece = rem.astype(_MXU_DTYPE)
            d = jnp.dot(tri, piece, preferred_element_type=_F32)
            acc = d if acc is None else acc + d
            rem = rem - piece.astype(_F32)
        out.append(acc)
    return jnp.concatenate(out, axis=0) if len(out) > 1 else out[0]


def _row_bcast(ref, row, n):
    return jnp.broadcast_to(ref[pl.ds(row, 1), :], (n, ref.shape[1]))


def _tail_rows(w_ref, n_valid):
    row = lax.broadcasted_iota(jnp.int32, w_ref.shape, 0)
    return jnp.where(row < n_valid, w_ref[...], 0.0).astype(_MXU_DTYPE)


def _gla_gate_kernel(u_ref, wgr_ref, wup_ref, b_ref, o_ref, *, chunk, rank):
    gr = lax.dot_general(u_ref[...], _tail_rows(wgr_ref, rank), _NT, preferred_element_type=_F32)
    pre = jnp.dot(gr.astype(_MXU_DTYPE), wup_ref[...], preferred_element_type=_F32) + b_ref[...]
    gk = -_softplus(-pre) / _GLA_GATE_NORMALIZER
    o_ref[...] = _chunk_cumsum(gk * _LOG2E, chunk)


def _gla_gate(u, wt_in, layer, col0, rank, w_up, b_up, rows=512):
    m, d = u.shape
    kd = w_up.shape[1]
    rows = min(rows, m)
    assert col0 % _LANES == 0 and col0 + rank == wt_in.shape[1] and rank <= _LANES
    return pl.pallas_call(
        functools.partial(_gla_gate_kernel, chunk=_CHUNK, rank=rank),
        grid=(m // rows,),
        in_specs=[pl.BlockSpec((rows, d), lambda i: (i, 0)),
                  pl.BlockSpec((None, _LANES, d), lambda i: (layer, col0 // _LANES, 0)),
                  pl.BlockSpec(w_up.shape, lambda i: (0, 0)),
                  pl.BlockSpec((1, kd), lambda i: (0, 0))],
        out_specs=pl.BlockSpec((rows, kd), lambda i: (i, 0)),
        out_shape=jax.ShapeDtypeStruct((m, kd), _F32),
        compiler_params=_params(("parallel",)),
        name="gla_gate",
    )(u, wt_in, w_up, b_up)


def _gla_levels(c, nd):
    i = lax.broadcasted_iota(jnp.int32, (c, c), 0)
    j = lax.broadcasted_iota(jnp.int32, (c, c), 1)
    x = i ^ j
    lvl = jnp.zeros((c, c), jnp.int32)
    s, level = nd, 1
    while s < c:
        lvl = jnp.where(x >= s, level, lvl)
        s, level = 2 * s, level + 1
    return jnp.where(j > i, -1, lvl)


def _gla_scan_head(q_ref, k_ref, v_ref, gs_ref, b_ref, onw_ref, lvl_ref, o_ref, st_ref, qf_ref, kf_ref, scale):
    c, dk = q_ref.shape
    qf_ref[...] = q_ref[...].astype(_F32) * scale
    kf_ref[...] = k_ref[...].astype(_F32)
    q = qf_ref[...]
    k = kf_ref[...]
    v = v_ref[...]
    bc = b_ref[...]
    st = st_ref[...]
    b_last = jnp.concatenate([_row_bcast(b_ref, c - 1, 8)] * (c // 8), axis=0)

    o = lax.dot_general((q * jnp.exp2(bc)).astype(_MXU_DTYPE), st.astype(_MXU_DTYPE), _NT,
                        preferred_element_type=_F32)
    k_dec = (k * jnp.exp2(b_last - bc)).astype(_MXU_DTYPE)

    nd = _GLA_DIAG
    lane = lax.broadcasted_iota(jnp.int32, (nd, c), 1)
    pieces = []
    for m in range(c // nd):
        qb = qf_ref[m * nd:(m + 1) * nd, :]
        bb = b_ref[m * nd:(m + 1) * nd, :]
        acc = jnp.zeros((nd, c), _F32)
        for j in range(nd):
            kj = _row_bcast(kf_ref, m * nd + j, nd)
            bj = _row_bcast(b_ref, m * nd + j, nd)
            term = qb * kj * jnp.exp2(jnp.minimum(bb - bj, 0.0))
            acc = jnp.where(lane == m * nd + j, jnp.sum(term, axis=1, keepdims=True), acc)
        pieces.append(acc)
    lvl = lvl_ref[...]
    att = jnp.where(lvl == 0, jnp.concatenate(pieces, axis=0), 0.0)

    s, level = nd, 1
    while s < c:
        prev = [jnp.zeros((s, dk), _F32)]
        end = []
        for m in range(c // s):
            if m:
                prev += [_row_bcast(b_ref, m * s - 1, 8)] * (s // 8)
            end += [_row_bcast(b_ref, (m + 1) * s - 1, 8)] * (s // 8)
        qs = (q * jnp.exp2(bc - jnp.concatenate(prev, axis=0))).astype(_MXU_DTYPE)
        ks = (k * jnp.exp2(jnp.concatenate(end, axis=0) - bc)).astype(_MXU_DTYPE)
        att = jnp.where(lvl == level, lax.dot_general(qs, ks, _NT, preferred_element_type=_F32), att)
        s, level = 2 * s, level + 1

    o = o + jnp.dot(att.astype(_MXU_DTYPE), v, preferred_element_type=_F32)
    st_ref[...] = st * jnp.exp2(b_last[:1, :]) + lax.dot_general(
        v, k_dec, _TN, preferred_element_type=_F32)

    ms = jnp.mean(o * o, axis=-1, keepdims=True)
    on = (o * lax.rsqrt(ms + _RMS_EPS)) * onw_ref[...]
    o_ref[...] = (on * gs_ref[...].astype(_F32)).astype(o_ref.dtype)


def _gla_scan_kernel(q_ref, k_ref, v_ref, gs_ref, b_ref, onw_ref, lvl_ref, o_ref, st_ref, qf_ref, kf_ref,
                     *, scale, dk, dv):
    @pl.when(pl.program_id(2) == 0)
    def _():
        st_ref[...] = jnp.zeros_like(st_ref)

    for h in range(st_ref.shape[0]):
        ks, vs = pl.ds(h * dk, dk), pl.ds(h * dv, dv)
        _gla_scan_head(q_ref.at[:, ks], k_ref.at[:, ks], v_ref.at[:, vs], gs_ref.at[:, vs], b_ref.at[:, ks],
                       onw_ref, lvl_ref, o_ref.at[:, vs], st_ref.at[h], qf_ref.at[h], kf_ref.at[h], scale)


def _gla_scan(proj, bcum, o_norm_w, batch, seq, heads, dk, dv):
    m = proj.shape[0]
    c = _CHUNK
    hb = _GLA_HEADS_PER_STEP if heads % _GLA_HEADS_PER_STEP == 0 else 1
    assert c == _LANES and dk % _LANES == 0 and dv % _LANES == 0 and (2 * heads * dk) % (hb * dv) == 0
    nt = seq // c
    kd, vd = heads * dk, heads * dv
    wk, wv = hb * dk, hb * dv
    k_blk, v_blk, g_blk = kd // wk, (2 * kd) // wv, (2 * kd + vd) // wv

    def rows(b, h, t):
        return b * nt + t

    return pl.pallas_call(
        functools.partial(_gla_scan_kernel, scale=dk ** -0.5, dk=dk, dv=dv),
        grid=(batch, heads // hb, nt),
        in_specs=[pl.BlockSpec((c, wk), lambda b, h, t: (rows(b, h, t), h)),
                  pl.BlockSpec((c, wk), lambda b, h, t: (rows(b, h, t), k_blk + h)),
                  pl.BlockSpec((c, wv), lambda b, h, t: (rows(b, h, t), v_blk + h)),
                  pl.BlockSpec((c, wv), lambda b, h, t: (rows(b, h, t), g_blk + h)),
                  pl.BlockSpec((c, wk), lambda b, h, t: (rows(b, h, t), h)),
                  pl.BlockSpec((1, dv), lambda b, h, t: (0, 0)),
                  pl.BlockSpec((c, c), lambda b, h, t: (0, 0))],
        out_specs=pl.BlockSpec((c, wv), lambda b, h, t: (rows(b, h, t), h)),
        out_shape=jax.ShapeDtypeStruct((m, vd), _MXU_DTYPE),
        scratch_shapes=[pltpu.VMEM((hb, dv, dk), _F32),
                        pltpu.VMEM((hb, c, dk), _F32),
                        pltpu.VMEM((hb, c, dk), _F32)],
        compiler_params=_params(("parallel", "parallel", "arbitrary")),
        name="gla_scan",
    )(proj, proj, proj, proj, bcum, o_norm_w.reshape(1, dv).astype(_F32), _gla_levels(c, _GLA_DIAG))


def _gla_mixer(h, norm_w, batch, seq, layer, wt_in, w_gk_up, b_gk_up, o_norm_w, w_out):
    rank, kd = w_gk_up.shape
    vd = (wt_in.shape[1] - rank - 2 * kd) // 2
    dv = o_norm_w.shape[0]
    heads = vd // dv
    n_main = 2 * kd + 2 * vd
    w_up = jnp.pad(w_gk_up, ((0, _LANES - rank), (0, 0))).astype(_MXU_DTYPE)
    proj, u = _matmul(h, wt_in, layer, wt=True, n=n_main, norm_w=norm_w, emit_norm=True, act="silu",
                      act_col0=2 * kd + vd, out_dtype=_MXU_DTYPE)
    bcum = _gla_gate(u, wt_in, layer, n_main, rank, w_up, b_gk_up.reshape(1, kd).astype(_F32))
    o = _gla_scan(proj, bcum, o_norm_w, batch, seq, heads, kd // heads, dv)
    return _matmul(o, w_out.astype(_MXU_DTYPE), layer, res=h, out_dtype=_F32)


def _ssd_gate_kernel(u_ref, wdt_ref, bias_ref, alog_ref, acol_ref, dcol_ref, arow_ref, drow_ref,
                     *, chunk, heads, groups):
    raw = lax.dot_general(u_ref[...], _tail_rows(wdt_ref, heads), _NT, preferred_element_type=_F32)
    dt = _softplus(raw + bias_ref[...])
    a = _chunk_cumsum(dt * (-jnp.exp(alog_ref[...])) * _LOG2E, chunk)
    hg = heads // groups
    for g in range(groups):
        acol_ref[g] = a[:, g * hg:(g + 1) * hg]
        dcol_ref[g] = dt[:, g * hg:(g + 1) * hg]
    for i in range(a.shape[0] // chunk):
        arow_ref[i] = a[i * chunk:(i + 1) * chunk, :].T[:heads, :]
        drow_ref[i] = dt[i * chunk:(i + 1) * chunk, :].T[:heads, :]


def _ssd_gate(u, wt_in, layer, col0, dt_bias, a_log, heads, rows=512):
    m, d = u.shape
    c = _CHUNK
    rows = min(rows, m)
    g = _SSD_GROUPS
    hg = heads // g
    assert col0 % _LANES == 0 and col0 + heads == wt_in.shape[1] and heads <= _LANES
    col = jax.ShapeDtypeStruct((g, m, hg), _F32)
    row = jax.ShapeDtypeStruct((m // c, heads, c), _F32)
    col_spec = pl.BlockSpec((g, rows, hg), lambda i: (0, i, 0))
    row_spec = pl.BlockSpec((rows // c, heads, c), lambda i: (i, 0, 0))
    return pl.pallas_call(
        functools.partial(_ssd_gate_kernel, chunk=c, heads=heads, groups=g),
        grid=(m // rows,),
        in_specs=[pl.BlockSpec((rows, d), lambda i: (i, 0)),
                  pl.BlockSpec((None, _LANES, d), lambda i: (layer, col0 // _LANES, 0)),
                  pl.BlockSpec((1, _LANES), lambda i: (0, 0)),
                  pl.BlockSpec((1, _LANES), lambda i: (0, 0))],
        out_specs=[col_spec, col_spec, row_spec, row_spec],
        out_shape=[col, col, row, row],
        compiler_params=_params(("parallel",)),
        name="ssd_gate",
    )(u, wt_in, dt_bias, a_log)


def _expand_heads(colarr, n_heads, head_dim):
    c = colarr.shape[0]
    per = _LANES // head_dim
    lane = lax.broadcasted_iota(jnp.int32, (c, _LANES), 1)
    pieces = []
    for p in range(n_heads // per):
        out = jnp.broadcast_to(colarr[:, p * per:p * per + 1], (c, _LANES))
        for i in range(1, per):
            nxt = jnp.broadcast_to(colarr[:, p * per + i:p * per + i + 1], (c, _LANES))
            out = jnp.where(lane >= i * head_dim, nxt, out)
        pieces.append(out)
    return jnp.concatenate(pieces, axis=1)


def _ssd_scan_group(zs_ref, x_ref, bm_ref, cm_ref, acol_ref, dcol_ref, arow_ref, drow_ref, dskip_ref, gnw_ref,
                    o_ref, st_ref, head_dim):
    c = zs_ref.shape[0]
    hg = acol_ref.shape[1]
    xs = x_ref[...].astype(_F32)
    bmx = bm_ref[...]
    cmx = cm_ref[...]

    acol = acol_ref[...]
    dcol = dcol_ref[...]
    a_last = acol[c - 1:c, :]
    w_state = jnp.exp2(a_last - acol) * dcol

    per = _LANES // head_dim
    width = per * c
    cb = lax.dot_general(cmx, jnp.concatenate([bmx] * per, axis=0), _NT,
                         preferred_element_type=_F32)
    ri = lax.broadcasted_iota(jnp.int32, (c, width), 0)
    ci = lax.broadcasted_iota(jnp.int32, (c, width), 1) & (c - 1)
    causal = ci <= ri
    lane = lax.broadcasted_iota(jnp.int32, (c, _LANES), 1)
    y_pieces = []
    for p in range(hg // per):
        ac = jnp.concatenate(
            [jnp.broadcast_to(acol[:, p * per + i:p * per + i + 1], (c, c)) for i in range(per)], axis=1)
        ar = arow_ref[:, p * width:(p + 1) * width]
        dr = drow_ref[:, p * width:(p + 1) * width]
        decay = jnp.exp2(jnp.where(causal, ac - ar, _NEG_BIG))
        sc = (cb * decay * dr).astype(_MXU_DTYPE)
        xp = x_ref[:, p * _LANES:(p + 1) * _LANES]
        bd = jnp.concatenate(
            [jnp.where((lane >= i * head_dim) & (lane < (i + 1) * head_dim), xp, jnp.zeros_like(xp))
             for i in range(per)], axis=0)
        y_pieces.append(jnp.dot(sc, bd, preferred_element_type=_F32))
    y = jnp.concatenate(y_pieces, axis=1)

    st = st_ref[...]
    ea = jnp.exp2(_expand_heads(acol, hg, head_dim))
    y = y + jnp.dot(cmx, st.astype(_MXU_DTYPE), preferred_element_type=_F32) * ea
    xw = (xs * _expand_heads(w_state, hg, head_dim)).astype(_MXU_DTYPE)
    st_ref[...] = st * ea[c - 1:c, :] + lax.dot_general(bmx, xw, _TN, preferred_element_type=_F32)

    y = y + dskip_ref[...] * xs
    y = y * zs_ref[...].astype(_F32)
    ms = jnp.mean(y * y, axis=-1, keepdims=True)
    o_ref[...] = ((y * lax.rsqrt(ms + _RMS_EPS)) * gnw_ref[...]).astype(o_ref.dtype)


def _ssd_scan_kernel(zs_ref, x_ref, bm_ref, cm_ref, acol_ref, dcol_ref, arow_ref, drow_ref, dskip_ref, gnw_ref,
                     o_ref, st_ref, *, head_dim):
    @pl.when(pl.program_id(2) == 0)
    def _():
        st_ref[...] = jnp.zeros_like(st_ref)

    gb, n, gw = st_ref.shape
    c = zs_ref.shape[0]
    hg = acol_ref.shape[2]
    for g in range(gb):
        xs_, ns_, rs_ = pl.ds(g * gw, gw), pl.ds(g * n, n), pl.ds(g * hg * c, hg * c)
        _ssd_scan_group(zs_ref.at[:, xs_], x_ref.at[:, xs_], bm_ref.at[:, ns_], cm_ref.at[:, ns_],
                        acol_ref.at[g], dcol_ref.at[g], arow_ref.at[0, :, rs_], drow_ref.at[0, :, rs_],
                        dskip_ref.at[:, xs_], gnw_ref.at[:, xs_], o_ref.at[:, xs_], st_ref.at[g], head_dim)


def _ssd_scan(zs, xbc, acol, dcol, arow, drow, dskip, gnorm_w, batch, seq, inner, heads, n_state):
    m = zs.shape[0]
    c = _CHUNK
    g = _SSD_GROUPS
    gb = _SSD_GROUPS_PER_STEP if g % _SSD_GROUPS_PER_STEP == 0 else 1
    hg = heads // g
    p = inner // heads
    gw = hg * p
    sw, sn = gb * gw, gb * n_state
    assert c == _LANES and gw % _LANES == 0 and n_state % _LANES == 0 and _LANES % p == 0
    assert inner % sn == 0 and (g * n_state) % sn == 0
    nt = seq // c
    b_blk = inner // sn
    c_blk = (inner + g * n_state) // sn

    def rows(b, t):
        return b * nt + t

    in_specs = [
        pl.BlockSpec((c, sw), lambda b, gi, t: (rows(b, t), gi)),
        pl.BlockSpec((c, sw), lambda b, gi, t: (rows(b, t), gi)),
        pl.BlockSpec((c, sn), lambda b, gi, t: (rows(b, t), b_blk + gi)),
        pl.BlockSpec((c, sn), lambda b, gi, t: (rows(b, t), c_blk + gi)),
        pl.BlockSpec((gb, c, hg), lambda b, gi, t: (gi, rows(b, t), 0)),
        pl.BlockSpec((gb, c, hg), lambda b, gi, t: (gi, rows(b, t), 0)),
        pl.BlockSpec((1, 1, gb * hg * c), lambda b, gi, t: (rows(b, t), 0, gi)),
        pl.BlockSpec((1, 1, gb * hg * c), lambda b, gi, t: (rows(b, t), 0, gi)),
        pl.BlockSpec((1, sw), lambda b, gi, t: (0, gi)),
        pl.BlockSpec((1, sw), lambda b, gi, t: (0, gi)),
    ]
    return pl.pallas_call(
        functools.partial(_ssd_scan_kernel, head_dim=p),
        grid=(batch, g // gb, nt),
        in_specs=in_specs,
        out_specs=pl.BlockSpec((c, sw), lambda b, gi, t: (rows(b, t), gi)),
        out_shape=jax.ShapeDtypeStruct((m, inner), _MXU_DTYPE),
        scratch_shapes=[pltpu.VMEM((gb, n_state, gw), _F32)],
        compiler_params=_params(("parallel", "parallel", "arbitrary")),
        name="ssd_scan",
    )(zs, xbc, xbc, xbc, acol, dcol, arow, drow, dskip, gnorm_w)


def _ssd_mixer(h, norm_w, batch, seq, layer, wt_in, conv_w, conv_b, dt_bias, a_log, d_skip, gnorm_w, w_out):
    heads = dt_bias.shape[0]
    conv_dim = conv_w.shape[1]
    inner = wt_in.shape[1] - conv_dim - heads
    n_state = (conv_dim - inner) // (2 * _SSD_GROUPS)
    n_main = inner + conv_dim
    m = h.shape[0]
    pad1 = lambda v: jnp.pad(v.astype(_F32), (0, _LANES - heads)).reshape(1, _LANES)
    zs, u = _matmul(h, wt_in, layer, wt=True, n=inner, norm_w=norm_w, emit_norm=True, act="silu",
                    out_dtype=_MXU_DTYPE)
    xbc = _proj_conv(u, wt_in, layer, inner, conv_w.astype(_F32), conv_b.reshape(1, conv_dim).astype(_F32),
                     seq, _MXU_DTYPE)
    acol, dcol, arow, drow = _ssd_gate(u, wt_in, layer, n_main, pad1(dt_bias), pad1(a_log), heads)
    arow = arow.reshape(m // _CHUNK, 1, heads * _CHUNK)
    drow = drow.reshape(m // _CHUNK, 1, heads * _CHUNK)
    dskip = jnp.repeat(d_skip.astype(_F32), inner // heads).reshape(1, inner)
    y = _ssd_scan(zs, xbc, acol, dcol, arow, drow, dskip, gnorm_w.reshape(1, inner).astype(_F32),
                  batch, seq, inner, heads, n_state)
    return _matmul(y, w_out.astype(_MXU_DTYPE), layer, res=h, out_dtype=_F32)


def kernel(x, mixer_norm_w, gla_w_in, gla_w_gk_up, gla_b_gk_up, gla_o_norm_w, gla_w_out, ssd_w_in, ssd_conv_w, ssd_conv_b, ssd_dt_bias, ssd_a_log, ssd_d_skip, ssd_gnorm_w, ssd_w_out, mlp_norm_w, mlp_w_fc1, mlp_w_fc2, final_norm_w):
    batch, seq, d = x.shape
    h = x.reshape(batch * seq, d)
    w_fc2 = mlp_w_fc2.astype(_MXU_DTYPE)
    gla_wt_in = jnp.swapaxes(gla_w_in, 1, 2)
    ssd_wt_in = jnp.swapaxes(ssd_w_in, 1, 2)
    for i in range(mixer_norm_w.shape[0]):
        j = i // 2
        if i % 2 == 0:
            h = _gla_mixer(h, mixer_norm_w[i], batch, seq, j, gla_wt_in, gla_w_gk_up[j], gla_b_gk_up[j],
                           gla_o_norm_w[j], gla_w_out)
        else:
            h = _ssd_mixer(h, mixer_norm_w[i], batch, seq, j, ssd_wt_in, ssd_conv_w[j], ssd_conv_b[j],
                           ssd_dt_bias[j], ssd_a_log[j], ssd_d_skip[j], ssd_gnorm_w[j], ssd_w_out)
        hidden = _matmul(h, mlp_w_fc1, i, norm_w=mlp_norm_w[i], act="relu2", out_dtype=_MXU_DTYPE)
        h = _matmul(hidden, w_fc2, i, res=h, out_dtype=_F32)
    return _rmsnorm(h, final_norm_w, _F32).reshape(batch, seq, d)
```

```python
import functools
import math

import jax
import jax.numpy as jnp
from jax import lax
from jax.experimental import pallas as pl
from jax.experimental.pallas import tpu as pltpu

_F32 = jnp.float32
_MXU_DTYPE = jnp.bfloat16
_RMS_EPS = 1e-5
_GLA_GATE_NORMALIZER = 16.0
_SSD_GROUPS = 8
_CHUNK = 128
_LANES = 128
_GLA_DIAG = 8
_GLA_HEADS_PER_STEP = 4
_SSD_GROUPS_PER_STEP = 2
_LOG2E = math.log2(math.e)
_NEG_BIG = -1e30
_HALO = 16
_MXU_WIDTH = 256
_VMEM_LIMIT_BYTES = 56 * 1024 * 1024
_MATMUL_VMEM_BUDGET = 52 * 1024 * 1024

_NN = (((1,), (0,)), ((), ()))
_NT = (((1,), (1,)), ((), ()))
_TN = (((0,), (0,)), ((), ()))


def _params(sem, flags=None):
    return pltpu.CompilerParams(dimension_semantics=sem, vmem_limit_bytes=_VMEM_LIMIT_BYTES, flags=flags)


def _rms(x, w):
    ms = jnp.mean(x * x, axis=-1, keepdims=True)
    return (x * lax.rsqrt(ms + _RMS_EPS)) * w


def _rmsnorm_kernel(x_ref, w_ref, o_ref):
    o_ref[...] = _rms(x_ref[...].astype(_F32), w_ref[...]).astype(o_ref.dtype)


def _rmsnorm(x, w, out_dtype, rows=512):
    m, d = x.shape
    rows = min(rows, m)
    return pl.pallas_call(
        _rmsnorm_kernel,
        grid=(m // rows,),
        in_specs=[pl.BlockSpec((rows, d), lambda i: (i, 0)),
                  pl.BlockSpec((1, d), lambda i: (0, 0))],
        out_specs=pl.BlockSpec((rows, d), lambda i: (i, 0)),
        out_shape=jax.ShapeDtypeStruct((m, d), out_dtype),
        compiler_params=_params(("parallel",)),
        name="rmsnorm",
    )(x, w.reshape(1, d).astype(_F32))


def _silu(x):
    half = 0.5 * x
    return half + half * jnp.tanh(half)


def _matmul_kernel(*refs, act, act_tile0, has_norm, has_res, wt):
    it = iter(refs)
    a_ref = next(it)
    nw_ref = next(it) if has_norm else None
    w_ref = next(it)
    r_ref = next(it) if has_res else None
    o_ref = next(it)
    if has_norm:
        an_ref = next(it)

        @pl.when(pl.program_id(1) == 0)
        def _():
            an_ref[...] = _rms(a_ref[...], nw_ref[...]).astype(_MXU_DTYPE)

        a = an_ref[...]
    else:
        a = a_ref[...]
    w = w_ref[...].astype(_MXU_DTYPE)
    acc = lax.dot_general(a, w, _NT if wt else _NN, preferred_element_type=_F32)
    if act == "relu2":
        act_acc = jnp.square(jnp.maximum(acc, 0.0))
    elif act == "silu":
        act_acc = _silu(acc)
    if act is not None:
        acc = act_acc if act_tile0 == 0 else jnp.where(pl.program_id(1) >= act_tile0, act_acc, acc)
    if has_res:
        acc = r_ref[...] + acc
    o_ref[...] = acc.astype(o_ref.dtype)


def _divisor_tile(n, pref, align):
    if n <= pref:
        return n
    t = (pref // align) * align
    while n % t:
        t -= align
    return t


def _matmul_tiles(m, k, n, a_bytes, w_bytes, out_bytes, norm_copies, has_res, conv_taps=0, row_cap=None):
    mxu_bytes = jnp.dtype(_MXU_DTYPE).itemsize
    for pm, pn in ((1024, 1024), (1024, 512), (1024, 256), (512, 256), (256, 256), (128, 128)):
        pm = pm if row_cap is None else min(pm, row_cap)
        tm, tn = _divisor_tile(m, pm, 8), _divisor_tile(n, pn, _LANES)
        need = 2 * tm * k * a_bytes + 2 * k * tn * w_bytes + 2 * tm * tn * out_bytes
        need += tm * tn * 4 if not conv_taps else (1 + conv_taps) * tm * _MXU_WIDTH * 4
        need += 2 * tm * tn * 4 if has_res else 0
        need += norm_copies * tm * k * mxu_bytes
        need += k * tn * mxu_bytes if w_bytes != mxu_bytes else 0
        if need <= _MATMUL_VMEM_BUDGET:
            break
    return tm, tn


def _matmul(a, w, layer, *, wt=False, n=None, norm_w=None, emit_norm=False, act=None, act_col0=0, res=None,
            out_dtype):
    m, k = a.shape
    n = w.shape[1 if wt else 2] if n is None else n
    tm, tn = _matmul_tiles(m, k, math.gcd(n, act_col0), a.dtype.itemsize, w.dtype.itemsize,
                           jnp.dtype(out_dtype).itemsize, (norm_w is not None) + emit_norm, res is not None)
    in_specs = [pl.BlockSpec((tm, k), lambda i, j: (i, 0))]
    args = [a]
    if norm_w is not None:
        in_specs.append(pl.BlockSpec((1, k), lambda i, j: (0, 0)))
        args.append(norm_w.reshape(1, k).astype(_F32))
    in_specs.append(pl.BlockSpec((None, tn, k), lambda i, j: (layer, j, 0)) if wt else
                    pl.BlockSpec((None, k, tn), lambda i, j: (layer, 0, j)))
    args.append(w)
    if res is not None:
        in_specs.append(pl.BlockSpec((tm, tn), lambda i, j: (i, j)))
        args.append(res)
    out_specs = [pl.BlockSpec((tm, tn), lambda i, j: (i, j))]
    out_shape = [jax.ShapeDtypeStruct((m, n), out_dtype)]
    scratch = []
    if emit_norm:
        out_specs.append(pl.BlockSpec((tm, k), lambda i, j: (i, 0)))
        out_shape.append(jax.ShapeDtypeStruct((m, k), _MXU_DTYPE))
    elif norm_w is not None:
        scratch.append(pltpu.VMEM((tm, k), _MXU_DTYPE))
    outs = pl.pallas_call(
        functools.partial(_matmul_kernel, act=act, act_tile0=act_col0 // tn, has_norm=norm_w is not None,
                          has_res=res is not None, wt=wt),
        grid=(m // tm, n // tn),
        in_specs=in_specs,
        out_specs=out_specs,
        out_shape=out_shape,
        scratch_shapes=scratch,
        compiler_params=_params(("parallel", "arbitrary")),
        name="matmul",
    )(*args)
    return outs if emit_norm else outs[0]


def _proj_conv_kernel(u_ref, halo_ref, w_ref, cw_ref, cb_ref, o_ref, an_ref, *, tiles_per_seq):
    kw = cw_ref.shape[0]

    @pl.when(pl.program_id(1) == 0)
    def _():
        halo = halo_ref[...]
        an_ref[0:_HALO, :] = jnp.where(pl.program_id(0) % tiles_per_seq == 0, jnp.zeros_like(halo), halo)
        an_ref[_HALO:, :] = u_ref[...]

    wb = w_ref[...].astype(_MXU_DTYPE)
    for c0 in range(0, o_ref.shape[1], _MXU_WIDTH):
        cols = pl.ds(c0, _MXU_WIDTH)
        acc = lax.dot_general(an_ref[...], wb[c0:c0 + _MXU_WIDTH, :], _NT, preferred_element_type=_F32)
        tm = o_ref.shape[0]
        out = cb_ref[:, cols] + cw_ref[kw - 1:kw, cols] * acc[_HALO:, :]
        for s in range(1, kw):
            out = out + cw_ref[kw - 1 - s:kw - s, cols] * acc[_HALO - s:_HALO - s + tm, :]
        o_ref[:, cols] = _silu(out).astype(o_ref.dtype)


def _proj_conv(u, w, layer, col0, conv_w, conv_b, seq, out_dtype):
    m, k = u.shape
    kw, n = conv_w.shape
    tm, tn = _matmul_tiles(m, k, math.gcd(n, col0), u.dtype.itemsize, w.dtype.itemsize,
                           jnp.dtype(out_dtype).itemsize, 1, False, conv_taps=kw, row_cap=seq)
    assert seq % tm == 0 and tm % _HALO == 0 and tn % _MXU_WIDTH == 0 and kw - 1 <= _HALO
    return pl.pallas_call(
        functools.partial(_proj_conv_kernel, tiles_per_seq=seq // tm),
        grid=(m // tm, n // tn),
        in_specs=[pl.BlockSpec((tm, k), lambda i, j: (i, 0)),
                  pl.BlockSpec((_HALO, k), lambda i, j: (jnp.maximum(i * (tm // _HALO) - 1, 0), 0)),
                  pl.BlockSpec((None, tn, k), lambda i, j: (layer, col0 // tn + j, 0)),
                  pl.BlockSpec((kw, tn), lambda i, j: (0, j)),
                  pl.BlockSpec((1, tn), lambda i, j: (0, j))],
        out_specs=pl.BlockSpec((tm, tn), lambda i, j: (i, j)),
        out_shape=jax.ShapeDtypeStruct((m, n), out_dtype),
        scratch_shapes=[pltpu.VMEM((tm + _HALO, k), _MXU_DTYPE)],
        compiler_params=_params(("parallel", "arbitrary")),
        name="proj_conv",
    )(u, u, w, conv_w, conv_b)


def _softplus(x):
    return jnp.maximum(x, 0.0) + jnp.log(1.0 + jnp.exp(-jnp.abs(x)))


def _chunk_cumsum(x, chunk):
    rows = x.shape[0]
    r = lax.broadcasted_iota(jnp.int32, (chunk, chunk), 0)
    c = lax.broadcasted_iota(jnp.int32, (chunk, chunk), 1)
    tri = jnp.where(c <= r, 1.0, 0.0).astype(_MXU_DTYPE)
    out = []
    for i in range(rows // chunk):
        rem = x[i * chunk:(i + 1) * chunk, :]
        acc = None
        for _ in range(3):
            piece = rem.astype(_MXU_DTYPE)
            d = jnp.dot(tri, piece, preferred_element_type=_F32)
            acc = d if acc is None else acc + d
            rem = rem - piece.astype(_F32)
        out.append(acc)
    return jnp.concatenate(out, axis=0) if len(out) > 1 else out[0]


def _row_bcast(ref, row, n):
    return jnp.broadcast_to(ref[pl.ds(row, 1), :], (n, ref.shape[1]))


def _tail_rows(w_ref, n_valid):
    row = lax.broadcasted_iota(jnp.int32, w_ref.shape, 0)
    return jnp.where(row < n_valid, w_ref[...], 0.0).astype(_MXU_DTYPE)


def _gla_gate_kernel(u_ref, wgr_ref, wup_ref, b_ref, o_ref, *, chunk, rank):
    gr = lax.dot_general(u_ref[...], _tail_rows(wgr_ref, rank), _NT, preferred_element_type=_F32)
    pre = jnp.dot(gr.astype(_MXU_DTYPE), wup_ref[...], preferred_element_type=_F32) + b_ref[...]
    gk = -_softplus(-pre) / _GLA_GATE_NORMALIZER
    o_ref[...] = _chunk_cumsum(gk * _LOG2E, chunk)


def _gla_gate(u, wt_in, layer, col0, rank, w_up, b_up, rows=512):
    m, d = u.shape
    kd = w_up.shape[1]
    rows = min(rows, m)
    assert col0 % _LANES == 0 and col0 + rank == wt_in.shape[1] and rank <= _LANES
    return pl.pallas_call(
        functools.partial(_gla_gate_kernel, chunk=_CHUNK, rank=rank),
        grid=(m // rows,),
        in_specs=[pl.BlockSpec((rows, d), lambda i: (i, 0)),
                  pl.BlockSpec((None, _LANES, d), lambda i: (layer, col0 // _LANES, 0)),
                  pl.BlockSpec(w_up.shape, lambda i: (0, 0)),
                  pl.BlockSpec((1, kd), lambda i: (0, 0))],
        out_specs=pl.BlockSpec((rows, kd), lambda i: (i, 0)),
        out_shape=jax.ShapeDtypeStruct((m, kd), _F32),
        compiler_params=_params(("parallel",)),
        name="gla_gate",
    )(u, wt_in, w_up, b_up)


def _gla_levels(c, nd):
    i = lax.broadcasted_iota(jnp.int32, (c, c), 0)
    j = lax.broadcasted_iota(jnp.int32, (c, c), 1)
    x = i ^ j
    lvl = jnp.zeros((c, c), jnp.int32)
    s, level = nd, 1
    while s < c:
        lvl = jnp.where(x >= s, level, lvl)
        s, level = 2 * s, level + 1
    return jnp.where(j > i, -1, lvl)


def _gla_scan_head(q_ref, k_ref, v_ref, gs_ref, b_ref, onw_ref, lvl_ref, o_ref, st_ref, qf_ref, kf_ref, scale):
    c, dk = q_ref.shape
    qf_ref[...] = q_ref[...].astype(_F32) * scale
    kf_ref[...] = k_ref[...].astype(_F32)
    q = qf_ref[...]
    k = kf_ref[...]
    v = v_ref[...]
    bc = b_ref[...]
    st = st_ref[...]
    b_last = jnp.concatenate([_row_bcast(b_ref, c - 1, 8)] * (c // 8), axis=0)

    o = lax.dot_general((q * jnp.exp2(bc)).astype(_MXU_DTYPE), st.astype(_MXU_DTYPE), _NT,
                        preferred_element_type=_F32)
    k_dec = (k * jnp.exp2(b_last - bc)).astype(_MXU_DTYPE)

    nd = _GLA_DIAG
    lane = lax.broadcasted_iota(jnp.int32, (nd, c), 1)
    pieces = []
    for m in range(c // nd):
        qb = qf_ref[m * nd:(m + 1) * nd, :]
        bb = b_ref[m * nd:(m + 1) * nd, :]
        acc = jnp.zeros((nd, c), _F32)
        for j in range(nd):
            kj = _row_bcast(kf_ref, m * nd + j, nd)
            bj = _row_bcast(b_ref, m * nd + j, nd)
            term = qb * kj * jnp.exp2(jnp.minimum(bb - bj, 0.0))
            acc = jnp.where(lane == m * nd + j, jnp.sum(term, axis=1, keepdims=True), acc)
        pieces.append(acc)
    lvl = lvl_ref[...]
    att = jnp.where(lvl == 0, jnp.concatenate(pieces, axis=0), 0.0)

    s, level = nd, 1
    while s < c:
        prev = [jnp.zeros((s, dk), _F32)]
        end = []
        for m in range(c // s):
            if m:
                prev += [_row_bcast(b_ref, m * s - 1, 8)] * (s // 8)
            end += [_row_bcast(b_ref, (m + 1) * s - 1, 8)] * (s // 8)
        qs = (q * jnp.exp2(bc - jnp.concatenate(prev, axis=0))).astype(_MXU_DTYPE)
        ks = (k * jnp.exp2(jnp.concatenate(end, axis=0) - bc)).astype(_MXU_DTYPE)
        att = jnp.where(lvl == level, lax.dot_general(qs, ks, _NT, preferred_element_type=_F32), att)
        s, level = 2 * s, level + 1

    o = o + jnp.dot(att.astype(_MXU_DTYPE), v, preferred_element_type=_F32)
    st_ref[...] = st * jnp.exp2(b_last[:1, :]) + lax.dot_general(
        v, k_dec, _TN, preferred_element_type=_F32)

    ms = jnp.mean(o * o, axis=-1, keepdims=True)
    on = (o * lax.rsqrt(ms + _RMS_EPS)) * onw_ref[...]
    o_ref[...] = (on * gs_ref[...].astype(_F32)).astype(o_ref.dtype)


def _gla_scan_kernel(q_ref, k_ref, v_ref, gs_ref, b_ref, onw_ref, lvl_ref, o_ref, st_ref, qf_ref, kf_ref,
                     *, scale, dk, dv):
    @pl.when(pl.program_id(2) == 0)
    def _():
        st_ref[...] = jnp.zeros_like(st_ref)

    for h in range(st_ref.shape[0]):
        ks, vs = pl.ds(h * dk, dk), pl.ds(h * dv, dv)
        _gla_scan_head(q_ref.at[:, ks], k_ref.at[:, ks], v_ref.at[:, vs], gs_ref.at[:, vs], b_ref.at[:, ks],
                       onw_ref, lvl_ref, o_ref.at[:, vs], st_ref.at[h], qf_ref.at[h], kf_ref.at[h], scale)


def _gla_scan(proj, bcum, o_norm_w, batch, seq, heads, dk, dv):
    m = proj.shape[0]
    c = _CHUNK
    hb = _GLA_HEADS_PER_STEP if heads % _GLA_HEADS_PER_STEP == 0 else 1
    assert c == _LANES and dk % _LANES == 0 and dv % _LANES == 0 and (2 * heads * dk) % (hb * dv) == 0
    nt = seq // c
    kd, vd = heads * dk, heads * dv
    wk, wv = hb * dk, hb * dv
    k_blk, v_blk, g_blk = kd // wk, (2 * kd) // wv, (2 * kd + vd) // wv

    def rows(b, h, t):
        return b * nt + t

    return pl.pallas_call(
        functools.partial(_gla_scan_kernel, scale=dk ** -0.5, dk=dk, dv=dv),
        grid=(batch, heads // hb, nt),
        in_specs=[pl.BlockSpec((c, wk), lambda b, h, t: (rows(b, h, t), h)),
                  pl.BlockSpec((c, wk), lambda b, h, t: (rows(b, h, t), k_blk + h)),
                  pl.BlockSpec((c, wv), lambda b, h, t: (rows(b, h, t), v_blk + h)),
                  pl.BlockSpec((c, wv), lambda b, h, t: (rows(b, h, t), g_blk + h)),
                  pl.BlockSpec((c, wk), lambda b, h, t: (rows(b, h, t), h)),
                  pl.BlockSpec((1, dv), lambda b, h, t: (0, 0)),
                  pl.BlockSpec((c, c), lambda b, h, t: (0, 0))],
        out_specs=pl.BlockSpec((c, wv), lambda b, h, t: (rows(b, h, t), h)),
        out_shape=jax.ShapeDtypeStruct((m, vd), _MXU_DTYPE),
        scratch_shapes=[pltpu.VMEM((hb, dv, dk), _F32),
                        pltpu.VMEM((hb, c, dk), _F32),
                        pltpu.VMEM((hb, c, dk), _F32)],
        compiler_params=_params(("parallel", "parallel", "arbitrary")),
        name="gla_scan",
    )(proj, proj, proj, proj, bcum, o_norm_w.reshape(1, dv).astype(_F32), _gla_levels(c, _GLA_DIAG))


def _gla_mixer(h, norm_w, batch, seq, layer, wt_in, w_gk_up, b_gk_up, o_norm_w, w_out):
    rank, kd = w_gk_up.shape
    vd = (wt_in.shape[1] - rank - 2 * kd) // 2
    dv = o_norm_w.shape[0]
    heads = vd // dv
    n_main = 2 * kd + 2 * vd
    w_up = jnp.pad(w_gk_up, ((0, _LANES - rank), (0, 0))).astype(_MXU_DTYPE)
    proj, u = _matmul(h, wt_in, layer, wt=True, n=n_main, norm_w=norm_w, emit_norm=True, act="silu",
                      act_col0=2 * kd + vd, out_dtype=_MXU_DTYPE)
    bcum = _gla_gate(u, wt_in, layer, n_main, rank, w_up, b_gk_up.reshape(1, kd).astype(_F32))
    o = _gla_scan(proj, bcum, o_norm_w, batch, seq, heads, kd // heads, dv)
    return _matmul(o, w_out.astype(_MXU_DTYPE), layer, res=h, out_dtype=_F32)


def _ssd_gate_kernel(u_ref, wdt_ref, bias_ref, alog_ref, acol_ref, dcol_ref, arow_ref, drow_ref,
                     *, chunk, heads, groups):
    raw = lax.dot_general(u_ref[...], _tail_rows(wdt_ref, heads), _NT, preferred_element_type=_F32)
    dt = _softplus(raw + bias_ref[...])
    a = _chunk_cumsum(dt * (-jnp.exp(alog_ref[...])) * _LOG2E, chunk)
    hg = heads // groups
    for g in range(groups):
        acol_ref[g] = a[:, g * hg:(g + 1) * hg]
        dcol_ref[g] = dt[:, g * hg:(g + 1) * hg]
    for i in range(a.shape[0] // chunk):
        arow_ref[i] = a[i * chunk:(i + 1) * chunk, :].T[:heads, :]
        drow_ref[i] = dt[i * chunk:(i + 1) * chunk, :].T[:heads, :]


def _ssd_gate(u, wt_in, layer, col0, dt_bias, a_log, heads, rows=512):
    m, d = u.shape
    c = _CHUNK
    rows = min(rows, m)
    g = _SSD_GROUPS
    hg = heads // g
    assert col0 % _LANES == 0 and col0 + heads == wt_in.shape[1] and heads <= _LANES
    col = jax.ShapeDtypeStruct((g, m, hg), _F32)
    row = jax.ShapeDtypeStruct((m // c, heads, c), _F32)
    col_spec = pl.BlockSpec((g, rows, hg), lambda i: (0, i, 0))
    row_spec = pl.BlockSpec((rows // c, heads, c), lambda i: (i, 0, 0))
    return pl.pallas_call(
        functools.partial(_ssd_gate_kernel, chunk=c, heads=heads, groups=g),
        grid=(m // rows,),
        in_specs=[pl.BlockSpec((rows, d), lambda i: (i, 0)),
                  pl.BlockSpec((None, _LANES, d), lambda i: (layer, col0 // _LANES, 0)),
                  pl.BlockSpec((1, _LANES), lambda i: (0, 0)),
                  pl.BlockSpec((1, _LANES), lambda i: (0, 0))],
        out_specs=[col_spec, col_spec, row_spec, row_spec],
        out_shape=[col, col, row, row],
        compiler_params=_params(("parallel",)),
        name="ssd_gate",
    )(u, wt_in, dt_bias, a_log)


def _lane_bcasts(colarr, n):
    return [jnp.broadcast_to(colarr[:, i:i + 1], (colarr.shape[0], _LANES)) for i in range(n)]


def _expand_heads(bcasts, head_dim):
    c = bcasts[0].shape[0]
    per = _LANES // head_dim
    lane = lax.broadcasted_iota(jnp.int32, (c, _LANES), 1)
    pieces = []
    for p in range(len(bcasts) // per):
        out = bcasts[p * per]
        for i in range(1, per):
            out = jnp.where(lane >= i * head_dim, bcasts[p * per + i], out)
        pieces.append(out)
    return jnp.concatenate(pieces, axis=1)


def _ssd_scan_group(zs_ref, x_ref, bm_ref, cm_ref, acol_ref, dcol_ref, arow_ref, drow_ref, dskip_ref, gnw_ref,
                    o_ref, st_ref, head_dim):
    c = zs_ref.shape[0]
    hg = acol_ref.shape[1]
    xs = x_ref[...].astype(_F32)
    bmx = bm_ref[...]
    cmx = cm_ref[...]

    acol = acol_ref[...]
    dcol = dcol_ref[...]
    a_last = acol[c - 1:c, :]
    w_state = jnp.exp2(a_last - acol) * dcol

    per = _LANES // head_dim
    width = per * c
    cb = lax.dot_general(cmx, jnp.concatenate([bmx] * per, axis=0), _NT,
                         preferred_element_type=_F32)
    ri = lax.broadcasted_iota(jnp.int32, (c, width), 0)
    ci = lax.broadcasted_iota(jnp.int32, (c, width), 1) & (c - 1)
    causal = ci <= ri
    lane = lax.broadcasted_iota(jnp.int32, (c, _LANES), 1)
    y_pieces = []
    a_bc = _lane_bcasts(acol, hg)
    for p in range(hg // per):
        ac = jnp.concatenate(a_bc[p * per:(p + 1) * per], axis=1)
        ar = arow_ref[:, p * width:(p + 1) * width]
        dr = drow_ref[:, p * width:(p + 1) * width]
        decay = jnp.exp2(jnp.where(causal, ac - ar, _NEG_BIG))
        sc = (cb * decay * dr).astype(_MXU_DTYPE)
        xp = x_ref[:, p * _LANES:(p + 1) * _LANES]
        bd = jnp.concatenate(
            [jnp.where((lane >= i * head_dim) & (lane < (i + 1) * head_dim), xp, jnp.zeros_like(xp))
             for i in range(per)], axis=0)
        y_pieces.append(jnp.dot(sc, bd, preferred_element_type=_F32))
    y = jnp.concatenate(y_pieces, axis=1)

    st = st_ref[...]
    ea = jnp.exp2(_expand_heads(a_bc, head_dim))
    y = y + jnp.dot(cmx, st.astype(_MXU_DTYPE), preferred_element_type=_F32) * ea
    xw = (xs * _expand_heads(_lane_bcasts(w_state, hg), head_dim)).astype(_MXU_DTYPE)
    st_ref[...] = st * ea[c - 1:c, :] + lax.dot_general(bmx, xw, _TN, preferred_element_type=_F32)

    y = y + dskip_ref[...] * xs
    y = y * zs_ref[...].astype(_F32)
    ms = jnp.mean(y * y, axis=-1, keepdims=True)
    o_ref[...] = ((y * lax.rsqrt(ms + _RMS_EPS)) * gnw_ref[...]).astype(o_ref.dtype)


def _ssd_scan_kernel(zs_ref, x_ref, bm_ref, cm_ref, acol_ref, dcol_ref, arow_ref, drow_ref, dskip_ref, gnw_ref,
                     o_ref, st_ref, *, head_dim):
    @pl.when(pl.program_id(2) == 0)
    def _():
        st_ref[...] = jnp.zeros_like(st_ref)

    gb, n, gw = st_ref.shape
    c = zs_ref.shape[0]
    hg = acol_ref.shape[2]
    for g in range(gb):
        xs_, ns_, rs_ = pl.ds(g * gw, gw), pl.ds(g * n, n), pl.ds(g * hg * c, hg * c)
        _ssd_scan_group(zs_ref.at[:, xs_], x_ref.at[:, xs_], bm_ref.at[:, ns_], cm_ref.at[:, ns_],
                        acol_ref.at[g], dcol_ref.at[g], arow_ref.at[0, :, rs_], drow_ref.at[0, :, rs_],
                        dskip_ref.at[:, xs_], gnw_ref.at[:, xs_], o_ref.at[:, xs_], st_ref.at[g], head_dim)


def _ssd_scan(zs, xbc, acol, dcol, arow, drow, dskip, gnorm_w, batch, seq, inner, heads, n_state):
    m = zs.shape[0]
    c = _CHUNK
    g = _SSD_GROUPS
    gb = _SSD_GROUPS_PER_STEP if g % _SSD_GROUPS_PER_STEP == 0 else 1
    hg = heads // g
    p = inner // heads
    gw = hg * p
    sw, sn = gb * gw, gb * n_state
    assert c == _LANES and gw % _LANES == 0 and n_state % _LANES == 0 and _LANES % p == 0
    assert inner % sn == 0 and (g * n_state) % sn == 0
    nt = seq // c
    b_blk = inner // sn
    c_blk = (inner + g * n_state) // sn

    def rows(b, t):
        return b * nt + t

    in_specs = [
        pl.BlockSpec((c, sw), lambda b, gi, t: (rows(b, t), gi)),
        pl.BlockSpec((c, sw), lambda b, gi, t: (rows(b, t), gi)),
        pl.BlockSpec((c, sn), lambda b, gi, t: (rows(b, t), b_blk + gi)),
        pl.BlockSpec((c, sn), lambda b, gi, t: (rows(b, t), c_blk + gi)),
        pl.BlockSpec((gb, c, hg), lambda b, gi, t: (gi, rows(b, t), 0)),
        pl.BlockSpec((gb, c, hg), lambda b, gi, t: (gi, rows(b, t), 0)),
        pl.BlockSpec((1, 1, gb * hg * c), lambda b, gi, t: (rows(b, t), 0, gi)),
        pl.BlockSpec((1, 1, gb * hg * c), lambda b, gi, t: (rows(b, t), 0, gi)),
        pl.BlockSpec((1, sw), lambda b, gi, t: (0, gi)),
        pl.BlockSpec((1, sw), lambda b, gi, t: (0, gi)),
    ]
    return pl.pallas_call(
        functools.partial(_ssd_scan_kernel, head_dim=p),
        grid=(batch, g // gb, nt),
        in_specs=in_specs,
        out_specs=pl.BlockSpec((c, sw), lambda b, gi, t: (rows(b, t), gi)),
        out_shape=jax.ShapeDtypeStruct((m, inner), _MXU_DTYPE),
        scratch_shapes=[pltpu.VMEM((gb, n_state, gw), _F32)],
        compiler_params=_params(("parallel", "parallel", "arbitrary")),
        name="ssd_scan",
    )(zs, xbc, xbc, xbc, acol, dcol, arow, drow, dskip, gnorm_w)


def _ssd_mixer(h, norm_w, batch, seq, layer, wt_in, conv_w, conv_b, dt_bias, a_log, d_skip, gnorm_w, w_out):
    heads = dt_bias.shape[0]
    conv_dim = conv_w.shape[1]
    inner = wt_in.shape[1] - conv_dim - heads
    n_state = (conv_dim - inner) // (2 * _SSD_GROUPS)
    n_main = inner + conv_dim
    m = h.shape[0]
    pad1 = lambda v: jnp.pad(v.astype(_F32), (0, _LANES - heads)).reshape(1, _LANES)
    zs, u = _matmul(h, wt_in, layer, wt=True, n=inner, norm_w=norm_w, emit_norm=True, act="silu",
                    out_dtype=_MXU_DTYPE)
    xbc = _proj_conv(u, wt_in, layer, inner, conv_w.astype(_F32), conv_b.reshape(1, conv_dim).astype(_F32),
                     seq, _MXU_DTYPE)
    acol, dcol, arow, drow = _ssd_gate(u, wt_in, layer, n_main, pad1(dt_bias), pad1(a_log), heads)
    arow = arow.reshape(m // _CHUNK, 1, heads * _CHUNK)
    drow = drow.reshape(m // _CHUNK, 1, heads * _CHUNK)
    dskip = jnp.repeat(d_skip.astype(_F32), inner // heads).reshape(1, inner)
    y = _ssd_scan(zs, xbc, acol, dcol, arow, drow, dskip, gnorm_w.reshape(1, inner).astype(_F32),
                  batch, seq, inner, heads, n_state)
    return _matmul(y, w_out.astype(_MXU_DTYPE), layer, res=h, out_dtype=_F32)


def kernel(x, mixer_norm_w, gla_w_in, gla_w_gk_up, gla_b_gk_up, gla_o_norm_w, gla_w_out, ssd_w_in, ssd_conv_w, ssd_conv_b, ssd_dt_bias, ssd_a_log, ssd_d_skip, ssd_gnorm_w, ssd_w_out, mlp_norm_w, mlp_w_fc1, mlp_w_fc2, final_norm_w):
    batch, seq, d = x.shape
    h = x.reshape(batch * seq, d)
    w_fc2 = mlp_w_fc2.astype(_MXU_DTYPE)
    gla_wt_in = jnp.swapaxes(gla_w_in, 1, 2)
    ssd_wt_in = jnp.swapaxes(ssd_w_in, 1, 2)
    for i in range(mixer_norm_w.shape[0]):
        j = i // 2
        if i % 2 == 0:
            h = _gla_mixer(h, mixer_norm_w[i], batch, seq, j, gla_wt_in, gla_w_gk_up[j], gla_b_gk_up[j],
                           gla_o_norm_w[j], gla_w_out)
        else:
            h = _ssd_mixer(h, mixer_norm_w[i], batch, seq, j, ssd_wt_in, ssd_conv_w[j], ssd_conv_b[j],
                           ssd_dt_bias[j], ssd_a_log[j], ssd_d_skip[j], ssd_gnorm_w[j], ssd_w_out)
        hidden = _matmul(h, mlp_w_fc1, i, norm_w=mlp_norm_w[i], act="relu2", out_dtype=_MXU_DTYPE)
        h = _matmul(hidden, w_fc2, i, res=h, out_dtype=_F32)
    return _rmsnorm(h, final_norm_w, _F32).reshape(batch, seq, d)
```

```python
import functools
import math

import jax
import jax.numpy as jnp
from jax import lax
from jax.experimental import pallas as pl
from jax.experimental.pallas import tpu as pltpu

_F32 = jnp.float32
_MXU_DTYPE = jnp.bfloat16
_RMS_EPS = 1e-5
_GLA_GATE_NORMALIZER = 16.0
_SSD_GROUPS = 8
_CHUNK = 128
_LANES = 128
_GLA_DIAG = 8
_GLA_HEADS_PER_STEP = 4
_SSD_GROUPS_PER_STEP = 4
_LOG2E = math.log2(math.e)
_NEG_BIG = -1e30
_HALO = 16
_MXU_WIDTH = 256
_VMEM_LIMIT_BYTES = 56 * 1024 * 1024
_MATMUL_VMEM_BUDGET = 52 * 1024 * 1024

_NN = (((1,), (0,)), ((), ()))
_NT = (((1,), (1,)), ((), ()))
_TN = (((0,), (0,)), ((), ()))


def _params(sem, flags=None):
    return pltpu.CompilerParams(dimension_semantics=sem, vmem_limit_bytes=_VMEM_LIMIT_BYTES, flags=flags)


def _rms(x, w):
    ms = jnp.mean(x * x, axis=-1, keepdims=True)
    return (x * lax.rsqrt(ms + _RMS_EPS)) * w


def _rmsnorm_kernel(x_ref, w_ref, o_ref):
    o_ref[...] = _rms(x_ref[...].astype(_F32), w_ref[...]).astype(o_ref.dtype)


def _rmsnorm(x, w, out_dtype, rows=512):
    m, d = x.shape
    rows = min(rows, m)
    return pl.pallas_call(
        _rmsnorm_kernel,
        grid=(m // rows,),
        in_specs=[pl.BlockSpec((rows, d), lambda i: (i, 0)),
                  pl.BlockSpec((1, d), lambda i: (0, 0))],
        out_specs=pl.BlockSpec((rows, d), lambda i: (i, 0)),
        out_shape=jax.ShapeDtypeStruct((m, d), out_dtype),
        compiler_params=_params(("parallel",)),
        name="rmsnorm",
    )(x, w.reshape(1, d).astype(_F32))


def _silu(x):
    half = 0.5 * x
    return half + half * jnp.tanh(half)


def _matmul_kernel(*refs, act, act_tile0, has_norm, has_res, wt):
    it = iter(refs)
    a_ref = next(it)
    nw_ref = next(it) if has_norm else None
    w_ref = next(it)
    r_ref = next(it) if has_res else None
    o_ref = next(it)
    if has_norm:
        an_ref = next(it)

        @pl.when(pl.program_id(1) == 0)
        def _():
            an_ref[...] = _rms(a_ref[...], nw_ref[...]).astype(_MXU_DTYPE)

        a = an_ref[...]
    else:
        a = a_ref[...]
    w = w_ref[...].astype(_MXU_DTYPE)
    acc = lax.dot_general(a, w, _NT if wt else _NN, preferred_element_type=_F32)
    if act == "relu2":
        act_acc = jnp.square(jnp.maximum(acc, 0.0))
    elif act == "silu":
        act_acc = _silu(acc)
    if act is not None:
        acc = act_acc if act_tile0 == 0 else jnp.where(pl.program_id(1) >= act_tile0, act_acc, acc)
    if has_res:
        acc = r_ref[...] + acc
    o_ref[...] = acc.astype(o_ref.dtype)


def _divisor_tile(n, pref, align):
    if n <= pref:
        return n
    t = (pref // align) * align
    while n % t:
        t -= align
    return t


def _matmul_tiles(m, k, n, a_bytes, w_bytes, out_bytes, norm_copies, has_res, conv_taps=0, row_cap=None):
    mxu_bytes = jnp.dtype(_MXU_DTYPE).itemsize
    for pm, pn in ((1024, 1024), (1024, 512), (1024, 256), (512, 256), (256, 256), (128, 128)):
        pm = pm if row_cap is None else min(pm, row_cap)
        tm, tn = _divisor_tile(m, pm, 8), _divisor_tile(n, pn, _LANES)
        need = 2 * tm * k * a_bytes + 2 * k * tn * w_bytes + 2 * tm * tn * out_bytes
        need += tm * tn * 4 if not conv_taps else (1 + conv_taps) * tm * _MXU_WIDTH * 4
        need += 2 * tm * tn * 4 if has_res else 0
        need += norm_copies * tm * k * mxu_bytes
        need += k * tn * mxu_bytes if w_bytes != mxu_bytes else 0
        if need <= _MATMUL_VMEM_BUDGET:
            break
    return tm, tn


def _matmul(a, w, layer, *, wt=False, n=None, norm_w=None, emit_norm=False, act=None, act_col0=0, res=None,
            out_dtype):
    m, k = a.shape
    n = w.shape[1 if wt else 2] if n is None else n
    tm, tn = _matmul_tiles(m, k, math.gcd(n, act_col0), a.dtype.itemsize, w.dtype.itemsize,
                           jnp.dtype(out_dtype).itemsize, (norm_w is not None) + emit_norm, res is not None)
    in_specs = [pl.BlockSpec((tm, k), lambda i, j: (i, 0))]
    args = [a]
    if norm_w is not None:
        in_specs.append(pl.BlockSpec((1, k), lambda i, j: (0, 0)))
        args.append(norm_w.reshape(1, k).astype(_F32))
    in_specs.append(pl.BlockSpec((None, tn, k), lambda i, j: (layer, j, 0)) if wt else
                    pl.BlockSpec((None, k, tn), lambda i, j: (layer, 0, j)))
    args.append(w)
    if res is not None:
        in_specs.append(pl.BlockSpec((tm, tn), lambda i, j: (i, j)))
        args.append(res)
    out_specs = [pl.BlockSpec((tm, tn), lambda i, j: (i, j))]
    out_shape = [jax.ShapeDtypeStruct((m, n), out_dtype)]
    scratch = []
    if emit_norm:
        out_specs.append(pl.BlockSpec((tm, k), lambda i, j: (i, 0)))
        out_shape.append(jax.ShapeDtypeStruct((m, k), _MXU_DTYPE))
    elif norm_w is not None:
        scratch.append(pltpu.VMEM((tm, k), _MXU_DTYPE))
    outs = pl.pallas_call(
        functools.partial(_matmul_kernel, act=act, act_tile0=act_col0 // tn, has_norm=norm_w is not None,
                          has_res=res is not None, wt=wt),
        grid=(m // tm, n // tn),
        in_specs=in_specs,
        out_specs=out_specs,
        out_shape=out_shape,
        scratch_shapes=scratch,
        compiler_params=_params(("parallel", "arbitrary")),
        name="matmul",
    )(*args)
    return outs if emit_norm else outs[0]


def _proj_conv_kernel(u_ref, halo_ref, w_ref, cw_ref, cb_ref, o_ref, an_ref, *, tiles_per_seq):
    kw = cw_ref.shape[0]

    @pl.when(pl.program_id(1) == 0)
    def _():
        halo = halo_ref[...]
        an_ref[0:_HALO, :] = jnp.where(pl.program_id(0) % tiles_per_seq == 0, jnp.zeros_like(halo), halo)
        an_ref[_HALO:, :] = u_ref[...]

    wb = w_ref[...].astype(_MXU_DTYPE)
    for c0 in range(0, o_ref.shape[1], _MXU_WIDTH):
        cols = pl.ds(c0, _MXU_WIDTH)
        acc = lax.dot_general(an_ref[...], wb[c0:c0 + _MXU_WIDTH, :], _NT, preferred_element_type=_F32)
        tm = o_ref.shape[0]
        out = cb_ref[:, cols] + cw_ref[kw - 1:kw, cols] * acc[_HALO:, :]
        for s in range(1, kw):
            out = out + cw_ref[kw - 1 - s:kw - s, cols] * acc[_HALO - s:_HALO - s + tm, :]
        o_ref[:, cols] = _silu(out).astype(o_ref.dtype)


def _proj_conv(u, w, layer, col0, conv_w, conv_b, seq, out_dtype):
    m, k = u.shape
    kw, n = conv_w.shape
    tm, tn = _matmul_tiles(m, k, math.gcd(n, col0), u.dtype.itemsize, w.dtype.itemsize,
                           jnp.dtype(out_dtype).itemsize, 1, False, conv_taps=kw, row_cap=seq)
    assert seq % tm == 0 and tm % _HALO == 0 and tn % _MXU_WIDTH == 0 and kw - 1 <= _HALO
    return pl.pallas_call(
        functools.partial(_proj_conv_kernel, tiles_per_seq=seq // tm),
        grid=(m // tm, n // tn),
        in_specs=[pl.BlockSpec((tm, k), lambda i, j: (i, 0)),
                  pl.BlockSpec((_HALO, k), lambda i, j: (jnp.maximum(i * (tm // _HALO) - 1, 0), 0)),
                  pl.BlockSpec((None, tn, k), lambda i, j: (layer, col0 // tn + j, 0)),
                  pl.BlockSpec((kw, tn), lambda i, j: (0, j)),
                  pl.BlockSpec((1, tn), lambda i, j: (0, j))],
        out_specs=pl.BlockSpec((tm, tn), lambda i, j: (i, j)),
        out_shape=jax.ShapeDtypeStruct((m, n), out_dtype),
        scratch_shapes=[pltpu.VMEM((tm + _HALO, k), _MXU_DTYPE)],
        compiler_params=_params(("parallel", "arbitrary")),
        name="proj_conv",
    )(u, u, w, conv_w, conv_b)


def _softplus(x):
    return jnp.maximum(x, 0.0) + jnp.log(1.0 + jnp.exp(-jnp.abs(x)))


def _chunk_cumsum(x, chunk):
    rows = x.shape[0]
    r = lax.broadcasted_iota(jnp.int32, (chunk, chunk), 0)
    c = lax.broadcasted_iota(jnp.int32, (chunk, chunk), 1)
    tri = jnp.where(c <= r, 1.0, 0.0).astype(_MXU_DTYPE)
    out = []
    for i in range(rows // chunk):
        rem = x[i * chunk:(i + 1) * chunk, :]
        acc = None
        for _ in range(3):
            piece = rem.astype(_MXU_DTYPE)
            d = jnp.dot(tri, piece, preferred_element_type=_F32)
            acc = d if acc is None else acc + d
            rem = rem - piece.astype(_F32)
        out.append(acc)
    return jnp.concatenate(out, axis=0) if len(out) > 1 else out[0]


def _row_bcast(ref, row, n):
    return jnp.broadcast_to(ref[pl.ds(row, 1), :], (n, ref.shape[1]))


def _tail_rows(w_ref, n_valid):
    row = lax.broadcasted_iota(jnp.int32, w_ref.shape, 0)
    return jnp.where(row < n_valid, w_ref[...], 0.0).astype(_MXU_DTYPE)


def _gla_gate_kernel(u_ref, wgr_ref, wup_ref, b_ref, o_ref, *, chunk, rank):
    gr = lax.dot_general(u_ref[...], _tail_rows(wgr_ref, rank), _NT, preferred_element_type=_F32)
    pre = jnp.dot(gr.astype(_MXU_DTYPE), wup_ref[...], preferred_element_type=_F32) + b_ref[...]
    gk = -_softplus(-pre) / _GLA_GATE_NORMALIZER
    o_ref[...] = _chunk_cumsum(gk * _LOG2E, chunk)


def _gla_gate(u, wt_in, layer, col0, rank, w_up, b_up, rows=512):
    m, d = u.shape
    kd = w_up.shape[1]
    rows = min(rows, m)
    assert col0 % _LANES == 0 and col0 + rank == wt_in.shape[1] and rank <= _LANES
    return pl.pallas_call(
        functools.partial(_gla_gate_kernel, chunk=_CHUNK, rank=rank),
        grid=(m // rows,),
        in_specs=[pl.BlockSpec((rows, d), lambda i: (i, 0)),
                  pl.BlockSpec((None, _LANES, d), lambda i: (layer, col0 // _LANES, 0)),
                  pl.BlockSpec(w_up.shape, lambda i: (0, 0)),
                  pl.BlockSpec((1, kd), lambda i: (0, 0))],
        out_specs=pl.BlockSpec((rows, kd), lambda i: (i, 0)),
        out_shape=jax.ShapeDtypeStruct((m, kd), _F32),
        compiler_params=_params(("parallel",)),
        name="gla_gate",
    )(u, wt_in, w_up, b_up)


def _gla_levels(c, nd):
    i = lax.broadcasted_iota(jnp.int32, (c, c), 0)
    j = lax.broadcasted_iota(jnp.int32, (c, c), 1)
    x = i ^ j
    lvl = jnp.zeros((c, c), jnp.int32)
    s, level = nd, 1
    while s < c:
        lvl = jnp.where(x >= s, level, lvl)
        s, level = 2 * s, level + 1
    return jnp.where(j > i, -1, lvl)


def _gla_scan_head(q_ref, k_ref, v_ref, gs_ref, b_ref, onw_ref, lvl_ref, o_ref, st_ref, qf_ref, kf_ref, scale):
    c, dk = q_ref.shape
    qf_ref[...] = q_ref[...].astype(_F32) * scale
    kf_ref[...] = k_ref[...].astype(_F32)
    q = qf_ref[...]
    k = kf_ref[...]
    v = v_ref[...]
    bc = b_ref[...]
    st = st_ref[...]
    b_last = jnp.concatenate([_row_bcast(b_ref, c - 1, 8)] * (c // 8), axis=0)

    o = lax.dot_general((q * jnp.exp2(bc)).astype(_MXU_DTYPE), st.astype(_MXU_DTYPE), _NT,
                        preferred_element_type=_F32)
    k_dec = (k * jnp.exp2(b_last - bc)).astype(_MXU_DTYPE)

    nd = _GLA_DIAG
    lane = lax.broadcasted_iota(jnp.int32, (nd, c), 1)
    pieces = []
    for m in range(c // nd):
        qb = qf_ref[m * nd:(m + 1) * nd, :]
        bb = b_ref[m * nd:(m + 1) * nd, :]
        acc = jnp.zeros((nd, c), _F32)
        for j in range(nd):
            kj = _row_bcast(kf_ref, m * nd + j, nd)
            bj = _row_bcast(b_ref, m * nd + j, nd)
            term = qb * kj * jnp.exp2(jnp.minimum(bb - bj, 0.0))
            acc = jnp.where(lane == m * nd + j, jnp.sum(term, axis=1, keepdims=True), acc)
        pieces.append(acc)
    lvl = lvl_ref[...]
    att = jnp.where(lvl == 0, jnp.concatenate(pieces, axis=0), 0.0)

    s, level = nd, 1
    while s < c:
        prev = [jnp.zeros((s, dk), _F32)]
        end = []
        for m in range(c // s):
            if m:
                prev += [_row_bcast(b_ref, m * s - 1, 8)] * (s // 8)
            end += [_row_bcast(b_ref, (m + 1) * s - 1, 8)] * (s // 8)
        qs = (q * jnp.exp2(bc - jnp.concatenate(prev, axis=0))).astype(_MXU_DTYPE)
        ks = (k * jnp.exp2(jnp.concatenate(end, axis=0) - bc)).astype(_MXU_DTYPE)
        att = jnp.where(lvl == level, lax.dot_general(qs, ks, _NT, preferred_element_type=_F32), att)
        s, level = 2 * s, level + 1

    o = o + jnp.dot(att.astype(_MXU_DTYPE), v, preferred_element_type=_F32)
    st_ref[...] = st * jnp.exp2(b_last[:1, :]) + lax.dot_general(
        v, k_dec, _TN, preferred_element_type=_F32)

    ms = jnp.mean(o * o, axis=-1, keepdims=True)
    on = (o * lax.rsqrt(ms + _RMS_EPS)) * onw_ref[...]
    o_ref[...] = (on * gs_ref[...].astype(_F32)).astype(o_ref.dtype)


def _gla_scan_kernel(q_ref, k_ref, v_ref, gs_ref, b_ref, onw_ref, lvl_ref, o_ref, st_ref, qf_ref, kf_ref,
                     *, scale, dk, dv):
    @pl.when(pl.program_id(2) == 0)
    def _():
        st_ref[...] = jnp.zeros_like(st_ref)

    for h in range(st_ref.shape[0]):
        ks, vs = pl.ds(h * dk, dk), pl.ds(h * dv, dv)
        _gla_scan_head(q_ref.at[:, ks], k_ref.at[:, ks], v_ref.at[:, vs], gs_ref.at[:, vs], b_ref.at[:, ks],
                       onw_ref, lvl_ref, o_ref.at[:, vs], st_ref.at[h], qf_ref.at[h], kf_ref.at[h], scale)


def _gla_scan(proj, bcum, o_norm_w, batch, seq, heads, dk, dv):
    m = proj.shape[0]
    c = _CHUNK
    hb = _GLA_HEADS_PER_STEP if heads % _GLA_HEADS_PER_STEP == 0 else 1
    assert c == _LANES and dk % _LANES == 0 and dv % _LANES == 0 and (2 * heads * dk) % (hb * dv) == 0
    nt = seq // c
    kd, vd = heads * dk, heads * dv
    wk, wv = hb * dk, hb * dv
    k_blk, v_blk, g_blk = kd // wk, (2 * kd) // wv, (2 * kd + vd) // wv

    def rows(b, h, t):
        return b * nt + t

    return pl.pallas_call(
        functools.partial(_gla_scan_kernel, scale=dk ** -0.5, dk=dk, dv=dv),
        grid=(batch, heads // hb, nt),
        in_specs=[pl.BlockSpec((c, wk), lambda b, h, t: (rows(b, h, t), h)),
                  pl.BlockSpec((c, wk), lambda b, h, t: (rows(b, h, t), k_blk + h)),
                  pl.BlockSpec((c, wv), lambda b, h, t: (rows(b, h, t), v_blk + h)),
                  pl.BlockSpec((c, wv), lambda b, h, t: (rows(b, h, t), g_blk + h)),
                  pl.BlockSpec((c, wk), lambda b, h, t: (rows(b, h, t), h)),
                  pl.BlockSpec((1, dv), lambda b, h, t: (0, 0)),
                  pl.BlockSpec((c, c), lambda b, h, t: (0, 0))],
        out_specs=pl.BlockSpec((c, wv), lambda b, h, t: (rows(b, h, t), h)),
        out_shape=jax.ShapeDtypeStruct((m, vd), _MXU_DTYPE),
        scratch_shapes=[pltpu.VMEM((hb, dv, dk), _F32),
                        pltpu.VMEM((hb, c, dk), _F32),
                        pltpu.VMEM((hb, c, dk), _F32)],
        compiler_params=_params(("parallel", "parallel", "arbitrary")),
        name="gla_scan",
    )(proj, proj, proj, proj, bcum, o_norm_w.reshape(1, dv).astype(_F32), _gla_levels(c, _GLA_DIAG))


def _gla_mixer(h, norm_w, batch, seq, layer, wt_in, w_gk_up, b_gk_up, o_norm_w, w_out):
    rank, kd = w_gk_up.shape
    vd = (wt_in.shape[1] - rank - 2 * kd) // 2
    dv = o_norm_w.shape[0]
    heads = vd // dv
    n_main = 2 * kd + 2 * vd
    w_up = jnp.pad(w_gk_up, ((0, _LANES - rank), (0, 0))).astype(_MXU_DTYPE)
    proj, u = _matmul(h, wt_in, layer, wt=True, n=n_main, norm_w=norm_w, emit_norm=True, act="silu",
                      act_col0=2 * kd + vd, out_dtype=_MXU_DTYPE)
    bcum = _gla_gate(u, wt_in, layer, n_main, rank, w_up, b_gk_up.reshape(1, kd).astype(_F32))
    o = _gla_scan(proj, bcum, o_norm_w, batch, seq, heads, kd // heads, dv)
    return _matmul(o, w_out.astype(_MXU_DTYPE), layer, res=h, out_dtype=_F32)


def _ssd_gate_kernel(u_ref, wdt_ref, bias_ref, alog_ref, acol_ref, dcol_ref, arow_ref, drow_ref,
                     *, chunk, heads):
    raw = lax.dot_general(u_ref[...], _tail_rows(wdt_ref, heads), _NT, preferred_element_type=_F32)
    dt = _softplus(raw + bias_ref[...])
    a = _chunk_cumsum(dt * (-jnp.exp(alog_ref[...])) * _LOG2E, chunk)
    acol_ref[...] = a
    dcol_ref[...] = dt
    for i in range(a.shape[0] // chunk):
        arow_ref[i] = a[i * chunk:(i + 1) * chunk, :].T[:heads, :]
        drow_ref[i] = dt[i * chunk:(i + 1) * chunk, :].T[:heads, :]


def _ssd_gate(u, wt_in, layer, col0, dt_bias, a_log, heads, rows=512):
    m, d = u.shape
    c = _CHUNK
    rows = min(rows, m)
    g = _SSD_GROUPS
    hg = heads // g
    assert col0 % _LANES == 0 and col0 + heads == wt_in.shape[1] and heads <= _LANES
    col = jax.ShapeDtypeStruct((m, _LANES), _F32)
    row = jax.ShapeDtypeStruct((m // c, heads, c), _F32)
    col_spec = pl.BlockSpec((rows, _LANES), lambda i: (i, 0))
    row_spec = pl.BlockSpec((rows // c, heads, c), lambda i: (i, 0, 0))
    return pl.pallas_call(
        functools.partial(_ssd_gate_kernel, chunk=c, heads=heads),
        grid=(m // rows,),
        in_specs=[pl.BlockSpec((rows, d), lambda i: (i, 0)),
                  pl.BlockSpec((None, _LANES, d), lambda i: (layer, col0 // _LANES, 0)),
                  pl.BlockSpec((1, _LANES), lambda i: (0, 0)),
                  pl.BlockSpec((1, _LANES), lambda i: (0, 0))],
        out_specs=[col_spec, col_spec, row_spec, row_spec],
        out_shape=[col, col, row, row],
        compiler_params=_params(("parallel",)),
        name="ssd_gate",
    )(u, wt_in, dt_bias, a_log)


def _head_expanders(groups, hg, c, head_dim):
    def one_hot(rows, width):
        row = lax.broadcasted_iota(jnp.int32, (groups, rows, hg * width), 1) % _LANES
        col = lax.broadcasted_iota(jnp.int32, (groups, rows, hg * width), 2)
        grp = lax.broadcasted_iota(jnp.int32, (groups, rows, hg * width), 0)
        return (row == grp * hg + col // width).astype(_MXU_DTYPE)
    return one_hot(2 * _LANES, c), one_hot(_LANES, head_dim)


def _expand_heads(bcasts, head_dim):
    c = bcasts[0].shape[0]
    per = _LANES // head_dim
    lane = lax.broadcasted_iota(jnp.int32, (c, _LANES), 1)
    pieces = []
    for p in range(len(bcasts) // per):
        out = bcasts[p * per]
        for i in range(1, per):
            out = jnp.where(lane >= i * head_dim, bcasts[p * per + i], out)
        pieces.append(out)
    return jnp.concatenate(pieces, axis=1)


def _ssd_scan_group(zs_ref, x_ref, bm_ref, cm_ref, acol_ref, dcol_ref, esc_ref, ehp_ref, arow_ref, drow_ref,
                    dskip_ref, gnw_ref, o_ref, st_ref, head_dim):
    c = zs_ref.shape[0]
    hg = x_ref.shape[1] // head_dim
    xs = x_ref[...].astype(_F32)
    bmx = bm_ref[...]
    cmx = cm_ref[...]

    acol = acol_ref[...]
    a_last = acol[c - 1:c, :]
    w_state = (jnp.exp2(a_last - acol) * dcol_ref[...]).astype(_MXU_DTYPE)
    a_hi = acol.astype(_MXU_DTYPE)
    a_lo = (acol - a_hi.astype(_F32)).astype(_MXU_DTYPE)
    a_sc = jnp.dot(jnp.concatenate([a_hi, a_lo], axis=1), esc_ref[...],
                   preferred_element_type=_F32)
    w_hp = jnp.dot(w_state, ehp_ref[...], preferred_element_type=_F32)

    per = _LANES // head_dim
    width = per * c
    cb = lax.dot_general(cmx, jnp.concatenate([bmx] * per, axis=0), _NT,
                         preferred_element_type=_F32)
    ri = lax.broadcasted_iota(jnp.int32, (c, width), 0)
    ci = lax.broadcasted_iota(jnp.int32, (c, width), 1) & (c - 1)
    causal = ci <= ri
    lane = lax.broadcasted_iota(jnp.int32, (c, _LANES), 1)
    y_pieces = []
    for p in range(hg // per):
        ac = a_sc[:, p * width:(p + 1) * width]
        ar = arow_ref[:, p * width:(p + 1) * width]
        dr = drow_ref[:, p * width:(p + 1) * width]
        decay = jnp.exp2(jnp.where(causal, ac - ar, _NEG_BIG))
        sc = (cb * decay * dr).astype(_MXU_DTYPE)
        xp = x_ref[:, p * _LANES:(p + 1) * _LANES]
        bd = jnp.concatenate(
            [jnp.where((lane >= i * head_dim) & (lane < (i + 1) * head_dim), xp, jnp.zeros_like(xp))
             for i in range(per)], axis=0)
        y_pieces.append(jnp.dot(sc, bd, preferred_element_type=_F32))
    y = jnp.concatenate(y_pieces, axis=1)

    st = st_ref[...]
    ea = jnp.exp2(_expand_heads([a_sc[:, h * c:(h + 1) * c] for h in range(hg)], head_dim))
    y = y + jnp.dot(cmx, st.astype(_MXU_DTYPE), preferred_element_type=_F32) * ea
    xw = (xs * w_hp).astype(_MXU_DTYPE)
    st_ref[...] = st * ea[c - 1:c, :] + lax.dot_general(bmx, xw, _TN, preferred_element_type=_F32)

    y = y + dskip_ref[...] * xs
    y = y * zs_ref[...].astype(_F32)
    ms = jnp.mean(y * y, axis=-1, keepdims=True)
    o_ref[...] = ((y * lax.rsqrt(ms + _RMS_EPS)) * gnw_ref[...]).astype(o_ref.dtype)


def _ssd_scan_kernel(zs_ref, x_ref, bm_ref, cm_ref, acol_ref, dcol_ref, esc_ref, ehp_ref, arow_ref, drow_ref,
                     dskip_ref, gnw_ref, o_ref, st_ref, *, head_dim):
    @pl.when(pl.program_id(2) == 0)
    def _():
        st_ref[...] = jnp.zeros_like(st_ref)

    gb, n, gw = st_ref.shape
    c = zs_ref.shape[0]
    hg = gw // head_dim
    for g in range(gb):
        xs_, ns_, rs_ = pl.ds(g * gw, gw), pl.ds(g * n, n), pl.ds(g * hg * c, hg * c)
        _ssd_scan_group(zs_ref.at[:, xs_], x_ref.at[:, xs_], bm_ref.at[:, ns_], cm_ref.at[:, ns_],
                        acol_ref, dcol_ref, esc_ref.at[g], ehp_ref.at[g],
                        arow_ref.at[0, :, rs_], drow_ref.at[0, :, rs_],
                        dskip_ref.at[:, xs_], gnw_ref.at[:, xs_], o_ref.at[:, xs_], st_ref.at[g], head_dim)


def _ssd_scan(zs, xbc, acol, dcol, arow, drow, dskip, gnorm_w, batch, seq, inner, heads, n_state):
    m = zs.shape[0]
    c = _CHUNK
    g = _SSD_GROUPS
    gb = _SSD_GROUPS_PER_STEP if g % _SSD_GROUPS_PER_STEP == 0 else 1
    hg = heads // g
    p = inner // heads
    gw = hg * p
    sw, sn = gb * gw, gb * n_state
    assert c == _LANES and gw % _LANES == 0 and n_state % _LANES == 0 and _LANES % p == 0
    assert inner % sn == 0 and (g * n_state) % sn == 0
    nt = seq // c
    b_blk = inner // sn
    c_blk = (inner + g * n_state) // sn

    def rows(b, t):
        return b * nt + t

    in_specs = [
        pl.BlockSpec((c, sw), lambda b, gi, t: (rows(b, t), gi)),
        pl.BlockSpec((c, sw), lambda b, gi, t: (rows(b, t), gi)),
        pl.BlockSpec((c, sn), lambda b, gi, t: (rows(b, t), b_blk + gi)),
        pl.BlockSpec((c, sn), lambda b, gi, t: (rows(b, t), c_blk + gi)),
        pl.BlockSpec((c, _LANES), lambda b, gi, t: (rows(b, t), 0)),
        pl.BlockSpec((c, _LANES), lambda b, gi, t: (rows(b, t), 0)),
        pl.BlockSpec((gb, 2 * _LANES, hg * c), lambda b, gi, t: (gi, 0, 0)),
        pl.BlockSpec((gb, _LANES, gw), lambda b, gi, t: (gi, 0, 0)),
        pl.BlockSpec((1, 1, gb * hg * c), lambda b, gi, t: (rows(b, t), 0, gi)),
        pl.BlockSpec((1, 1, gb * hg * c), lambda b, gi, t: (rows(b, t), 0, gi)),
        pl.BlockSpec((1, sw), lambda b, gi, t: (0, gi)),
        pl.BlockSpec((1, sw), lambda b, gi, t: (0, gi)),
    ]
    return pl.pallas_call(
        functools.partial(_ssd_scan_kernel, head_dim=p),
        grid=(batch, g // gb, nt),
        in_specs=in_specs,
        out_specs=pl.BlockSpec((c, sw), lambda b, gi, t: (rows(b, t), gi)),
        out_shape=jax.ShapeDtypeStruct((m, inner), _MXU_DTYPE),
        scratch_shapes=[pltpu.VMEM((gb, n_state, gw), _F32)],
        compiler_params=_params(("parallel", "parallel", "arbitrary")),
        name="ssd_scan",
    )(zs, xbc, xbc, xbc, acol, dcol, *_head_expanders(g, hg, c, p), arow, drow, dskip, gnorm_w)


def _ssd_mixer(h, norm_w, batch, seq, layer, wt_in, conv_w, conv_b, dt_bias, a_log, d_skip, gnorm_w, w_out):
    heads = dt_bias.shape[0]
    conv_dim = conv_w.shape[1]
    inner = wt_in.shape[1] - conv_dim - heads
    n_state = (conv_dim - inner) // (2 * _SSD_GROUPS)
    n_main = inner + conv_dim
    m = h.shape[0]
    pad1 = lambda v: jnp.pad(v.astype(_F32), (0, _LANES - heads)).reshape(1, _LANES)
    zs, u = _matmul(h, wt_in, layer, wt=True, n=inner, norm_w=norm_w, emit_norm=True, act="silu",
                    out_dtype=_MXU_DTYPE)
    xbc = _proj_conv(u, wt_in, layer, inner, conv_w.astype(_F32), conv_b.reshape(1, conv_dim).astype(_F32),
                     seq, _MXU_DTYPE)
    acol, dcol, arow, drow = _ssd_gate(u, wt_in, layer, n_main, pad1(dt_bias), pad1(a_log), heads)
    arow = arow.reshape(m // _CHUNK, 1, heads * _CHUNK)
    drow = drow.reshape(m // _CHUNK, 1, heads * _CHUNK)
    dskip = jnp.repeat(d_skip.astype(_F32), inner // heads).reshape(1, inner)
    y = _ssd_scan(zs, xbc, acol, dcol, arow, drow, dskip, gnorm_w.reshape(1, inner).astype(_F32),
                  batch, seq, inner, heads, n_state)
    return _matmul(y, w_out.astype(_MXU_DTYPE), layer, res=h, out_dtype=_F32)


def kernel(x, mixer_norm_w, gla_w_in, gla_w_gk_up, gla_b_gk_up, gla_o_norm_w, gla_w_out, ssd_w_in, ssd_conv_w, ssd_conv_b, ssd_dt_bias, ssd_a_log, ssd_d_skip, ssd_gnorm_w, ssd_w_out, mlp_norm_w, mlp_w_fc1, mlp_w_fc2, final_norm_w):
    batch, seq, d = x.shape
    h = x.reshape(batch * seq, d)
    w_fc2 = mlp_w_fc2.astype(_MXU_DTYPE)
    gla_wt_in = jnp.swapaxes(gla_w_in, 1, 2)
    ssd_wt_in = jnp.swapaxes(ssd_w_in, 1, 2)
    for i in range(mixer_norm_w.shape[0]):
        j = i // 2
        if i % 2 == 0:
            h = _gla_mixer(h, mixer_norm_w[i], batch, seq, j, gla_wt_in, gla_w_gk_up[j], gla_b_gk_up[j],
                           gla_o_norm_w[j], gla_w_out)
        else:
            h = _ssd_mixer(h, mixer_norm_w[i], batch, seq, j, ssd_wt_in, ssd_conv_w[j], ssd_conv_b[j],
                           ssd_dt_bias[j], ssd_a_log[j], ssd_d_skip[j], ssd_gnorm_w[j], ssd_w_out)
        hidden = _matmul(h, mlp_w_fc1, i, norm_w=mlp_norm_w[i], act="relu2", out_dtype=_MXU_DTYPE)
        h = _matmul(hidden, w_fc2, i, res=h, out_dtype=_F32)
    return _rmsnorm(h, final_norm_w, _F32).reshape(batch, seq, d)
```

```python
import functools
import math

import jax
import jax.numpy as jnp
from jax import lax
from jax.experimental import pallas as pl
from jax.experimental.pallas import tpu as pltpu

_F32 = jnp.float32
_MXU_DTYPE = jnp.bfloat16
_RMS_EPS = 1e-5
_GLA_GATE_NORMALIZER = 16.0
_SSD_GROUPS = 8
_CHUNK = 128
_LANES = 128
_GLA_DIAG = 8
_GLA_HEADS_PER_STEP = 4
_SSD_GROUPS_PER_STEP = 4
_LOG2E = math.log2(math.e)
_NEG_BIG = -1e30
_HALO = 16
_MXU_WIDTH = 256
_VMEM_LIMIT_BYTES = 56 * 1024 * 1024
_MATMUL_VMEM_BUDGET = 52 * 1024 * 1024

_NN = (((1,), (0,)), ((), ()))
_NT = (((1,), (1,)), ((), ()))
_TN = (((0,), (0,)), ((), ()))


def _params(sem, flags=None):
    return pltpu.CompilerParams(dimension_semantics=sem, vmem_limit_bytes=_VMEM_LIMIT_BYTES, flags=flags)


def _rms(x, w):
    ms = jnp.mean(x * x, axis=-1, keepdims=True)
    return (x * lax.rsqrt(ms + _RMS_EPS)) * w


def _rmsnorm_kernel(x_ref, w_ref, o_ref):
    o_ref[...] = _rms(x_ref[...].astype(_F32), w_ref[...]).astype(o_ref.dtype)


def _rmsnorm(x, w, out_dtype, rows=512):
    m, d = x.shape
    rows = min(rows, m)
    return pl.pallas_call(
        _rmsnorm_kernel,
        grid=(m // rows,),
        in_specs=[pl.BlockSpec((rows, d), lambda i: (i, 0)),
                  pl.BlockSpec((1, d), lambda i: (0, 0))],
        out_specs=pl.BlockSpec((rows, d), lambda i: (i, 0)),
        out_shape=jax.ShapeDtypeStruct((m, d), out_dtype),
        compiler_params=_params(("parallel",)),
        name="rmsnorm",
    )(x, w.reshape(1, d).astype(_F32))


def _silu(x):
    half = 0.5 * x
    return half + half * jnp.tanh(half)


def _matmul_kernel(*refs, act, act_tile0, has_norm, has_res, wt):
    it = iter(refs)
    a_ref = next(it)
    nw_ref = next(it) if has_norm else None
    w_ref = next(it)
    r_ref = next(it) if has_res else None
    o_ref = next(it)
    if has_norm:
        an_ref = next(it)

        @pl.when(pl.program_id(1) == 0)
        def _():
            an_ref[...] = _rms(a_ref[...], nw_ref[...]).astype(_MXU_DTYPE)

        a = an_ref[...]
    else:
        a = a_ref[...]
    w = w_ref[...].astype(_MXU_DTYPE)
    acc = lax.dot_general(a, w, _NT if wt else _NN, preferred_element_type=_F32)
    if act == "relu2":
        act_acc = jnp.square(jnp.maximum(acc, 0.0))
    elif act == "silu":
        act_acc = _silu(acc)
    if act is not None:
        acc = act_acc if act_tile0 == 0 else jnp.where(pl.program_id(1) >= act_tile0, act_acc, acc)
    if has_res:
        acc = r_ref[...] + acc
    o_ref[...] = acc.astype(o_ref.dtype)


def _divisor_tile(n, pref, align):
    if n <= pref:
        return n
    t = (pref // align) * align
    while n % t:
        t -= align
    return t


def _matmul_tiles(m, k, n, a_bytes, w_bytes, out_bytes, norm_copies, has_res, conv_taps=0, row_cap=None):
    mxu_bytes = jnp.dtype(_MXU_DTYPE).itemsize
    for pm, pn in ((1024, 2048), (1024, 1024), (1024, 512), (1024, 256), (512, 256), (256, 256), (128, 128)):
        pm = pm if row_cap is None else min(pm, row_cap)
        tm, tn = _divisor_tile(m, pm, 8), _divisor_tile(n, pn, _LANES)
        need = 2 * tm * k * a_bytes + 2 * k * tn * w_bytes + 2 * tm * tn * out_bytes
        need += tm * tn * 4 if not conv_taps else (1 + conv_taps) * tm * _MXU_WIDTH * 4
        need += 2 * tm * tn * 4 if has_res else 0
        need += norm_copies * tm * k * mxu_bytes
        need += k * tn * mxu_bytes if w_bytes != mxu_bytes else 0
        if need <= _MATMUL_VMEM_BUDGET:
            break
    return tm, tn


def _matmul(a, w, layer, *, wt=False, n=None, norm_w=None, emit_norm=False, act=None, act_col0=0, res=None,
            out_dtype):
    m, k = a.shape
    n = w.shape[1 if wt else 2] if n is None else n
    tm, tn = _matmul_tiles(m, k, math.gcd(n, act_col0), a.dtype.itemsize, w.dtype.itemsize,
                           jnp.dtype(out_dtype).itemsize, (norm_w is not None) + emit_norm, res is not None)
    in_specs = [pl.BlockSpec((tm, k), lambda i, j: (i, 0))]
    args = [a]
    if norm_w is not None:
        in_specs.append(pl.BlockSpec((1, k), lambda i, j: (0, 0)))
        args.append(norm_w.reshape(1, k).astype(_F32))
    in_specs.append(pl.BlockSpec((None, tn, k), lambda i, j: (layer, j, 0)) if wt else
                    pl.BlockSpec((None, k, tn), lambda i, j: (layer, 0, j)))
    args.append(w)
    if res is not None:
        in_specs.append(pl.BlockSpec((tm, tn), lambda i, j: (i, j)))
        args.append(res)
    out_specs = [pl.BlockSpec((tm, tn), lambda i, j: (i, j))]
    out_shape = [jax.ShapeDtypeStruct((m, n), out_dtype)]
    scratch = []
    if emit_norm:
        out_specs.append(pl.BlockSpec((tm, k), lambda i, j: (i, 0)))
        out_shape.append(jax.ShapeDtypeStruct((m, k), _MXU_DTYPE))
    elif norm_w is not None:
        scratch.append(pltpu.VMEM((tm, k), _MXU_DTYPE))
    outs = pl.pallas_call(
        functools.partial(_matmul_kernel, act=act, act_tile0=act_col0 // tn, has_norm=norm_w is not None,
                          has_res=res is not None, wt=wt),
        grid=(m // tm, n // tn),
        in_specs=in_specs,
        out_specs=out_specs,
        out_shape=out_shape,
        scratch_shapes=scratch,
        compiler_params=_params(("parallel", "arbitrary")),
        name="matmul",
    )(*args)
    return outs if emit_norm else outs[0]


def _proj_conv_kernel(u_ref, halo_ref, w_ref, cw_ref, cb_ref, o_ref, an_ref, *, tiles_per_seq):
    kw = cw_ref.shape[0]

    @pl.when(pl.program_id(1) == 0)
    def _():
        halo = halo_ref[...]
        an_ref[0:_HALO, :] = jnp.where(pl.program_id(0) % tiles_per_seq == 0, jnp.zeros_like(halo), halo)
        an_ref[_HALO:, :] = u_ref[...]

    wb = w_ref[...].astype(_MXU_DTYPE)
    for c0 in range(0, o_ref.shape[1], _MXU_WIDTH):
        cols = pl.ds(c0, _MXU_WIDTH)
        acc = lax.dot_general(an_ref[...], wb[c0:c0 + _MXU_WIDTH, :], _NT, preferred_element_type=_F32)
        tm = o_ref.shape[0]
        out = cb_ref[:, cols] + cw_ref[kw - 1:kw, cols] * acc[_HALO:, :]
        for s in range(1, kw):
            out = out + cw_ref[kw - 1 - s:kw - s, cols] * acc[_HALO - s:_HALO - s + tm, :]
        o_ref[:, cols] = _silu(out).astype(o_ref.dtype)


def _proj_conv(u, w, layer, col0, conv_w, conv_b, seq, out_dtype):
    m, k = u.shape
    kw, n = conv_w.shape
    tm, tn = _matmul_tiles(m, k, math.gcd(n, col0), u.dtype.itemsize, w.dtype.itemsize,
                           jnp.dtype(out_dtype).itemsize, 1, False, conv_taps=kw, row_cap=seq)
    assert seq % tm == 0 and tm % _HALO == 0 and tn % _MXU_WIDTH == 0 and kw - 1 <= _HALO
    return pl.pallas_call(
        functools.partial(_proj_conv_kernel, tiles_per_seq=seq // tm),
        grid=(m // tm, n // tn),
        in_specs=[pl.BlockSpec((tm, k), lambda i, j: (i, 0)),
                  pl.BlockSpec((_HALO, k), lambda i, j: (jnp.maximum(i * (tm // _HALO) - 1, 0), 0)),
                  pl.BlockSpec((None, tn, k), lambda i, j: (layer, col0 // tn + j, 0)),
                  pl.BlockSpec((kw, tn), lambda i, j: (0, j)),
                  pl.BlockSpec((1, tn), lambda i, j: (0, j))],
        out_specs=pl.BlockSpec((tm, tn), lambda i, j: (i, j)),
        out_shape=jax.ShapeDtypeStruct((m, n), out_dtype),
        scratch_shapes=[pltpu.VMEM((tm + _HALO, k), _MXU_DTYPE)],
        compiler_params=_params(("parallel", "arbitrary")),
        name="proj_conv",
    )(u, u, w, conv_w, conv_b)


def _softplus(x):
    return jnp.maximum(x, 0.0) + jnp.log(1.0 + jnp.exp(-jnp.abs(x)))


def _chunk_cumsum(x, chunk):
    rows = x.shape[0]
    r = lax.broadcasted_iota(jnp.int32, (chunk, chunk), 0)
    c = lax.broadcasted_iota(jnp.int32, (chunk, chunk), 1)
    tri = jnp.where(c <= r, 1.0, 0.0).astype(_MXU_DTYPE)
    out = []
    for i in range(rows // chunk):
        rem = x[i * chunk:(i + 1) * chunk, :]
        acc = None
        for _ in range(3):
            piece = rem.astype(_MXU_DTYPE)
            d = jnp.dot(tri, piece, preferred_element_type=_F32)
            acc = d if acc is None else acc + d
            rem = rem - piece.astype(_F32)
        out.append(acc)
    return jnp.concatenate(out, axis=0) if len(out) > 1 else out[0]


def _row_bcast(ref, row, n):
    return jnp.broadcast_to(ref[pl.ds(row, 1), :], (n, ref.shape[1]))


def _tail_rows(w_ref, n_valid):
    row = lax.broadcasted_iota(jnp.int32, w_ref.shape, 0)
    return jnp.where(row < n_valid, w_ref[...], 0.0).astype(_MXU_DTYPE)


def _gla_gate_kernel(u_ref, wgr_ref, wup_ref, b_ref, o_ref, *, chunk, rank):
    gr = lax.dot_general(u_ref[...], _tail_rows(wgr_ref, rank), _NT, preferred_element_type=_F32)
    pre = jnp.dot(gr.astype(_MXU_DTYPE), wup_ref[...], preferred_element_type=_F32) + b_ref[...]
    gk = -_softplus(-pre) / _GLA_GATE_NORMALIZER
    o_ref[...] = _chunk_cumsum(gk * _LOG2E, chunk)


def _gla_gate(u, wt_in, layer, col0, rank, w_up, b_up, rows=512):
    m, d = u.shape
    kd = w_up.shape[1]
    rows = min(rows, m)
    assert col0 % _LANES == 0 and col0 + rank == wt_in.shape[1] and rank <= _LANES
    return pl.pallas_call(
        functools.partial(_gla_gate_kernel, chunk=_CHUNK, rank=rank),
        grid=(m // rows,),
        in_specs=[pl.BlockSpec((rows, d), lambda i: (i, 0)),
                  pl.BlockSpec((None, _LANES, d), lambda i: (layer, col0 // _LANES, 0)),
                  pl.BlockSpec(w_up.shape, lambda i: (0, 0)),
                  pl.BlockSpec((1, kd), lambda i: (0, 0))],
        out_specs=pl.BlockSpec((rows, kd), lambda i: (i, 0)),
        out_shape=jax.ShapeDtypeStruct((m, kd), _F32),
        compiler_params=_params(("parallel",)),
        name="gla_gate",
    )(u, wt_in, w_up, b_up)


def _gla_levels(c, nd):
    i = lax.broadcasted_iota(jnp.int32, (c, c), 0)
    j = lax.broadcasted_iota(jnp.int32, (c, c), 1)
    x = i ^ j
    lvl = jnp.zeros((c, c), jnp.int32)
    s, level = nd, 1
    while s < c:
        lvl = jnp.where(x >= s, level, lvl)
        s, level = 2 * s, level + 1
    return jnp.where(j > i, -1, lvl)


def _gla_scan_head(q_ref, k_ref, v_ref, gs_ref, b_ref, onw_ref, lvl_ref, o_ref, st_ref, qf_ref, kf_ref, scale):
    c, dk = q_ref.shape
    qf_ref[...] = q_ref[...].astype(_F32) * scale
    kf_ref[...] = k_ref[...].astype(_F32)
    q = qf_ref[...]
    k = kf_ref[...]
    v = v_ref[...]
    bc = b_ref[...]
    st = st_ref[...]
    b_last = jnp.concatenate([_row_bcast(b_ref, c - 1, 8)] * (c // 8), axis=0)

    o = lax.dot_general((q * jnp.exp2(bc)).astype(_MXU_DTYPE), st.astype(_MXU_DTYPE), _NT,
                        preferred_element_type=_F32)
    k_dec = (k * jnp.exp2(b_last - bc)).astype(_MXU_DTYPE)

    nd = _GLA_DIAG
    lane = lax.broadcasted_iota(jnp.int32, (nd, c), 1)
    pieces = []
    for m in range(c // nd):
        qb = qf_ref[m * nd:(m + 1) * nd, :]
        bb = b_ref[m * nd:(m + 1) * nd, :]
        acc = jnp.zeros((nd, c), _F32)
        for j in range(nd):
            kj = _row_bcast(kf_ref, m * nd + j, nd)
            bj = _row_bcast(b_ref, m * nd + j, nd)
            term = qb * kj * jnp.exp2(jnp.minimum(bb - bj, 0.0))
            acc = jnp.where(lane == m * nd + j, jnp.sum(term, axis=1, keepdims=True), acc)
        pieces.append(acc)
    lvl = lvl_ref[...]
    att = jnp.where(lvl == 0, jnp.concatenate(pieces, axis=0), 0.0)

    s, level = nd, 1
    while s < c:
        prev = [jnp.zeros((s, dk), _F32)]
        end = []
        for m in range(c // s):
            if m:
                prev += [_row_bcast(b_ref, m * s - 1, 8)] * (s // 8)
            end += [_row_bcast(b_ref, (m + 1) * s - 1, 8)] * (s // 8)
        qs = (q * jnp.exp2(bc - jnp.concatenate(prev, axis=0))).astype(_MXU_DTYPE)
        ks = (k * jnp.exp2(jnp.concatenate(end, axis=0) - bc)).astype(_MXU_DTYPE)
        att = jnp.where(lvl == level, lax.dot_general(qs, ks, _NT, preferred_element_type=_F32), att)
        s, level = 2 * s, level + 1

    o = o + jnp.dot(att.astype(_MXU_DTYPE), v, preferred_element_type=_F32)
    st_ref[...] = st * jnp.exp2(b_last[:1, :]) + lax.dot_general(
        v, k_dec, _TN, preferred_element_type=_F32)

    ms = jnp.mean(o * o, axis=-1, keepdims=True)
    on = (o * lax.rsqrt(ms + _RMS_EPS)) * onw_ref[...]
    o_ref[...] = (on * gs_ref[...].astype(_F32)).astype(o_ref.dtype)


def _gla_scan_kernel(q_ref, k_ref, v_ref, gs_ref, b_ref, onw_ref, lvl_ref, o_ref, st_ref, qf_ref, kf_ref,
                     *, scale, dk, dv):
    @pl.when(pl.program_id(2) == 0)
    def _():
        st_ref[...] = jnp.zeros_like(st_ref)

    for h in range(st_ref.shape[0]):
        ks, vs = pl.ds(h * dk, dk), pl.ds(h * dv, dv)
        _gla_scan_head(q_ref.at[:, ks], k_ref.at[:, ks], v_ref.at[:, vs], gs_ref.at[:, vs], b_ref.at[:, ks],
                       onw_ref, lvl_ref, o_ref.at[:, vs], st_ref.at[h], qf_ref.at[h], kf_ref.at[h], scale)


def _gla_scan(proj, bcum, o_norm_w, batch, seq, heads, dk, dv):
    m = proj.shape[0]
    c = _CHUNK
    hb = _GLA_HEADS_PER_STEP if heads % _GLA_HEADS_PER_STEP == 0 else 1
    assert c == _LANES and dk % _LANES == 0 and dv % _LANES == 0 and (2 * heads * dk) % (hb * dv) == 0
    nt = seq // c
    kd, vd = heads * dk, heads * dv
    wk, wv = hb * dk, hb * dv
    k_blk, v_blk, g_blk = kd // wk, (2 * kd) // wv, (2 * kd + vd) // wv

    def rows(b, h, t):
        return b * nt + t

    return pl.pallas_call(
        functools.partial(_gla_scan_kernel, scale=dk ** -0.5, dk=dk, dv=dv),
        grid=(batch, heads // hb, nt),
        in_specs=[pl.BlockSpec((c, wk), lambda b, h, t: (rows(b, h, t), h)),
                  pl.BlockSpec((c, wk), lambda b, h, t: (rows(b, h, t), k_blk + h)),
                  pl.BlockSpec((c, wv), lambda b, h, t: (rows(b, h, t), v_blk + h)),
                  pl.BlockSpec((c, wv), lambda b, h, t: (rows(b, h, t), g_blk + h)),
                  pl.BlockSpec((c, wk), lambda b, h, t: (rows(b, h, t), h)),
                  pl.BlockSpec((1, dv), lambda b, h, t: (0, 0)),
                  pl.BlockSpec((c, c), lambda b, h, t: (0, 0))],
        out_specs=pl.BlockSpec((c, wv), lambda b, h, t: (rows(b, h, t), h)),
        out_shape=jax.ShapeDtypeStruct((m, vd), _MXU_DTYPE),
        scratch_shapes=[pltpu.VMEM((hb, dv, dk), _F32),
                        pltpu.VMEM((hb, c, dk), _F32),
                        pltpu.VMEM((hb, c, dk), _F32)],
        compiler_params=_params(("parallel", "parallel", "arbitrary")),
        name="gla_scan",
    )(proj, proj, proj, proj, bcum, o_norm_w.reshape(1, dv).astype(_F32), _gla_levels(c, _GLA_DIAG))


def _gla_mixer(h, norm_w, batch, seq, layer, wt_in, w_gk_up, b_gk_up, o_norm_w, w_out):
    rank, kd = w_gk_up.shape
    vd = (wt_in.shape[1] - rank - 2 * kd) // 2
    dv = o_norm_w.shape[0]
    heads = vd // dv
    n_main = 2 * kd + 2 * vd
    w_up = jnp.pad(w_gk_up, ((0, _LANES - rank), (0, 0))).astype(_MXU_DTYPE)
    proj, u = _matmul(h, wt_in, layer, wt=True, n=n_main, norm_w=norm_w, emit_norm=True, act="silu",
                      act_col0=2 * kd + vd, out_dtype=_MXU_DTYPE)
    bcum = _gla_gate(u, wt_in, layer, n_main, rank, w_up, b_gk_up.reshape(1, kd).astype(_F32))
    o = _gla_scan(proj, bcum, o_norm_w, batch, seq, heads, kd // heads, dv)
    return _matmul(o, w_out.astype(_MXU_DTYPE), layer, res=h, out_dtype=_F32)


def _ssd_gate_kernel(u_ref, wdt_ref, bias_ref, alog_ref, acol_ref, dcol_ref, arow_ref, drow_ref,
                     *, chunk, heads):
    raw = lax.dot_general(u_ref[...], _tail_rows(wdt_ref, heads), _NT, preferred_element_type=_F32)
    dt = _softplus(raw + bias_ref[...])
    a = _chunk_cumsum(dt * (-jnp.exp(alog_ref[...])) * _LOG2E, chunk)
    acol_ref[...] = a
    dcol_ref[...] = dt
    for i in range(a.shape[0] // chunk):
        arow_ref[i] = a[i * chunk:(i + 1) * chunk, :].T[:heads, :]
        drow_ref[i] = dt[i * chunk:(i + 1) * chunk, :].T[:heads, :]


def _ssd_gate(u, wt_in, layer, col0, dt_bias, a_log, heads, rows=512):
    m, d = u.shape
    c = _CHUNK
    rows = min(rows, m)
    g = _SSD_GROUPS
    hg = heads // g
    assert col0 % _LANES == 0 and col0 + heads == wt_in.shape[1] and heads <= _LANES
    col = jax.ShapeDtypeStruct((m, _LANES), _F32)
    row = jax.ShapeDtypeStruct((m // c, heads, c), _F32)
    col_spec = pl.BlockSpec((rows, _LANES), lambda i: (i, 0))
    row_spec = pl.BlockSpec((rows // c, heads, c), lambda i: (i, 0, 0))
    return pl.pallas_call(
        functools.partial(_ssd_gate_kernel, chunk=c, heads=heads),
        grid=(m // rows,),
        in_specs=[pl.BlockSpec((rows, d), lambda i: (i, 0)),
                  pl.BlockSpec((None, _LANES, d), lambda i: (layer, col0 // _LANES, 0)),
                  pl.BlockSpec((1, _LANES), lambda i: (0, 0)),
                  pl.BlockSpec((1, _LANES), lambda i: (0, 0))],
        out_specs=[col_spec, col_spec, row_spec, row_spec],
        out_shape=[col, col, row, row],
        compiler_params=_params(("parallel",)),
        name="ssd_gate",
    )(u, wt_in, dt_bias, a_log)


def _head_expanders(groups, hg, c, head_dim):
    def one_hot(rows, width):
        row = lax.broadcasted_iota(jnp.int32, (groups, rows, hg * width), 1) % _LANES
        col = lax.broadcasted_iota(jnp.int32, (groups, rows, hg * width), 2)
        grp = lax.broadcasted_iota(jnp.int32, (groups, rows, hg * width), 0)
        return (row == grp * hg + col // width).astype(_MXU_DTYPE)
    return one_hot(2 * _LANES, c), one_hot(_LANES, head_dim)


def _expand_heads(bcasts, head_dim):
    c = bcasts[0].shape[0]
    per = _LANES // head_dim
    lane = lax.broadcasted_iota(jnp.int32, (c, _LANES), 1)
    pieces = []
    for p in range(len(bcasts) // per):
        out = bcasts[p * per]
        for i in range(1, per):
            out = jnp.where(lane >= i * head_dim, bcasts[p * per + i], out)
        pieces.append(out)
    return jnp.concatenate(pieces, axis=1)


def _ssd_scan_group(zs_ref, x_ref, bm_ref, cm_ref, acol_ref, dcol_ref, esc_ref, ehp_ref, arow_ref, drow_ref,
                    dskip_ref, gnw_ref, o_ref, st_ref, head_dim):
    c = zs_ref.shape[0]
    hg = x_ref.shape[1] // head_dim
    xs = x_ref[...].astype(_F32)
    bmx = bm_ref[...]
    cmx = cm_ref[...]

    acol = acol_ref[...]
    a_last = acol[c - 1:c, :]
    w_state = (jnp.exp2(a_last - acol) * dcol_ref[...]).astype(_MXU_DTYPE)
    a_hi = acol.astype(_MXU_DTYPE)
    a_lo = (acol - a_hi.astype(_F32)).astype(_MXU_DTYPE)
    a_sc = jnp.dot(jnp.concatenate([a_hi, a_lo], axis=1), esc_ref[...],
                   preferred_element_type=_F32)
    w_hp = jnp.dot(w_state, ehp_ref[...], preferred_element_type=_F32)

    per = _LANES // head_dim
    width = per * c
    cb = lax.dot_general(cmx, jnp.concatenate([bmx] * per, axis=0), _NT,
                         preferred_element_type=_F32)
    ri = lax.broadcasted_iota(jnp.int32, (c, width), 0)
    ci = lax.broadcasted_iota(jnp.int32, (c, width), 1) & (c - 1)
    causal = ci <= ri
    lane = lax.broadcasted_iota(jnp.int32, (c, _LANES), 1)
    y_pieces = []
    for p in range(hg // per):
        ac = a_sc[:, p * width:(p + 1) * width]
        ar = arow_ref[:, p * width:(p + 1) * width]
        dr = drow_ref[:, p * width:(p + 1) * width]
        decay = jnp.exp2(jnp.where(causal, ac - ar, _NEG_BIG))
        sc = (cb * decay * dr).astype(_MXU_DTYPE)
        xp = x_ref[:, p * _LANES:(p + 1) * _LANES]
        bd = jnp.concatenate(
            [jnp.where((lane >= i * head_dim) & (lane < (i + 1) * head_dim), xp, jnp.zeros_like(xp))
             for i in range(per)], axis=0)
        y_pieces.append(jnp.dot(sc, bd, preferred_element_type=_F32))
    y = jnp.concatenate(y_pieces, axis=1)

    st = st_ref[...]
    ea = jnp.exp2(_expand_heads([a_sc[:, h * c:(h + 1) * c] for h in range(hg)], head_dim))
    y = y + jnp.dot(cmx, st.astype(_MXU_DTYPE), preferred_element_type=_F32) * ea
    xw = (xs * w_hp).astype(_MXU_DTYPE)
    st_ref[...] = st * ea[c - 1:c, :] + lax.dot_general(bmx, xw, _TN, preferred_element_type=_F32)

    y = y + dskip_ref[...] * xs
    y = y * zs_ref[...].astype(_F32)
    ms = jnp.mean(y * y, axis=-1, keepdims=True)
    o_ref[...] = ((y * lax.rsqrt(ms + _RMS_EPS)) * gnw_ref[...]).astype(o_ref.dtype)


def _ssd_scan_kernel(zs_ref, x_ref, bm_ref, cm_ref, acol_ref, dcol_ref, esc_ref, ehp_ref, arow_ref, drow_ref,
                     dskip_ref, gnw_ref, o_ref, st_ref, *, head_dim):
    @pl.when(pl.program_id(2) == 0)
    def _():
        st_ref[...] = jnp.zeros_like(st_ref)

    gb, n, gw = st_ref.shape
    c = zs_ref.shape[0]
    hg = gw // head_dim
    for g in range(gb):
        xs_, ns_, rs_ = pl.ds(g * gw, gw), pl.ds(g * n, n), pl.ds(g * hg * c, hg * c)
        _ssd_scan_group(zs_ref.at[:, xs_], x_ref.at[:, xs_], bm_ref.at[:, ns_], cm_ref.at[:, ns_],
                        acol_ref, dcol_ref, esc_ref.at[g], ehp_ref.at[g],
                        arow_ref.at[0, :, rs_], drow_ref.at[0, :, rs_],
                        dskip_ref.at[:, xs_], gnw_ref.at[:, xs_], o_ref.at[:, xs_], st_ref.at[g], head_dim)


def _ssd_scan(zs, xbc, acol, dcol, arow, drow, dskip, gnorm_w, batch, seq, inner, heads, n_state):
    m = zs.shape[0]
    c = _CHUNK
    g = _SSD_GROUPS
    gb = _SSD_GROUPS_PER_STEP if g % _SSD_GROUPS_PER_STEP == 0 else 1
    hg = heads // g
    p = inner // heads
    gw = hg * p
    sw, sn = gb * gw, gb * n_state
    assert c == _LANES and gw % _LANES == 0 and n_state % _LANES == 0 and _LANES % p == 0
    assert inner % sn == 0 and (g * n_state) % sn == 0
    nt = seq // c
    b_blk = inner // sn
    c_blk = (inner + g * n_state) // sn

    def rows(b, t):
        return b * nt + t

    in_specs = [
        pl.BlockSpec((c, sw), lambda b, gi, t: (rows(b, t), gi)),
        pl.BlockSpec((c, sw), lambda b, gi, t: (rows(b, t), gi)),
        pl.BlockSpec((c, sn), lambda b, gi, t: (rows(b, t), b_blk + gi)),
        pl.BlockSpec((c, sn), lambda b, gi, t: (rows(b, t), c_blk + gi)),
        pl.BlockSpec((c, _LANES), lambda b, gi, t: (rows(b, t), 0)),
        pl.BlockSpec((c, _LANES), lambda b, gi, t: (rows(b, t), 0)),
        pl.BlockSpec((gb, 2 * _LANES, hg * c), lambda b, gi, t: (gi, 0, 0)),
        pl.BlockSpec((gb, _LANES, gw), lambda b, gi, t: (gi, 0, 0)),
        pl.BlockSpec((1, 1, gb * hg * c), lambda b, gi, t: (rows(b, t), 0, gi)),
        pl.BlockSpec((1, 1, gb * hg * c), lambda b, gi, t: (rows(b, t), 0, gi)),
        pl.BlockSpec((1, sw), lambda b, gi, t: (0, gi)),
        pl.BlockSpec((1, sw), lambda b, gi, t: (0, gi)),
    ]
    return pl.pallas_call(
        functools.partial(_ssd_scan_kernel, head_dim=p),
        grid=(batch, g // gb, nt),
        in_specs=in_specs,
        out_specs=pl.BlockSpec((c, sw), lambda b, gi, t: (rows(b, t), gi)),
        out_shape=jax.ShapeDtypeStruct((m, inner), _MXU_DTYPE),
        scratch_shapes=[pltpu.VMEM((gb, n_state, gw), _F32)],
        compiler_params=_params(("parallel", "parallel", "arbitrary")),
        name="ssd_scan",
    )(zs, xbc, xbc, xbc, acol, dcol, *_head_expanders(g, hg, c, p), arow, drow, dskip, gnorm_w)


def _ssd_mixer(h, norm_w, batch, seq, layer, wt_in, conv_w, conv_b, dt_bias, a_log, d_skip, gnorm_w, w_out):
    heads = dt_bias.shape[0]
    conv_dim = conv_w.shape[1]
    inner = wt_in.shape[1] - conv_dim - heads
    n_state = (conv_dim - inner) // (2 * _SSD_GROUPS)
    n_main = inner + conv_dim
    m = h.shape[0]
    pad1 = lambda v: jnp.pad(v.astype(_F32), (0, _LANES - heads)).reshape(1, _LANES)
    zs, u = _matmul(h, wt_in, layer, wt=True, n=inner, norm_w=norm_w, emit_norm=True, act="silu",
                    out_dtype=_MXU_DTYPE)
    xbc = _proj_conv(u, wt_in, layer, inner, conv_w.astype(_F32), conv_b.reshape(1, conv_dim).astype(_F32),
                     seq, _MXU_DTYPE)
    acol, dcol, arow, drow = _ssd_gate(u, wt_in, layer, n_main, pad1(dt_bias), pad1(a_log), heads)
    arow = arow.reshape(m // _CHUNK, 1, heads * _CHUNK)
    drow = drow.reshape(m // _CHUNK, 1, heads * _CHUNK)
    dskip = jnp.repeat(d_skip.astype(_F32), inner // heads).reshape(1, inner)
    y = _ssd_scan(zs, xbc, acol, dcol, arow, drow, dskip, gnorm_w.reshape(1, inner).astype(_F32),
                  batch, seq, inner, heads, n_state)
    return _matmul(y, w_out.astype(_MXU_DTYPE), layer, res=h, out_dtype=_F32)


def kernel(x, mixer_norm_w, gla_w_in, gla_w_gk_up, gla_b_gk_up, gla_o_norm_w, gla_w_out, ssd_w_in, ssd_conv_w, ssd_conv_b, ssd_dt_bias, ssd_a_log, ssd_d_skip, ssd_gnorm_w, ssd_w_out, mlp_norm_w, mlp_w_fc1, mlp_w_fc2, final_norm_w):
    batch, seq, d = x.shape
    h = x.reshape(batch * seq, d)
    w_fc1 = mlp_w_fc1.astype(_MXU_DTYPE)
    w_fc2 = mlp_w_fc2.astype(_MXU_DTYPE)
    gla_wt_in = jnp.swapaxes(gla_w_in, 1, 2).astype(_MXU_DTYPE)
    ssd_wt_in = jnp.swapaxes(ssd_w_in, 1, 2).astype(_MXU_DTYPE)
    for i in range(mixer_norm_w.shape[0]):
        j = i // 2
        if i % 2 == 0:
            h = _gla_mixer(h, mixer_norm_w[i], batch, seq, j, gla_wt_in, gla_w_gk_up[j], gla_b_gk_up[j],
                           gla_o_norm_w[j], gla_w_out)
        else:
            h = _ssd_mixer(h, mixer_norm_w[i], batch, seq, j, ssd_wt_in, ssd_conv_w[j], ssd_conv_b[j],
                           ssd_dt_bias[j], ssd_a_log[j], ssd_d_skip[j], ssd_gnorm_w[j], ssd_w_out)
        hidden = _matmul(h, w_fc1, i, norm_w=mlp_norm_w[i], act="relu2", out_dtype=_MXU_DTYPE)
        h = _matmul(hidden, w_fc2, i, res=h, out_dtype=_F32)
    return _rmsnorm(h, final_norm_w, _F32).reshape(batch, seq, d)
```

```python
import functools
import math

import jax
import jax.numpy as jnp
from jax import lax
from jax.experimental import pallas as pl
from jax.experimental.pallas import tpu as pltpu

_F32 = jnp.float32
_MXU_DTYPE = jnp.bfloat16
_RMS_EPS = 1e-5
_GLA_GATE_NORMALIZER = 16.0
_SSD_GROUPS = 8
_CHUNK = 128
_LANES = 128
_GLA_DIAG = 8
_GLA_HEADS_PER_STEP = 4
_SSD_GROUPS_PER_STEP = 4
_LOG2E = math.log2(math.e)
_NEG_BIG = -1e30
_HALO = 16
_MXU_WIDTH = 256
_VMEM_LIMIT_BYTES = 56 * 1024 * 1024
_MATMUL_VMEM_BUDGET = 52 * 1024 * 1024

_NN = (((1,), (0,)), ((), ()))
_NT = (((1,), (1,)), ((), ()))
_TN = (((0,), (0,)), ((), ()))


def _params(sem, flags=None):
    return pltpu.CompilerParams(dimension_semantics=sem, vmem_limit_bytes=_VMEM_LIMIT_BYTES, flags=flags)


def _rms(x, w):
    ms = jnp.mean(x * x, axis=-1, keepdims=True)
    return (x * lax.rsqrt(ms + _RMS_EPS)) * w


def _rmsnorm_kernel(x_ref, w_ref, o_ref):
    o_ref[...] = _rms(x_ref[...].astype(_F32), w_ref[...]).astype(o_ref.dtype)


def _rmsnorm(x, w, out_dtype, rows=512):
    m, d = x.shape
    rows = min(rows, m)
    return pl.pallas_call(
        _rmsnorm_kernel,
        grid=(m // rows,),
        in_specs=[pl.BlockSpec((rows, d), lambda i: (i, 0)),
                  pl.BlockSpec((1, d), lambda i: (0, 0))],
        out_specs=pl.BlockSpec((rows, d), lambda i: (i, 0)),
        out_shape=jax.ShapeDtypeStruct((m, d), out_dtype),
        compiler_params=_params(("parallel",)),
        name="rmsnorm",
    )(x, w.reshape(1, d).astype(_F32))


def _silu(x):
    half = 0.5 * x
    return half + half * jnp.tanh(half)


def _matmul_kernel(*refs, act, act_tile0, has_norm, has_res, wt):
    it = iter(refs)
    a_ref = next(it)
    nw_ref = next(it) if has_norm else None
    w_ref = next(it)
    r_ref = next(it) if has_res else None
    o_ref = next(it)
    if has_norm:
        an_ref = next(it)

        @pl.when(pl.program_id(1) == 0)
        def _():
            an_ref[...] = _rms(a_ref[...], nw_ref[...]).astype(_MXU_DTYPE)

        a = an_ref[...]
    else:
        a = a_ref[...]
    w = w_ref[...].astype(_MXU_DTYPE)
    acc = lax.dot_general(a, w, _NT if wt else _NN, preferred_element_type=_F32)
    if act == "relu2":
        act_acc = jnp.square(jnp.maximum(acc, 0.0))
    elif act == "silu":
        act_acc = _silu(acc)
    if act is not None:
        acc = act_acc if act_tile0 == 0 else jnp.where(pl.program_id(1) >= act_tile0, act_acc, acc)
    if has_res:
        acc = r_ref[...] + acc
    o_ref[...] = acc.astype(o_ref.dtype)


def _divisor_tile(n, pref, align):
    if n <= pref:
        return n
    t = (pref // align) * align
    while n % t:
        t -= align
    return t


def _matmul_tiles(m, k, n, a_bytes, w_bytes, out_bytes, norm_copies, has_res, conv_taps=0, row_cap=None):
    mxu_bytes = jnp.dtype(_MXU_DTYPE).itemsize
    for pm, pn in ((1024, 2048), (1024, 1024), (1024, 512), (1024, 256), (512, 256), (256, 256), (128, 128)):
        pm = pm if row_cap is None else min(pm, row_cap)
        tm, tn = _divisor_tile(m, pm, 8), _divisor_tile(n, pn, _LANES)
        need = 2 * tm * k * a_bytes + 2 * k * tn * w_bytes + 2 * tm * tn * out_bytes
        need += tm * tn * 4 if not conv_taps else (1 + conv_taps) * tm * _MXU_WIDTH * 4
        need += 2 * tm * tn * 4 if has_res else 0
        need += norm_copies * tm * k * mxu_bytes
        need += k * tn * mxu_bytes if w_bytes != mxu_bytes else 0
        if need <= _MATMUL_VMEM_BUDGET:
            break
    return tm, tn


def _matmul(a, w, layer, *, wt=False, n=None, norm_w=None, emit_norm=False, act=None, act_col0=0, res=None,
            out_dtype):
    m, k = a.shape
    n = w.shape[1 if wt else 2] if n is None else n
    tm, tn = _matmul_tiles(m, k, math.gcd(n, act_col0), a.dtype.itemsize, w.dtype.itemsize,
                           jnp.dtype(out_dtype).itemsize, (norm_w is not None) + emit_norm, res is not None)
    in_specs = [pl.BlockSpec((tm, k), lambda i, j: (i, 0))]
    args = [a]
    if norm_w is not None:
        in_specs.append(pl.BlockSpec((1, k), lambda i, j: (0, 0)))
        args.append(norm_w.reshape(1, k).astype(_F32))
    in_specs.append(pl.BlockSpec((None, tn, k), lambda i, j: (layer, j, 0)) if wt else
                    pl.BlockSpec((None, k, tn), lambda i, j: (layer, 0, j)))
    args.append(w)
    if res is not None:
        in_specs.append(pl.BlockSpec((tm, tn), lambda i, j: (i, j)))
        args.append(res)
    out_specs = [pl.BlockSpec((tm, tn), lambda i, j: (i, j))]
    out_shape = [jax.ShapeDtypeStruct((m, n), out_dtype)]
    scratch = []
    if emit_norm:
        out_specs.append(pl.BlockSpec((tm, k), lambda i, j: (i, 0)))
        out_shape.append(jax.ShapeDtypeStruct((m, k), _MXU_DTYPE))
    elif norm_w is not None:
        scratch.append(pltpu.VMEM((tm, k), _MXU_DTYPE))
    outs = pl.pallas_call(
        functools.partial(_matmul_kernel, act=act, act_tile0=act_col0 // tn, has_norm=norm_w is not None,
                          has_res=res is not None, wt=wt),
        grid=(m // tm, n // tn),
        in_specs=in_specs,
        out_specs=out_specs,
        out_shape=out_shape,
        scratch_shapes=scratch,
        compiler_params=_params(("parallel", "arbitrary")),
        name="matmul",
    )(*args)
    return outs if emit_norm else outs[0]


def _matmul_splitk_kernel(a_ref, w_ref, r_ref, o_ref):
    @pl.when(pl.program_id(2) == 0)
    def _():
        o_ref[...] = r_ref[...]

    o_ref[...] += jnp.dot(a_ref[...], w_ref[...].astype(_MXU_DTYPE), preferred_element_type=_F32)


def _matmul_residual(a, w, layer, res):
    m, k = a.shape
    n = w.shape[2]
    for nk in (1, 2, 4):
        tm, tn = _matmul_tiles(m, k // nk, n, a.dtype.itemsize, w.dtype.itemsize, 4, 0, True)
        if tm * tn >= min(m, 1024) * min(n, 1024):
            break
    if nk == 1:
        return _matmul(a, w, layer, res=res, out_dtype=_F32)
    tk = k // nk
    return pl.pallas_call(
        _matmul_splitk_kernel,
        grid=(m // tm, n // tn, nk),
        in_specs=[pl.BlockSpec((tm, tk), lambda i, j, kk: (i, kk)),
                  pl.BlockSpec((None, tk, tn), lambda i, j, kk: (layer, kk, j)),
                  pl.BlockSpec((tm, tn), lambda i, j, kk: (i, j))],
        out_specs=pl.BlockSpec((tm, tn), lambda i, j, kk: (i, j)),
        out_shape=jax.ShapeDtypeStruct((m, n), _F32),
        compiler_params=_params(("parallel", "parallel", "arbitrary")),
        name="matmul_splitk",
    )(a, w, res)


def _proj_conv_kernel(u_ref, halo_ref, w_ref, cw_ref, cb_ref, o_ref, an_ref, *, tiles_per_seq):
    kw = cw_ref.shape[0]

    @pl.when(pl.program_id(1) == 0)
    def _():
        halo = halo_ref[...]
        an_ref[0:_HALO, :] = jnp.where(pl.program_id(0) % tiles_per_seq == 0, jnp.zeros_like(halo), halo)
        an_ref[_HALO:, :] = u_ref[...]

    wb = w_ref[...].astype(_MXU_DTYPE)
    for c0 in range(0, o_ref.shape[1], _MXU_WIDTH):
        cols = pl.ds(c0, _MXU_WIDTH)
        acc = lax.dot_general(an_ref[...], wb[c0:c0 + _MXU_WIDTH, :], _NT, preferred_element_type=_F32)
        tm = o_ref.shape[0]
        out = cb_ref[:, cols] + cw_ref[kw - 1:kw, cols] * acc[_HALO:, :]
        for s in range(1, kw):
            out = out + cw_ref[kw - 1 - s:kw - s, cols] * acc[_HALO - s:_HALO - s + tm, :]
        o_ref[:, cols] = _silu(out).astype(o_ref.dtype)


def _proj_conv(u, w, layer, col0, conv_w, conv_b, seq, out_dtype):
    m, k = u.shape
    kw, n = conv_w.shape
    tm, tn = _matmul_tiles(m, k, math.gcd(n, col0), u.dtype.itemsize, w.dtype.itemsize,
                           jnp.dtype(out_dtype).itemsize, 1, False, conv_taps=kw, row_cap=seq)
    assert seq % tm == 0 and tm % _HALO == 0 and tn % _MXU_WIDTH == 0 and kw - 1 <= _HALO
    return pl.pallas_call(
        functools.partial(_proj_conv_kernel, tiles_per_seq=seq // tm),
        grid=(m // tm, n // tn),
        in_specs=[pl.BlockSpec((tm, k), lambda i, j: (i, 0)),
                  pl.BlockSpec((_HALO, k), lambda i, j: (jnp.maximum(i * (tm // _HALO) - 1, 0), 0)),
                  pl.BlockSpec((None, tn, k), lambda i, j: (layer, col0 // tn + j, 0)),
                  pl.BlockSpec((kw, tn), lambda i, j: (0, j)),
                  pl.BlockSpec((1, tn), lambda i, j: (0, j))],
        out_specs=pl.BlockSpec((tm, tn), lambda i, j: (i, j)),
        out_shape=jax.ShapeDtypeStruct((m, n), out_dtype),
        scratch_shapes=[pltpu.VMEM((tm + _HALO, k), _MXU_DTYPE)],
        compiler_params=_params(("parallel", "arbitrary")),
        name="proj_conv",
    )(u, u, w, conv_w, conv_b)


def _softplus(x):
    return jnp.maximum(x, 0.0) + jnp.log(1.0 + jnp.exp(-jnp.abs(x)))


def _chunk_cumsum(x, chunk):
    rows = x.shape[0]
    r = lax.broadcasted_iota(jnp.int32, (chunk, chunk), 0)
    c = lax.broadcasted_iota(jnp.int32, (chunk, chunk), 1)
    tri = jnp.where(c <= r, 1.0, 0.0).astype(_MXU_DTYPE)
    out = []
    for i in range(rows // chunk):
        rem = x[i * chunk:(i + 1) * chunk, :]
        acc = None
        for _ in range(3):
            piece = rem.astype(_MXU_DTYPE)
            d = jnp.dot(tri, piece, preferred_element_type=_F32)
            acc = d if acc is None else acc + d
            rem = rem - piece.astype(_F32)
        out.append(acc)
    return jnp.concatenate(out, axis=0) if len(out) > 1 else out[0]


def _row_bcast(ref, row, n):
    return jnp.broadcast_to(ref[pl.ds(row, 1), :], (n, ref.shape[1]))


def _tail_rows(w_ref, n_valid):
    row = lax.broadcasted_iota(jnp.int32, w_ref.shape, 0)
    return jnp.where(row < n_valid, w_ref[...], 0.0).astype(_MXU_DTYPE)


def _gla_gate_kernel(u_ref, wgr_ref, wup_ref, b_ref, o_ref, *, chunk, rank):
    gr = lax.dot_general(u_ref[...], _tail_rows(wgr_ref, rank), _NT, preferred_element_type=_F32)
    pre = jnp.dot(gr.astype(_MXU_DTYPE), wup_ref[...], preferred_element_type=_F32) + b_ref[...]
    gk = -_softplus(-pre) / _GLA_GATE_NORMALIZER
    o_ref[...] = _chunk_cumsum(gk * _LOG2E, chunk)


def _gla_gate(u, wt_in, layer, col0, rank, w_up, b_up, rows=512):
    m, d = u.shape
    kd = w_up.shape[1]
    rows = min(rows, m)
    assert col0 % _LANES == 0 and col0 + rank == wt_in.shape[1] and rank <= _LANES
    return pl.pallas_call(
        functools.partial(_gla_gate_kernel, chunk=_CHUNK, rank=rank),
        grid=(m // rows,),
        in_specs=[pl.BlockSpec((rows, d), lambda i: (i, 0)),
                  pl.BlockSpec((None, _LANES, d), lambda i: (layer, col0 // _LANES, 0)),
                  pl.BlockSpec(w_up.shape, lambda i: (0, 0)),
                  pl.BlockSpec((1, kd), lambda i: (0, 0))],
        out_specs=pl.BlockSpec((rows, kd), lambda i: (i, 0)),
        out_shape=jax.ShapeDtypeStruct((m, kd), _F32),
        compiler_params=_params(("parallel",)),
        name="gla_gate",
    )(u, wt_in, w_up, b_up)


def _gla_levels(c, nd):
    i = lax.broadcasted_iota(jnp.int32, (c, c), 0)
    j = lax.broadcasted_iota(jnp.int32, (c, c), 1)
    x = i ^ j
    lvl = jnp.zeros((c, c), jnp.int32)
    s, level = nd, 1
    while s < c:
        lvl = jnp.where(x >= s, level, lvl)
        s, level = 2 * s, level + 1
    return jnp.where(j > i, -1, lvl)


def _gla_scan_head(q_ref, k_ref, v_ref, gs_ref, b_ref, onw_ref, lvl_ref, o_ref, st_ref, qf_ref, kf_ref, scale):
    c, dk = q_ref.shape
    qf_ref[...] = q_ref[...].astype(_F32) * scale
    kf_ref[...] = k_ref[...].astype(_F32)
    q = qf_ref[...]
    k = kf_ref[...]
    v = v_ref[...]
    bc = b_ref[...]
    st = st_ref[...]
    b_last = jnp.concatenate([_row_bcast(b_ref, c - 1, 8)] * (c // 8), axis=0)

    o = lax.dot_general((q * jnp.exp2(bc)).astype(_MXU_DTYPE), st.astype(_MXU_DTYPE), _NT,
                        preferred_element_type=_F32)
    k_dec = (k * jnp.exp2(b_last - bc)).astype(_MXU_DTYPE)

    nd = _GLA_DIAG
    lane = lax.broadcasted_iota(jnp.int32, (nd, c), 1)
    pieces = []
    for m in range(c // nd):
        qb = qf_ref[m * nd:(m + 1) * nd, :]
        bb = b_ref[m * nd:(m + 1) * nd, :]
        acc = jnp.zeros((nd, c), _F32)
        for j in range(nd):
            kj = _row_bcast(kf_ref, m * nd + j, nd)
            bj = _row_bcast(b_ref, m * nd + j, nd)
            term = qb * kj * jnp.exp2(jnp.minimum(bb - bj, 0.0))
            acc = jnp.where(lane == m * nd + j, jnp.sum(term, axis=1, keepdims=True), acc)
        pieces.append(acc)
    lvl = lvl_ref[...]
    att = jnp.where(lvl == 0, jnp.concatenate(pieces, axis=0), 0.0)

    s, level = nd, 1
    while s < c:
        prev = [jnp.zeros((s, dk), _F32)]
        end = []
        for m in range(c // s):
            if m:
                prev += [_row_bcast(b_ref, m * s - 1, 8)] * (s // 8)
            end += [_row_bcast(b_ref, (m + 1) * s - 1, 8)] * (s // 8)
        qs = (q * jnp.exp2(bc - jnp.concatenate(prev, axis=0))).astype(_MXU_DTYPE)
        ks = (k * jnp.exp2(jnp.concatenate(end, axis=0) - bc)).astype(_MXU_DTYPE)
        att = jnp.where(lvl == level, lax.dot_general(qs, ks, _NT, preferred_element_type=_F32), att)
        s, level = 2 * s, level + 1

    o = o + jnp.dot(att.astype(_MXU_DTYPE), v, preferred_element_type=_F32)
    st_ref[...] = st * jnp.exp2(b_last[:1, :]) + lax.dot_general(
        v, k_dec, _TN, preferred_element_type=_F32)

    ms = jnp.mean(o * o, axis=-1, keepdims=True)
    on = (o * lax.rsqrt(ms + _RMS_EPS)) * onw_ref[...]
    o_ref[...] = (on * gs_ref[...].astype(_F32)).astype(o_ref.dtype)


def _gla_scan_kernel(q_ref, k_ref, v_ref, gs_ref, b_ref, onw_ref, lvl_ref, o_ref, st_ref, qf_ref, kf_ref,
                     *, scale, dk, dv):
    @pl.when(pl.program_id(2) == 0)
    def _():
        st_ref[...] = jnp.zeros_like(st_ref)

    for h in range(st_ref.shape[0]):
        ks, vs = pl.ds(h * dk, dk), pl.ds(h * dv, dv)
        _gla_scan_head(q_ref.at[:, ks], k_ref.at[:, ks], v_ref.at[:, vs], gs_ref.at[:, vs], b_ref.at[:, ks],
                       onw_ref, lvl_ref, o_ref.at[:, vs], st_ref.at[h], qf_ref.at[h], kf_ref.at[h], scale)


def _gla_scan(proj, bcum, o_norm_w, batch, seq, heads, dk, dv):
    m = proj.shape[0]
    c = _CHUNK
    hb = _GLA_HEADS_PER_STEP if heads % _GLA_HEADS_PER_STEP == 0 else 1
    assert c == _LANES and dk % _LANES == 0 and dv % _LANES == 0 and (2 * heads * dk) % (hb * dv) == 0
    nt = seq // c
    kd, vd = heads * dk, heads * dv
    wk, wv = hb * dk, hb * dv
    k_blk, v_blk, g_blk = kd // wk, (2 * kd) // wv, (2 * kd + vd) // wv

    def rows(b, h, t):
        return b * nt + t

    return pl.pallas_call(
        functools.partial(_gla_scan_kernel, scale=dk ** -0.5, dk=dk, dv=dv),
        grid=(batch, heads // hb, nt),
        in_specs=[pl.BlockSpec((c, wk), lambda b, h, t: (rows(b, h, t), h)),
                  pl.BlockSpec((c, wk), lambda b, h, t: (rows(b, h, t), k_blk + h)),
                  pl.BlockSpec((c, wv), lambda b, h, t: (rows(b, h, t), v_blk + h)),
                  pl.BlockSpec((c, wv), lambda b, h, t: (rows(b, h, t), g_blk + h)),
                  pl.BlockSpec((c, wk), lambda b, h, t: (rows(b, h, t), h)),
                  pl.BlockSpec((1, dv), lambda b, h, t: (0, 0)),
                  pl.BlockSpec((c, c), lambda b, h, t: (0, 0))],
        out_specs=pl.BlockSpec((c, wv), lambda b, h, t: (rows(b, h, t), h)),
        out_shape=jax.ShapeDtypeStruct((m, vd), _MXU_DTYPE),
        scratch_shapes=[pltpu.VMEM((hb, dv, dk), _F32),
                        pltpu.VMEM((hb, c, dk), _F32),
                        pltpu.VMEM((hb, c, dk), _F32)],
        compiler_params=_params(("parallel", "parallel", "arbitrary")),
        name="gla_scan",
    )(proj, proj, proj, proj, bcum, o_norm_w.reshape(1, dv).astype(_F32), _gla_levels(c, _GLA_DIAG))


def _gla_mixer(h, norm_w, batch, seq, layer, wt_in, w_gk_up, b_gk_up, o_norm_w, w_out):
    rank, kd = w_gk_up.shape
    vd = (wt_in.shape[1] - rank - 2 * kd) // 2
    dv = o_norm_w.shape[0]
    heads = vd // dv
    n_main = 2 * kd + 2 * vd
    w_up = jnp.pad(w_gk_up, ((0, _LANES - rank), (0, 0))).astype(_MXU_DTYPE)
    proj, u = _matmul(h, wt_in, layer, wt=True, n=n_main, norm_w=norm_w, emit_norm=True, act="silu",
                      act_col0=2 * kd + vd, out_dtype=_MXU_DTYPE)
    bcum = _gla_gate(u, wt_in, layer, n_main, rank, w_up, b_gk_up.reshape(1, kd).astype(_F32))
    o = _gla_scan(proj, bcum, o_norm_w, batch, seq, heads, kd // heads, dv)
    return _matmul_residual(o, w_out.astype(_MXU_DTYPE), layer, h)


def _ssd_gate_kernel(u_ref, wdt_ref, bias_ref, alog_ref, acol_ref, dcol_ref, arow_ref, drow_ref,
                     *, chunk, heads):
    raw = lax.dot_general(u_ref[...], _tail_rows(wdt_ref, heads), _NT, preferred_element_type=_F32)
    dt = _softplus(raw + bias_ref[...])
    a = _chunk_cumsum(dt * (-jnp.exp(alog_ref[...])) * _LOG2E, chunk)
    acol_ref[...] = a
    dcol_ref[...] = dt
    for i in range(a.shape[0] // chunk):
        arow_ref[i] = a[i * chunk:(i + 1) * chunk, :].T[:heads, :]
        drow_ref[i] = dt[i * chunk:(i + 1) * chunk, :].T[:heads, :]


def _ssd_gate(u, wt_in, layer, col0, dt_bias, a_log, heads, rows=512):
    m, d = u.shape
    c = _CHUNK
    rows = min(rows, m)
    g = _SSD_GROUPS
    hg = heads // g
    assert col0 % _LANES == 0 and col0 + heads == wt_in.shape[1] and heads <= _LANES
    col = jax.ShapeDtypeStruct((m, _LANES), _F32)
    row = jax.ShapeDtypeStruct((m // c, heads, c), _F32)
    col_spec = pl.BlockSpec((rows, _LANES), lambda i: (i, 0))
    row_spec = pl.BlockSpec((rows // c, heads, c), lambda i: (i, 0, 0))
    return pl.pallas_call(
        functools.partial(_ssd_gate_kernel, chunk=c, heads=heads),
        grid=(m // rows,),
        in_specs=[pl.BlockSpec((rows, d), lambda i: (i, 0)),
                  pl.BlockSpec((None, _LANES, d), lambda i: (layer, col0 // _LANES, 0)),
                  pl.BlockSpec((1, _LANES), lambda i: (0, 0)),
                  pl.BlockSpec((1, _LANES), lambda i: (0, 0))],
        out_specs=[col_spec, col_spec, row_spec, row_spec],
        out_shape=[col, col, row, row],
        compiler_params=_params(("parallel",)),
        name="ssd_gate",
    )(u, wt_in, dt_bias, a_log)


def _head_expanders(groups, hg, c, head_dim):
    def one_hot(rows, width):
        row = lax.broadcasted_iota(jnp.int32, (groups, rows, hg * width), 1) % _LANES
        col = lax.broadcasted_iota(jnp.int32, (groups, rows, hg * width), 2)
        grp = lax.broadcasted_iota(jnp.int32, (groups, rows, hg * width), 0)
        return (row == grp * hg + col // width).astype(_MXU_DTYPE)
    return one_hot(2 * _LANES, c), one_hot(_LANES, head_dim)


def _expand_heads(bcasts, head_dim):
    c = bcasts[0].shape[0]
    per = _LANES // head_dim
    lane = lax.broadcasted_iota(jnp.int32, (c, _LANES), 1)
    pieces = []
    for p in range(len(bcasts) // per):
        out = bcasts[p * per]
        for i in range(1, per):
            out = jnp.where(lane >= i * head_dim, bcasts[p * per + i], out)
        pieces.append(out)
    return jnp.concatenate(pieces, axis=1)


def _ssd_scan_group(zs_ref, x_ref, bm_ref, cm_ref, acol_ref, dcol_ref, esc_ref, ehp_ref, arow_ref, drow_ref,
                    dskip_ref, gnw_ref, o_ref, st_ref, head_dim):
    c = zs_ref.shape[0]
    hg = x_ref.shape[1] // head_dim
    xs = x_ref[...].astype(_F32)
    bmx = bm_ref[...]
    cmx = cm_ref[...]

    acol = acol_ref[...]
    a_last = acol[c - 1:c, :]
    w_state = (jnp.exp2(a_last - acol) * dcol_ref[...]).astype(_MXU_DTYPE)
    a_hi = acol.astype(_MXU_DTYPE)
    a_lo = (acol - a_hi.astype(_F32)).astype(_MXU_DTYPE)
    a_sc = jnp.dot(jnp.concatenate([a_hi, a_lo], axis=1), esc_ref[...],
                   preferred_element_type=_F32)
    w_hp = jnp.dot(w_state, ehp_ref[...], preferred_element_type=_F32)

    per = _LANES // head_dim
    width = per * c
    cb = lax.dot_general(cmx, jnp.concatenate([bmx] * per, axis=0), _NT,
                         preferred_element_type=_F32)
    ri = lax.broadcasted_iota(jnp.int32, (c, width), 0)
    ci = lax.broadcasted_iota(jnp.int32, (c, width), 1) & (c - 1)
    causal = ci <= ri
    lane = lax.broadcasted_iota(jnp.int32, (c, _LANES), 1)
    y_pieces = []
    for p in range(hg // per):
        ac = a_sc[:, p * width:(p + 1) * width]
        ar = arow_ref[:, p * width:(p + 1) * width]
        dr = drow_ref[:, p * width:(p + 1) * width]
        decay = jnp.exp2(jnp.where(causal, ac - ar, _NEG_BIG))
        sc = (cb * decay * dr).astype(_MXU_DTYPE)
        xp = x_ref[:, p * _LANES:(p + 1) * _LANES]
        bd = jnp.concatenate(
            [jnp.where((lane >= i * head_dim) & (lane < (i + 1) * head_dim), xp, jnp.zeros_like(xp))
             for i in range(per)], axis=0)
        y_pieces.append(jnp.dot(sc, bd, preferred_element_type=_F32))
    y = jnp.concatenate(y_pieces, axis=1)

    st = st_ref[...]
    ea = jnp.exp2(_expand_heads([a_sc[:, h * c:(h + 1) * c] for h in range(hg)], head_dim))
    y = y + jnp.dot(cmx, st.astype(_MXU_DTYPE), preferred_element_type=_F32) * ea
    xw = (xs * w_hp).astype(_MXU_DTYPE)
    st_ref[...] = st * ea[c - 1:c, :] + lax.dot_general(bmx, xw, _TN, preferred_element_type=_F32)

    y = y + dskip_ref[...] * xs
    y = y * zs_ref[...].astype(_F32)
    ms = jnp.mean(y * y, axis=-1, keepdims=True)
    o_ref[...] = ((y * lax.rsqrt(ms + _RMS_EPS)) * gnw_ref[...]).astype(o_ref.dtype)


def _ssd_scan_kernel(zs_ref, x_ref, bm_ref, cm_ref, acol_ref, dcol_ref, esc_ref, ehp_ref, arow_ref, drow_ref,
                     dskip_ref, gnw_ref, o_ref, st_ref, *, head_dim):
    @pl.when(pl.program_id(2) == 0)
    def _():
        st_ref[...] = jnp.zeros_like(st_ref)

    gb, n, gw = st_ref.shape
    c = zs_ref.shape[0]
    hg = gw // head_dim
    for g in range(gb):
        xs_, ns_, rs_ = pl.ds(g * gw, gw), pl.ds(g * n, n), pl.ds(g * hg * c, hg * c)
        _ssd_scan_group(zs_ref.at[:, xs_], x_ref.at[:, xs_], bm_ref.at[:, ns_], cm_ref.at[:, ns_],
                        acol_ref, dcol_ref, esc_ref.at[g], ehp_ref.at[g],
                        arow_ref.at[0, :, rs_], drow_ref.at[0, :, rs_],
                        dskip_ref.at[:, xs_], gnw_ref.at[:, xs_], o_ref.at[:, xs_], st_ref.at[g], head_dim)


def _ssd_scan(zs, xbc, acol, dcol, arow, drow, dskip, gnorm_w, batch, seq, inner, heads, n_state):
    m = zs.shape[0]
    c = _CHUNK
    g = _SSD_GROUPS
    gb = _SSD_GROUPS_PER_STEP if g % _SSD_GROUPS_PER_STEP == 0 else 1
    hg = heads // g
    p = inner // heads
    gw = hg * p
    sw, sn = gb * gw, gb * n_state
    assert c == _LANES and gw % _LANES == 0 and n_state % _LANES == 0 and _LANES % p == 0
    assert inner % sn == 0 and (g * n_state) % sn == 0
    nt = seq // c
    b_blk = inner // sn
    c_blk = (inner + g * n_state) // sn

    def rows(b, t):
        return b * nt + t

    in_specs = [
        pl.BlockSpec((c, sw), lambda b, gi, t: (rows(b, t), gi)),
        pl.BlockSpec((c, sw), lambda b, gi, t: (rows(b, t), gi)),
        pl.BlockSpec((c, sn), lambda b, gi, t: (rows(b, t), b_blk + gi)),
        pl.BlockSpec((c, sn), lambda b, gi, t: (rows(b, t), c_blk + gi)),
        pl.BlockSpec((c, _LANES), lambda b, gi, t: (rows(b, t), 0)),
        pl.BlockSpec((c, _LANES), lambda b, gi, t: (rows(b, t), 0)),
        pl.BlockSpec((gb, 2 * _LANES, hg * c), lambda b, gi, t: (gi, 0, 0)),
        pl.BlockSpec((gb, _LANES, gw), lambda b, gi, t: (gi, 0, 0)),
        pl.BlockSpec((1, 1, gb * hg * c), lambda b, gi, t: (rows(b, t), 0, gi)),
        pl.BlockSpec((1, 1, gb * hg * c), lambda b, gi, t: (rows(b, t), 0, gi)),
        pl.BlockSpec((1, sw), lambda b, gi, t: (0, gi)),
        pl.BlockSpec((1, sw), lambda b, gi, t: (0, gi)),
    ]
    return pl.pallas_call(
        functools.partial(_ssd_scan_kernel, head_dim=p),
        grid=(batch, g // gb, nt),
        in_specs=in_specs,
        out_specs=pl.BlockSpec((c, sw), lambda b, gi, t: (rows(b, t), gi)),
        out_shape=jax.ShapeDtypeStruct((m, inner), _MXU_DTYPE),
        scratch_shapes=[pltpu.VMEM((gb, n_state, gw), _F32)],
        compiler_params=_params(("parallel", "parallel", "arbitrary")),
        name="ssd_scan",
    )(zs, xbc, xbc, xbc, acol, dcol, *_head_expanders(g, hg, c, p), arow, drow, dskip, gnorm_w)


def _ssd_mixer(h, norm_w, batch, seq, layer, wt_in, conv_w, conv_b, dt_bias, a_log, d_skip, gnorm_w, w_out):
    heads = dt_bias.shape[0]
    conv_dim = conv_w.shape[1]
    inner = wt_in.shape[1] - conv_dim - heads
    n_state = (conv_dim - inner) // (2 * _SSD_GROUPS)
    n_main = inner + conv_dim
    m = h.shape[0]
    pad1 = lambda v: jnp.pad(v.astype(_F32), (0, _LANES - heads)).reshape(1, _LANES)
    zs, u = _matmul(h, wt_in, layer, wt=True, n=inner, norm_w=norm_w, emit_norm=True, act="silu",
                    out_dtype=_MXU_DTYPE)
    xbc = _proj_conv(u, wt_in, layer, inner, conv_w.astype(_F32), conv_b.reshape(1, conv_dim).astype(_F32),
                     seq, _MXU_DTYPE)
    acol, dcol, arow, drow = _ssd_gate(u, wt_in, layer, n_main, pad1(dt_bias), pad1(a_log), heads)
    arow = arow.reshape(m // _CHUNK, 1, heads * _CHUNK)
    drow = drow.reshape(m // _CHUNK, 1, heads * _CHUNK)
    dskip = jnp.repeat(d_skip.astype(_F32), inner // heads).reshape(1, inner)
    y = _ssd_scan(zs, xbc, acol, dcol, arow, drow, dskip, gnorm_w.reshape(1, inner).astype(_F32),
                  batch, seq, inner, heads, n_state)
    return _matmul_residual(y, w_out.astype(_MXU_DTYPE), layer, h)


def kernel(x, mixer_norm_w, gla_w_in, gla_w_gk_up, gla_b_gk_up, gla_o_norm_w, gla_w_out, ssd_w_in, ssd_conv_w, ssd_conv_b, ssd_dt_bias, ssd_a_log, ssd_d_skip, ssd_gnorm_w, ssd_w_out, mlp_norm_w, mlp_w_fc1, mlp_w_fc2, final_norm_w):
    batch, seq, d = x.shape
    h = x.reshape(batch * seq, d)
    w_fc1 = mlp_w_fc1.astype(_MXU_DTYPE)
    w_fc2 = mlp_w_fc2.astype(_MXU_DTYPE)
    gla_wt_in = jnp.swapaxes(gla_w_in, 1, 2).astype(_MXU_DTYPE)
    ssd_wt_in = jnp.swapaxes(ssd_w_in, 1, 2).astype(_MXU_DTYPE)
    for i in range(mixer_norm_w.shape[0]):
        j = i // 2
        if i % 2 == 0:
            h = _gla_mixer(h, mixer_norm_w[i], batch, seq, j, gla_wt_in, gla_w_gk_up[j], gla_b_gk_up[j],
                           gla_o_norm_w[j], gla_w_out)
        else:
            h = _ssd_mixer(h, mixer_norm_w[i], batch, seq, j, ssd_wt_in, ssd_conv_w[j], ssd_conv_b[j],
                           ssd_dt_bias[j], ssd_a_log[j], ssd_d_skip[j], ssd_gnorm_w[j], ssd_w_out)
        hidden = _matmul(h, w_fc1, i, norm_w=mlp_norm_w[i], act="relu2", out_dtype=_MXU_DTYPE)
        h = _matmul_residual(hidden, w_fc2, i, h)
    return _rmsnorm(h, final_norm_w, _F32).reshape(batch, seq, d)
```

```python
import functools
import math

import jax
import jax.numpy as jnp
from jax import lax
from jax.experimental import pallas as pl
from jax.experimental.pallas import tpu as pltpu

_F32 = jnp.float32
_MXU_DTYPE = jnp.bfloat16
_RMS_EPS = 1e-5
_GLA_GATE_NORMALIZER = 16.0
_SSD_GROUPS = 8
_CHUNK = 128
_LANES = 128
_GLA_DIAG = 8
_GLA_HEADS_PER_STEP = 4
_SSD_GROUPS_PER_STEP = 4
_LOG2E = math.log2(math.e)
_NEG_BIG = -1e30
_HALO = 16
_MXU_WIDTH = 256
_VMEM_LIMIT_BYTES = 56 * 1024 * 1024
_MATMUL_VMEM_BUDGET = 52 * 1024 * 1024

_NN = (((1,), (0,)), ((), ()))
_NT = (((1,), (1,)), ((), ()))
_TN = (((0,), (0,)), ((), ()))


def _params(sem, flags=None):
    return pltpu.CompilerParams(dimension_semantics=sem, vmem_limit_bytes=_VMEM_LIMIT_BYTES, flags=flags)


def _rms(x, w):
    ms = jnp.mean(x * x, axis=-1, keepdims=True)
    return (x * lax.rsqrt(ms + _RMS_EPS)) * w


def _rmsnorm_kernel(x_ref, w_ref, o_ref):
    o_ref[...] = _rms(x_ref[...].astype(_F32), w_ref[...]).astype(o_ref.dtype)


def _rmsnorm(x, w, out_dtype, rows=512):
    m, d = x.shape
    rows = min(rows, m)
    return pl.pallas_call(
        _rmsnorm_kernel,
        grid=(m // rows,),
        in_specs=[pl.BlockSpec((rows, d), lambda i: (i, 0)),
                  pl.BlockSpec((1, d), lambda i: (0, 0))],
        out_specs=pl.BlockSpec((rows, d), lambda i: (i, 0)),
        out_shape=jax.ShapeDtypeStruct((m, d), out_dtype),
        compiler_params=_params(("parallel",)),
        name="rmsnorm",
    )(x, w.reshape(1, d).astype(_F32))


def _silu(x):
    half = 0.5 * x
    return half + half * jnp.tanh(half)


def _matmul_kernel(*refs, act, act_tile0, has_norm, has_res, wt):
    it = iter(refs)
    a_ref = next(it)
    nw_ref = next(it) if has_norm else None
    w_ref = next(it)
    r_ref = next(it) if has_res else None
    o_ref = next(it)
    if has_norm:
        an_ref = next(it)

        @pl.when(pl.program_id(1) == 0)
        def _():
            an_ref[...] = _rms(a_ref[...], nw_ref[...]).astype(_MXU_DTYPE)

        a = an_ref[...]
    else:
        a = a_ref[...]
    w = w_ref[...].astype(_MXU_DTYPE)
    acc = lax.dot_general(a, w, _NT if wt else _NN, preferred_element_type=_F32)
    if act == "relu2":
        act_acc = jnp.square(jnp.maximum(acc, 0.0))
    elif act == "silu":
        act_acc = _silu(acc)
    if act is not None:
        acc = act_acc if act_tile0 == 0 else jnp.where(pl.program_id(1) >= act_tile0, act_acc, acc)
    if has_res:
        acc = r_ref[...] + acc
    o_ref[...] = acc.astype(o_ref.dtype)


def _divisor_tile(n, pref, align):
    if n <= pref:
        return n
    t = (pref // align) * align
    while n % t:
        t -= align
    return t


def _matmul_tiles(m, k, n, a_bytes, w_bytes, out_bytes, norm_copies, has_res, conv_taps=0, row_cap=None):
    mxu_bytes = jnp.dtype(_MXU_DTYPE).itemsize
    for pm, pn in ((1024, 2048), (1024, 1024), (1024, 512), (1024, 256), (512, 256), (256, 256), (128, 128)):
        pm = pm if row_cap is None else min(pm, row_cap)
        tm, tn = _divisor_tile(m, pm, 8), _divisor_tile(n, pn, _LANES)
        need = 2 * tm * k * a_bytes + 2 * k * tn * w_bytes + 2 * tm * tn * out_bytes
        need += (2 + conv_taps) * tm * _MXU_WIDTH * 4
        need += 2 * tm * tn * 4 if has_res else 0
        need += norm_copies * tm * k * mxu_bytes
        need += k * tn * mxu_bytes if w_bytes != mxu_bytes else 0
        if need <= _MATMUL_VMEM_BUDGET:
            break
    return tm, tn


def _matmul(a, w, layer, *, wt=False, n=None, norm_w=None, emit_norm=False, act=None, act_col0=0, res=None,
            out_dtype):
    m, k = a.shape
    n = w.shape[1 if wt else 2] if n is None else n
    tm, tn = _matmul_tiles(m, k, math.gcd(n, act_col0), a.dtype.itemsize, w.dtype.itemsize,
                           jnp.dtype(out_dtype).itemsize, (norm_w is not None) + emit_norm, res is not None)
    in_specs = [pl.BlockSpec((tm, k), lambda i, j: (i, 0))]
    args = [a]
    if norm_w is not None:
        in_specs.append(pl.BlockSpec((1, k), lambda i, j: (0, 0)))
        args.append(norm_w.reshape(1, k).astype(_F32))
    in_specs.append(pl.BlockSpec((None, tn, k), lambda i, j: (layer, j, 0)) if wt else
                    pl.BlockSpec((None, k, tn), lambda i, j: (layer, 0, j)))
    args.append(w)
    if res is not None:
        in_specs.append(pl.BlockSpec((tm, tn), lambda i, j: (i, j)))
        args.append(res)
    out_specs = [pl.BlockSpec((tm, tn), lambda i, j: (i, j))]
    out_shape = [jax.ShapeDtypeStruct((m, n), out_dtype)]
    scratch = []
    if emit_norm:
        out_specs.append(pl.BlockSpec((tm, k), lambda i, j: (i, 0)))
        out_shape.append(jax.ShapeDtypeStruct((m, k), _MXU_DTYPE))
    elif norm_w is not None:
        scratch.append(pltpu.VMEM((tm, k), _MXU_DTYPE))
    outs = pl.pallas_call(
        functools.partial(_matmul_kernel, act=act, act_tile0=act_col0 // tn, has_norm=norm_w is not None,
                          has_res=res is not None, wt=wt),
        grid=(m // tm, n // tn),
        in_specs=in_specs,
        out_specs=out_specs,
        out_shape=out_shape,
        scratch_shapes=scratch,
        compiler_params=_params(("parallel", "arbitrary")),
        name="matmul",
    )(*args)
    return outs if emit_norm else outs[0]


def _matmul_splitk_kernel(a_ref, w_ref, r_ref, o_ref):
    @pl.when(pl.program_id(2) == 0)
    def _():
        o_ref[...] = r_ref[...]

    o_ref[...] += jnp.dot(a_ref[...], w_ref[...].astype(_MXU_DTYPE), preferred_element_type=_F32)


def _matmul_residual(a, w, layer, res):
    m, k = a.shape
    n = w.shape[2]
    for nk in (1, 2, 4):
        tm, tn = _matmul_tiles(m, k // nk, n, a.dtype.itemsize, w.dtype.itemsize, 4, 0, True)
        if tm * tn >= min(m, 1024) * min(n, 1024):
            break
    if nk == 1:
        return _matmul(a, w, layer, res=res, out_dtype=_F32)
    tk = k // nk
    return pl.pallas_call(
        _matmul_splitk_kernel,
        grid=(m // tm, n // tn, nk),
        in_specs=[pl.BlockSpec((tm, tk), lambda i, j, kk: (i, kk)),
                  pl.BlockSpec((None, tk, tn), lambda i, j, kk: (layer, kk, j)),
                  pl.BlockSpec((tm, tn), lambda i, j, kk: (i, j))],
        out_specs=pl.BlockSpec((tm, tn), lambda i, j, kk: (i, j)),
        out_shape=jax.ShapeDtypeStruct((m, n), _F32),
        compiler_params=_params(("parallel", "parallel", "arbitrary")),
        name="matmul_splitk",
    )(a, w, res)


def _proj_conv_kernel(u_ref, halo_ref, w_ref, cw_ref, cb_ref, o_ref, an_ref, *, tiles_per_seq):
    kw = cw_ref.shape[0]

    @pl.when(pl.program_id(1) == 0)
    def _():
        halo = halo_ref[...]
        an_ref[0:_HALO, :] = jnp.where(pl.program_id(0) % tiles_per_seq == 0, jnp.zeros_like(halo), halo)
        an_ref[_HALO:, :] = u_ref[...]

    wb = w_ref[...].astype(_MXU_DTYPE)
    for c0 in range(0, o_ref.shape[1], _MXU_WIDTH):
        cols = pl.ds(c0, _MXU_WIDTH)
        acc = lax.dot_general(an_ref[...], wb[c0:c0 + _MXU_WIDTH, :], _NT, preferred_element_type=_F32)
        tm = o_ref.shape[0]
        out = cb_ref[:, cols] + cw_ref[kw - 1:kw, cols] * acc[_HALO:, :]
        for s in range(1, kw):
            out = out + cw_ref[kw - 1 - s:kw - s, cols] * acc[_HALO - s:_HALO - s + tm, :]
        o_ref[:, cols] = _silu(out).astype(o_ref.dtype)


def _proj_conv(u, w, layer, col0, conv_w, conv_b, seq, out_dtype):
    m, k = u.shape
    kw, n = conv_w.shape
    tm, tn = _matmul_tiles(m, k, math.gcd(n, col0), u.dtype.itemsize, w.dtype.itemsize,
                           jnp.dtype(out_dtype).itemsize, 1, False, conv_taps=kw, row_cap=seq)
    assert seq % tm == 0 and tm % _HALO == 0 and tn % _MXU_WIDTH == 0 and kw - 1 <= _HALO
    return pl.pallas_call(
        functools.partial(_proj_conv_kernel, tiles_per_seq=seq // tm),
        grid=(m // tm, n // tn),
        in_specs=[pl.BlockSpec((tm, k), lambda i, j: (i, 0)),
                  pl.BlockSpec((_HALO, k), lambda i, j: (jnp.maximum(i * (tm // _HALO) - 1, 0), 0)),
                  pl.BlockSpec((None, tn, k), lambda i, j: (layer, col0 // tn + j, 0)),
                  pl.BlockSpec((kw, tn), lambda i, j: (0, j)),
                  pl.BlockSpec((1, tn), lambda i, j: (0, j))],
        out_specs=pl.BlockSpec((tm, tn), lambda i, j: (i, j)),
        out_shape=jax.ShapeDtypeStruct((m, n), out_dtype),
        scratch_shapes=[pltpu.VMEM((tm + _HALO, k), _MXU_DTYPE)],
        compiler_params=_params(("parallel", "arbitrary")),
        name="proj_conv",
    )(u, u, w, conv_w, conv_b)


def _softplus(x):
    return jnp.maximum(x, 0.0) + jnp.log(1.0 + jnp.exp(-jnp.abs(x)))


def _chunk_cumsum(x, chunk):
    rows = x.shape[0]
    r = lax.broadcasted_iota(jnp.int32, (chunk, chunk), 0)
    c = lax.broadcasted_iota(jnp.int32, (chunk, chunk), 1)
    tri = jnp.where(c <= r, 1.0, 0.0).astype(_MXU_DTYPE)
    out = []
    for i in range(rows // chunk):
        rem = x[i * chunk:(i + 1) * chunk, :]
        acc = None
        for _ in range(3):
            piece = rem.astype(_MXU_DTYPE)
            d = jnp.dot(tri, piece, preferred_element_type=_F32)
            acc = d if acc is None else acc + d
            rem = rem - piece.astype(_F32)
        out.append(acc)
    return jnp.concatenate(out, axis=0) if len(out) > 1 else out[0]


def _row_bcast(ref, row, n):
    return jnp.broadcast_to(ref[pl.ds(row, 1), :], (n, ref.shape[1]))


def _tail_rows(w_ref, n_valid):
    row = lax.broadcasted_iota(jnp.int32, w_ref.shape, 0)
    return jnp.where(row < n_valid, w_ref[...], 0.0).astype(_MXU_DTYPE)


def _gla_gate_kernel(u_ref, wgr_ref, wup_ref, b_ref, o_ref, *, chunk, rank):
    gr = lax.dot_general(u_ref[...], _tail_rows(wgr_ref, rank), _NT, preferred_element_type=_F32)
    pre = jnp.dot(gr.astype(_MXU_DTYPE), wup_ref[...], preferred_element_type=_F32) + b_ref[...]
    gk = -_softplus(-pre) / _GLA_GATE_NORMALIZER
    o_ref[...] = _chunk_cumsum(gk * _LOG2E, chunk)


def _gla_gate(u, wt_in, layer, col0, rank, w_up, b_up, rows=512):
    m, d = u.shape
    kd = w_up.shape[1]
    rows = min(rows, m)
    assert col0 % _LANES == 0 and col0 + rank == wt_in.shape[1] and rank <= _LANES
    return pl.pallas_call(
        functools.partial(_gla_gate_kernel, chunk=_CHUNK, rank=rank),
        grid=(m // rows,),
        in_specs=[pl.BlockSpec((rows, d), lambda i: (i, 0)),
                  pl.BlockSpec((None, _LANES, d), lambda i: (layer, col0 // _LANES, 0)),
                  pl.BlockSpec(w_up.shape, lambda i: (0, 0)),
                  pl.BlockSpec((1, kd), lambda i: (0, 0))],
        out_specs=pl.BlockSpec((rows, kd), lambda i: (i, 0)),
        out_shape=jax.ShapeDtypeStruct((m, kd), _F32),
        compiler_params=_params(("parallel",)),
        name="gla_gate",
    )(u, wt_in, w_up, b_up)


def _gla_levels(c, nd):
    i = lax.broadcasted_iota(jnp.int32, (c, c), 0)
    j = lax.broadcasted_iota(jnp.int32, (c, c), 1)
    x = i ^ j
    lvl = jnp.zeros((c, c), jnp.int32)
    s, level = nd, 1
    while s < c:
        lvl = jnp.where(x >= s, level, lvl)
        s, level = 2 * s, level + 1
    return jnp.where(j > i, -1, lvl)


def _gla_scan_head(q_ref, k_ref, v_ref, gs_ref, b_ref, onw_ref, lvl_ref, o_ref, st_ref, qf_ref, kf_ref, scale):
    c, dk = q_ref.shape
    qf_ref[...] = q_ref[...].astype(_F32) * scale
    kf_ref[...] = k_ref[...].astype(_F32)
    q = qf_ref[...]
    k = kf_ref[...]
    v = v_ref[...]
    bc = b_ref[...]
    st = st_ref[...]
    b_last = jnp.concatenate([_row_bcast(b_ref, c - 1, 8)] * (c // 8), axis=0)

    o = lax.dot_general((q * jnp.exp2(bc)).astype(_MXU_DTYPE), st.astype(_MXU_DTYPE), _NT,
                        preferred_element_type=_F32)
    k_dec = (k * jnp.exp2(b_last - bc)).astype(_MXU_DTYPE)

    nd = _GLA_DIAG
    lane = lax.broadcasted_iota(jnp.int32, (nd, c), 1)
    pieces = []
    for m in range(c // nd):
        qb = qf_ref[m * nd:(m + 1) * nd, :]
        bb = b_ref[m * nd:(m + 1) * nd, :]
        acc = jnp.zeros((nd, c), _F32)
        for j in range(nd):
            kj = _row_bcast(kf_ref, m * nd + j, nd)
            bj = _row_bcast(b_ref, m * nd + j, nd)
            term = qb * kj * jnp.exp2(jnp.minimum(bb - bj, 0.0))
            acc = jnp.where(lane == m * nd + j, jnp.sum(term, axis=1, keepdims=True), acc)
        pieces.append(acc)
    lvl = lvl_ref[...]
    att = jnp.where(lvl == 0, jnp.concatenate(pieces, axis=0), 0.0)

    s, level = nd, 1
    while s < c:
        prev = [jnp.zeros((s, dk), _F32)]
        end = []
        for m in range(c // s):
            if m:
                prev += [_row_bcast(b_ref, m * s - 1, 8)] * (s // 8)
            end += [_row_bcast(b_ref, (m + 1) * s - 1, 8)] * (s // 8)
        qs = (q * jnp.exp2(bc - jnp.concatenate(prev, axis=0))).astype(_MXU_DTYPE)
        ks = (k * jnp.exp2(jnp.concatenate(end, axis=0) - bc)).astype(_MXU_DTYPE)
        att = jnp.where(lvl == level, lax.dot_general(qs, ks, _NT, preferred_element_type=_F32), att)
        s, level = 2 * s, level + 1

    o = o + jnp.dot(att.astype(_MXU_DTYPE), v, preferred_element_type=_F32)
    st_ref[...] = st * jnp.exp2(b_last[:1, :]) + lax.dot_general(
        v, k_dec, _TN, preferred_element_type=_F32)

    ms = jnp.mean(o * o, axis=-1, keepdims=True)
    on = (o * lax.rsqrt(ms + _RMS_EPS)) * onw_ref[...]
    o_ref[...] = (on * gs_ref[...].astype(_F32)).astype(o_ref.dtype)


def _gla_scan_kernel(q_ref, k_ref, v_ref, gs_ref, b_ref, onw_ref, lvl_ref, o_ref, st_ref, qf_ref, kf_ref,
                     *, scale, dk, dv):
    @pl.when(pl.program_id(2) == 0)
    def _():
        st_ref[...] = jnp.zeros_like(st_ref)

    for h in range(st_ref.shape[0]):
        ks, vs = pl.ds(h * dk, dk), pl.ds(h * dv, dv)
        _gla_scan_head(q_ref.at[:, ks], k_ref.at[:, ks], v_ref.at[:, vs], gs_ref.at[:, vs], b_ref.at[:, ks],
                       onw_ref, lvl_ref, o_ref.at[:, vs], st_ref.at[h], qf_ref.at[h], kf_ref.at[h], scale)


def _gla_scan(proj, bcum, o_norm_w, batch, seq, heads, dk, dv):
    m = proj.shape[0]
    c = _CHUNK
    hb = _GLA_HEADS_PER_STEP if heads % _GLA_HEADS_PER_STEP == 0 else 1
    assert c == _LANES and dk % _LANES == 0 and dv % _LANES == 0 and (2 * heads * dk) % (hb * dv) == 0
    nt = seq // c
    kd, vd = heads * dk, heads * dv
    wk, wv = hb * dk, hb * dv
    k_blk, v_blk, g_blk = kd // wk, (2 * kd) // wv, (2 * kd + vd) // wv

    def rows(b, h, t):
        return b * nt + t

    return pl.pallas_call(
        functools.partial(_gla_scan_kernel, scale=dk ** -0.5, dk=dk, dv=dv),
        grid=(batch, heads // hb, nt),
        in_specs=[pl.BlockSpec((c, wk), lambda b, h, t: (rows(b, h, t), h)),
                  pl.BlockSpec((c, wk), lambda b, h, t: (rows(b, h, t), k_blk + h)),
                  pl.BlockSpec((c, wv), lambda b, h, t: (rows(b, h, t), v_blk + h)),
                  pl.BlockSpec((c, wv), lambda b, h, t: (rows(b, h, t), g_blk + h)),
                  pl.BlockSpec((c, wk), lambda b, h, t: (rows(b, h, t), h)),
                  pl.BlockSpec((1, dv), lambda b, h, t: (0, 0)),
                  pl.BlockSpec((c, c), lambda b, h, t: (0, 0))],
        out_specs=pl.BlockSpec((c, wv), lambda b, h, t: (rows(b, h, t), h)),
        out_shape=jax.ShapeDtypeStruct((m, vd), _MXU_DTYPE),
        scratch_shapes=[pltpu.VMEM((hb, dv, dk), _F32),
                        pltpu.VMEM((hb, c, dk), _F32),
                        pltpu.VMEM((hb, c, dk), _F32)],
        compiler_params=_params(("parallel", "parallel", "arbitrary")),
        name="gla_scan",
    )(proj, proj, proj, proj, bcum, o_norm_w.reshape(1, dv).astype(_F32), _gla_levels(c, _GLA_DIAG))


def _gla_mixer(h, norm_w, batch, seq, layer, wt_in, w_gk_up, b_gk_up, o_norm_w, w_out):
    rank, kd = w_gk_up.shape
    vd = (wt_in.shape[1] - rank - 2 * kd) // 2
    dv = o_norm_w.shape[0]
    heads = vd // dv
    n_main = 2 * kd + 2 * vd
    w_up = jnp.pad(w_gk_up, ((0, _LANES - rank), (0, 0))).astype(_MXU_DTYPE)
    proj, u = _matmul(h, wt_in, layer, wt=True, n=n_main, norm_w=norm_w, emit_norm=True, act="silu",
                      act_col0=2 * kd + vd, out_dtype=_MXU_DTYPE)
    bcum = _gla_gate(u, wt_in, layer, n_main, rank, w_up, b_gk_up.reshape(1, kd).astype(_F32))
    o = _gla_scan(proj, bcum, o_norm_w, batch, seq, heads, kd // heads, dv)
    return _matmul_residual(o, w_out.astype(_MXU_DTYPE), layer, h)


def _ssd_gate_kernel(u_ref, wdt_ref, bias_ref, alog_ref, acol_ref, dcol_ref, arow_ref, drow_ref,
                     *, chunk, heads):
    raw = lax.dot_general(u_ref[...], _tail_rows(wdt_ref, heads), _NT, preferred_element_type=_F32)
    dt = _softplus(raw + bias_ref[...])
    a = _chunk_cumsum(dt * (-jnp.exp(alog_ref[...])) * _LOG2E, chunk)
    acol_ref[...] = a
    dcol_ref[...] = dt
    for i in range(a.shape[0] // chunk):
        arow_ref[i] = a[i * chunk:(i + 1) * chunk, :].T[:heads, :]
        drow_ref[i] = dt[i * chunk:(i + 1) * chunk, :].T[:heads, :]


def _ssd_gate(u, wt_in, layer, col0, dt_bias, a_log, heads, rows=512):
    m, d = u.shape
    c = _CHUNK
    rows = min(rows, m)
    g = _SSD_GROUPS
    hg = heads // g
    assert col0 % _LANES == 0 and col0 + heads == wt_in.shape[1] and heads <= _LANES
    col = jax.ShapeDtypeStruct((m, _LANES), _F32)
    row = jax.ShapeDtypeStruct((m // c, heads, c), _F32)
    col_spec = pl.BlockSpec((rows, _LANES), lambda i: (i, 0))
    row_spec = pl.BlockSpec((rows // c, heads, c), lambda i: (i, 0, 0))
    return pl.pallas_call(
        functools.partial(_ssd_gate_kernel, chunk=c, heads=heads),
        grid=(m // rows,),
        in_specs=[pl.BlockSpec((rows, d), lambda i: (i, 0)),
                  pl.BlockSpec((None, _LANES, d), lambda i: (layer, col0 // _LANES, 0)),
                  pl.BlockSpec((1, _LANES), lambda i: (0, 0)),
                  pl.BlockSpec((1, _LANES), lambda i: (0, 0))],
        out_specs=[col_spec, col_spec, row_spec, row_spec],
        out_shape=[col, col, row, row],
        compiler_params=_params(("parallel",)),
        name="ssd_gate",
    )(u, wt_in, dt_bias, a_log)


def _head_expanders(groups, hg, c, head_dim):
    def one_hot(rows, width):
        row = lax.broadcasted_iota(jnp.int32, (groups, rows, hg * width), 1) % _LANES
        col = lax.broadcasted_iota(jnp.int32, (groups, rows, hg * width), 2)
        grp = lax.broadcasted_iota(jnp.int32, (groups, rows, hg * width), 0)
        return (row == grp * hg + col // width).astype(_MXU_DTYPE)
    return one_hot(2 * _LANES, c), one_hot(_LANES, head_dim)


def _expand_heads(bcasts, head_dim):
    c = bcasts[0].shape[0]
    per = _LANES // head_dim
    lane = lax.broadcasted_iota(jnp.int32, (c, _LANES), 1)
    pieces = []
    for p in range(len(bcasts) // per):
        out = bcasts[p * per]
        for i in range(1, per):
            out = jnp.where(lane >= i * head_dim, bcasts[p * per + i], out)
        pieces.append(out)
    return jnp.concatenate(pieces, axis=1)


def _ssd_scan_group(zs_ref, x_ref, bm_ref, cm_ref, acol_ref, dcol_ref, esc_ref, ehp_ref, arow_ref, drow_ref,
                    dskip_ref, gnw_ref, o_ref, st_ref, head_dim):
    c = zs_ref.shape[0]
    hg = x_ref.shape[1] // head_dim
    xs = x_ref[...].astype(_F32)
    bmx = bm_ref[...]
    cmx = cm_ref[...]

    acol = acol_ref[...]
    a_last = acol[c - 1:c, :]
    w_state = (jnp.exp2(a_last - acol) * dcol_ref[...]).astype(_MXU_DTYPE)
    a_hi = acol.astype(_MXU_DTYPE)
    a_lo = (acol - a_hi.astype(_F32)).astype(_MXU_DTYPE)
    a_sc = jnp.dot(jnp.concatenate([a_hi, a_lo], axis=1), esc_ref[...],
                   preferred_element_type=_F32)
    w_hp = jnp.dot(w_state, ehp_ref[...], preferred_element_type=_F32)

    per = _LANES // head_dim
    width = per * c
    cb = lax.dot_general(cmx, jnp.concatenate([bmx] * per, axis=0), _NT,
                         preferred_element_type=_F32)
    ri = lax.broadcasted_iota(jnp.int32, (c, width), 0)
    ci = lax.broadcasted_iota(jnp.int32, (c, width), 1) & (c - 1)
    causal = ci <= ri
    lane = lax.broadcasted_iota(jnp.int32, (c, _LANES), 1)
    y_pieces = []
    for p in range(hg // per):
        ac = a_sc[:, p * width:(p + 1) * width]
        ar = arow_ref[:, p * width:(p + 1) * width]
        dr = drow_ref[:, p * width:(p + 1) * width]
        decay = jnp.exp2(jnp.where(causal, ac - ar, _NEG_BIG))
        sc = (cb * decay * dr).astype(_MXU_DTYPE)
        xp = x_ref[:, p * _LANES:(p + 1) * _LANES]
        bd = jnp.concatenate(
            [jnp.where((lane >= i * head_dim) & (lane < (i + 1) * head_dim), xp, jnp.zeros_like(xp))
             for i in range(per)], axis=0)
        y_pieces.append(jnp.dot(sc, bd, preferred_element_type=_F32))
    y = jnp.concatenate(y_pieces, axis=1)

    st = st_ref[...]
    ea = jnp.exp2(_expand_heads([a_sc[:, h * c:(h + 1) * c] for h in range(hg)], head_dim))
    y = y + jnp.dot(cmx, st.astype(_MXU_DTYPE), preferred_element_type=_F32) * ea
    xw = (xs * w_hp).astype(_MXU_DTYPE)
    st_ref[...] = st * ea[c - 1:c, :] + lax.dot_general(bmx, xw, _TN, preferred_element_type=_F32)

    y = y + dskip_ref[...] * xs
    y = y * zs_ref[...].astype(_F32)
    ms = jnp.mean(y * y, axis=-1, keepdims=True)
    o_ref[...] = ((y * lax.rsqrt(ms + _RMS_EPS)) * gnw_ref[...]).astype(o_ref.dtype)


def _ssd_scan_kernel(zs_ref, x_ref, bm_ref, cm_ref, acol_ref, dcol_ref, esc_ref, ehp_ref, arow_ref, drow_ref,
                     dskip_ref, gnw_ref, o_ref, st_ref, *, head_dim):
    @pl.when(pl.program_id(2) == 0)
    def _():
        st_ref[...] = jnp.zeros_like(st_ref)

    gb, n, gw = st_ref.shape
    c = zs_ref.shape[0]
    hg = gw // head_dim
    for g in range(gb):
        xs_, ns_, rs_ = pl.ds(g * gw, gw), pl.ds(g * n, n), pl.ds(g * hg * c, hg * c)
        _ssd_scan_group(zs_ref.at[:, xs_], x_ref.at[:, xs_], bm_ref.at[:, ns_], cm_ref.at[:, ns_],
                        acol_ref, dcol_ref, esc_ref.at[g], ehp_ref.at[g],
                        arow_ref.at[0, :, rs_], drow_ref.at[0, :, rs_],
                        dskip_ref.at[:, xs_], gnw_ref.at[:, xs_], o_ref.at[:, xs_], st_ref.at[g], head_dim)


def _ssd_scan(zs, xbc, acol, dcol, arow, drow, dskip, gnorm_w, batch, seq, inner, heads, n_state):
    m = zs.shape[0]
    c = _CHUNK
    g = _SSD_GROUPS
    gb = _SSD_GROUPS_PER_STEP if g % _SSD_GROUPS_PER_STEP == 0 else 1
    hg = heads // g
    p = inner // heads
    gw = hg * p
    sw, sn = gb * gw, gb * n_state
    assert c == _LANES and gw % _LANES == 0 and n_state % _LANES == 0 and _LANES % p == 0
    assert inner % sn == 0 and (g * n_state) % sn == 0
    nt = seq // c
    b_blk = inner // sn
    c_blk = (inner + g * n_state) // sn

    def rows(b, t):
        return b * nt + t

    in_specs = [
        pl.BlockSpec((c, sw), lambda b, gi, t: (rows(b, t), gi)),
        pl.BlockSpec((c, sw), lambda b, gi, t: (rows(b, t), gi)),
        pl.BlockSpec((c, sn), lambda b, gi, t: (rows(b, t), b_blk + gi)),
        pl.BlockSpec((c, sn), lambda b, gi, t: (rows(b, t), c_blk + gi)),
        pl.BlockSpec((c, _LANES), lambda b, gi, t: (rows(b, t), 0)),
        pl.BlockSpec((c, _LANES), lambda b, gi, t: (rows(b, t), 0)),
        pl.BlockSpec((gb, 2 * _LANES, hg * c), lambda b, gi, t: (gi, 0, 0)),
        pl.BlockSpec((gb, _LANES, gw), lambda b, gi, t: (gi, 0, 0)),
        pl.BlockSpec((1, 1, gb * hg * c), lambda b, gi, t: (rows(b, t), 0, gi)),
        pl.BlockSpec((1, 1, gb * hg * c), lambda b, gi, t: (rows(b, t), 0, gi)),
        pl.BlockSpec((1, sw), lambda b, gi, t: (0, gi)),
        pl.BlockSpec((1, sw), lambda b, gi, t: (0, gi)),
    ]
    return pl.pallas_call(
        functools.partial(_ssd_scan_kernel, head_dim=p),
        grid=(batch, g // gb, nt),
        in_specs=in_specs,
        out_specs=pl.BlockSpec((c, sw), lambda b, gi, t: (rows(b, t), gi)),
        out_shape=jax.ShapeDtypeStruct((m, inner), _MXU_DTYPE),
        scratch_shapes=[pltpu.VMEM((gb, n_state, gw), _F32)],
        compiler_params=_params(("parallel", "parallel", "arbitrary")),
        name="ssd_scan",
    )(zs, xbc, xbc, xbc, acol, dcol, *_head_expanders(g, hg, c, p), arow, drow, dskip, gnorm_w)


def _ssd_mixer(h, norm_w, batch, seq, layer, wt_in, conv_w, conv_b, dt_bias, a_log, d_skip, gnorm_w, w_out):
    heads = dt_bias.shape[0]
    conv_dim = conv_w.shape[1]
    inner = wt_in.shape[1] - conv_dim - heads
    n_state = (conv_dim - inner) // (2 * _SSD_GROUPS)
    n_main = inner + conv_dim
    m = h.shape[0]
    pad1 = lambda v: jnp.pad(v.astype(_F32), (0, _LANES - heads)).reshape(1, _LANES)
    zs, u = _matmul(h, wt_in, layer, wt=True, n=inner, norm_w=norm_w, emit_norm=True, act="silu",
                    out_dtype=_MXU_DTYPE)
    xbc = _proj_conv(u, wt_in, layer, inner, conv_w.astype(_F32), conv_b.reshape(1, conv_dim).astype(_F32),
                     seq, _MXU_DTYPE)
    acol, dcol, arow, drow = _ssd_gate(u, wt_in, layer, n_main, pad1(dt_bias), pad1(a_log), heads)
    arow = arow.reshape(m // _CHUNK, 1, heads * _CHUNK)
    drow = drow.reshape(m // _CHUNK, 1, heads * _CHUNK)
    dskip = jnp.repeat(d_skip.astype(_F32), inner // heads).reshape(1, inner)
    y = _ssd_scan(zs, xbc, acol, dcol, arow, drow, dskip, gnorm_w.reshape(1, inner).astype(_F32),
                  batch, seq, inner, heads, n_state)
    return _matmul_residual(y, w_out.astype(_MXU_DTYPE), layer, h)


def kernel(x, mixer_norm_w, gla_w_in, gla_w_gk_up, gla_b_gk_up, gla_o_norm_w, gla_w_out, ssd_w_in, ssd_conv_w, ssd_conv_b, ssd_dt_bias, ssd_a_log, ssd_d_skip, ssd_gnorm_w, ssd_w_out, mlp_norm_w, mlp_w_fc1, mlp_w_fc2, final_norm_w):
    batch, seq, d = x.shape
    h = x.reshape(batch * seq, d)
    w_fc1 = mlp_w_fc1.astype(_MXU_DTYPE)
    w_fc2 = mlp_w_fc2.astype(_MXU_DTYPE)
    gla_wt_in = jnp.swapaxes(gla_w_in, 1, 2).astype(_MXU_DTYPE)
    ssd_wt_in = jnp.swapaxes(ssd_w_in, 1, 2).astype(_MXU_DTYPE)
    for i in range(mixer_norm_w.shape[0]):
        j = i // 2
        if i % 2 == 0:
            h = _gla_mixer(h, mixer_norm_w[i], batch, seq, j, gla_wt_in, gla_w_gk_up[j], gla_b_gk_up[j],
                           gla_o_norm_w[j], gla_w_out)
        else:
            h = _ssd_mixer(h, mixer_norm_w[i], batch, seq, j, ssd_wt_in, ssd_conv_w[j], ssd_conv_b[j],
                           ssd_dt_bias[j], ssd_a_log[j], ssd_d_skip[j], ssd_gnorm_w[j], ssd_w_out)
        hidden = _matmul(h, w_fc1, i, norm_w=mlp_norm_w[i], act="relu2", out_dtype=_MXU_DTYPE)
        h = _matmul_residual(hidden, w_fc2, i, h)
    return _rmsnorm(h, final_norm_w, _F32).reshape(batch, seq, d)
```

```python
import functools
import math

import jax
import jax.numpy as jnp
from jax import lax
from jax.experimental import pallas as pl
from jax.experimental.pallas import tpu as pltpu

_F32 = jnp.float32
_MXU_DTYPE = jnp.bfloat16
_RMS_EPS = 1e-5
_GLA_GATE_NORMALIZER = 16.0
_SSD_GROUPS = 8
_CHUNK = 128
_LANES = 128
_GLA_DIAG = 8
_GLA_HEADS_PER_STEP = 4
_SSD_GROUPS_PER_STEP = 8
_LOG2E = math.log2(math.e)
_NEG_BIG = -1e30
_HALO = 16
_MXU_WIDTH = 256
_VMEM_LIMIT_BYTES = 56 * 1024 * 1024
_MATMUL_VMEM_BUDGET = 52 * 1024 * 1024

_NN = (((1,), (0,)), ((), ()))
_NT = (((1,), (1,)), ((), ()))
_TN = (((0,), (0,)), ((), ()))


def _params(sem, flags=None):
    return pltpu.CompilerParams(dimension_semantics=sem, vmem_limit_bytes=_VMEM_LIMIT_BYTES, flags=flags)


def _rms(x, w):
    ms = jnp.mean(x * x, axis=-1, keepdims=True)
    return (x * lax.rsqrt(ms + _RMS_EPS)) * w


def _rmsnorm_kernel(x_ref, w_ref, o_ref):
    o_ref[...] = _rms(x_ref[...].astype(_F32), w_ref[...]).astype(o_ref.dtype)


def _rmsnorm(x, w, out_dtype, rows=512):
    m, d = x.shape
    rows = min(rows, m)
    return pl.pallas_call(
        _rmsnorm_kernel,
        grid=(m // rows,),
        in_specs=[pl.BlockSpec((rows, d), lambda i: (i, 0)),
                  pl.BlockSpec((1, d), lambda i: (0, 0))],
        out_specs=pl.BlockSpec((rows, d), lambda i: (i, 0)),
        out_shape=jax.ShapeDtypeStruct((m, d), out_dtype),
        compiler_params=_params(("parallel",)),
        name="rmsnorm",
    )(x, w.reshape(1, d).astype(_F32))


def _silu(x):
    half = 0.5 * x
    return half + half * jnp.tanh(half)


def _matmul_kernel(*refs, act, act_tile0, has_norm, has_res, wt):
    it = iter(refs)
    a_ref = next(it)
    nw_ref = next(it) if has_norm else None
    w_ref = next(it)
    r_ref = next(it) if has_res else None
    o_ref = next(it)
    if has_norm:
        an_ref = next(it)

        @pl.when(pl.program_id(1) == 0)
        def _():
            an_ref[...] = _rms(a_ref[...], nw_ref[...]).astype(_MXU_DTYPE)

        a = an_ref[...]
    else:
        a = a_ref[...]
    w = w_ref[...].astype(_MXU_DTYPE)
    acc = lax.dot_general(a, w, _NT if wt else _NN, preferred_element_type=_F32)
    if act == "relu2":
        act_acc = jnp.square(jnp.maximum(acc, 0.0))
    elif act == "silu":
        act_acc = _silu(acc)
    if act is not None:
        acc = act_acc if act_tile0 == 0 else jnp.where(pl.program_id(1) >= act_tile0, act_acc, acc)
    if has_res:
        acc = r_ref[...] + acc
    o_ref[...] = acc.astype(o_ref.dtype)


def _divisor_tile(n, pref, align):
    if n <= pref:
        return n
    t = (pref // align) * align
    while n % t:
        t -= align
    return t


def _matmul_tiles(m, k, n, a_bytes, w_bytes, out_bytes, norm_copies, has_res, conv_taps=0, row_cap=None):
    mxu_bytes = jnp.dtype(_MXU_DTYPE).itemsize
    for pm, pn in ((1024, 2048), (1024, 1024), (1024, 512), (1024, 256), (512, 256), (256, 256), (128, 128)):
        pm = pm if row_cap is None else min(pm, row_cap)
        tm, tn = _divisor_tile(m, pm, 8), _divisor_tile(n, pn, _LANES)
        need = 2 * tm * k * a_bytes + 2 * k * tn * w_bytes + 2 * tm * tn * out_bytes
        need += (2 + conv_taps) * tm * _MXU_WIDTH * 4
        need += 2 * tm * tn * 4 if has_res else 0
        need += norm_copies * tm * k * mxu_bytes
        need += k * tn * mxu_bytes if w_bytes != mxu_bytes else 0
        if need <= _MATMUL_VMEM_BUDGET:
            break
    return tm, tn


def _matmul(a, w, layer, *, wt=False, n=None, norm_w=None, emit_norm=False, act=None, act_col0=0, res=None,
            out_dtype):
    m, k = a.shape
    n = w.shape[1 if wt else 2] if n is None else n
    tm, tn = _matmul_tiles(m, k, math.gcd(n, act_col0), a.dtype.itemsize, w.dtype.itemsize,
                           jnp.dtype(out_dtype).itemsize, (norm_w is not None) + emit_norm, res is not None)
    in_specs = [pl.BlockSpec((tm, k), lambda i, j: (i, 0))]
    args = [a]
    if norm_w is not None:
        in_specs.append(pl.BlockSpec((1, k), lambda i, j: (0, 0)))
        args.append(norm_w.reshape(1, k).astype(_F32))
    in_specs.append(pl.BlockSpec((None, tn, k), lambda i, j: (layer, j, 0)) if wt else
                    pl.BlockSpec((None, k, tn), lambda i, j: (layer, 0, j)))
    args.append(w)
    if res is not None:
        in_specs.append(pl.BlockSpec((tm, tn), lambda i, j: (i, j)))
        args.append(res)
    out_specs = [pl.BlockSpec((tm, tn), lambda i, j: (i, j))]
    out_shape = [jax.ShapeDtypeStruct((m, n), out_dtype)]
    scratch = []
    if emit_norm:
        out_specs.append(pl.BlockSpec((tm, k), lambda i, j: (i, 0)))
        out_shape.append(jax.ShapeDtypeStruct((m, k), _MXU_DTYPE))
    elif norm_w is not None:
        scratch.append(pltpu.VMEM((tm, k), _MXU_DTYPE))
    outs = pl.pallas_call(
        functools.partial(_matmul_kernel, act=act, act_tile0=act_col0 // tn, has_norm=norm_w is not None,
                          has_res=res is not None, wt=wt),
        grid=(m // tm, n // tn),
        in_specs=in_specs,
        out_specs=out_specs,
        out_shape=out_shape,
        scratch_shapes=scratch,
        compiler_params=_params(("parallel", "arbitrary")),
        name="matmul",
    )(*args)
    return outs if emit_norm else outs[0]


def _matmul_splitk_kernel(a_ref, w_ref, r_ref, o_ref):
    @pl.when(pl.program_id(2) == 0)
    def _():
        o_ref[...] = r_ref[...]

    o_ref[...] += jnp.dot(a_ref[...], w_ref[...].astype(_MXU_DTYPE), preferred_element_type=_F32)


def _matmul_residual(a, w, layer, res):
    m, k = a.shape
    n = w.shape[2]
    for nk in (1, 2, 4):
        tm, tn = _matmul_tiles(m, k // nk, n, a.dtype.itemsize, w.dtype.itemsize, 4, 0, True)
        if tm * tn >= min(m, 1024) * min(n, 1024):
            break
    if nk == 1:
        return _matmul(a, w, layer, res=res, out_dtype=_F32)
    tk = k // nk
    return pl.pallas_call(
        _matmul_splitk_kernel,
        grid=(m // tm, n // tn, nk),
        in_specs=[pl.BlockSpec((tm, tk), lambda i, j, kk: (i, kk)),
                  pl.BlockSpec((None, tk, tn), lambda i, j, kk: (layer, kk, j)),
                  pl.BlockSpec((tm, tn), lambda i, j, kk: (i, j))],
        out_specs=pl.BlockSpec((tm, tn), lambda i, j, kk: (i, j)),
        out_shape=jax.ShapeDtypeStruct((m, n), _F32),
        compiler_params=_params(("parallel", "parallel", "arbitrary")),
        name="matmul_splitk",
    )(a, w, res)


def _proj_conv_kernel(u_ref, halo_ref, w_ref, cw_ref, cb_ref, o_ref, an_ref, *, tiles_per_seq):
    kw = cw_ref.shape[0]

    @pl.when(pl.program_id(1) == 0)
    def _():
        halo = halo_ref[...]
        an_ref[0:_HALO, :] = jnp.where(pl.program_id(0) % tiles_per_seq == 0, jnp.zeros_like(halo), halo)
        an_ref[_HALO:, :] = u_ref[...]

    wb = w_ref[...].astype(_MXU_DTYPE)
    for c0 in range(0, o_ref.shape[1], _MXU_WIDTH):
        cols = pl.ds(c0, _MXU_WIDTH)
        acc = lax.dot_general(an_ref[...], wb[c0:c0 + _MXU_WIDTH, :], _NT, preferred_element_type=_F32)
        tm = o_ref.shape[0]
        out = cb_ref[:, cols] + cw_ref[kw - 1:kw, cols] * acc[_HALO:, :]
        for s in range(1, kw):
            out = out + cw_ref[kw - 1 - s:kw - s, cols] * acc[_HALO - s:_HALO - s + tm, :]
        o_ref[:, cols] = _silu(out).astype(o_ref.dtype)


def _proj_conv(u, w, layer, col0, conv_w, conv_b, seq, out_dtype):
    m, k = u.shape
    kw, n = conv_w.shape
    tm, tn = _matmul_tiles(m, k, math.gcd(n, col0), u.dtype.itemsize, w.dtype.itemsize,
                           jnp.dtype(out_dtype).itemsize, 1, False, conv_taps=kw, row_cap=seq)
    assert seq % tm == 0 and tm % _HALO == 0 and tn % _MXU_WIDTH == 0 and kw - 1 <= _HALO
    return pl.pallas_call(
        functools.partial(_proj_conv_kernel, tiles_per_seq=seq // tm),
        grid=(m // tm, n // tn),
        in_specs=[pl.BlockSpec((tm, k), lambda i, j: (i, 0)),
                  pl.BlockSpec((_HALO, k), lambda i, j: (jnp.maximum(i * (tm // _HALO) - 1, 0), 0)),
                  pl.BlockSpec((None, tn, k), lambda i, j: (layer, col0 // tn + j, 0)),
                  pl.BlockSpec((kw, tn), lambda i, j: (0, j)),
                  pl.BlockSpec((1, tn), lambda i, j: (0, j))],
        out_specs=pl.BlockSpec((tm, tn), lambda i, j: (i, j)),
        out_shape=jax.ShapeDtypeStruct((m, n), out_dtype),
        scratch_shapes=[pltpu.VMEM((tm + _HALO, k), _MXU_DTYPE)],
        compiler_params=_params(("parallel", "arbitrary")),
        name="proj_conv",
    )(u, u, w, conv_w, conv_b)


def _softplus(x):
    return jnp.maximum(x, 0.0) + jnp.log(1.0 + jnp.exp(-jnp.abs(x)))


def _chunk_cumsum(x, chunk):
    rows = x.shape[0]
    r = lax.broadcasted_iota(jnp.int32, (chunk, chunk), 0)
    c = lax.broadcasted_iota(jnp.int32, (chunk, chunk), 1)
    tri = jnp.where(c <= r, 1.0, 0.0).astype(_MXU_DTYPE)
    out = []
    for i in range(rows // chunk):
        rem = x[i * chunk:(i + 1) * chunk, :]
        acc = None
        for _ in range(3):
            piece = rem.astype(_MXU_DTYPE)
            d = jnp.dot(tri, piece, preferred_element_type=_F32)
            acc = d if acc is None else acc + d
            rem = rem - piece.astype(_F32)
        out.append(acc)
    return jnp.concatenate(out, axis=0) if len(out) > 1 else out[0]


def _row_bcast(ref, row, n):
    return jnp.broadcast_to(ref[pl.ds(row, 1), :], (n, ref.shape[1]))


def _tail_rows(w_ref, n_valid):
    row = lax.broadcasted_iota(jnp.int32, w_ref.shape, 0)
    return jnp.where(row < n_valid, w_ref[...], 0.0).astype(_MXU_DTYPE)


def _gla_gate_kernel(u_ref, wgr_ref, wup_ref, b_ref, o_ref, *, chunk, rank):
    gr = lax.dot_general(u_ref[...], _tail_rows(wgr_ref, rank), _NT, preferred_element_type=_F32)
    pre = jnp.dot(gr.astype(_MXU_DTYPE), wup_ref[...], preferred_element_type=_F32) + b_ref[...]
    o_ref[...] = _chunk_cumsum(_softplus(-pre) * (-_LOG2E / _GLA_GATE_NORMALIZER), chunk)


def _gla_gate(u, wt_in, layer, col0, rank, w_up, b_up, rows=1024):
    m, d = u.shape
    kd = w_up.shape[1]
    rows = min(rows, m)
    assert col0 % _LANES == 0 and col0 + rank == wt_in.shape[1] and rank <= _LANES
    return pl.pallas_call(
        functools.partial(_gla_gate_kernel, chunk=_CHUNK, rank=rank),
        grid=(m // rows,),
        in_specs=[pl.BlockSpec((rows, d), lambda i: (i, 0)),
                  pl.BlockSpec((None, _LANES, d), lambda i: (layer, col0 // _LANES, 0)),
                  pl.BlockSpec(w_up.shape, lambda i: (0, 0)),
                  pl.BlockSpec((1, kd), lambda i: (0, 0))],
        out_specs=pl.BlockSpec((rows, kd), lambda i: (i, 0)),
        out_shape=jax.ShapeDtypeStruct((m, kd), _F32),
        compiler_params=_params(("parallel",)),
        name="gla_gate",
    )(u, wt_in, w_up, b_up)


def _gla_levels(c, nd):
    i = lax.broadcasted_iota(jnp.int32, (c, c), 0)
    j = lax.broadcasted_iota(jnp.int32, (c, c), 1)
    x = i ^ j
    lvl = jnp.zeros((c, c), jnp.int32)
    s, level = nd, 1
    while s < c:
        lvl = jnp.where(x >= s, level, lvl)
        s, level = 2 * s, level + 1
    return jnp.where(j > i, -1, lvl)


def _gla_scan_head(q_ref, k_ref, v_ref, gs_ref, b_ref, onw_ref, lvl_ref, o_ref, st_ref, qf_ref, kf_ref, scale):
    c, dk = q_ref.shape
    qf_ref[...] = q_ref[...].astype(_F32) * scale
    kf_ref[...] = k_ref[...].astype(_F32)
    q = qf_ref[...]
    k = kf_ref[...]
    v = v_ref[...]
    bc = b_ref[...]
    st = st_ref[...]
    b_last = jnp.concatenate([_row_bcast(b_ref, c - 1, 8)] * (c // 8), axis=0)

    o = lax.dot_general((q * jnp.exp2(bc)).astype(_MXU_DTYPE), st.astype(_MXU_DTYPE), _NT,
                        preferred_element_type=_F32)
    k_dec = (k * jnp.exp2(b_last - bc)).astype(_MXU_DTYPE)

    nd = _GLA_DIAG
    lane = lax.broadcasted_iota(jnp.int32, (nd, c), 1)
    pieces = []
    for m in range(c // nd):
        qb = qf_ref[m * nd:(m + 1) * nd, :]
        bb = b_ref[m * nd:(m + 1) * nd, :]
        acc = jnp.zeros((nd, c), _F32)
        for j in range(nd):
            kj = _row_bcast(kf_ref, m * nd + j, nd)
            bj = _row_bcast(b_ref, m * nd + j, nd)
            term = qb * kj * jnp.exp2(jnp.minimum(bb - bj, 0.0))
            acc = jnp.where(lane == m * nd + j, jnp.sum(term, axis=1, keepdims=True), acc)
        pieces.append(acc)
    lvl = lvl_ref[...]
    att = jnp.where(lvl == 0, jnp.concatenate(pieces, axis=0), 0.0)

    s, level = nd, 1
    while s < c:
        prev = [jnp.zeros((s, dk), _F32)]
        end = []
        for m in range(c // s):
            if m:
                prev += [_row_bcast(b_ref, m * s - 1, 8)] * (s // 8)
            end += [_row_bcast(b_ref, (m + 1) * s - 1, 8)] * (s // 8)
        qs = (q * jnp.exp2(bc - jnp.concatenate(prev, axis=0))).astype(_MXU_DTYPE)
        ks = (k * jnp.exp2(jnp.concatenate(end, axis=0) - bc)).astype(_MXU_DTYPE)
        att = jnp.where(lvl == level, lax.dot_general(qs, ks, _NT, preferred_element_type=_F32), att)
        s, level = 2 * s, level + 1

    o = o + jnp.dot(att.astype(_MXU_DTYPE), v, preferred_element_type=_F32)
    st_ref[...] = st * jnp.exp2(b_last[:1, :]) + lax.dot_general(
        v, k_dec, _TN, preferred_element_type=_F32)

    ms = jnp.mean(o * o, axis=-1, keepdims=True)
    on = (o * lax.rsqrt(ms + _RMS_EPS)) * onw_ref[...]
    o_ref[...] = (on * gs_ref[...].astype(_F32)).astype(o_ref.dtype)


def _gla_scan_kernel(q_ref, k_ref, v_ref, gs_ref, b_ref, onw_ref, lvl_ref, o_ref, st_ref, qf_ref, kf_ref,
                     *, scale, dk, dv):
    @pl.when(pl.program_id(2) == 0)
    def _():
        st_ref[...] = jnp.zeros_like(st_ref)

    for h in range(st_ref.shape[0]):
        ks, vs = pl.ds(h * dk, dk), pl.ds(h * dv, dv)
        _gla_scan_head(q_ref.at[:, ks], k_ref.at[:, ks], v_ref.at[:, vs], gs_ref.at[:, vs], b_ref.at[:, ks],
                       onw_ref, lvl_ref, o_ref.at[:, vs], st_ref.at[h], qf_ref.at[h], kf_ref.at[h], scale)


def _gla_scan(proj, bcum, o_norm_w, batch, seq, heads, dk, dv):
    m = proj.shape[0]
    c = _CHUNK
    hb = _GLA_HEADS_PER_STEP if heads % _GLA_HEADS_PER_STEP == 0 else 1
    assert c == _LANES and dk % _LANES == 0 and dv % _LANES == 0 and (2 * heads * dk) % (hb * dv) == 0
    nt = seq // c
    kd, vd = heads * dk, heads * dv
    wk, wv = hb * dk, hb * dv
    k_blk, v_blk, g_blk = kd // wk, (2 * kd) // wv, (2 * kd + vd) // wv

    def rows(b, h, t):
        return b * nt + t

    return pl.pallas_call(
        functools.partial(_gla_scan_kernel, scale=dk ** -0.5, dk=dk, dv=dv),
        grid=(batch, heads // hb, nt),
        in_specs=[pl.BlockSpec((c, wk), lambda b, h, t: (rows(b, h, t), h)),
                  pl.BlockSpec((c, wk), lambda b, h, t: (rows(b, h, t), k_blk + h)),
                  pl.BlockSpec((c, wv), lambda b, h, t: (rows(b, h, t), v_blk + h)),
                  pl.BlockSpec((c, wv), lambda b, h, t: (rows(b, h, t), g_blk + h)),
                  pl.BlockSpec((c, wk), lambda b, h, t: (rows(b, h, t), h)),
                  pl.BlockSpec((1, dv), lambda b, h, t: (0, 0)),
                  pl.BlockSpec((c, c), lambda b, h, t: (0, 0))],
        out_specs=pl.BlockSpec((c, wv), lambda b, h, t: (rows(b, h, t), h)),
        out_shape=jax.ShapeDtypeStruct((m, vd), _MXU_DTYPE),
        scratch_shapes=[pltpu.VMEM((hb, dv, dk), _F32),
                        pltpu.VMEM((hb, c, dk), _F32),
                        pltpu.VMEM((hb, c, dk), _F32)],
        compiler_params=_params(("parallel", "parallel", "arbitrary")),
        name="gla_scan",
    )(proj, proj, proj, proj, bcum, o_norm_w.reshape(1, dv).astype(_F32), _gla_levels(c, _GLA_DIAG))


def _gla_mixer(h, norm_w, batch, seq, layer, wt_in, w_gk_up, b_gk_up, o_norm_w, w_out):
    rank, kd = w_gk_up.shape
    vd = (wt_in.shape[1] - rank - 2 * kd) // 2
    dv = o_norm_w.shape[0]
    heads = vd // dv
    n_main = 2 * kd + 2 * vd
    w_up = jnp.pad(w_gk_up, ((0, _LANES - rank), (0, 0))).astype(_MXU_DTYPE)
    proj, u = _matmul(h, wt_in, layer, wt=True, n=n_main, norm_w=norm_w, emit_norm=True, act="silu",
                      act_col0=2 * kd + vd, out_dtype=_MXU_DTYPE)
    bcum = _gla_gate(u, wt_in, layer, n_main, rank, w_up, b_gk_up.reshape(1, kd).astype(_F32))
    o = _gla_scan(proj, bcum, o_norm_w, batch, seq, heads, kd // heads, dv)
    return _matmul_residual(o, w_out.astype(_MXU_DTYPE), layer, h)


def _ssd_gate_kernel(u_ref, wdt_ref, bias_ref, alog_ref, acol_ref, dcol_ref, arow_ref, drow_ref,
                     *, chunk, heads):
    raw = lax.dot_general(u_ref[...], _tail_rows(wdt_ref, heads), _NT, preferred_element_type=_F32)
    dt = _softplus(raw + bias_ref[...])
    a = _chunk_cumsum(dt * (-jnp.exp(alog_ref[...])) * _LOG2E, chunk)
    acol_ref[...] = a
    dcol_ref[...] = dt
    for i in range(a.shape[0] // chunk):
        arow_ref[i] = a[i * chunk:(i + 1) * chunk, :].T[:heads, :]
        drow_ref[i] = dt[i * chunk:(i + 1) * chunk, :].T[:heads, :]


def _ssd_gate(u, wt_in, layer, col0, dt_bias, a_log, heads, rows=1024):
    m, d = u.shape
    c = _CHUNK
    rows = min(rows, m)
    g = _SSD_GROUPS
    hg = heads // g
    assert col0 % _LANES == 0 and col0 + heads == wt_in.shape[1] and heads <= _LANES
    col = jax.ShapeDtypeStruct((m, _LANES), _F32)
    row = jax.ShapeDtypeStruct((m // c, heads, c), _F32)
    col_spec = pl.BlockSpec((rows, _LANES), lambda i: (i, 0))
    row_spec = pl.BlockSpec((rows // c, heads, c), lambda i: (i, 0, 0))
    return pl.pallas_call(
        functools.partial(_ssd_gate_kernel, chunk=c, heads=heads),
        grid=(m // rows,),
        in_specs=[pl.BlockSpec((rows, d), lambda i: (i, 0)),
                  pl.BlockSpec((None, _LANES, d), lambda i: (layer, col0 // _LANES, 0)),
                  pl.BlockSpec((1, _LANES), lambda i: (0, 0)),
                  pl.BlockSpec((1, _LANES), lambda i: (0, 0))],
        out_specs=[col_spec, col_spec, row_spec, row_spec],
        out_shape=[col, col, row, row],
        compiler_params=_params(("parallel",)),
        name="ssd_gate",
    )(u, wt_in, dt_bias, a_log)


def _head_expanders(groups, hg, c, head_dim):
    def one_hot(rows, width):
        row = lax.broadcasted_iota(jnp.int32, (groups, rows, hg * width), 1) % _LANES
        col = lax.broadcasted_iota(jnp.int32, (groups, rows, hg * width), 2)
        grp = lax.broadcasted_iota(jnp.int32, (groups, rows, hg * width), 0)
        return (row == grp * hg + col // width).astype(_MXU_DTYPE)
    return one_hot(2 * _LANES, c), one_hot(_LANES, head_dim)


def _expand_heads(bcasts, head_dim):
    c = bcasts[0].shape[0]
    per = _LANES // head_dim
    lane = lax.broadcasted_iota(jnp.int32, (c, _LANES), 1)
    pieces = []
    for p in range(len(bcasts) // per):
        out = bcasts[p * per]
        for i in range(1, per):
            out = jnp.where(lane >= i * head_dim, bcasts[p * per + i], out)
        pieces.append(out)
    return jnp.concatenate(pieces, axis=1)


def _ssd_scan_group(zs_ref, x_ref, bm_ref, cm_ref, acol_ref, dcol_ref, esc_ref, ehp_ref, arow_ref, drow_ref,
                    dskip_ref, gnw_ref, o_ref, st_ref, head_dim):
    c = zs_ref.shape[0]
    hg = x_ref.shape[1] // head_dim
    xs = x_ref[...].astype(_F32)
    bmx = bm_ref[...]
    cmx = cm_ref[...]

    acol = acol_ref[...]
    a_last = acol[c - 1:c, :]
    w_state = (jnp.exp2(a_last - acol) * dcol_ref[...]).astype(_MXU_DTYPE)
    a_hi = acol.astype(_MXU_DTYPE)
    a_lo = (acol - a_hi.astype(_F32)).astype(_MXU_DTYPE)
    a_sc = jnp.dot(jnp.concatenate([a_hi, a_lo], axis=1), esc_ref[...],
                   preferred_element_type=_F32)
    w_hp = jnp.dot(w_state, ehp_ref[...], preferred_element_type=_F32)

    per = _LANES // head_dim
    width = per * c
    cb = lax.dot_general(cmx, jnp.concatenate([bmx] * per, axis=0), _NT,
                         preferred_element_type=_F32)
    ri = lax.broadcasted_iota(jnp.int32, (c, width), 0)
    ci = lax.broadcasted_iota(jnp.int32, (c, width), 1) & (c - 1)
    causal = ci <= ri
    lane = lax.broadcasted_iota(jnp.int32, (c, _LANES), 1)
    y_pieces = []
    for p in range(hg // per):
        ac = a_sc[:, p * width:(p + 1) * width]
        ar = arow_ref[:, p * width:(p + 1) * width]
        dr = drow_ref[:, p * width:(p + 1) * width]
        decay = jnp.exp2(jnp.where(causal, ac - ar, _NEG_BIG))
        sc = (cb * decay * dr).astype(_MXU_DTYPE)
        xp = x_ref[:, p * _LANES:(p + 1) * _LANES]
        bd = jnp.concatenate(
            [jnp.where((lane >= i * head_dim) & (lane < (i + 1) * head_dim), xp, jnp.zeros_like(xp))
             for i in range(per)], axis=0)
        y_pieces.append(jnp.dot(sc, bd, preferred_element_type=_F32))
    y = jnp.concatenate(y_pieces, axis=1)

    st = st_ref[...]
    ea = jnp.exp2(_expand_heads([a_sc[:, h * c:(h + 1) * c] for h in range(hg)], head_dim))
    y = y + jnp.dot(cmx, st.astype(_MXU_DTYPE), preferred_element_type=_F32) * ea
    xw = (xs * w_hp).astype(_MXU_DTYPE)
    st_ref[...] = st * ea[c - 1:c, :] + lax.dot_general(bmx, xw, _TN, preferred_element_type=_F32)

    y = y + dskip_ref[...] * xs
    y = y * zs_ref[...].astype(_F32)
    ms = jnp.mean(y * y, axis=-1, keepdims=True)
    o_ref[...] = ((y * lax.rsqrt(ms + _RMS_EPS)) * gnw_ref[...]).astype(o_ref.dtype)


def _ssd_scan_kernel(zs_ref, x_ref, bm_ref, cm_ref, acol_ref, dcol_ref, esc_ref, ehp_ref, arow_ref, drow_ref,
                     dskip_ref, gnw_ref, o_ref, st_ref, *, head_dim):
    @pl.when(pl.program_id(2) == 0)
    def _():
        st_ref[...] = jnp.zeros_like(st_ref)

    gb, n, gw = st_ref.shape
    c = zs_ref.shape[0]
    hg = gw // head_dim
    for g in range(gb):
        xs_, ns_, rs_ = pl.ds(g * gw, gw), pl.ds(g * n, n), pl.ds(g * hg * c, hg * c)
        _ssd_scan_group(zs_ref.at[:, xs_], x_ref.at[:, xs_], bm_ref.at[:, ns_], cm_ref.at[:, ns_],
                        acol_ref, dcol_ref, esc_ref.at[g], ehp_ref.at[g],
                        arow_ref.at[0, :, rs_], drow_ref.at[0, :, rs_],
                        dskip_ref.at[:, xs_], gnw_ref.at[:, xs_], o_ref.at[:, xs_], st_ref.at[g], head_dim)


def _ssd_scan(zs, xbc, acol, dcol, arow, drow, dskip, gnorm_w, batch, seq, inner, heads, n_state):
    m = zs.shape[0]
    c = _CHUNK
    g = _SSD_GROUPS
    gb = _SSD_GROUPS_PER_STEP if g % _SSD_GROUPS_PER_STEP == 0 else 1
    hg = heads // g
    p = inner // heads
    gw = hg * p
    sw, sn = gb * gw, gb * n_state
    assert c == _LANES and gw % _LANES == 0 and n_state % _LANES == 0 and _LANES % p == 0
    assert inner % sn == 0 and (g * n_state) % sn == 0
    nt = seq // c
    b_blk = inner // sn
    c_blk = (inner + g * n_state) // sn

    def rows(b, t):
        return b * nt + t

    in_specs = [
        pl.BlockSpec((c, sw), lambda b, gi, t: (rows(b, t), gi)),
        pl.BlockSpec((c, sw), lambda b, gi, t: (rows(b, t), gi)),
        pl.BlockSpec((c, sn), lambda b, gi, t: (rows(b, t), b_blk + gi)),
        pl.BlockSpec((c, sn), lambda b, gi, t: (rows(b, t), c_blk + gi)),
        pl.BlockSpec((c, _LANES), lambda b, gi, t: (rows(b, t), 0)),
        pl.BlockSpec((c, _LANES), lambda b, gi, t: (rows(b, t), 0)),
        pl.BlockSpec((gb, 2 * _LANES, hg * c), lambda b, gi, t: (gi, 0, 0)),
        pl.BlockSpec((gb, _LANES, gw), lambda b, gi, t: (gi, 0, 0)),
        pl.BlockSpec((1, 1, gb * hg * c), lambda b, gi, t: (rows(b, t), 0, gi)),
        pl.BlockSpec((1, 1, gb * hg * c), lambda b, gi, t: (rows(b, t), 0, gi)),
        pl.BlockSpec((1, sw), lambda b, gi, t: (0, gi)),
        pl.BlockSpec((1, sw), lambda b, gi, t: (0, gi)),
    ]
    return pl.pallas_call(
        functools.partial(_ssd_scan_kernel, head_dim=p),
        grid=(batch, g // gb, nt),
        in_specs=in_specs,
        out_specs=pl.BlockSpec((c, sw), lambda b, gi, t: (rows(b, t), gi)),
        out_shape=jax.ShapeDtypeStruct((m, inner), _MXU_DTYPE),
        scratch_shapes=[pltpu.VMEM((gb, n_state, gw), _F32)],
        compiler_params=_params(("parallel", "parallel", "arbitrary")),
        name="ssd_scan",
    )(zs, xbc, xbc, xbc, acol, dcol, *_head_expanders(g, hg, c, p), arow, drow, dskip, gnorm_w)


def _ssd_mixer(h, norm_w, batch, seq, layer, wt_in, conv_w, conv_b, dt_bias, a_log, d_skip, gnorm_w, w_out):
    heads = dt_bias.shape[0]
    conv_dim = conv_w.shape[1]
    inner = wt_in.shape[1] - conv_dim - heads
    n_state = (conv_dim - inner) // (2 * _SSD_GROUPS)
    n_main = inner + conv_dim
    m = h.shape[0]
    pad1 = lambda v: jnp.pad(v.astype(_F32), (0, _LANES - heads)).reshape(1, _LANES)
    zs, u = _matmul(h, wt_in, layer, wt=True, n=inner, norm_w=norm_w, emit_norm=True, act="silu",
                    out_dtype=_MXU_DTYPE)
    xbc = _proj_conv(u, wt_in, layer, inner, conv_w.astype(_F32), conv_b.reshape(1, conv_dim).astype(_F32),
                     seq, _MXU_DTYPE)
    acol, dcol, arow, drow = _ssd_gate(u, wt_in, layer, n_main, pad1(dt_bias), pad1(a_log), heads)
    arow = arow.reshape(m // _CHUNK, 1, heads * _CHUNK)
    drow = drow.reshape(m // _CHUNK, 1, heads * _CHUNK)
    dskip = jnp.repeat(d_skip.astype(_F32), inner // heads).reshape(1, inner)
    y = _ssd_scan(zs, xbc, acol, dcol, arow, drow, dskip, gnorm_w.reshape(1, inner).astype(_F32),
                  batch, seq, inner, heads, n_state)
    return _matmul_residual(y, w_out.astype(_MXU_DTYPE), layer, h)


def kernel(x, mixer_norm_w, gla_w_in, gla_w_gk_up, gla_b_gk_up, gla_o_norm_w, gla_w_out, ssd_w_in, ssd_conv_w, ssd_conv_b, ssd_dt_bias, ssd_a_log, ssd_d_skip, ssd_gnorm_w, ssd_w_out, mlp_norm_w, mlp_w_fc1, mlp_w_fc2, final_norm_w):
    batch, seq, d = x.shape
    h = x.reshape(batch * seq, d)
    w_fc1 = mlp_w_fc1.astype(_MXU_DTYPE)
    w_fc2 = mlp_w_fc2.astype(_MXU_DTYPE)
    gla_wt_in = jnp.swapaxes(gla_w_in, 1, 2).astype(_MXU_DTYPE)
    ssd_wt_in = jnp.swapaxes(ssd_w_in, 1, 2).astype(_MXU_DTYPE)
    for i in range(mixer_norm_w.shape[0]):
        j = i // 2
        if i % 2 == 0:
            h = _gla_mixer(h, mixer_norm_w[i], batch, seq, j, gla_wt_in, gla_w_gk_up[j], gla_b_gk_up[j],
                           gla_o_norm_w[j], gla_w_out)
        else:
            h = _ssd_mixer(h, mixer_norm_w[i], batch, seq, j, ssd_wt_in, ssd_conv_w[j], ssd_conv_b[j],
                           ssd_dt_bias[j], ssd_a_log[j], ssd_d_skip[j], ssd_gnorm_w[j], ssd_w_out)
        hidden = _matmul(h, w_fc1, i, norm_w=mlp_norm_w[i], act="relu2", out_dtype=_MXU_DTYPE)
        h = _matmul_residual(hidden, w_fc2, i, h)
    return _rmsnorm(h, final_norm_w, _F32).reshape(batch, seq, d)
```

```python
import functools
import math

import jax
import jax.numpy as jnp
from jax import lax
from jax.experimental import pallas as pl
from jax.experimental.pallas import tpu as pltpu

_F32 = jnp.float32
_MXU_DTYPE = jnp.bfloat16
_RMS_EPS = 1e-5
_GLA_GATE_NORMALIZER = 16.0
_SSD_GROUPS = 8
_CHUNK = 128
_LANES = 128
_GLA_DIAG = 8
_GLA_HEADS_PER_STEP = 4
_SSD_GROUPS_PER_STEP = 8
_LOG2E = math.log2(math.e)
_NEG_BIG = -1e30
_HALO = 16
_MXU_WIDTH = 256
_VMEM_LIMIT_BYTES = 60 * 1024 * 1024
_MATMUL_VMEM_BUDGET = 58 * 1024 * 1024

_NN = (((1,), (0,)), ((), ()))
_NT = (((1,), (1,)), ((), ()))
_TN = (((0,), (0,)), ((), ()))


def _params(sem, flags=None):
    return pltpu.CompilerParams(dimension_semantics=sem, vmem_limit_bytes=_VMEM_LIMIT_BYTES, flags=flags)


def _rms(x, w):
    ms = jnp.mean(x * x, axis=-1, keepdims=True)
    return (x * lax.rsqrt(ms + _RMS_EPS)) * w


def _rmsnorm_kernel(x_ref, w_ref, o_ref):
    o_ref[...] = _rms(x_ref[...].astype(_F32), w_ref[...]).astype(o_ref.dtype)


def _rmsnorm(x, w, out_dtype, rows=512):
    m, d = x.shape
    rows = min(rows, m)
    return pl.pallas_call(
        _rmsnorm_kernel,
        grid=(m // rows,),
        in_specs=[pl.BlockSpec((rows, d), lambda i: (i, 0)),
                  pl.BlockSpec((1, d), lambda i: (0, 0))],
        out_specs=pl.BlockSpec((rows, d), lambda i: (i, 0)),
        out_shape=jax.ShapeDtypeStruct((m, d), out_dtype),
        compiler_params=_params(("parallel",)),
        name="rmsnorm",
    )(x, w.reshape(1, d).astype(_F32))


def _silu(x):
    half = 0.5 * x
    return half + half * jnp.tanh(half)


def _matmul_kernel(*refs, act, act_tile0, has_norm, has_res, wt):
    it = iter(refs)
    a_ref = next(it)
    nw_ref = next(it) if has_norm else None
    w_ref = next(it)
    r_ref = next(it) if has_res else None
    o_ref = next(it)
    if has_norm:
        an_ref = next(it)

        @pl.when(pl.program_id(1) == 0)
        def _():
            an_ref[...] = _rms(a_ref[...], nw_ref[...]).astype(_MXU_DTYPE)

        a = an_ref[...]
    else:
        a = a_ref[...]
    w = w_ref[...].astype(_MXU_DTYPE)
    acc = lax.dot_general(a, w, _NT if wt else _NN, preferred_element_type=_F32)
    if act == "relu2":
        act_acc = jnp.square(jnp.maximum(acc, 0.0))
    elif act == "silu":
        act_acc = _silu(acc)
    if act is not None:
        acc = act_acc if act_tile0 == 0 else jnp.where(pl.program_id(1) >= act_tile0, act_acc, acc)
    if has_res:
        acc = r_ref[...] + acc
    o_ref[...] = acc.astype(o_ref.dtype)


def _divisor_tile(n, pref, align):
    if n <= pref:
        return n
    t = (pref // align) * align
    while n % t:
        t -= align
    return t


def _matmul_tiles(m, k, n, a_bytes, w_bytes, out_bytes, norm_copies, has_res, conv_taps=0, row_cap=None):
    mxu_bytes = jnp.dtype(_MXU_DTYPE).itemsize
    for pm, pn in ((1024, 2048), (1024, 1024), (1024, 512), (1024, 256), (512, 256), (256, 256), (128, 128)):
        pm = pm if row_cap is None else min(pm, row_cap)
        tm, tn = _divisor_tile(m, pm, 8), _divisor_tile(n, pn, _LANES)
        need = 2 * tm * k * a_bytes + 2 * k * tn * w_bytes + 2 * tm * tn * out_bytes
        need += (2 + conv_taps) * tm * _MXU_WIDTH * 4
        need += 2 * tm * tn * 4 if has_res else 0
        need += norm_copies * tm * k * mxu_bytes
        need += k * tn * mxu_bytes if w_bytes != mxu_bytes else 0
        if need <= _MATMUL_VMEM_BUDGET:
            break
    return tm, tn


def _matmul(a, w, layer, *, wt=False, n=None, norm_w=None, emit_norm=False, act=None, act_col0=0, res=None,
            out_dtype):
    m, k = a.shape
    n = w.shape[1 if wt else 2] if n is None else n
    tm, tn = _matmul_tiles(m, k, math.gcd(n, act_col0), a.dtype.itemsize, w.dtype.itemsize,
                           jnp.dtype(out_dtype).itemsize, (norm_w is not None) + emit_norm, res is not None)
    in_specs = [pl.BlockSpec((tm, k), lambda i, j: (i, 0))]
    args = [a]
    if norm_w is not None:
        in_specs.append(pl.BlockSpec((1, k), lambda i, j: (0, 0)))
        args.append(norm_w.reshape(1, k).astype(_F32))
    in_specs.append(pl.BlockSpec((None, tn, k), lambda i, j: (layer, j, 0)) if wt else
                    pl.BlockSpec((None, k, tn), lambda i, j: (layer, 0, j)))
    args.append(w)
    if res is not None:
        in_specs.append(pl.BlockSpec((tm, tn), lambda i, j: (i, j)))
        args.append(res)
    out_specs = [pl.BlockSpec((tm, tn), lambda i, j: (i, j))]
    out_shape = [jax.ShapeDtypeStruct((m, n), out_dtype)]
    scratch = []
    if emit_norm:
        out_specs.append(pl.BlockSpec((tm, k), lambda i, j: (i, 0)))
        out_shape.append(jax.ShapeDtypeStruct((m, k), _MXU_DTYPE))
    elif norm_w is not None:
        scratch.append(pltpu.VMEM((tm, k), _MXU_DTYPE))
    outs = pl.pallas_call(
        functools.partial(_matmul_kernel, act=act, act_tile0=act_col0 // tn, has_norm=norm_w is not None,
                          has_res=res is not None, wt=wt),
        grid=(m // tm, n // tn),
        in_specs=in_specs,
        out_specs=out_specs,
        out_shape=out_shape,
        scratch_shapes=scratch,
        compiler_params=_params(("parallel", "arbitrary")),
        name="matmul",
    )(*args)
    return outs if emit_norm else outs[0]


def _matmul_splitk_kernel(a_ref, w_ref, r_ref, o_ref):
    @pl.when(pl.program_id(2) == 0)
    def _():
        o_ref[...] = r_ref[...]

    o_ref[...] += jnp.dot(a_ref[...], w_ref[...].astype(_MXU_DTYPE), preferred_element_type=_F32)


def _matmul_residual(a, w, layer, res):
    m, k = a.shape
    n = w.shape[2]
    for nk in (1, 2, 4):
        tm, tn = _matmul_tiles(m, k // nk, n, a.dtype.itemsize, w.dtype.itemsize, 4, 0, True)
        if tm * tn >= min(m, 1024) * min(n, 1024):
            break
    if nk == 1:
        return _matmul(a, w, layer, res=res, out_dtype=_F32)
    tk = k // nk
    return pl.pallas_call(
        _matmul_splitk_kernel,
        grid=(m // tm, n // tn, nk),
        in_specs=[pl.BlockSpec((tm, tk), lambda i, j, kk: (i, kk)),
                  pl.BlockSpec((None, tk, tn), lambda i, j, kk: (layer, kk, j)),
                  pl.BlockSpec((tm, tn), lambda i, j, kk: (i, j))],
        out_specs=pl.BlockSpec((tm, tn), lambda i, j, kk: (i, j)),
        out_shape=jax.ShapeDtypeStruct((m, n), _F32),
        compiler_params=_params(("parallel", "parallel", "arbitrary")),
        name="matmul_splitk",
    )(a, w, res)


def _proj_conv_kernel(u_ref, halo_ref, w_ref, cw_ref, cb_ref, o_ref, an_ref, *, tiles_per_seq):
    kw = cw_ref.shape[0]

    @pl.when(pl.program_id(1) == 0)
    def _():
        halo = halo_ref[...]
        an_ref[0:_HALO, :] = jnp.where(pl.program_id(0) % tiles_per_seq == 0, jnp.zeros_like(halo), halo)
        an_ref[_HALO:, :] = u_ref[...]

    wb = w_ref[...].astype(_MXU_DTYPE)
    for c0 in range(0, o_ref.shape[1], _MXU_WIDTH):
        cols = pl.ds(c0, _MXU_WIDTH)
        acc = lax.dot_general(an_ref[...], wb[c0:c0 + _MXU_WIDTH, :], _NT, preferred_element_type=_F32)
        tm = o_ref.shape[0]
        out = cb_ref[:, cols] + cw_ref[kw - 1:kw, cols] * acc[_HALO:, :]
        for s in range(1, kw):
            out = out + cw_ref[kw - 1 - s:kw - s, cols] * acc[_HALO - s:_HALO - s + tm, :]
        o_ref[:, cols] = _silu(out).astype(o_ref.dtype)


def _proj_conv(u, w, layer, col0, conv_w, conv_b, seq, out_dtype):
    m, k = u.shape
    kw, n = conv_w.shape
    tm, tn = _matmul_tiles(m, k, math.gcd(n, col0), u.dtype.itemsize, w.dtype.itemsize,
                           jnp.dtype(out_dtype).itemsize, 1, False, conv_taps=kw, row_cap=seq)
    assert seq % tm == 0 and tm % _HALO == 0 and tn % _MXU_WIDTH == 0 and kw - 1 <= _HALO
    return pl.pallas_call(
        functools.partial(_proj_conv_kernel, tiles_per_seq=seq // tm),
        grid=(m // tm, n // tn),
        in_specs=[pl.BlockSpec((tm, k), lambda i, j: (i, 0)),
                  pl.BlockSpec((_HALO, k), lambda i, j: (jnp.maximum(i * (tm // _HALO) - 1, 0), 0)),
                  pl.BlockSpec((None, tn, k), lambda i, j: (layer, col0 // tn + j, 0)),
                  pl.BlockSpec((kw, tn), lambda i, j: (0, j)),
                  pl.BlockSpec((1, tn), lambda i, j: (0, j))],
        out_specs=pl.BlockSpec((tm, tn), lambda i, j: (i, j)),
        out_shape=jax.ShapeDtypeStruct((m, n), out_dtype),
        scratch_shapes=[pltpu.VMEM((tm + _HALO, k), _MXU_DTYPE)],
        compiler_params=_params(("parallel", "arbitrary")),
        name="proj_conv",
    )(u, u, w, conv_w, conv_b)


def _softplus(x):
    return jnp.maximum(x, 0.0) + jnp.log(1.0 + jnp.exp(-jnp.abs(x)))


def _chunk_cumsum(x, chunk):
    rows = x.shape[0]
    r = lax.broadcasted_iota(jnp.int32, (chunk, chunk), 0)
    c = lax.broadcasted_iota(jnp.int32, (chunk, chunk), 1)
    tri = jnp.where(c <= r, 1.0, 0.0).astype(_MXU_DTYPE)
    out = []
    for i in range(rows // chunk):
        rem = x[i * chunk:(i + 1) * chunk, :]
        acc = None
        for _ in range(3):
            piece = rem.astype(_MXU_DTYPE)
            d = jnp.dot(tri, piece, preferred_element_type=_F32)
            acc = d if acc is None else acc + d
            rem = rem - piece.astype(_F32)
        out.append(acc)
    return jnp.concatenate(out, axis=0) if len(out) > 1 else out[0]


def _row_bcast(ref, row, n):
    return jnp.broadcast_to(ref[pl.ds(row, 1), :], (n, ref.shape[1]))


def _tail_rows(w_ref, n_valid):
    row = lax.broadcasted_iota(jnp.int32, w_ref.shape, 0)
    return jnp.where(row < n_valid, w_ref[...], 0.0).astype(_MXU_DTYPE)


def _gla_gate_kernel(u_ref, wgr_ref, wup_ref, b_ref, o_ref, *, chunk, rank):
    gr = lax.dot_general(u_ref[...], _tail_rows(wgr_ref, rank), _NT, preferred_element_type=_F32)
    pre = jnp.dot(gr.astype(_MXU_DTYPE), wup_ref[...], preferred_element_type=_F32) + b_ref[...]
    o_ref[...] = _chunk_cumsum(_softplus(-pre) * (-_LOG2E / _GLA_GATE_NORMALIZER), chunk)


def _gla_gate(u, wt_in, layer, col0, rank, w_up, b_up, rows=1024):
    m, d = u.shape
    kd = w_up.shape[1]
    rows = min(rows, m)
    assert col0 % _LANES == 0 and col0 + rank == wt_in.shape[1] and rank <= _LANES
    return pl.pallas_call(
        functools.partial(_gla_gate_kernel, chunk=_CHUNK, rank=rank),
        grid=(m // rows,),
        in_specs=[pl.BlockSpec((rows, d), lambda i: (i, 0)),
                  pl.BlockSpec((None, _LANES, d), lambda i: (layer, col0 // _LANES, 0)),
                  pl.BlockSpec(w_up.shape, lambda i: (0, 0)),
                  pl.BlockSpec((1, kd), lambda i: (0, 0))],
        out_specs=pl.BlockSpec((rows, kd), lambda i: (i, 0)),
        out_shape=jax.ShapeDtypeStruct((m, kd), _F32),
        compiler_params=_params(("parallel",)),
        name="gla_gate",
    )(u, wt_in, w_up, b_up)


def _gla_levels(c, nd):
    i = lax.broadcasted_iota(jnp.int32, (c, c), 0)
    j = lax.broadcasted_iota(jnp.int32, (c, c), 1)
    x = i ^ j
    lvl = jnp.zeros((c, c), jnp.int32)
    s, level = nd, 1
    while s < c:
        lvl = jnp.where(x >= s, level, lvl)
        s, level = 2 * s, level + 1
    return jnp.where(j > i, -1, lvl)


def _gla_scan_head(q_ref, k_ref, v_ref, gs_ref, b_ref, onw_ref, lvl_ref, o_ref, st_ref, qf_ref, kf_ref, scale):
    c, dk = q_ref.shape
    qf_ref[...] = q_ref[...].astype(_F32) * scale
    kf_ref[...] = k_ref[...].astype(_F32)
    q = qf_ref[...]
    k = kf_ref[...]
    v = v_ref[...]
    bc = b_ref[...]
    st = st_ref[...]
    b_last = jnp.concatenate([_row_bcast(b_ref, c - 1, 8)] * (c // 8), axis=0)

    o = lax.dot_general((q * jnp.exp2(bc)).astype(_MXU_DTYPE), st.astype(_MXU_DTYPE), _NT,
                        preferred_element_type=_F32)
    k_dec = (k * jnp.exp2(b_last - bc)).astype(_MXU_DTYPE)

    nd = _GLA_DIAG
    lane = lax.broadcasted_iota(jnp.int32, (nd, c), 1)
    pieces = []
    for m in range(c // nd):
        qb = qf_ref[m * nd:(m + 1) * nd, :]
        bb = b_ref[m * nd:(m + 1) * nd, :]
        acc = jnp.zeros((nd, c), _F32)
        for j in range(nd):
            kj = _row_bcast(kf_ref, m * nd + j, nd)
            bj = _row_bcast(b_ref, m * nd + j, nd)
            term = qb * kj * jnp.exp2(jnp.minimum(bb - bj, 0.0))
            acc = jnp.where(lane == m * nd + j, jnp.sum(term, axis=1, keepdims=True), acc)
        pieces.append(acc)
    lvl = lvl_ref[...]
    att = jnp.where(lvl == 0, jnp.concatenate(pieces, axis=0), 0.0)

    s, level = nd, 1
    while s < c:
        prev = [jnp.zeros((s, dk), _F32)]
        end = []
        for m in range(c // s):
            if m:
                prev += [_row_bcast(b_ref, m * s - 1, 8)] * (s // 8)
            end += [_row_bcast(b_ref, (m + 1) * s - 1, 8)] * (s // 8)
        qs = (q * jnp.exp2(bc - jnp.concatenate(prev, axis=0))).astype(_MXU_DTYPE)
        ks = (k * jnp.exp2(jnp.concatenate(end, axis=0) - bc)).astype(_MXU_DTYPE)
        att = jnp.where(lvl == level, lax.dot_general(qs, ks, _NT, preferred_element_type=_F32), att)
        s, level = 2 * s, level + 1

    o = o + jnp.dot(att.astype(_MXU_DTYPE), v, preferred_element_type=_F32)
    st_ref[...] = st * jnp.exp2(b_last[:1, :]) + lax.dot_general(
        v, k_dec, _TN, preferred_element_type=_F32)

    ms = jnp.mean(o * o, axis=-1, keepdims=True)
    on = (o * lax.rsqrt(ms + _RMS_EPS)) * onw_ref[...]
    o_ref[...] = (on * gs_ref[...].astype(_F32)).astype(o_ref.dtype)


def _gla_scan_kernel(q_ref, k_ref, v_ref, gs_ref, b_ref, onw_ref, lvl_ref, o_ref, st_ref, qf_ref, kf_ref,
                     *, scale, dk, dv):
    @pl.when(pl.program_id(2) == 0)
    def _():
        st_ref[...] = jnp.zeros_like(st_ref)

    for h in range(st_ref.shape[0]):
        ks, vs = pl.ds(h * dk, dk), pl.ds(h * dv, dv)
        _gla_scan_head(q_ref.at[:, ks], k_ref.at[:, ks], v_ref.at[:, vs], gs_ref.at[:, vs], b_ref.at[:, ks],
                       onw_ref, lvl_ref, o_ref.at[:, vs], st_ref.at[h], qf_ref.at[h], kf_ref.at[h], scale)


def _gla_scan(proj, bcum, o_norm_w, batch, seq, heads, dk, dv):
    m = proj.shape[0]
    c = _CHUNK
    hb = _GLA_HEADS_PER_STEP if heads % _GLA_HEADS_PER_STEP == 0 else 1
    assert c == _LANES and dk % _LANES == 0 and dv % _LANES == 0 and (2 * heads * dk) % (hb * dv) == 0
    nt = seq // c
    kd, vd = heads * dk, heads * dv
    wk, wv = hb * dk, hb * dv
    k_blk, v_blk, g_blk = kd // wk, (2 * kd) // wv, (2 * kd + vd) // wv

    def rows(b, h, t):
        return b * nt + t

    return pl.pallas_call(
        functools.partial(_gla_scan_kernel, scale=dk ** -0.5, dk=dk, dv=dv),
        grid=(batch, heads // hb, nt),
        in_specs=[pl.BlockSpec((c, wk), lambda b, h, t: (rows(b, h, t), h)),
                  pl.BlockSpec((c, wk), lambda b, h, t: (rows(b, h, t), k_blk + h)),
                  pl.BlockSpec((c, wv), lambda b, h, t: (rows(b, h, t), v_blk + h)),
                  pl.BlockSpec((c, wv), lambda b, h, t: (rows(b, h, t), g_blk + h)),
                  pl.BlockSpec((c, wk), lambda b, h, t: (rows(b, h, t), h)),
                  pl.BlockSpec((1, dv), lambda b, h, t: (0, 0)),
                  pl.BlockSpec((c, c), lambda b, h, t: (0, 0))],
        out_specs=pl.BlockSpec((c, wv), lambda b, h, t: (rows(b, h, t), h)),
        out_shape=jax.ShapeDtypeStruct((m, vd), _MXU_DTYPE),
        scratch_shapes=[pltpu.VMEM((hb, dv, dk), _F32),
                        pltpu.VMEM((hb, c, dk), _F32),
                        pltpu.VMEM((hb, c, dk), _F32)],
        compiler_params=_params(("parallel", "parallel", "arbitrary")),
        name="gla_scan",
    )(proj, proj, proj, proj, bcum, o_norm_w.reshape(1, dv).astype(_F32), _gla_levels(c, _GLA_DIAG))


def _gla_mixer(h, norm_w, batch, seq, layer, wt_in, w_gk_up, b_gk_up, o_norm_w, w_out):
    rank, kd = w_gk_up.shape
    vd = (wt_in.shape[1] - rank - 2 * kd) // 2
    dv = o_norm_w.shape[0]
    heads = vd // dv
    n_main = 2 * kd + 2 * vd
    w_up = jnp.pad(w_gk_up, ((0, _LANES - rank), (0, 0))).astype(_MXU_DTYPE)
    proj, u = _matmul(h, wt_in, layer, wt=True, n=n_main, norm_w=norm_w, emit_norm=True, act="silu",
                      act_col0=2 * kd + vd, out_dtype=_MXU_DTYPE)
    bcum = _gla_gate(u, wt_in, layer, n_main, rank, w_up, b_gk_up.reshape(1, kd).astype(_F32))
    o = _gla_scan(proj, bcum, o_norm_w, batch, seq, heads, kd // heads, dv)
    return _matmul_residual(o, w_out.astype(_MXU_DTYPE), layer, h)


def _ssd_gate_kernel(u_ref, wdt_ref, bias_ref, alog_ref, acol_ref, dcol_ref, arow_ref, drow_ref,
                     *, chunk, heads):
    raw = lax.dot_general(u_ref[...], _tail_rows(wdt_ref, heads), _NT, preferred_element_type=_F32)
    dt = _softplus(raw + bias_ref[...])
    a = _chunk_cumsum(dt * (-jnp.exp(alog_ref[...])) * _LOG2E, chunk)
    acol_ref[...] = a
    dcol_ref[...] = dt
    for i in range(a.shape[0] // chunk):
        arow_ref[i] = a[i * chunk:(i + 1) * chunk, :].T[:heads, :]
        drow_ref[i] = dt[i * chunk:(i + 1) * chunk, :].T[:heads, :]


def _ssd_gate(u, wt_in, layer, col0, dt_bias, a_log, heads, rows=1024):
    m, d = u.shape
    c = _CHUNK
    rows = min(rows, m)
    g = _SSD_GROUPS
    hg = heads // g
    assert col0 % _LANES == 0 and col0 + heads == wt_in.shape[1] and heads <= _LANES
    col = jax.ShapeDtypeStruct((m, _LANES), _F32)
    row = jax.ShapeDtypeStruct((m // c, heads, c), _F32)
    col_spec = pl.BlockSpec((rows, _LANES), lambda i: (i, 0))
    row_spec = pl.BlockSpec((rows // c, heads, c), lambda i: (i, 0, 0))
    return pl.pallas_call(
        functools.partial(_ssd_gate_kernel, chunk=c, heads=heads),
        grid=(m // rows,),
        in_specs=[pl.BlockSpec((rows, d), lambda i: (i, 0)),
                  pl.BlockSpec((None, _LANES, d), lambda i: (layer, col0 // _LANES, 0)),
                  pl.BlockSpec((1, _LANES), lambda i: (0, 0)),
                  pl.BlockSpec((1, _LANES), lambda i: (0, 0))],
        out_specs=[col_spec, col_spec, row_spec, row_spec],
        out_shape=[col, col, row, row],
        compiler_params=_params(("parallel",)),
        name="ssd_gate",
    )(u, wt_in, dt_bias, a_log)


def _head_expanders(groups, hg, c, head_dim):
    def one_hot(rows, width):
        row = lax.broadcasted_iota(jnp.int32, (groups, rows, hg * width), 1) % _LANES
        col = lax.broadcasted_iota(jnp.int32, (groups, rows, hg * width), 2)
        grp = lax.broadcasted_iota(jnp.int32, (groups, rows, hg * width), 0)
        return (row == grp * hg + col // width).astype(_MXU_DTYPE)
    return one_hot(2 * _LANES, c), one_hot(_LANES, head_dim)


def _expand_heads(bcasts, head_dim):
    c = bcasts[0].shape[0]
    per = _LANES // head_dim
    lane = lax.broadcasted_iota(jnp.int32, (c, _LANES), 1)
    pieces = []
    for p in range(len(bcasts) // per):
        out = bcasts[p * per]
        for i in range(1, per):
            out = jnp.where(lane >= i * head_dim, bcasts[p * per + i], out)
        pieces.append(out)
    return jnp.concatenate(pieces, axis=1)


def _ssd_scan_group(zs_ref, x_ref, bm_ref, cm_ref, acol_ref, dcol_ref, esc_ref, ehp_ref, arow_ref, drow_ref,
                    dskip_ref, gnw_ref, o_ref, st_ref, head_dim):
    c = zs_ref.shape[0]
    hg = x_ref.shape[1] // head_dim
    xs = x_ref[...].astype(_F32)
    bmx = bm_ref[...]
    cmx = cm_ref[...]

    acol = acol_ref[...]
    a_last = acol[c - 1:c, :]
    w_state = (jnp.exp2(a_last - acol) * dcol_ref[...]).astype(_MXU_DTYPE)
    a_hi = acol.astype(_MXU_DTYPE)
    a_lo = (acol - a_hi.astype(_F32)).astype(_MXU_DTYPE)
    a_sc = jnp.dot(jnp.concatenate([a_hi, a_lo], axis=1), esc_ref[...],
                   preferred_element_type=_F32)
    w_hp = jnp.dot(w_state, ehp_ref[...], preferred_element_type=_F32)

    per = _LANES // head_dim
    width = per * c
    cb = lax.dot_general(cmx, jnp.concatenate([bmx] * per, axis=0), _NT,
                         preferred_element_type=_F32)
    ri = lax.broadcasted_iota(jnp.int32, (c, width), 0)
    ci = lax.broadcasted_iota(jnp.int32, (c, width), 1) & (c - 1)
    causal = ci <= ri
    lane = lax.broadcasted_iota(jnp.int32, (c, _LANES), 1)
    y_pieces = []
    for p in range(hg // per):
        ac = a_sc[:, p * width:(p + 1) * width]
        ar = arow_ref[:, p * width:(p + 1) * width]
        dr = drow_ref[:, p * width:(p + 1) * width]
        decay = jnp.exp2(jnp.where(causal, ac - ar, _NEG_BIG))
        sc = (cb * decay * dr).astype(_MXU_DTYPE)
        xp = x_ref[:, p * _LANES:(p + 1) * _LANES]
        bd = jnp.concatenate(
            [jnp.where((lane >= i * head_dim) & (lane < (i + 1) * head_dim), xp, jnp.zeros_like(xp))
             for i in range(per)], axis=0)
        y_pieces.append(jnp.dot(sc, bd, preferred_element_type=_F32))
    y = jnp.concatenate(y_pieces, axis=1)

    st = st_ref[...]
    ea = jnp.exp2(_expand_heads([a_sc[:, h * c:(h + 1) * c] for h in range(hg)], head_dim))
    y = y + jnp.dot(cmx, st.astype(_MXU_DTYPE), preferred_element_type=_F32) * ea
    xw = (xs * w_hp).astype(_MXU_DTYPE)
    st_ref[...] = st * ea[c - 1:c, :] + lax.dot_general(bmx, xw, _TN, preferred_element_type=_F32)

    y = y + dskip_ref[...] * xs
    y = y * zs_ref[...].astype(_F32)
    ms = jnp.mean(y * y, axis=-1, keepdims=True)
    o_ref[...] = ((y * lax.rsqrt(ms + _RMS_EPS)) * gnw_ref[...]).astype(o_ref.dtype)


def _ssd_scan_kernel(zs_ref, x_ref, bm_ref, cm_ref, acol_ref, dcol_ref, esc_ref, ehp_ref, arow_ref, drow_ref,
                     dskip_ref, gnw_ref, o_ref, st_ref, *, head_dim):
    @pl.when(pl.program_id(2) == 0)
    def _():
        st_ref[...] = jnp.zeros_like(st_ref)

    gb, n, gw = st_ref.shape
    c = zs_ref.shape[0]
    hg = gw // head_dim
    for g in range(gb):
        xs_, ns_, rs_ = pl.ds(g * gw, gw), pl.ds(g * n, n), pl.ds(g * hg * c, hg * c)
        _ssd_scan_group(zs_ref.at[:, xs_], x_ref.at[:, xs_], bm_ref.at[:, ns_], cm_ref.at[:, ns_],
                        acol_ref, dcol_ref, esc_ref.at[g], ehp_ref.at[g],
                        arow_ref.at[0, :, rs_], drow_ref.at[0, :, rs_],
                        dskip_ref.at[:, xs_], gnw_ref.at[:, xs_], o_ref.at[:, xs_], st_ref.at[g], head_dim)


def _ssd_scan(zs, xbc, acol, dcol, arow, drow, dskip, gnorm_w, batch, seq, inner, heads, n_state):
    m = zs.shape[0]
    c = _CHUNK
    g = _SSD_GROUPS
    gb = _SSD_GROUPS_PER_STEP if g % _SSD_GROUPS_PER_STEP == 0 else 1
    hg = heads // g
    p = inner // heads
    gw = hg * p
    sw, sn = gb * gw, gb * n_state
    assert c == _LANES and gw % _LANES == 0 and n_state % _LANES == 0 and _LANES % p == 0
    assert inner % sn == 0 and (g * n_state) % sn == 0
    nt = seq // c
    b_blk = inner // sn
    c_blk = (inner + g * n_state) // sn

    def rows(b, t):
        return b * nt + t

    in_specs = [
        pl.BlockSpec((c, sw), lambda b, gi, t: (rows(b, t), gi)),
        pl.BlockSpec((c, sw), lambda b, gi, t: (rows(b, t), gi)),
        pl.BlockSpec((c, sn), lambda b, gi, t: (rows(b, t), b_blk + gi)),
        pl.BlockSpec((c, sn), lambda b, gi, t: (rows(b, t), c_blk + gi)),
        pl.BlockSpec((c, _LANES), lambda b, gi, t: (rows(b, t), 0)),
        pl.BlockSpec((c, _LANES), lambda b, gi, t: (rows(b, t), 0)),
        pl.BlockSpec((gb, 2 * _LANES, hg * c), lambda b, gi, t: (gi, 0, 0)),
        pl.BlockSpec((gb, _LANES, gw), lambda b, gi, t: (gi, 0, 0)),
        pl.BlockSpec((1, 1, gb * hg * c), lambda b, gi, t: (rows(b, t), 0, gi)),
        pl.BlockSpec((1, 1, gb * hg * c), lambda b, gi, t: (rows(b, t), 0, gi)),
        pl.BlockSpec((1, sw), lambda b, gi, t: (0, gi)),
        pl.BlockSpec((1, sw), lambda b, gi, t: (0, gi)),
    ]
    return pl.pallas_call(
        functools.partial(_ssd_scan_kernel, head_dim=p),
        grid=(batch, g // gb, nt),
        in_specs=in_specs,
        out_specs=pl.BlockSpec((c, sw), lambda b, gi, t: (rows(b, t), gi)),
        out_shape=jax.ShapeDtypeStruct((m, inner), _MXU_DTYPE),
        scratch_shapes=[pltpu.VMEM((gb, n_state, gw), _F32)],
        compiler_params=_params(("parallel", "parallel", "arbitrary")),
        name="ssd_scan",
    )(zs, xbc, xbc, xbc, acol, dcol, *_head_expanders(g, hg, c, p), arow, drow, dskip, gnorm_w)


def _ssd_mixer(h, norm_w, batch, seq, layer, wt_in, conv_w, conv_b, dt_bias, a_log, d_skip, gnorm_w, w_out):
    heads = dt_bias.shape[0]
    conv_dim = conv_w.shape[1]
    inner = wt_in.shape[1] - conv_dim - heads
    n_state = (conv_dim - inner) // (2 * _SSD_GROUPS)
    n_main = inner + conv_dim
    m = h.shape[0]
    pad1 = lambda v: jnp.pad(v.astype(_F32), (0, _LANES - heads)).reshape(1, _LANES)
    zs, u = _matmul(h, wt_in, layer, wt=True, n=inner, norm_w=norm_w, emit_norm=True, act="silu",
                    out_dtype=_MXU_DTYPE)
    xbc = _proj_conv(u, wt_in, layer, inner, conv_w.astype(_F32), conv_b.reshape(1, conv_dim).astype(_F32),
                     seq, _MXU_DTYPE)
    acol, dcol, arow, drow = _ssd_gate(u, wt_in, layer, n_main, pad1(dt_bias), pad1(a_log), heads)
    arow = arow.reshape(m // _CHUNK, 1, heads * _CHUNK)
    drow = drow.reshape(m // _CHUNK, 1, heads * _CHUNK)
    dskip = jnp.repeat(d_skip.astype(_F32), inner // heads).reshape(1, inner)
    y = _ssd_scan(zs, xbc, acol, dcol, arow, drow, dskip, gnorm_w.reshape(1, inner).astype(_F32),
                  batch, seq, inner, heads, n_state)
    return _matmul_residual(y, w_out.astype(_MXU_DTYPE), layer, h)


def kernel(x, mixer_norm_w, gla_w_in, gla_w_gk_up, gla_b_gk_up, gla_o_norm_w, gla_w_out, ssd_w_in, ssd_conv_w, ssd_conv_b, ssd_dt_bias, ssd_a_log, ssd_d_skip, ssd_gnorm_w, ssd_w_out, mlp_norm_w, mlp_w_fc1, mlp_w_fc2, final_norm_w):
    batch, seq, d = x.shape
    h = x.reshape(batch * seq, d)
    w_fc1 = mlp_w_fc1.astype(_MXU_DTYPE)
    w_fc2 = mlp_w_fc2.astype(_MXU_DTYPE)
    gla_wt_in = jnp.swapaxes(gla_w_in, 1, 2).astype(_MXU_DTYPE)
    ssd_wt_in = jnp.swapaxes(ssd_w_in, 1, 2).astype(_MXU_DTYPE)
    for i in range(mixer_norm_w.shape[0]):
        j = i // 2
        if i % 2 == 0:
            h = _gla_mixer(h, mixer_norm_w[i], batch, seq, j, gla_wt_in, gla_w_gk_up[j], gla_b_gk_up[j],
                           gla_o_norm_w[j], gla_w_out)
        else:
            h = _ssd_mixer(h, mixer_norm_w[i], batch, seq, j, ssd_wt_in, ssd_conv_w[j], ssd_conv_b[j],
                           ssd_dt_bias[j], ssd_a_log[j], ssd_d_skip[j], ssd_gnorm_w[j], ssd_w_out)
        hidden = _matmul(h, w_fc1, i, norm_w=mlp_norm_w[i], act="relu2", out_dtype=_MXU_DTYPE)
        h = _matmul_residual(hidden, w_fc2, i, h)
    return _rmsnorm(h, final_norm_w, _F32).reshape(batch, seq, d)
```

```python
import functools
import math

import jax
import jax.numpy as jnp
from jax import lax
from jax.experimental import pallas as pl
from jax.experimental.pallas import tpu as pltpu

_F32 = jnp.float32
_MXU_DTYPE = jnp.bfloat16
_RMS_EPS = 1e-5
_GLA_GATE_NORMALIZER = 16.0
_SSD_GROUPS = 8
_CHUNK = 128
_LANES = 128
_GLA_DIAG = 8
_GLA_HEADS_PER_STEP = 4
_SSD_GROUPS_PER_STEP = 8
_LOG2E = math.log2(math.e)
_NEG_BIG = -1e30
_HALO = 16
_MXU_WIDTH = 256
_VMEM_LIMIT_BYTES = 60 * 1024 * 1024
_MATMUL_VMEM_BUDGET = 58 * 1024 * 1024

_NN = (((1,), (0,)), ((), ()))
_NT = (((1,), (1,)), ((), ()))
_TN = (((0,), (0,)), ((), ()))


def _params(sem, flags=None):
    return pltpu.CompilerParams(dimension_semantics=sem, vmem_limit_bytes=_VMEM_LIMIT_BYTES, flags=flags)


def _rms(x, w):
    ms = jnp.mean(x * x, axis=-1, keepdims=True)
    return (x * lax.rsqrt(ms + _RMS_EPS)) * w


def _silu(x):
    half = 0.5 * x
    return half + half * jnp.tanh(half)


def _matmul_kernel(*refs, act, act_tile0, has_norm, has_res, wt):
    it = iter(refs)
    a_ref = next(it)
    nw_ref = next(it) if has_norm else None
    w_ref = next(it)
    r_ref = next(it) if has_res else None
    o_ref = next(it)
    if has_norm:
        an_ref = next(it)

        @pl.when(pl.program_id(1) == 0)
        def _():
            an_ref[...] = _rms(a_ref[...], nw_ref[...]).astype(_MXU_DTYPE)

        a = an_ref[...]
    else:
        a = a_ref[...]
    w = w_ref[...].astype(_MXU_DTYPE)
    acc = lax.dot_general(a, w, _NT if wt else _NN, preferred_element_type=_F32)
    if act == "relu2":
        act_acc = jnp.square(jnp.maximum(acc, 0.0))
    elif act == "silu":
        act_acc = _silu(acc)
    if act is not None:
        acc = act_acc if act_tile0 == 0 else jnp.where(pl.program_id(1) >= act_tile0, act_acc, acc)
    if has_res:
        acc = r_ref[...] + acc
    o_ref[...] = acc.astype(o_ref.dtype)


def _divisor_tile(n, pref, align):
    if n <= pref:
        return n
    t = (pref // align) * align
    while n % t:
        t -= align
    return t


def _matmul_tiles(m, k, n, a_bytes, w_bytes, out_bytes, norm_copies, has_res, conv_taps=0, row_cap=None):
    mxu_bytes = jnp.dtype(_MXU_DTYPE).itemsize
    for pm, pn in ((1024, 2048), (1024, 1024), (1024, 512), (1024, 256), (512, 256), (256, 256), (128, 128)):
        pm = pm if row_cap is None else min(pm, row_cap)
        tm, tn = _divisor_tile(m, pm, 8), _divisor_tile(n, pn, _LANES)
        need = 2 * tm * k * a_bytes + 2 * k * tn * w_bytes + 2 * tm * tn * out_bytes
        need += (2 + conv_taps) * tm * _MXU_WIDTH * 4
        need += 2 * tm * tn * 4 if has_res else 0
        need += norm_copies * tm * k * mxu_bytes
        need += k * tn * mxu_bytes if w_bytes != mxu_bytes else 0
        if need <= _MATMUL_VMEM_BUDGET:
            break
    return tm, tn


def _matmul(a, w, layer, *, wt=False, n=None, norm_w=None, emit_norm=False, act=None, act_col0=0, res=None,
            out_dtype):
    m, k = a.shape
    n = w.shape[1 if wt else 2] if n is None else n
    tm, tn = _matmul_tiles(m, k, math.gcd(n, act_col0), a.dtype.itemsize, w.dtype.itemsize,
                           jnp.dtype(out_dtype).itemsize, (norm_w is not None) + emit_norm, res is not None)
    in_specs = [pl.BlockSpec((tm, k), lambda i, j: (i, 0))]
    args = [a]
    if norm_w is not None:
        in_specs.append(pl.BlockSpec((1, k), lambda i, j: (0, 0)))
        args.append(norm_w.reshape(1, k).astype(_F32))
    in_specs.append(pl.BlockSpec((None, tn, k), lambda i, j: (layer, j, 0)) if wt else
                    pl.BlockSpec((None, k, tn), lambda i, j: (layer, 0, j)))
    args.append(w)
    if res is not None:
        in_specs.append(pl.BlockSpec((tm, tn), lambda i, j: (i, j)))
        args.append(res)
    out_specs = [pl.BlockSpec((tm, tn), lambda i, j: (i, j))]
    out_shape = [jax.ShapeDtypeStruct((m, n), out_dtype)]
    scratch = []
    if emit_norm:
        out_specs.append(pl.BlockSpec((tm, k), lambda i, j: (i, 0)))
        out_shape.append(jax.ShapeDtypeStruct((m, k), _MXU_DTYPE))
    elif norm_w is not None:
        scratch.append(pltpu.VMEM((tm, k), _MXU_DTYPE))
    outs = pl.pallas_call(
        functools.partial(_matmul_kernel, act=act, act_tile0=act_col0 // tn, has_norm=norm_w is not None,
                          has_res=res is not None, wt=wt),
        grid=(m // tm, n // tn),
        in_specs=in_specs,
        out_specs=out_specs,
        out_shape=out_shape,
        scratch_shapes=scratch,
        compiler_params=_params(("parallel", "arbitrary")),
        name="matmul",
    )(*args)
    return outs if emit_norm else outs[0]


def _matmul_splitk_kernel(a_ref, w_ref, r_ref, *rest, nk):
    o_ref = rest[-1]

    @pl.when(pl.program_id(2) == 0)
    def _():
        o_ref[...] = r_ref[...]

    o_ref[...] += jnp.dot(a_ref[...], w_ref[...].astype(_MXU_DTYPE), preferred_element_type=_F32)
    if len(rest) == 2:
        @pl.when(pl.program_id(2) == nk - 1)
        def _():
            o_ref[...] = _rms(o_ref[...], rest[0][...])


def _matmul_residual(a, w, layer, res, out_norm_w=None):
    m, k = a.shape
    n = w.shape[2]
    for nk in (1, 2, 4, 8):
        tm, tn = _matmul_tiles(m, k // nk, n, a.dtype.itemsize, w.dtype.itemsize, 4, 0, True)
        if (tn == n and tm >= min(m, 1024)) if out_norm_w is not None else (
                tm * tn >= min(m, 1024) * min(n, 1024)):
            break
    if nk == 1 and out_norm_w is None:
        return _matmul(a, w, layer, res=res, out_dtype=_F32)
    tk = k // nk
    in_specs = [pl.BlockSpec((tm, tk), lambda i, j, kk: (i, kk)),
                pl.BlockSpec((None, tk, tn), lambda i, j, kk: (layer, kk, j)),
                pl.BlockSpec((tm, tn), lambda i, j, kk: (i, j))]
    args = [a, w, res]
    if out_norm_w is not None:
        assert tn == n
        in_specs.append(pl.BlockSpec((1, n), lambda i, j, kk: (0, 0)))
        args.append(out_norm_w.reshape(1, n).astype(_F32))
    return pl.pallas_call(
        functools.partial(_matmul_splitk_kernel, nk=nk),
        grid=(m // tm, n // tn, nk),
        in_specs=in_specs,
        out_specs=pl.BlockSpec((tm, tn), lambda i, j, kk: (i, j)),
        out_shape=jax.ShapeDtypeStruct((m, n), _F32),
        compiler_params=_params(("parallel", "parallel", "arbitrary")),
        name="matmul_splitk",
    )(*args)


def _proj_conv_kernel(u_ref, halo_ref, w_ref, cw_ref, cb_ref, o_ref, an_ref, *, tiles_per_seq):
    kw = cw_ref.shape[0]

    @pl.when(pl.program_id(1) == 0)
    def _():
        halo = halo_ref[...]
        an_ref[0:_HALO, :] = jnp.where(pl.program_id(0) % tiles_per_seq == 0, jnp.zeros_like(halo), halo)
        an_ref[_HALO:, :] = u_ref[...]

    wb = w_ref[...].astype(_MXU_DTYPE)
    for c0 in range(0, o_ref.shape[1], _MXU_WIDTH):
        cols = pl.ds(c0, _MXU_WIDTH)
        acc = lax.dot_general(an_ref[...], wb[c0:c0 + _MXU_WIDTH, :], _NT, preferred_element_type=_F32)
        tm = o_ref.shape[0]
        out = cb_ref[:, cols] + cw_ref[kw - 1:kw, cols] * acc[_HALO:, :]
        for s in range(1, kw):
            out = out + cw_ref[kw - 1 - s:kw - s, cols] * acc[_HALO - s:_HALO - s + tm, :]
        o_ref[:, cols] = _silu(out).astype(o_ref.dtype)


def _proj_conv(u, w, layer, col0, conv_w, conv_b, seq, out_dtype):
    m, k = u.shape
    kw, n = conv_w.shape
    tm, tn = _matmul_tiles(m, k, math.gcd(n, col0), u.dtype.itemsize, w.dtype.itemsize,
                           jnp.dtype(out_dtype).itemsize, 1, False, conv_taps=kw, row_cap=seq)
    assert seq % tm == 0 and tm % _HALO == 0 and tn % _MXU_WIDTH == 0 and kw - 1 <= _HALO
    return pl.pallas_call(
        functools.partial(_proj_conv_kernel, tiles_per_seq=seq // tm),
        grid=(m // tm, n // tn),
        in_specs=[pl.BlockSpec((tm, k), lambda i, j: (i, 0)),
                  pl.BlockSpec((_HALO, k), lambda i, j: (jnp.maximum(i * (tm // _HALO) - 1, 0), 0)),
                  pl.BlockSpec((None, tn, k), lambda i, j: (layer, col0 // tn + j, 0)),
                  pl.BlockSpec((kw, tn), lambda i, j: (0, j)),
                  pl.BlockSpec((1, tn), lambda i, j: (0, j))],
        out_specs=pl.BlockSpec((tm, tn), lambda i, j: (i, j)),
        out_shape=jax.ShapeDtypeStruct((m, n), out_dtype),
        scratch_shapes=[pltpu.VMEM((tm + _HALO, k), _MXU_DTYPE)],
        compiler_params=_params(("parallel", "arbitrary")),
        name="proj_conv",
    )(u, u, w, conv_w, conv_b)


def _softplus(x):
    return jnp.maximum(x, 0.0) + jnp.log(1.0 + jnp.exp(-jnp.abs(x)))


def _chunk_cumsum(x, chunk):
    rows = x.shape[0]
    r = lax.broadcasted_iota(jnp.int32, (chunk, chunk), 0)
    c = lax.broadcasted_iota(jnp.int32, (chunk, chunk), 1)
    tri = jnp.where(c <= r, 1.0, 0.0).astype(_MXU_DTYPE)
    out = []
    for i in range(rows // chunk):
        rem = x[i * chunk:(i + 1) * chunk, :]
        acc = None
        for _ in range(3):
            piece = rem.astype(_MXU_DTYPE)
            d = jnp.dot(tri, piece, preferred_element_type=_F32)
            acc = d if acc is None else acc + d
            rem = rem - piece.astype(_F32)
        out.append(acc)
    return jnp.concatenate(out, axis=0) if len(out) > 1 else out[0]


def _row_bcast(ref, row, n):
    return jnp.broadcast_to(ref[pl.ds(row, 1), :], (n, ref.shape[1]))


def _tail_rows(w_ref, n_valid):
    row = lax.broadcasted_iota(jnp.int32, w_ref.shape, 0)
    return jnp.where(row < n_valid, w_ref[...], 0.0).astype(_MXU_DTYPE)


def _gla_gate_kernel(u_ref, wgr_ref, wup_ref, b_ref, o_ref, *, chunk, rank):
    gr = lax.dot_general(u_ref[...], _tail_rows(wgr_ref, rank), _NT, preferred_element_type=_F32)
    pre = jnp.dot(gr.astype(_MXU_DTYPE), wup_ref[...], preferred_element_type=_F32) + b_ref[...]
    o_ref[...] = _chunk_cumsum(_softplus(-pre) * (-_LOG2E / _GLA_GATE_NORMALIZER), chunk)


def _gla_gate(u, wt_in, layer, col0, rank, w_up, b_up, rows=1024):
    m, d = u.shape
    kd = w_up.shape[1]
    rows = min(rows, m)
    assert col0 % _LANES == 0 and col0 + rank == wt_in.shape[1] and rank <= _LANES
    return pl.pallas_call(
        functools.partial(_gla_gate_kernel, chunk=_CHUNK, rank=rank),
        grid=(m // rows,),
        in_specs=[pl.BlockSpec((rows, d), lambda i: (i, 0)),
                  pl.BlockSpec((None, _LANES, d), lambda i: (layer, col0 // _LANES, 0)),
                  pl.BlockSpec(w_up.shape, lambda i: (0, 0)),
                  pl.BlockSpec((1, kd), lambda i: (0, 0))],
        out_specs=pl.BlockSpec((rows, kd), lambda i: (i, 0)),
        out_shape=jax.ShapeDtypeStruct((m, kd), _F32),
        compiler_params=_params(("parallel",)),
        name="gla_gate",
    )(u, wt_in, w_up, b_up)


def _gla_levels(c, nd):
    i = lax.broadcasted_iota(jnp.int32, (c, c), 0)
    j = lax.broadcasted_iota(jnp.int32, (c, c), 1)
    x = i ^ j
    lvl = jnp.zeros((c, c), jnp.int32)
    s, level = nd, 1
    while s < c:
        lvl = jnp.where(x >= s, level, lvl)
        s, level = 2 * s, level + 1
    return jnp.where(j > i, -1, lvl)


def _gla_scan_head(q_ref, k_ref, v_ref, gs_ref, b_ref, onw_ref, lvl_ref, o_ref, st_ref, qf_ref, kf_ref, scale):
    c, dk = q_ref.shape
    qf_ref[...] = q_ref[...].astype(_F32) * scale
    kf_ref[...] = k_ref[...].astype(_F32)
    q = qf_ref[...]
    k = kf_ref[...]
    v = v_ref[...]
    bc = b_ref[...]
    st = st_ref[...]
    b_last = jnp.concatenate([_row_bcast(b_ref, c - 1, 8)] * (c // 8), axis=0)

    o = lax.dot_general((q * jnp.exp2(bc)).astype(_MXU_DTYPE), st.astype(_MXU_DTYPE), _NT,
                        preferred_element_type=_F32)
    k_dec = (k * jnp.exp2(b_last - bc)).astype(_MXU_DTYPE)

    nd = _GLA_DIAG
    lane = lax.broadcasted_iota(jnp.int32, (nd, c), 1)
    pieces = []
    for m in range(c // nd):
        qb = qf_ref[m * nd:(m + 1) * nd, :]
        bb = b_ref[m * nd:(m + 1) * nd, :]
        acc = jnp.zeros((nd, c), _F32)
        for j in range(nd):
            kj = _row_bcast(kf_ref, m * nd + j, nd)
            bj = _row_bcast(b_ref, m * nd + j, nd)
            term = qb * kj * jnp.exp2(jnp.minimum(bb - bj, 0.0))
            acc = jnp.where(lane == m * nd + j, jnp.sum(term, axis=1, keepdims=True), acc)
        pieces.append(acc)
    lvl = lvl_ref[...]
    att = jnp.where(lvl == 0, jnp.concatenate(pieces, axis=0), 0.0)

    s, level = nd, 1
    while s < c:
        prev = [jnp.zeros((s, dk), _F32)]
        end = []
        for m in range(c // s):
            if m:
                prev += [_row_bcast(b_ref, m * s - 1, 8)] * (s // 8)
            end += [_row_bcast(b_ref, (m + 1) * s - 1, 8)] * (s // 8)
        qs = (q * jnp.exp2(bc - jnp.concatenate(prev, axis=0))).astype(_MXU_DTYPE)
        ks = (k * jnp.exp2(jnp.concatenate(end, axis=0) - bc)).astype(_MXU_DTYPE)
        att = jnp.where(lvl == level, lax.dot_general(qs, ks, _NT, preferred_element_type=_F32), att)
        s, level = 2 * s, level + 1

    o = o + jnp.dot(att.astype(_MXU_DTYPE), v, preferred_element_type=_F32)
    st_ref[...] = st * jnp.exp2(b_last[:1, :]) + lax.dot_general(
        v, k_dec, _TN, preferred_element_type=_F32)

    ms = jnp.mean(o * o, axis=-1, keepdims=True)
    on = (o * lax.rsqrt(ms + _RMS_EPS)) * onw_ref[...]
    o_ref[...] = (on * gs_ref[...].astype(_F32)).astype(o_ref.dtype)


def _gla_scan_kernel(q_ref, k_ref, v_ref, gs_ref, b_ref, onw_ref, lvl_ref, o_ref, st_ref, qf_ref, kf_ref,
                     *, scale, dk, dv):
    @pl.when(pl.program_id(2) == 0)
    def _():
        st_ref[...] = jnp.zeros_like(st_ref)

    for h in range(st_ref.shape[0]):
        ks, vs = pl.ds(h * dk, dk), pl.ds(h * dv, dv)
        _gla_scan_head(q_ref.at[:, ks], k_ref.at[:, ks], v_ref.at[:, vs], gs_ref.at[:, vs], b_ref.at[:, ks],
                       onw_ref, lvl_ref, o_ref.at[:, vs], st_ref.at[h], qf_ref.at[h], kf_ref.at[h], scale)


def _gla_scan(proj, bcum, o_norm_w, batch, seq, heads, dk, dv):
    m = proj.shape[0]
    c = _CHUNK
    hb = _GLA_HEADS_PER_STEP if heads % _GLA_HEADS_PER_STEP == 0 else 1
    assert c == _LANES and dk % _LANES == 0 and dv % _LANES == 0 and (2 * heads * dk) % (hb * dv) == 0
    nt = seq // c
    kd, vd = heads * dk, heads * dv
    wk, wv = hb * dk, hb * dv
    k_blk, v_blk, g_blk = kd // wk, (2 * kd) // wv, (2 * kd + vd) // wv

    def rows(b, h, t):
        return b * nt + t

    return pl.pallas_call(
        functools.partial(_gla_scan_kernel, scale=dk ** -0.5, dk=dk, dv=dv),
        grid=(batch, heads // hb, nt),
        in_specs=[pl.BlockSpec((c, wk), lambda b, h, t: (rows(b, h, t), h)),
                  pl.BlockSpec((c, wk), lambda b, h, t: (rows(b, h, t), k_blk + h)),
                  pl.BlockSpec((c, wv), lambda b, h, t: (rows(b, h, t), v_blk + h)),
                  pl.BlockSpec((c, wv), lambda b, h, t: (rows(b, h, t), g_blk + h)),
                  pl.BlockSpec((c, wk), lambda b, h, t: (rows(b, h, t), h)),
                  pl.BlockSpec((1, dv), lambda b, h, t: (0, 0)),
                  pl.BlockSpec((c, c), lambda b, h, t: (0, 0))],
        out_specs=pl.BlockSpec((c, wv), lambda b, h, t: (rows(b, h, t), h)),
        out_shape=jax.ShapeDtypeStruct((m, vd), _MXU_DTYPE),
        scratch_shapes=[pltpu.VMEM((hb, dv, dk), _F32),
                        pltpu.VMEM((hb, c, dk), _F32),
                        pltpu.VMEM((hb, c, dk), _F32)],
        compiler_params=_params(("parallel", "parallel", "arbitrary")),
        name="gla_scan",
    )(proj, proj, proj, proj, bcum, o_norm_w.reshape(1, dv).astype(_F32), _gla_levels(c, _GLA_DIAG))


def _gla_mixer(h, norm_w, batch, seq, layer, wt_in, w_gk_up, b_gk_up, o_norm_w, w_out):
    rank, kd = w_gk_up.shape
    vd = (wt_in.shape[1] - rank - 2 * kd) // 2
    dv = o_norm_w.shape[0]
    heads = vd // dv
    n_main = 2 * kd + 2 * vd
    w_up = jnp.pad(w_gk_up, ((0, _LANES - rank), (0, 0))).astype(_MXU_DTYPE)
    proj, u = _matmul(h, wt_in, layer, wt=True, n=n_main, norm_w=norm_w, emit_norm=True, act="silu",
                      act_col0=2 * kd + vd, out_dtype=_MXU_DTYPE)
    bcum = _gla_gate(u, wt_in, layer, n_main, rank, w_up, b_gk_up.reshape(1, kd).astype(_F32))
    o = _gla_scan(proj, bcum, o_norm_w, batch, seq, heads, kd // heads, dv)
    return _matmul_residual(o, w_out.astype(_MXU_DTYPE), layer, h)


def _ssd_gate_kernel(u_ref, wdt_ref, bias_ref, alog_ref, acol_ref, dcol_ref, arow_ref, drow_ref,
                     *, chunk, heads):
    raw = lax.dot_general(u_ref[...], _tail_rows(wdt_ref, heads), _NT, preferred_element_type=_F32)
    dt = _softplus(raw + bias_ref[...])
    a = _chunk_cumsum(dt * (-jnp.exp(alog_ref[...])) * _LOG2E, chunk)
    acol_ref[...] = a
    dcol_ref[...] = dt
    for i in range(a.shape[0] // chunk):
        arow_ref[i] = a[i * chunk:(i + 1) * chunk, :].T[:heads, :]
        drow_ref[i] = dt[i * chunk:(i + 1) * chunk, :].T[:heads, :]


def _ssd_gate(u, wt_in, layer, col0, dt_bias, a_log, heads, rows=1024):
    m, d = u.shape
    c = _CHUNK
    rows = min(rows, m)
    g = _SSD_GROUPS
    hg = heads // g
    assert col0 % _LANES == 0 and col0 + heads == wt_in.shape[1] and heads <= _LANES
    col = jax.ShapeDtypeStruct((m, _LANES), _F32)
    row = jax.ShapeDtypeStruct((m // c, heads, c), _F32)
    col_spec = pl.BlockSpec((rows, _LANES), lambda i: (i, 0))
    row_spec = pl.BlockSpec((rows // c, heads, c), lambda i: (i, 0, 0))
    return pl.pallas_call(
        functools.partial(_ssd_gate_kernel, chunk=c, heads=heads),
        grid=(m // rows,),
        in_specs=[pl.BlockSpec((rows, d), lambda i: (i, 0)),
                  pl.BlockSpec((None, _LANES, d), lambda i: (layer, col0 // _LANES, 0)),
                  pl.BlockSpec((1, _LANES), lambda i: (0, 0)),
                  pl.BlockSpec((1, _LANES), lambda i: (0, 0))],
        out_specs=[col_spec, col_spec, row_spec, row_spec],
        out_shape=[col, col, row, row],
        compiler_params=_params(("parallel",)),
        name="ssd_gate",
    )(u, wt_in, dt_bias, a_log)


def _head_expanders(groups, hg, c, head_dim):
    def one_hot(rows, width):
        row = lax.broadcasted_iota(jnp.int32, (groups, rows, hg * width), 1) % _LANES
        col = lax.broadcasted_iota(jnp.int32, (groups, rows, hg * width), 2)
        grp = lax.broadcasted_iota(jnp.int32, (groups, rows, hg * width), 0)
        return (row == grp * hg + col // width).astype(_MXU_DTYPE)
    return one_hot(2 * _LANES, c), one_hot(_LANES, head_dim)


def _expand_heads(bcasts, head_dim):
    c = bcasts[0].shape[0]
    per = _LANES // head_dim
    lane = lax.broadcasted_iota(jnp.int32, (c, _LANES), 1)
    pieces = []
    for p in range(len(bcasts) // per):
        out = bcasts[p * per]
        for i in range(1, per):
            out = jnp.where(lane >= i * head_dim, bcasts[p * per + i], out)
        pieces.append(out)
    return jnp.concatenate(pieces, axis=1)


def _ssd_scan_group(zs_ref, x_ref, bm_ref, cm_ref, acol_ref, dcol_ref, esc_ref, ehp_ref, arow_ref, drow_ref,
                    dskip_ref, gnw_ref, o_ref, st_ref, head_dim):
    c = zs_ref.shape[0]
    hg = x_ref.shape[1] // head_dim
    xs = x_ref[...].astype(_F32)
    bmx = bm_ref[...]
    cmx = cm_ref[...]

    acol = acol_ref[...]
    a_last = acol[c - 1:c, :]
    w_state = (jnp.exp2(a_last - acol) * dcol_ref[...]).astype(_MXU_DTYPE)
    a_hi = acol.astype(_MXU_DTYPE)
    a_lo = (acol - a_hi.astype(_F32)).astype(_MXU_DTYPE)
    a_sc = jnp.dot(jnp.concatenate([a_hi, a_lo], axis=1), esc_ref[...],
                   preferred_element_type=_F32)
    w_hp = jnp.dot(w_state, ehp_ref[...], preferred_element_type=_F32)

    per = _LANES // head_dim
    width = per * c
    cb = lax.dot_general(cmx, jnp.concatenate([bmx] * per, axis=0), _NT,
                         preferred_element_type=_F32)
    ri = lax.broadcasted_iota(jnp.int32, (c, width), 0)
    ci = lax.broadcasted_iota(jnp.int32, (c, width), 1) & (c - 1)
    causal = ci <= ri
    lane = lax.broadcasted_iota(jnp.int32, (c, _LANES), 1)
    y_pieces = []
    for p in range(hg // per):
        ac = a_sc[:, p * width:(p + 1) * width]
        ar = arow_ref[:, p * width:(p + 1) * width]
        dr = drow_ref[:, p * width:(p + 1) * width]
        decay = jnp.exp2(jnp.where(causal, ac - ar, _NEG_BIG))
        sc = (cb * decay * dr).astype(_MXU_DTYPE)
        xp = x_ref[:, p * _LANES:(p + 1) * _LANES]
        bd = jnp.concatenate(
            [jnp.where((lane >= i * head_dim) & (lane < (i + 1) * head_dim), xp, jnp.zeros_like(xp))
             for i in range(per)], axis=0)
        y_pieces.append(jnp.dot(sc, bd, preferred_element_type=_F32))
    y = jnp.concatenate(y_pieces, axis=1)

    st = st_ref[...]
    ea = jnp.exp2(_expand_heads([a_sc[:, h * c:(h + 1) * c] for h in range(hg)], head_dim))
    y = y + jnp.dot(cmx, st.astype(_MXU_DTYPE), preferred_element_type=_F32) * ea
    xw = (xs * w_hp).astype(_MXU_DTYPE)
    st_ref[...] = st * ea[c - 1:c, :] + lax.dot_general(bmx, xw, _TN, preferred_element_type=_F32)

    y = y + dskip_ref[...] * xs
    y = y * zs_ref[...].astype(_F32)
    ms = jnp.mean(y * y, axis=-1, keepdims=True)
    o_ref[...] = ((y * lax.rsqrt(ms + _RMS_EPS)) * gnw_ref[...]).astype(o_ref.dtype)


def _ssd_scan_kernel(zs_ref, x_ref, bm_ref, cm_ref, acol_ref, dcol_ref, esc_ref, ehp_ref, arow_ref, drow_ref,
                     dskip_ref, gnw_ref, o_ref, st_ref, *, head_dim):
    @pl.when(pl.program_id(2) == 0)
    def _():
        st_ref[...] = jnp.zeros_like(st_ref)

    gb, n, gw = st_ref.shape
    c = zs_ref.shape[0]
    hg = gw // head_dim
    for g in range(gb):
        xs_, ns_, rs_ = pl.ds(g * gw, gw), pl.ds(g * n, n), pl.ds(g * hg * c, hg * c)
        _ssd_scan_group(zs_ref.at[:, xs_], x_ref.at[:, xs_], bm_ref.at[:, ns_], cm_ref.at[:, ns_],
                        acol_ref, dcol_ref, esc_ref.at[g], ehp_ref.at[g],
                        arow_ref.at[0, :, rs_], drow_ref.at[0, :, rs_],
                        dskip_ref.at[:, xs_], gnw_ref.at[:, xs_], o_ref.at[:, xs_], st_ref.at[g], head_dim)


def _ssd_scan(zs, xbc, acol, dcol, arow, drow, dskip, gnorm_w, batch, seq, inner, heads, n_state):
    m = zs.shape[0]
    c = _CHUNK
    g = _SSD_GROUPS
    gb = _SSD_GROUPS_PER_STEP if g % _SSD_GROUPS_PER_STEP == 0 else 1
    hg = heads // g
    p = inner // heads
    gw = hg * p
    sw, sn = gb * gw, gb * n_state
    assert c == _LANES and gw % _LANES == 0 and n_state % _LANES == 0 and _LANES % p == 0
    assert inner % sn == 0 and (g * n_state) % sn == 0
    nt = seq // c
    b_blk = inner // sn
    c_blk = (inner + g * n_state) // sn

    def rows(b, t):
        return b * nt + t

    in_specs = [
        pl.BlockSpec((c, sw), lambda b, gi, t: (rows(b, t), gi)),
        pl.BlockSpec((c, sw), lambda b, gi, t: (rows(b, t), gi)),
        pl.BlockSpec((c, sn), lambda b, gi, t: (rows(b, t), b_blk + gi)),
        pl.BlockSpec((c, sn), lambda b, gi, t: (rows(b, t), c_blk + gi)),
        pl.BlockSpec((c, _LANES), lambda b, gi, t: (rows(b, t), 0)),
        pl.BlockSpec((c, _LANES), lambda b, gi, t: (rows(b, t), 0)),
        pl.BlockSpec((gb, 2 * _LANES, hg * c), lambda b, gi, t: (gi, 0, 0)),
        pl.BlockSpec((gb, _LANES, gw), lambda b, gi, t: (gi, 0, 0)),
        pl.BlockSpec((1, 1, gb * hg * c), lambda b, gi, t: (rows(b, t), 0, gi)),
        pl.BlockSpec((1, 1, gb * hg * c), lambda b, gi, t: (rows(b, t), 0, gi)),
        pl.BlockSpec((1, sw), lambda b, gi, t: (0, gi)),
        pl.BlockSpec((1, sw), lambda b, gi, t: (0, gi)),
    ]
    return pl.pallas_call(
        functools.partial(_ssd_scan_kernel, head_dim=p),
        grid=(batch, g // gb, nt),
        in_specs=in_specs,
        out_specs=pl.BlockSpec((c, sw), lambda b, gi, t: (rows(b, t), gi)),
        out_shape=jax.ShapeDtypeStruct((m, inner), _MXU_DTYPE),
        scratch_shapes=[pltpu.VMEM((gb, n_state, gw), _F32)],
        compiler_params=_params(("parallel", "parallel", "arbitrary")),
        name="ssd_scan",
    )(zs, xbc, xbc, xbc, acol, dcol, *_head_expanders(g, hg, c, p), arow, drow, dskip, gnorm_w)


def _ssd_mixer(h, norm_w, batch, seq, layer, wt_in, conv_w, conv_b, dt_bias, a_log, d_skip, gnorm_w, w_out):
    heads = dt_bias.shape[0]
    conv_dim = conv_w.shape[1]
    inner = wt_in.shape[1] - conv_dim - heads
    n_state = (conv_dim - inner) // (2 * _SSD_GROUPS)
    n_main = inner + conv_dim
    m = h.shape[0]
    pad1 = lambda v: jnp.pad(v.astype(_F32), (0, _LANES - heads)).reshape(1, _LANES)
    zs, u = _matmul(h, wt_in, layer, wt=True, n=inner, norm_w=norm_w, emit_norm=True, act="silu",
                    out_dtype=_MXU_DTYPE)
    xbc = _proj_conv(u, wt_in, layer, inner, conv_w.astype(_F32), conv_b.reshape(1, conv_dim).astype(_F32),
                     seq, _MXU_DTYPE)
    acol, dcol, arow, drow = _ssd_gate(u, wt_in, layer, n_main, pad1(dt_bias), pad1(a_log), heads)
    arow = arow.reshape(m // _CHUNK, 1, heads * _CHUNK)
    drow = drow.reshape(m // _CHUNK, 1, heads * _CHUNK)
    dskip = jnp.repeat(d_skip.astype(_F32), inner // heads).reshape(1, inner)
    y = _ssd_scan(zs, xbc, acol, dcol, arow, drow, dskip, gnorm_w.reshape(1, inner).astype(_F32),
                  batch, seq, inner, heads, n_state)
    return _matmul_residual(y, w_out.astype(_MXU_DTYPE), layer, h)


def kernel(x, mixer_norm_w, gla_w_in, gla_w_gk_up, gla_b_gk_up, gla_o_norm_w, gla_w_out, ssd_w_in, ssd_conv_w, ssd_conv_b, ssd_dt_bias, ssd_a_log, ssd_d_skip, ssd_gnorm_w, ssd_w_out, mlp_norm_w, mlp_w_fc1, mlp_w_fc2, final_norm_w):
    batch, seq, d = x.shape
    h = x.reshape(batch * seq, d)
    w_fc1 = mlp_w_fc1.astype(_MXU_DTYPE)
    w_fc2 = mlp_w_fc2.astype(_MXU_DTYPE)
    gla_wt_in = jnp.swapaxes(gla_w_in, 1, 2).astype(_MXU_DTYPE)
    ssd_wt_in = jnp.swapaxes(ssd_w_in, 1, 2).astype(_MXU_DTYPE)
    depth = mixer_norm_w.shape[0]
    assert depth >= 1
    for i in range(depth):
        j = i // 2
        if i % 2 == 0:
            h = _gla_mixer(h, mixer_norm_w[i], batch, seq, j, gla_wt_in, gla_w_gk_up[j], gla_b_gk_up[j],
                           gla_o_norm_w[j], gla_w_out)
        else:
            h = _ssd_mixer(h, mixer_norm_w[i], batch, seq, j, ssd_wt_in, ssd_conv_w[j], ssd_conv_b[j],
                           ssd_dt_bias[j], ssd_a_log[j], ssd_d_skip[j], ssd_gnorm_w[j], ssd_w_out)
        hidden = _matmul(h, w_fc1, i, norm_w=mlp_norm_w[i], act="relu2", out_dtype=_MXU_DTYPE)
        h = _matmul_residual(hidden, w_fc2, i, h, final_norm_w if i == depth - 1 else None)
    return h.reshape(batch, seq, d)
```

```python
import functools
import math

import jax
import jax.numpy as jnp
from jax import lax
from jax.experimental import pallas as pl
from jax.experimental.pallas import tpu as pltpu

_F32 = jnp.float32
_MXU_DTYPE = jnp.bfloat16
_RMS_EPS = 1e-5
_GLA_GATE_NORMALIZER = 16.0
_SSD_GROUPS = 8
_CHUNK = 128
_LANES = 128
_GLA_DIAG = 8
_GLA_HEADS_PER_STEP = 4
_GLA_CHUNKS_PER_STEP = 4
_SSD_CHUNKS_PER_STEP = 2
_SSD_GROUPS_PER_STEP = 8
_LOG2E = math.log2(math.e)
_NEG_BIG = -1e30
_HALO = 16
_MXU_WIDTH = 256
_VMEM_LIMIT_BYTES = 60 * 1024 * 1024
_MATMUL_VMEM_BUDGET = 58 * 1024 * 1024

_NN = (((1,), (0,)), ((), ()))
_NT = (((1,), (1,)), ((), ()))
_TN = (((0,), (0,)), ((), ()))


def _params(sem, flags=None):
    return pltpu.CompilerParams(dimension_semantics=sem, vmem_limit_bytes=_VMEM_LIMIT_BYTES, flags=flags)


def _rms(x, w):
    ms = jnp.mean(x * x, axis=-1, keepdims=True)
    return (x * lax.rsqrt(ms + _RMS_EPS)) * w


def _silu(x):
    half = 0.5 * x
    return half + half * jnp.tanh(half)


def _matmul_kernel(*refs, act, act_tile0, has_norm, has_res, wt):
    it = iter(refs)
    a_ref = next(it)
    nw_ref = next(it) if has_norm else None
    w_ref = next(it)
    r_ref = next(it) if has_res else None
    o_ref = next(it)
    if has_norm:
        an_ref = next(it)

        @pl.when(pl.program_id(1) == 0)
        def _():
            an_ref[...] = _rms(a_ref[...], nw_ref[...]).astype(_MXU_DTYPE)

        a = an_ref[...]
    else:
        a = a_ref[...]
    w = w_ref[...].astype(_MXU_DTYPE)
    acc = lax.dot_general(a, w, _NT if wt else _NN, preferred_element_type=_F32)
    if act == "relu2":
        act_acc = jnp.square(jnp.maximum(acc, 0.0))
    elif act == "silu":
        act_acc = _silu(acc)
    if act is not None:
        acc = act_acc if act_tile0 == 0 else jnp.where(pl.program_id(1) >= act_tile0, act_acc, acc)
    if has_res:
        acc = r_ref[...] + acc
    o_ref[...] = acc.astype(o_ref.dtype)


def _divisor_tile(n, pref, align):
    if n <= pref:
        return n
    t = (pref // align) * align
    while n % t:
        t -= align
    return t


def _matmul_tiles(m, k, n, a_bytes, w_bytes, out_bytes, norm_copies, has_res, conv_taps=0, row_cap=None):
    mxu_bytes = jnp.dtype(_MXU_DTYPE).itemsize
    for pm, pn in ((1024, 2048), (1024, 1024), (1024, 512), (1024, 256), (512, 256), (256, 256), (128, 128)):
        pm = pm if row_cap is None else min(pm, row_cap)
        tm, tn = _divisor_tile(m, pm, 8), _divisor_tile(n, pn, _LANES)
        need = 2 * tm * k * a_bytes + 2 * k * tn * w_bytes + 2 * tm * tn * out_bytes
        need += (2 + conv_taps) * tm * _MXU_WIDTH * 4
        need += 2 * tm * tn * 4 if has_res else 0
        need += norm_copies * tm * k * mxu_bytes
        need += k * tn * mxu_bytes if w_bytes != mxu_bytes else 0
        if need <= _MATMUL_VMEM_BUDGET:
            break
    return tm, tn


def _matmul(a, w, layer, *, wt=False, n=None, norm_w=None, emit_norm=False, act=None, act_col0=0, res=None,
            out_dtype):
    m, k = a.shape
    n = w.shape[1 if wt else 2] if n is None else n
    tm, tn = _matmul_tiles(m, k, math.gcd(n, act_col0), a.dtype.itemsize, w.dtype.itemsize,
                           jnp.dtype(out_dtype).itemsize, (norm_w is not None) + emit_norm, res is not None)
    in_specs = [pl.BlockSpec((tm, k), lambda i, j: (i, 0))]
    args = [a]
    if norm_w is not None:
        in_specs.append(pl.BlockSpec((1, k), lambda i, j: (0, 0)))
        args.append(norm_w.reshape(1, k).astype(_F32))
    in_specs.append(pl.BlockSpec((None, tn, k), lambda i, j: (layer, j, 0)) if wt else
                    pl.BlockSpec((None, k, tn), lambda i, j: (layer, 0, j)))
    args.append(w)
    if res is not None:
        in_specs.append(pl.BlockSpec((tm, tn), lambda i, j: (i, j)))
        args.append(res)
    out_specs = [pl.BlockSpec((tm, tn), lambda i, j: (i, j))]
    out_shape = [jax.ShapeDtypeStruct((m, n), out_dtype)]
    scratch = []
    if emit_norm:
        out_specs.append(pl.BlockSpec((tm, k), lambda i, j: (i, 0)))
        out_shape.append(jax.ShapeDtypeStruct((m, k), _MXU_DTYPE))
    elif norm_w is not None:
        scratch.append(pltpu.VMEM((tm, k), _MXU_DTYPE))
    outs = pl.pallas_call(
        functools.partial(_matmul_kernel, act=act, act_tile0=act_col0 // tn, has_norm=norm_w is not None,
                          has_res=res is not None, wt=wt),
        grid=(m // tm, n // tn),
        in_specs=in_specs,
        out_specs=out_specs,
        out_shape=out_shape,
        scratch_shapes=scratch,
        compiler_params=_params(("parallel", "arbitrary")),
        name="matmul",
    )(*args)
    return outs if emit_norm else outs[0]


def _matmul_splitk_kernel(a_ref, w_ref, r_ref, *rest, nk):
    o_ref = rest[-1]

    @pl.when(pl.program_id(2) == 0)
    def _():
        o_ref[...] = r_ref[...]

    o_ref[...] += jnp.dot(a_ref[...], w_ref[...].astype(_MXU_DTYPE), preferred_element_type=_F32)
    if len(rest) == 2:
        @pl.when(pl.program_id(2) == nk - 1)
        def _():
            o_ref[...] = _rms(o_ref[...], rest[0][...])


def _matmul_residual(a, w, layer, res, out_norm_w=None):
    m, k = a.shape
    n = w.shape[2]
    for nk in (1, 2, 4, 8):
        tm, tn = _matmul_tiles(m, k // nk, n, a.dtype.itemsize, w.dtype.itemsize, 4, 0, True)
        if (tn == n and tm >= min(m, 1024)) if out_norm_w is not None else (
                tm * tn >= min(m, 1024) * min(n, 1024)):
            break
    if nk == 1 and out_norm_w is None:
        return _matmul(a, w, layer, res=res, out_dtype=_F32)
    tk = k // nk
    in_specs = [pl.BlockSpec((tm, tk), lambda i, j, kk: (i, kk)),
                pl.BlockSpec((None, tk, tn), lambda i, j, kk: (layer, kk, j)),
                pl.BlockSpec((tm, tn), lambda i, j, kk: (i, j))]
    args = [a, w, res]
    if out_norm_w is not None:
        assert tn == n
        in_specs.append(pl.BlockSpec((1, n), lambda i, j, kk: (0, 0)))
        args.append(out_norm_w.reshape(1, n).astype(_F32))
    return pl.pallas_call(
        functools.partial(_matmul_splitk_kernel, nk=nk),
        grid=(m // tm, n // tn, nk),
        in_specs=in_specs,
        out_specs=pl.BlockSpec((tm, tn), lambda i, j, kk: (i, j)),
        out_shape=jax.ShapeDtypeStruct((m, n), _F32),
        compiler_params=_params(("parallel", "parallel", "arbitrary")),
        name="matmul_splitk",
    )(*args)


def _proj_conv_kernel(u_ref, halo_ref, w_ref, cw_ref, cb_ref, o_ref, an_ref, *, tiles_per_seq):
    kw = cw_ref.shape[0]

    @pl.when(pl.program_id(1) == 0)
    def _():
        halo = halo_ref[...]
        an_ref[0:_HALO, :] = jnp.where(pl.program_id(0) % tiles_per_seq == 0, jnp.zeros_like(halo), halo)
        an_ref[_HALO:, :] = u_ref[...]

    wb = w_ref[...].astype(_MXU_DTYPE)
    for c0 in range(0, o_ref.shape[1], _MXU_WIDTH):
        cols = pl.ds(c0, _MXU_WIDTH)
        acc = lax.dot_general(an_ref[...], wb[c0:c0 + _MXU_WIDTH, :], _NT, preferred_element_type=_F32)
        tm = o_ref.shape[0]
        out = cb_ref[:, cols] + cw_ref[kw - 1:kw, cols] * acc[_HALO:, :]
        for s in range(1, kw):
            out = out + cw_ref[kw - 1 - s:kw - s, cols] * acc[_HALO - s:_HALO - s + tm, :]
        o_ref[:, cols] = _silu(out).astype(o_ref.dtype)


def _proj_conv(u, w, layer, col0, conv_w, conv_b, seq, out_dtype):
    m, k = u.shape
    kw, n = conv_w.shape
    tm, tn = _matmul_tiles(m, k, math.gcd(n, col0), u.dtype.itemsize, w.dtype.itemsize,
                           jnp.dtype(out_dtype).itemsize, 1, False, conv_taps=kw, row_cap=seq)
    assert seq % tm == 0 and tm % _HALO == 0 and tn % _MXU_WIDTH == 0 and kw - 1 <= _HALO
    return pl.pallas_call(
        functools.partial(_proj_conv_kernel, tiles_per_seq=seq // tm),
        grid=(m // tm, n // tn),
        in_specs=[pl.BlockSpec((tm, k), lambda i, j: (i, 0)),
                  pl.BlockSpec((_HALO, k), lambda i, j: (jnp.maximum(i * (tm // _HALO) - 1, 0), 0)),
                  pl.BlockSpec((None, tn, k), lambda i, j: (layer, col0 // tn + j, 0)),
                  pl.BlockSpec((kw, tn), lambda i, j: (0, j)),
                  pl.BlockSpec((1, tn), lambda i, j: (0, j))],
        out_specs=pl.BlockSpec((tm, tn), lambda i, j: (i, j)),
        out_shape=jax.ShapeDtypeStruct((m, n), out_dtype),
        scratch_shapes=[pltpu.VMEM((tm + _HALO, k), _MXU_DTYPE)],
        compiler_params=_params(("parallel", "arbitrary")),
        name="proj_conv",
    )(u, u, w, conv_w, conv_b)


def _softplus(x):
    return jnp.maximum(x, 0.0) + jnp.log(1.0 + jnp.exp(-jnp.abs(x)))


def _chunk_cumsum(x, chunk):
    rows = x.shape[0]
    r = lax.broadcasted_iota(jnp.int32, (chunk, chunk), 0)
    c = lax.broadcasted_iota(jnp.int32, (chunk, chunk), 1)
    tri = jnp.where(c <= r, 1.0, 0.0).astype(_MXU_DTYPE)
    out = []
    for i in range(rows // chunk):
        rem = x[i * chunk:(i + 1) * chunk, :]
        acc = None
        for _ in range(3):
            piece = rem.astype(_MXU_DTYPE)
            d = jnp.dot(tri, piece, preferred_element_type=_F32)
            acc = d if acc is None else acc + d
            rem = rem - piece.astype(_F32)
        out.append(acc)
    return jnp.concatenate(out, axis=0) if len(out) > 1 else out[0]


def _row_bcast(ref, row, n):
    return jnp.broadcast_to(ref[pl.ds(row, 1), :], (n, ref.shape[1]))


def _tail_rows(w_ref, n_valid):
    row = lax.broadcasted_iota(jnp.int32, w_ref.shape, 0)
    return jnp.where(row < n_valid, w_ref[...], 0.0).astype(_MXU_DTYPE)


def _gla_gate_kernel(u_ref, wgr_ref, wup_ref, b_ref, o_ref, *, chunk, rank):
    gr = lax.dot_general(u_ref[...], _tail_rows(wgr_ref, rank), _NT, preferred_element_type=_F32)
    pre = jnp.dot(gr.astype(_MXU_DTYPE), wup_ref[...], preferred_element_type=_F32) + b_ref[...]
    o_ref[...] = _chunk_cumsum(_softplus(-pre) * (-_LOG2E / _GLA_GATE_NORMALIZER), chunk)


def _gla_gate(u, wt_in, layer, col0, rank, w_up, b_up, rows=1024):
    m, d = u.shape
    kd = w_up.shape[1]
    rows = min(rows, m)
    assert col0 % _LANES == 0 and col0 + rank == wt_in.shape[1] and rank <= _LANES
    return pl.pallas_call(
        functools.partial(_gla_gate_kernel, chunk=_CHUNK, rank=rank),
        grid=(m // rows,),
        in_specs=[pl.BlockSpec((rows, d), lambda i: (i, 0)),
                  pl.BlockSpec((None, _LANES, d), lambda i: (layer, col0 // _LANES, 0)),
                  pl.BlockSpec(w_up.shape, lambda i: (0, 0)),
                  pl.BlockSpec((1, kd), lambda i: (0, 0))],
        out_specs=pl.BlockSpec((rows, kd), lambda i: (i, 0)),
        out_shape=jax.ShapeDtypeStruct((m, kd), _F32),
        compiler_params=_params(("parallel",)),
        name="gla_gate",
    )(u, wt_in, w_up, b_up)


def _gla_levels(c, nd):
    i = lax.broadcasted_iota(jnp.int32, (c, c), 0)
    j = lax.broadcasted_iota(jnp.int32, (c, c), 1)
    x = i ^ j
    lvl = jnp.zeros((c, c), jnp.int32)
    s, level = nd, 1
    while s < c:
        lvl = jnp.where(x >= s, level, lvl)
        s, level = 2 * s, level + 1
    return jnp.where(j > i, -1, lvl)


def _gla_scan_head(q_ref, k_ref, v_ref, gs_ref, b_ref, onw_ref, lvl_ref, o_ref, st_ref, qf_ref, kf_ref, scale):
    c, dk = q_ref.shape
    qf_ref[...] = q_ref[...].astype(_F32) * scale
    kf_ref[...] = k_ref[...].astype(_F32)
    q = qf_ref[...]
    k = kf_ref[...]
    v = v_ref[...]
    bc = b_ref[...]
    st = st_ref[...]
    b_last = jnp.concatenate([_row_bcast(b_ref, c - 1, 8)] * (c // 8), axis=0)

    o = lax.dot_general((q * jnp.exp2(bc)).astype(_MXU_DTYPE), st.astype(_MXU_DTYPE), _NT,
                        preferred_element_type=_F32)
    k_dec = (k * jnp.exp2(b_last - bc)).astype(_MXU_DTYPE)

    nd = _GLA_DIAG
    lane = lax.broadcasted_iota(jnp.int32, (nd, c), 1)
    pieces = []
    for m in range(c // nd):
        qb = qf_ref[m * nd:(m + 1) * nd, :]
        bb = b_ref[m * nd:(m + 1) * nd, :]
        acc = jnp.zeros((nd, c), _F32)
        for j in range(nd):
            kj = _row_bcast(kf_ref, m * nd + j, nd)
            bj = _row_bcast(b_ref, m * nd + j, nd)
            term = qb * kj * jnp.exp2(jnp.minimum(bb - bj, 0.0))
            acc = jnp.where(lane == m * nd + j, jnp.sum(term, axis=1, keepdims=True), acc)
        pieces.append(acc)
    lvl = lvl_ref[...]
    att = jnp.where(lvl == 0, jnp.concatenate(pieces, axis=0), 0.0)

    s, level = nd, 1
    while s < c:
        prev = [jnp.zeros((s, dk), _F32)]
        end = []
        for m in range(c // s):
            if m:
                prev += [_row_bcast(b_ref, m * s - 1, 8)] * (s // 8)
            end += [_row_bcast(b_ref, (m + 1) * s - 1, 8)] * (s // 8)
        qs = (q * jnp.exp2(bc - jnp.concatenate(prev, axis=0))).astype(_MXU_DTYPE)
        ks = (k * jnp.exp2(jnp.concatenate(end, axis=0) - bc)).astype(_MXU_DTYPE)
        att = jnp.where(lvl == level, lax.dot_general(qs, ks, _NT, preferred_element_type=_F32), att)
        s, level = 2 * s, level + 1

    o = o + jnp.dot(att.astype(_MXU_DTYPE), v, preferred_element_type=_F32)
    st_ref[...] = st * jnp.exp2(b_last[:1, :]) + lax.dot_general(
        v, k_dec, _TN, preferred_element_type=_F32)

    ms = jnp.mean(o * o, axis=-1, keepdims=True)
    on = (o * lax.rsqrt(ms + _RMS_EPS)) * onw_ref[...]
    o_ref[...] = (on * gs_ref[...].astype(_F32)).astype(o_ref.dtype)


def _gla_scan_kernel(q_ref, k_ref, v_ref, gs_ref, b_ref, onw_ref, lvl_ref, o_ref, st_ref, qf_ref, kf_ref,
                     *, scale, dk, dv):
    @pl.when(pl.program_id(2) == 0)
    def _():
        st_ref[...] = jnp.zeros_like(st_ref)

    c = lvl_ref.shape[0]
    hb = st_ref.shape[0]
    for s in range(q_ref.shape[0] // c):
        rs = pl.ds(s * c, c)
        for h in range(hb):
            ks, vs = pl.ds(h * dk, dk), pl.ds(h * dv, dv)
            _gla_scan_head(q_ref.at[rs, ks], k_ref.at[rs, ks], v_ref.at[rs, vs], gs_ref.at[rs, vs],
                           b_ref.at[rs, ks], onw_ref, lvl_ref, o_ref.at[rs, vs], st_ref.at[h],
                           qf_ref.at[s * hb + h], kf_ref.at[s * hb + h], scale)


def _gla_scan(proj, bcum, o_norm_w, batch, seq, heads, dk, dv):
    m = proj.shape[0]
    c = _CHUNK
    hb = _GLA_HEADS_PER_STEP if heads % _GLA_HEADS_PER_STEP == 0 else 1
    assert c == _LANES and dk % _LANES == 0 and dv % _LANES == 0 and (2 * heads * dk) % (hb * dv) == 0
    cs = _GLA_CHUNKS_PER_STEP if seq % (_GLA_CHUNKS_PER_STEP * c) == 0 else 1
    tr = cs * c
    nt = seq // tr
    kd, vd = heads * dk, heads * dv
    wk, wv = hb * dk, hb * dv
    k_blk, v_blk, g_blk = kd // wk, (2 * kd) // wv, (2 * kd + vd) // wv

    def rows(b, h, t):
        return b * nt + t

    return pl.pallas_call(
        functools.partial(_gla_scan_kernel, scale=dk ** -0.5, dk=dk, dv=dv),
        grid=(batch, heads // hb, nt),
        in_specs=[pl.BlockSpec((tr, wk), lambda b, h, t: (rows(b, h, t), h)),
                  pl.BlockSpec((tr, wk), lambda b, h, t: (rows(b, h, t), k_blk + h)),
                  pl.BlockSpec((tr, wv), lambda b, h, t: (rows(b, h, t), v_blk + h)),
                  pl.BlockSpec((tr, wv), lambda b, h, t: (rows(b, h, t), g_blk + h)),
                  pl.BlockSpec((tr, wk), lambda b, h, t: (rows(b, h, t), h)),
                  pl.BlockSpec((1, dv), lambda b, h, t: (0, 0)),
                  pl.BlockSpec((c, c), lambda b, h, t: (0, 0))],
        out_specs=pl.BlockSpec((tr, wv), lambda b, h, t: (rows(b, h, t), h)),
        out_shape=jax.ShapeDtypeStruct((m, vd), _MXU_DTYPE),
        scratch_shapes=[pltpu.VMEM((hb, dv, dk), _F32),
                        pltpu.VMEM((cs * hb, c, dk), _F32),
                        pltpu.VMEM((cs * hb, c, dk), _F32)],
        compiler_params=_params(("parallel", "parallel", "arbitrary")),
        name="gla_scan",
    )(proj, proj, proj, proj, bcum, o_norm_w.reshape(1, dv).astype(_F32), _gla_levels(c, _GLA_DIAG))


def _gla_mixer(h, norm_w, batch, seq, layer, wt_in, w_gk_up, b_gk_up, o_norm_w, w_out):
    rank, kd = w_gk_up.shape
    vd = (wt_in.shape[1] - rank - 2 * kd) // 2
    dv = o_norm_w.shape[0]
    heads = vd // dv
    n_main = 2 * kd + 2 * vd
    w_up = jnp.pad(w_gk_up, ((0, _LANES - rank), (0, 0))).astype(_MXU_DTYPE)
    proj, u = _matmul(h, wt_in, layer, wt=True, n=n_main, norm_w=norm_w, emit_norm=True, act="silu",
                      act_col0=2 * kd + vd, out_dtype=_MXU_DTYPE)
    bcum = _gla_gate(u, wt_in, layer, n_main, rank, w_up, b_gk_up.reshape(1, kd).astype(_F32))
    o = _gla_scan(proj, bcum, o_norm_w, batch, seq, heads, kd // heads, dv)
    return _matmul_residual(o, w_out.astype(_MXU_DTYPE), layer, h)


def _ssd_gate_kernel(u_ref, wdt_ref, bias_ref, alog_ref, acol_ref, dcol_ref, arow_ref, drow_ref,
                     *, chunk, heads):
    raw = lax.dot_general(u_ref[...], _tail_rows(wdt_ref, heads), _NT, preferred_element_type=_F32)
    dt = _softplus(raw + bias_ref[...])
    a = _chunk_cumsum(dt * (-jnp.exp(alog_ref[...])) * _LOG2E, chunk)
    acol_ref[...] = a
    dcol_ref[...] = dt
    for i in range(a.shape[0] // chunk):
        arow_ref[i] = a[i * chunk:(i + 1) * chunk, :].T[:heads, :]
        drow_ref[i] = dt[i * chunk:(i + 1) * chunk, :].T[:heads, :]


def _ssd_gate(u, wt_in, layer, col0, dt_bias, a_log, heads, rows=1024):
    m, d = u.shape
    c = _CHUNK
    rows = min(rows, m)
    g = _SSD_GROUPS
    hg = heads // g
    assert col0 % _LANES == 0 and col0 + heads == wt_in.shape[1] and heads <= _LANES
    col = jax.ShapeDtypeStruct((m, _LANES), _F32)
    row = jax.ShapeDtypeStruct((m // c, heads, c), _F32)
    col_spec = pl.BlockSpec((rows, _LANES), lambda i: (i, 0))
    row_spec = pl.BlockSpec((rows // c, heads, c), lambda i: (i, 0, 0))
    return pl.pallas_call(
        functools.partial(_ssd_gate_kernel, chunk=c, heads=heads),
        grid=(m // rows,),
        in_specs=[pl.BlockSpec((rows, d), lambda i: (i, 0)),
                  pl.BlockSpec((None, _LANES, d), lambda i: (layer, col0 // _LANES, 0)),
                  pl.BlockSpec((1, _LANES), lambda i: (0, 0)),
                  pl.BlockSpec((1, _LANES), lambda i: (0, 0))],
        out_specs=[col_spec, col_spec, row_spec, row_spec],
        out_shape=[col, col, row, row],
        compiler_params=_params(("parallel",)),
        name="ssd_gate",
    )(u, wt_in, dt_bias, a_log)


def _head_expanders(groups, hg, c, head_dim):
    def one_hot(rows, width):
        row = lax.broadcasted_iota(jnp.int32, (groups, rows, hg * width), 1) % _LANES
        col = lax.broadcasted_iota(jnp.int32, (groups, rows, hg * width), 2)
        grp = lax.broadcasted_iota(jnp.int32, (groups, rows, hg * width), 0)
        return (row == grp * hg + col // width).astype(_MXU_DTYPE)
    return one_hot(2 * _LANES, c), one_hot(_LANES, head_dim)


def _expand_heads(bcasts, head_dim):
    c = bcasts[0].shape[0]
    per = _LANES // head_dim
    lane = lax.broadcasted_iota(jnp.int32, (c, _LANES), 1)
    pieces = []
    for p in range(len(bcasts) // per):
        out = bcasts[p * per]
        for i in range(1, per):
            out = jnp.where(lane >= i * head_dim, bcasts[p * per + i], out)
        pieces.append(out)
    return jnp.concatenate(pieces, axis=1)


def _ssd_scan_group(zs_ref, x_ref, bm_ref, cm_ref, acol_ref, dcol_ref, esc_ref, ehp_ref, arow_ref, drow_ref,
                    dskip_ref, gnw_ref, o_ref, st_ref, head_dim):
    c = zs_ref.shape[0]
    hg = x_ref.shape[1] // head_dim
    xs = x_ref[...].astype(_F32)
    bmx = bm_ref[...]
    cmx = cm_ref[...]

    acol = acol_ref[...]
    a_last = acol[c - 1:c, :]
    w_state = (jnp.exp2(a_last - acol) * dcol_ref[...]).astype(_MXU_DTYPE)
    a_hi = acol.astype(_MXU_DTYPE)
    a_lo = (acol - a_hi.astype(_F32)).astype(_MXU_DTYPE)
    a_sc = jnp.dot(jnp.concatenate([a_hi, a_lo], axis=1), esc_ref[...],
                   preferred_element_type=_F32)
    w_hp = jnp.dot(w_state, ehp_ref[...], preferred_element_type=_F32)

    per = _LANES // head_dim
    width = per * c
    cb = lax.dot_general(cmx, jnp.concatenate([bmx] * per, axis=0), _NT,
                         preferred_element_type=_F32)
    ri = lax.broadcasted_iota(jnp.int32, (c, width), 0)
    ci = lax.broadcasted_iota(jnp.int32, (c, width), 1) & (c - 1)
    causal = ci <= ri
    lane = lax.broadcasted_iota(jnp.int32, (c, _LANES), 1)
    y_pieces = []
    for p in range(hg // per):
        ac = a_sc[:, p * width:(p + 1) * width]
        ar = arow_ref[:, p * width:(p + 1) * width]
        dr = drow_ref[:, p * width:(p + 1) * width]
        decay = jnp.exp2(jnp.where(causal, ac - ar, _NEG_BIG))
        sc = (cb * decay * dr).astype(_MXU_DTYPE)
        xp = x_ref[:, p * _LANES:(p + 1) * _LANES]
        bd = jnp.concatenate(
            [jnp.where((lane >= i * head_dim) & (lane < (i + 1) * head_dim), xp, jnp.zeros_like(xp))
             for i in range(per)], axis=0)
        y_pieces.append(jnp.dot(sc, bd, preferred_element_type=_F32))
    y = jnp.concatenate(y_pieces, axis=1)

    st = st_ref[...]
    ea = jnp.exp2(_expand_heads([a_sc[:, h * c:(h + 1) * c] for h in range(hg)], head_dim))
    y = y + jnp.dot(cmx, st.astype(_MXU_DTYPE), preferred_element_type=_F32) * ea
    xw = (xs * w_hp).astype(_MXU_DTYPE)
    st_ref[...] = st * ea[c - 1:c, :] + lax.dot_general(bmx, xw, _TN, preferred_element_type=_F32)

    y = y + dskip_ref[...] * xs
    y = y * zs_ref[...].astype(_F32)
    ms = jnp.mean(y * y, axis=-1, keepdims=True)
    o_ref[...] = ((y * lax.rsqrt(ms + _RMS_EPS)) * gnw_ref[...]).astype(o_ref.dtype)


def _ssd_scan_kernel(zs_ref, x_ref, bm_ref, cm_ref, acol_ref, dcol_ref, esc_ref, ehp_ref, arow_ref, drow_ref,
                     dskip_ref, gnw_ref, o_ref, st_ref, *, head_dim):
    @pl.when(pl.program_id(2) == 0)
    def _():
        st_ref[...] = jnp.zeros_like(st_ref)

    gb, n, gw = st_ref.shape
    cs = arow_ref.shape[0]
    c = zs_ref.shape[0] // cs
    hg = gw // head_dim
    for s in range(cs):
        ts = pl.ds(s * c, c)
        for g in range(gb):
            xs_, ns_, rs_ = pl.ds(g * gw, gw), pl.ds(g * n, n), pl.ds(g * hg * c, hg * c)
            _ssd_scan_group(zs_ref.at[ts, xs_], x_ref.at[ts, xs_], bm_ref.at[ts, ns_], cm_ref.at[ts, ns_],
                            acol_ref.at[ts, :], dcol_ref.at[ts, :], esc_ref.at[g], ehp_ref.at[g],
                            arow_ref.at[s, :, rs_], drow_ref.at[s, :, rs_],
                            dskip_ref.at[:, xs_], gnw_ref.at[:, xs_], o_ref.at[ts, xs_], st_ref.at[g], head_dim)


def _ssd_scan(zs, xbc, acol, dcol, arow, drow, dskip, gnorm_w, batch, seq, inner, heads, n_state):
    m = zs.shape[0]
    c = _CHUNK
    g = _SSD_GROUPS
    gb = _SSD_GROUPS_PER_STEP if g % _SSD_GROUPS_PER_STEP == 0 else 1
    hg = heads // g
    p = inner // heads
    gw = hg * p
    sw, sn = gb * gw, gb * n_state
    assert c == _LANES and gw % _LANES == 0 and n_state % _LANES == 0 and _LANES % p == 0
    assert inner % sn == 0 and (g * n_state) % sn == 0
    cs = _SSD_CHUNKS_PER_STEP if seq % (_SSD_CHUNKS_PER_STEP * c) == 0 else 1
    tr = cs * c
    nt = seq // tr
    b_blk = inner // sn
    c_blk = (inner + g * n_state) // sn

    def rows(b, t):
        return b * nt + t

    in_specs = [
        pl.BlockSpec((tr, sw), lambda b, gi, t: (rows(b, t), gi)),
        pl.BlockSpec((tr, sw), lambda b, gi, t: (rows(b, t), gi)),
        pl.BlockSpec((tr, sn), lambda b, gi, t: (rows(b, t), b_blk + gi)),
        pl.BlockSpec((tr, sn), lambda b, gi, t: (rows(b, t), c_blk + gi)),
        pl.BlockSpec((tr, _LANES), lambda b, gi, t: (rows(b, t), 0)),
        pl.BlockSpec((tr, _LANES), lambda b, gi, t: (rows(b, t), 0)),
        pl.BlockSpec((gb, 2 * _LANES, hg * c), lambda b, gi, t: (gi, 0, 0)),
        pl.BlockSpec((gb, _LANES, gw), lambda b, gi, t: (gi, 0, 0)),
        pl.BlockSpec((cs, 1, gb * hg * c), lambda b, gi, t: (rows(b, t), 0, gi)),
        pl.BlockSpec((cs, 1, gb * hg * c), lambda b, gi, t: (rows(b, t), 0, gi)),
        pl.BlockSpec((1, sw), lambda b, gi, t: (0, gi)),
        pl.BlockSpec((1, sw), lambda b, gi, t: (0, gi)),
    ]
    return pl.pallas_call(
        functools.partial(_ssd_scan_kernel, head_dim=p),
        grid=(batch, g // gb, nt),
        in_specs=in_specs,
        out_specs=pl.BlockSpec((tr, sw), lambda b, gi, t: (rows(b, t), gi)),
        out_shape=jax.ShapeDtypeStruct((m, inner), _MXU_DTYPE),
        scratch_shapes=[pltpu.VMEM((gb, n_state, gw), _F32)],
        compiler_params=_params(("parallel", "parallel", "arbitrary")),
        name="ssd_scan",
    )(zs, xbc, xbc, xbc, acol, dcol, *_head_expanders(g, hg, c, p), arow, drow, dskip, gnorm_w)


def _ssd_mixer(h, norm_w, batch, seq, layer, wt_in, conv_w, conv_b, dt_bias, a_log, d_skip, gnorm_w, w_out):
    heads = dt_bias.shape[0]
    conv_dim = conv_w.shape[1]
    inner = wt_in.shape[1] - conv_dim - heads
    n_state = (conv_dim - inner) // (2 * _SSD_GROUPS)
    n_main = inner + conv_dim
    m = h.shape[0]
    pad1 = lambda v: jnp.pad(v.astype(_F32), (0, _LANES - heads)).reshape(1, _LANES)
    zs, u = _matmul(h, wt_in, layer, wt=True, n=inner, norm_w=norm_w, emit_norm=True, act="silu",
                    out_dtype=_MXU_DTYPE)
    xbc = _proj_conv(u, wt_in, layer, inner, conv_w.astype(_F32), conv_b.reshape(1, conv_dim).astype(_F32),
                     seq, _MXU_DTYPE)
    acol, dcol, arow, drow = _ssd_gate(u, wt_in, layer, n_main, pad1(dt_bias), pad1(a_log), heads)
    arow = arow.reshape(m // _CHUNK, 1, heads * _CHUNK)
    drow = drow.reshape(m // _CHUNK, 1, heads * _CHUNK)
    dskip = jnp.repeat(d_skip.astype(_F32), inner // heads).reshape(1, inner)
    y = _ssd_scan(zs, xbc, acol, dcol, arow, drow, dskip, gnorm_w.reshape(1, inner).astype(_F32),
                  batch, seq, inner, heads, n_state)
    return _matmul_residual(y, w_out.astype(_MXU_DTYPE), layer, h)


def kernel(x, mixer_norm_w, gla_w_in, gla_w_gk_up, gla_b_gk_up, gla_o_norm_w, gla_w_out, ssd_w_in, ssd_conv_w, ssd_conv_b, ssd_dt_bias, ssd_a_log, ssd_d_skip, ssd_gnorm_w, ssd_w_out, mlp_norm_w, mlp_w_fc1, mlp_w_fc2, final_norm_w):
    batch, seq, d = x.shape
    h = x.reshape(batch * seq, d)
    w_fc1 = mlp_w_fc1.astype(_MXU_DTYPE)
    w_fc2 = mlp_w_fc2.astype(_MXU_DTYPE)
    gla_wt_in = jnp.swapaxes(gla_w_in, 1, 2).astype(_MXU_DTYPE)
    ssd_wt_in = jnp.swapaxes(ssd_w_in, 1, 2).astype(_MXU_DTYPE)
    depth = mixer_norm_w.shape[0]
    assert depth >= 1
    for i in range(depth):
        j = i // 2
        if i % 2 == 0:
            h = _gla_mixer(h, mixer_norm_w[i], batch, seq, j, gla_wt_in, gla_w_gk_up[j], gla_b_gk_up[j],
                           gla_o_norm_w[j], gla_w_out)
        else:
            h = _ssd_mixer(h, mixer_norm_w[i], batch, seq, j, ssd_wt_in, ssd_conv_w[j], ssd_conv_b[j],
                           ssd_dt_bias[j], ssd_a_log[j], ssd_d_skip[j], ssd_gnorm_w[j], ssd_w_out)
        hidden = _matmul(h, w_fc1, i, norm_w=mlp_norm_w[i], act="relu2", out_dtype=_MXU_DTYPE)
        h = _matmul_residual(hidden, w_fc2, i, h, final_norm_w if i == depth - 1 else None)
    return h.reshape(batch, seq, d)
```

```python
import functools
import math

import jax
import jax.numpy as jnp
from jax import lax
from jax.experimental import pallas as pl
from jax.experimental.pallas import tpu as pltpu

_F32 = jnp.float32
_MXU_DTYPE = jnp.bfloat16
_RMS_EPS = 1e-5
_GLA_GATE_NORMALIZER = 16.0
_SSD_GROUPS = 8
_CHUNK = 128
_LANES = 128
_GLA_DIAG = 8
_GLA_HEADS_PER_STEP = 4
_GLA_CHUNKS_PER_STEP = 4
_SSD_CHUNKS_PER_STEP = 2
_SSD_GROUPS_PER_STEP = 8
_LOG2E = math.log2(math.e)
_NEG_BIG = -1e30
_HALO = 16
_MXU_WIDTH = 256
_VMEM_LIMIT_BYTES = 60 * 1024 * 1024
_MATMUL_VMEM_BUDGET = 58 * 1024 * 1024

_NN = (((1,), (0,)), ((), ()))
_NT = (((1,), (1,)), ((), ()))
_TN = (((0,), (0,)), ((), ()))


def _params(sem, flags=None):
    return pltpu.CompilerParams(dimension_semantics=sem, vmem_limit_bytes=_VMEM_LIMIT_BYTES, flags=flags)


def _rms(x, w):
    ms = jnp.mean(x * x, axis=-1, keepdims=True)
    return (x * lax.rsqrt(ms + _RMS_EPS)) * w


def _silu(x):
    half = 0.5 * x
    return half + half * jnp.tanh(half)


def _matmul_kernel(*refs, act, act_tile0, has_norm, has_res, wt):
    it = iter(refs)
    a_ref = next(it)
    nw_ref = next(it) if has_norm else None
    w_ref = next(it)
    r_ref = next(it) if has_res else None
    o_ref = next(it)
    if has_norm:
        an_ref = next(it)

        @pl.when(pl.program_id(1) == 0)
        def _():
            an_ref[...] = _rms(a_ref[...], nw_ref[...]).astype(_MXU_DTYPE)

        a = an_ref[...]
    else:
        a = a_ref[...]
    w = w_ref[...].astype(_MXU_DTYPE)
    acc = lax.dot_general(a, w, _NT if wt else _NN, preferred_element_type=_F32)
    if act == "relu2":
        act_acc = jnp.square(jnp.maximum(acc, 0.0))
    elif act == "silu":
        act_acc = _silu(acc)
    if act is not None:
        acc = act_acc if act_tile0 == 0 else jnp.where(pl.program_id(1) >= act_tile0, act_acc, acc)
    if has_res:
        acc = r_ref[...] + acc
    o_ref[...] = acc.astype(o_ref.dtype)


def _divisor_tile(n, pref, align):
    if n <= pref:
        return n
    t = (pref // align) * align
    while n % t:
        t -= align
    return t


def _matmul_tiles(m, k, n, a_bytes, w_bytes, out_bytes, norm_copies, has_res, conv_taps=0, row_cap=None):
    mxu_bytes = jnp.dtype(_MXU_DTYPE).itemsize
    for pm, pn in ((1024, 2048), (1024, 1024), (1024, 512), (1024, 256), (512, 256), (256, 256), (128, 128)):
        pm = pm if row_cap is None else min(pm, row_cap)
        tm, tn = _divisor_tile(m, pm, 8), _divisor_tile(n, pn, _LANES)
        need = 2 * tm * k * a_bytes + 2 * k * tn * w_bytes + 2 * tm * tn * out_bytes
        need += (2 + conv_taps) * tm * _MXU_WIDTH * 4
        need += 2 * tm * tn * 4 if has_res else 0
        need += norm_copies * tm * k * mxu_bytes
        need += k * tn * mxu_bytes if w_bytes != mxu_bytes else 0
        if need <= _MATMUL_VMEM_BUDGET:
            break
    return tm, tn


def _matmul(a, w, layer, *, wt=False, n=None, norm_w=None, emit_norm=False, act=None, act_col0=0, res=None,
            out_dtype):
    m, k = a.shape
    n = w.shape[1 if wt else 2] if n is None else n
    tm, tn = _matmul_tiles(m, k, math.gcd(n, act_col0), a.dtype.itemsize, w.dtype.itemsize,
                           jnp.dtype(out_dtype).itemsize, (norm_w is not None) + emit_norm, res is not None)
    in_specs = [pl.BlockSpec((tm, k), lambda i, j: (i, 0))]
    args = [a]
    if norm_w is not None:
        in_specs.append(pl.BlockSpec((1, k), lambda i, j: (0, 0)))
        args.append(norm_w.reshape(1, k).astype(_F32))
    in_specs.append(pl.BlockSpec((None, tn, k), lambda i, j: (layer, j, 0)) if wt else
                    pl.BlockSpec((None, k, tn), lambda i, j: (layer, 0, j)))
    args.append(w)
    if res is not None:
        in_specs.append(pl.BlockSpec((tm, tn), lambda i, j: (i, j)))
        args.append(res)
    out_specs = [pl.BlockSpec((tm, tn), lambda i, j: (i, j))]
    out_shape = [jax.ShapeDtypeStruct((m, n), out_dtype)]
    scratch = []
    if emit_norm:
        out_specs.append(pl.BlockSpec((tm, k), lambda i, j: (i, 0)))
        out_shape.append(jax.ShapeDtypeStruct((m, k), _MXU_DTYPE))
    elif norm_w is not None:
        scratch.append(pltpu.VMEM((tm, k), _MXU_DTYPE))
    outs = pl.pallas_call(
        functools.partial(_matmul_kernel, act=act, act_tile0=act_col0 // tn, has_norm=norm_w is not None,
                          has_res=res is not None, wt=wt),
        grid=(m // tm, n // tn),
        in_specs=in_specs,
        out_specs=out_specs,
        out_shape=out_shape,
        scratch_shapes=scratch,
        compiler_params=_params(("parallel", "arbitrary")),
        name="matmul",
    )(*args)
    return outs if emit_norm else outs[0]


def _matmul_splitk_kernel(a_ref, w_ref, r_ref, *rest, nk):
    o_ref = rest[-1]

    @pl.when(pl.program_id(2) == 0)
    def _():
        o_ref[...] = r_ref[...]

    o_ref[...] += jnp.dot(a_ref[...], w_ref[...].astype(_MXU_DTYPE), preferred_element_type=_F32)
    if len(rest) == 2:
        @pl.when(pl.program_id(2) == nk - 1)
        def _():
            o_ref[...] = _rms(o_ref[...], rest[0][...])


def _matmul_residual(a, w, layer, res, out_norm_w=None):
    m, k = a.shape
    n = w.shape[2]
    for nk in (1, 2, 4, 8):
        tm, tn = _matmul_tiles(m, k // nk, n, a.dtype.itemsize, w.dtype.itemsize, 4, 0, True)
        if (tn == n and tm >= min(m, 1024)) if out_norm_w is not None else (
                tm * tn >= min(m, 1024) * min(n, 1024)):
            break
    if nk == 1 and out_norm_w is None:
        return _matmul(a, w, layer, res=res, out_dtype=_F32)
    tk = k // nk
    in_specs = [pl.BlockSpec((tm, tk), lambda i, j, kk: (i, kk)),
                pl.BlockSpec((None, tk, tn), lambda i, j, kk: (layer, kk, j)),
                pl.BlockSpec((tm, tn), lambda i, j, kk: (i, j))]
    args = [a, w, res]
    if out_norm_w is not None:
        assert tn == n
        in_specs.append(pl.BlockSpec((1, n), lambda i, j, kk: (0, 0)))
        args.append(out_norm_w.reshape(1, n).astype(_F32))
    return pl.pallas_call(
        functools.partial(_matmul_splitk_kernel, nk=nk),
        grid=(m // tm, n // tn, nk),
        in_specs=in_specs,
        out_specs=pl.BlockSpec((tm, tn), lambda i, j, kk: (i, j)),
        out_shape=jax.ShapeDtypeStruct((m, n), _F32),
        compiler_params=_params(("parallel", "parallel", "arbitrary")),
        name="matmul_splitk",
    )(*args)


def _proj_conv_kernel(u_ref, halo_ref, w_ref, cw_ref, cb_ref, o_ref, an_ref, *, tiles_per_seq):
    kw = cw_ref.shape[0]

    @pl.when(pl.program_id(1) == 0)
    def _():
        halo = halo_ref[...]
        an_ref[0:_HALO, :] = jnp.where(pl.program_id(0) % tiles_per_seq == 0, jnp.zeros_like(halo), halo)
        an_ref[_HALO:, :] = u_ref[...]

    wb = w_ref[...].astype(_MXU_DTYPE)
    for c0 in range(0, o_ref.shape[1], _MXU_WIDTH):
        cols = pl.ds(c0, _MXU_WIDTH)
        acc = lax.dot_general(an_ref[...], wb[c0:c0 + _MXU_WIDTH, :], _NT, preferred_element_type=_F32)
        tm = o_ref.shape[0]
        out = cb_ref[:, cols] + cw_ref[kw - 1:kw, cols] * acc[_HALO:, :]
        for s in range(1, kw):
            out = out + cw_ref[kw - 1 - s:kw - s, cols] * acc[_HALO - s:_HALO - s + tm, :]
        o_ref[:, cols] = _silu(out).astype(o_ref.dtype)


def _proj_conv(u, w, layer, col0, conv_w, conv_b, seq, out_dtype):
    m, k = u.shape
    kw, n = conv_w.shape
    tm, tn = _matmul_tiles(m, k, math.gcd(n, col0), u.dtype.itemsize, w.dtype.itemsize,
                           jnp.dtype(out_dtype).itemsize, 1, False, conv_taps=kw, row_cap=seq)
    assert seq % tm == 0 and tm % _HALO == 0 and tn % _MXU_WIDTH == 0 and kw - 1 <= _HALO
    return pl.pallas_call(
        functools.partial(_proj_conv_kernel, tiles_per_seq=seq // tm),
        grid=(m // tm, n // tn),
        in_specs=[pl.BlockSpec((tm, k), lambda i, j: (i, 0)),
                  pl.BlockSpec((_HALO, k), lambda i, j: (jnp.maximum(i * (tm // _HALO) - 1, 0), 0)),
                  pl.BlockSpec((None, tn, k), lambda i, j: (layer, col0 // tn + j, 0)),
                  pl.BlockSpec((kw, tn), lambda i, j: (0, j)),
                  pl.BlockSpec((1, tn), lambda i, j: (0, j))],
        out_specs=pl.BlockSpec((tm, tn), lambda i, j: (i, j)),
        out_shape=jax.ShapeDtypeStruct((m, n), out_dtype),
        scratch_shapes=[pltpu.VMEM((tm + _HALO, k), _MXU_DTYPE)],
        compiler_params=_params(("parallel", "arbitrary")),
        name="proj_conv",
    )(u, u, w, conv_w, conv_b)


def _softplus(x):
    return jnp.maximum(x, 0.0) + jnp.log(1.0 + jnp.exp(-jnp.abs(x)))


def _chunk_cumsum(x, chunk):
    rows = x.shape[0]
    r = lax.broadcasted_iota(jnp.int32, (chunk, chunk), 0)
    c = lax.broadcasted_iota(jnp.int32, (chunk, chunk), 1)
    tri = jnp.where(c <= r, 1.0, 0.0).astype(_MXU_DTYPE)
    out = []
    for i in range(rows // chunk):
        rem = x[i * chunk:(i + 1) * chunk, :]
        acc = None
        for _ in range(3):
            piece = rem.astype(_MXU_DTYPE)
            d = jnp.dot(tri, piece, preferred_element_type=_F32)
            acc = d if acc is None else acc + d
            rem = rem - piece.astype(_F32)
        out.append(acc)
    return jnp.concatenate(out, axis=0) if len(out) > 1 else out[0]


def _row_bcast(ref, row, n):
    return jnp.broadcast_to(ref[pl.ds(row, 1), :], (n, ref.shape[1]))


def _tail_rows(w_ref, n_valid):
    row = lax.broadcasted_iota(jnp.int32, w_ref.shape, 0)
    return jnp.where(row < n_valid, w_ref[...], 0.0).astype(_MXU_DTYPE)


def _gla_gate_kernel(u_ref, wgr_ref, wup_ref, b_ref, o_ref, *, chunk, rank):
    gr = lax.dot_general(u_ref[...], _tail_rows(wgr_ref, rank), _NT, preferred_element_type=_F32)
    pre = jnp.dot(gr.astype(_MXU_DTYPE), wup_ref[...], preferred_element_type=_F32) + b_ref[...]
    o_ref[...] = _chunk_cumsum(_softplus(-pre) * (-_LOG2E / _GLA_GATE_NORMALIZER), chunk)


def _gla_gate(u, wt_in, layer, col0, rank, w_up, b_up, rows=1024):
    m, d = u.shape
    kd = w_up.shape[1]
    rows = min(rows, m)
    assert col0 % _LANES == 0 and col0 + rank == wt_in.shape[1] and rank <= _LANES
    return pl.pallas_call(
        functools.partial(_gla_gate_kernel, chunk=_CHUNK, rank=rank),
        grid=(m // rows,),
        in_specs=[pl.BlockSpec((rows, d), lambda i: (i, 0)),
                  pl.BlockSpec((None, _LANES, d), lambda i: (layer, col0 // _LANES, 0)),
                  pl.BlockSpec(w_up.shape, lambda i: (0, 0)),
                  pl.BlockSpec((1, kd), lambda i: (0, 0))],
        out_specs=pl.BlockSpec((rows, kd), lambda i: (i, 0)),
        out_shape=jax.ShapeDtypeStruct((m, kd), _F32),
        compiler_params=_params(("parallel",)),
        name="gla_gate",
    )(u, wt_in, w_up, b_up)


def _gla_levels(c, nd):
    i = lax.broadcasted_iota(jnp.int32, (c, c), 0)
    j = lax.broadcasted_iota(jnp.int32, (c, c), 1)
    x = i ^ j
    lvl = jnp.zeros((c, c), jnp.int32)
    s, level = nd, 1
    while s < c:
        lvl = jnp.where(x >= s, level, lvl)
        s, level = 2 * s, level + 1
    return jnp.where(j > i, -1, lvl)


def _row_repeater(c, nd):
    r = lax.broadcasted_iota(jnp.int32, (c * nd, c), 0)
    t = lax.broadcasted_iota(jnp.int32, (c * nd, c), 1)
    return (r // nd == t).astype(_MXU_DTYPE)


def _gla_scan_head(q_ref, k_ref, v_ref, gs_ref, b_ref, onw_ref, lvl_ref, sel_ref, o_ref, st_ref, qf_ref, kf_ref,
                   scale):
    c, dk = q_ref.shape
    qf_ref[...] = q_ref[...].astype(_F32) * scale
    kf_ref[...] = k_ref[...].astype(_F32)
    q = qf_ref[...]
    k = kf_ref[...]
    v = v_ref[...]
    bc = b_ref[...]
    st = st_ref[...]
    b_last = jnp.concatenate([_row_bcast(b_ref, c - 1, 8)] * (c // 8), axis=0)

    o = lax.dot_general((q * jnp.exp2(bc)).astype(_MXU_DTYPE), st.astype(_MXU_DTYPE), _NT,
                        preferred_element_type=_F32)
    k_dec = (k * jnp.exp2(b_last - bc)).astype(_MXU_DTYPE)

    nd = _GLA_DIAG
    lane = lax.broadcasted_iota(jnp.int32, (nd, c), 1)
    k_rep = jnp.dot(sel_ref[...], k_ref[...], preferred_element_type=_F32)
    pieces = []
    for m in range(c // nd):
        qb = qf_ref[m * nd:(m + 1) * nd, :]
        bb = b_ref[m * nd:(m + 1) * nd, :]
        acc = jnp.zeros((nd, c), _F32)
        for j in range(nd):
            kj = k_rep[(m * nd + j) * nd:(m * nd + j + 1) * nd, :]
            bj = _row_bcast(b_ref, m * nd + j, nd)
            term = qb * kj * jnp.exp2(jnp.minimum(bb - bj, 0.0))
            acc = jnp.where(lane == m * nd + j, jnp.sum(term, axis=1, keepdims=True), acc)
        pieces.append(acc)
    lvl = lvl_ref[...]
    att = jnp.where(lvl == 0, jnp.concatenate(pieces, axis=0), 0.0)

    s, level = nd, 1
    while s < c:
        prev = [jnp.zeros((s, dk), _F32)]
        end = []
        for m in range(c // s):
            if m:
                prev += [_row_bcast(b_ref, m * s - 1, 8)] * (s // 8)
            end += [_row_bcast(b_ref, (m + 1) * s - 1, 8)] * (s // 8)
        qs = (q * jnp.exp2(bc - jnp.concatenate(prev, axis=0))).astype(_MXU_DTYPE)
        ks = (k * jnp.exp2(jnp.concatenate(end, axis=0) - bc)).astype(_MXU_DTYPE)
        att = jnp.where(lvl == level, lax.dot_general(qs, ks, _NT, preferred_element_type=_F32), att)
        s, level = 2 * s, level + 1

    o = o + jnp.dot(att.astype(_MXU_DTYPE), v, preferred_element_type=_F32)
    st_ref[...] = st * jnp.exp2(b_last[:1, :]) + lax.dot_general(
        v, k_dec, _TN, preferred_element_type=_F32)

    ms = jnp.mean(o * o, axis=-1, keepdims=True)
    on = (o * lax.rsqrt(ms + _RMS_EPS)) * onw_ref[...]
    o_ref[...] = (on * gs_ref[...].astype(_F32)).astype(o_ref.dtype)


def _gla_scan_kernel(q_ref, k_ref, v_ref, gs_ref, b_ref, onw_ref, lvl_ref, sel_ref, o_ref, st_ref, qf_ref, kf_ref,
                     *, scale, dk, dv):
    @pl.when(pl.program_id(2) == 0)
    def _():
        st_ref[...] = jnp.zeros_like(st_ref)

    c = lvl_ref.shape[0]
    hb = st_ref.shape[0]
    for s in range(q_ref.shape[0] // c):
        rs = pl.ds(s * c, c)
        for h in range(hb):
            ks, vs = pl.ds(h * dk, dk), pl.ds(h * dv, dv)
            _gla_scan_head(q_ref.at[rs, ks], k_ref.at[rs, ks], v_ref.at[rs, vs], gs_ref.at[rs, vs],
                           b_ref.at[rs, ks], onw_ref, lvl_ref, sel_ref, o_ref.at[rs, vs], st_ref.at[h],
                           qf_ref.at[s * hb + h], kf_ref.at[s * hb + h], scale)


def _gla_scan(proj, bcum, o_norm_w, batch, seq, heads, dk, dv):
    m = proj.shape[0]
    c = _CHUNK
    hb = _GLA_HEADS_PER_STEP if heads % _GLA_HEADS_PER_STEP == 0 else 1
    assert c == _LANES and dk % _LANES == 0 and dv % _LANES == 0 and (2 * heads * dk) % (hb * dv) == 0
    cs = _GLA_CHUNKS_PER_STEP if seq % (_GLA_CHUNKS_PER_STEP * c) == 0 else 1
    tr = cs * c
    nt = seq // tr
    kd, vd = heads * dk, heads * dv
    wk, wv = hb * dk, hb * dv
    k_blk, v_blk, g_blk = kd // wk, (2 * kd) // wv, (2 * kd + vd) // wv

    def rows(b, h, t):
        return b * nt + t

    return pl.pallas_call(
        functools.partial(_gla_scan_kernel, scale=dk ** -0.5, dk=dk, dv=dv),
        grid=(batch, heads // hb, nt),
        in_specs=[pl.BlockSpec((tr, wk), lambda b, h, t: (rows(b, h, t), h)),
                  pl.BlockSpec((tr, wk), lambda b, h, t: (rows(b, h, t), k_blk + h)),
                  pl.BlockSpec((tr, wv), lambda b, h, t: (rows(b, h, t), v_blk + h)),
                  pl.BlockSpec((tr, wv), lambda b, h, t: (rows(b, h, t), g_blk + h)),
                  pl.BlockSpec((tr, wk), lambda b, h, t: (rows(b, h, t), h)),
                  pl.BlockSpec((1, dv), lambda b, h, t: (0, 0)),
                  pl.BlockSpec((c, c), lambda b, h, t: (0, 0)),
                  pl.BlockSpec((c * _GLA_DIAG, c), lambda b, h, t: (0, 0))],
        out_specs=pl.BlockSpec((tr, wv), lambda b, h, t: (rows(b, h, t), h)),
        out_shape=jax.ShapeDtypeStruct((m, vd), _MXU_DTYPE),
        scratch_shapes=[pltpu.VMEM((hb, dv, dk), _F32),
                        pltpu.VMEM((cs * hb, c, dk), _F32),
                        pltpu.VMEM((cs * hb, c, dk), _F32)],
        compiler_params=_params(("parallel", "parallel", "arbitrary")),
        name="gla_scan",
    )(proj, proj, proj, proj, bcum, o_norm_w.reshape(1, dv).astype(_F32), _gla_levels(c, _GLA_DIAG),
      _row_repeater(c, _GLA_DIAG))


def _gla_mixer(h, norm_w, batch, seq, layer, wt_in, w_gk_up, b_gk_up, o_norm_w, w_out):
    rank, kd = w_gk_up.shape
    vd = (wt_in.shape[1] - rank - 2 * kd) // 2
    dv = o_norm_w.shape[0]
    heads = vd // dv
    n_main = 2 * kd + 2 * vd
    w_up = jnp.pad(w_gk_up, ((0, _LANES - rank), (0, 0))).astype(_MXU_DTYPE)
    proj, u = _matmul(h, wt_in, layer, wt=True, n=n_main, norm_w=norm_w, emit_norm=True, act="silu",
                      act_col0=2 * kd + vd, out_dtype=_MXU_DTYPE)
    bcum = _gla_gate(u, wt_in, layer, n_main, rank, w_up, b_gk_up.reshape(1, kd).astype(_F32))
    o = _gla_scan(proj, bcum, o_norm_w, batch, seq, heads, kd // heads, dv)
    return _matmul_residual(o, w_out.astype(_MXU_DTYPE), layer, h)


def _ssd_gate_kernel(u_ref, wdt_ref, bias_ref, alog_ref, acol_ref, dcol_ref, arow_ref, drow_ref,
                     *, chunk, heads):
    raw = lax.dot_general(u_ref[...], _tail_rows(wdt_ref, heads), _NT, preferred_element_type=_F32)
    dt = _softplus(raw + bias_ref[...])
    a = _chunk_cumsum(dt * (-jnp.exp(alog_ref[...])) * _LOG2E, chunk)
    acol_ref[...] = a
    dcol_ref[...] = dt
    for i in range(a.shape[0] // chunk):
        arow_ref[i] = a[i * chunk:(i + 1) * chunk, :].T[:heads, :]
        drow_ref[i] = dt[i * chunk:(i + 1) * chunk, :].T[:heads, :]


def _ssd_gate(u, wt_in, layer, col0, dt_bias, a_log, heads, rows=1024):
    m, d = u.shape
    c = _CHUNK
    rows = min(rows, m)
    g = _SSD_GROUPS
    hg = heads // g
    assert col0 % _LANES == 0 and col0 + heads == wt_in.shape[1] and heads <= _LANES
    col = jax.ShapeDtypeStruct((m, _LANES), _F32)
    row = jax.ShapeDtypeStruct((m // c, heads, c), _F32)
    col_spec = pl.BlockSpec((rows, _LANES), lambda i: (i, 0))
    row_spec = pl.BlockSpec((rows // c, heads, c), lambda i: (i, 0, 0))
    return pl.pallas_call(
        functools.partial(_ssd_gate_kernel, chunk=c, heads=heads),
        grid=(m // rows,),
        in_specs=[pl.BlockSpec((rows, d), lambda i: (i, 0)),
                  pl.BlockSpec((None, _LANES, d), lambda i: (layer, col0 // _LANES, 0)),
                  pl.BlockSpec((1, _LANES), lambda i: (0, 0)),
                  pl.BlockSpec((1, _LANES), lambda i: (0, 0))],
        out_specs=[col_spec, col_spec, row_spec, row_spec],
        out_shape=[col, col, row, row],
        compiler_params=_params(("parallel",)),
        name="ssd_gate",
    )(u, wt_in, dt_bias, a_log)


def _head_expanders(groups, hg, c, head_dim):
    def one_hot(rows, width):
        row = lax.broadcasted_iota(jnp.int32, (groups, rows, hg * width), 1) % _LANES
        col = lax.broadcasted_iota(jnp.int32, (groups, rows, hg * width), 2)
        grp = lax.broadcasted_iota(jnp.int32, (groups, rows, hg * width), 0)
        return (row == grp * hg + col // width).astype(_MXU_DTYPE)
    return one_hot(2 * _LANES, c), one_hot(_LANES, head_dim)


def _expand_heads(bcasts, head_dim):
    c = bcasts[0].shape[0]
    per = _LANES // head_dim
    lane = lax.broadcasted_iota(jnp.int32, (c, _LANES), 1)
    pieces = []
    for p in range(len(bcasts) // per):
        out = bcasts[p * per]
        for i in range(1, per):
            out = jnp.where(lane >= i * head_dim, bcasts[p * per + i], out)
        pieces.append(out)
    return jnp.concatenate(pieces, axis=1)


def _ssd_scan_group(zs_ref, x_ref, bm_ref, cm_ref, acol_ref, dcol_ref, esc_ref, ehp_ref, arow_ref, drow_ref,
                    dskip_ref, gnw_ref, o_ref, st_ref, head_dim):
    c = zs_ref.shape[0]
    hg = x_ref.shape[1] // head_dim
    xs = x_ref[...].astype(_F32)
    bmx = bm_ref[...]
    cmx = cm_ref[...]

    acol = acol_ref[...]
    a_last = acol[c - 1:c, :]
    w_state = (jnp.exp2(a_last - acol) * dcol_ref[...]).astype(_MXU_DTYPE)
    a_hi = acol.astype(_MXU_DTYPE)
    a_lo = (acol - a_hi.astype(_F32)).astype(_MXU_DTYPE)
    a_sc = jnp.dot(jnp.concatenate([a_hi, a_lo], axis=1), esc_ref[...],
                   preferred_element_type=_F32)
    w_hp = jnp.dot(w_state, ehp_ref[...], preferred_element_type=_F32)

    per = _LANES // head_dim
    width = per * c
    cb = lax.dot_general(cmx, jnp.concatenate([bmx] * per, axis=0), _NT,
                         preferred_element_type=_F32)
    ri = lax.broadcasted_iota(jnp.int32, (c, width), 0)
    ci = lax.broadcasted_iota(jnp.int32, (c, width), 1) & (c - 1)
    causal = ci <= ri
    lane = lax.broadcasted_iota(jnp.int32, (c, _LANES), 1)
    y_pieces = []
    for p in range(hg // per):
        ac = a_sc[:, p * width:(p + 1) * width]
        ar = arow_ref[:, p * width:(p + 1) * width]
        dr = drow_ref[:, p * width:(p + 1) * width]
        decay = jnp.exp2(jnp.where(causal, ac - ar, _NEG_BIG))
        sc = (cb * decay * dr).astype(_MXU_DTYPE)
        xp = x_ref[:, p * _LANES:(p + 1) * _LANES]
        bd = jnp.concatenate(
            [jnp.where((lane >= i * head_dim) & (lane < (i + 1) * head_dim), xp, jnp.zeros_like(xp))
             for i in range(per)], axis=0)
        y_pieces.append(jnp.dot(sc, bd, preferred_element_type=_F32))
    y = jnp.concatenate(y_pieces, axis=1)

    st = st_ref[...]
    ea = jnp.exp2(_expand_heads([a_sc[:, h * c:(h + 1) * c] for h in range(hg)], head_dim))
    y = y + jnp.dot(cmx, st.astype(_MXU_DTYPE), preferred_element_type=_F32) * ea
    xw = (xs * w_hp).astype(_MXU_DTYPE)
    st_ref[...] = st * ea[c - 1:c, :] + lax.dot_general(bmx, xw, _TN, preferred_element_type=_F32)

    y = y + dskip_ref[...] * xs
    y = y * zs_ref[...].astype(_F32)
    ms = jnp.mean(y * y, axis=-1, keepdims=True)
    o_ref[...] = ((y * lax.rsqrt(ms + _RMS_EPS)) * gnw_ref[...]).astype(o_ref.dtype)


def _ssd_scan_kernel(zs_ref, x_ref, bm_ref, cm_ref, acol_ref, dcol_ref, esc_ref, ehp_ref, arow_ref, drow_ref,
                     dskip_ref, gnw_ref, o_ref, st_ref, *, head_dim):
    @pl.when(pl.program_id(2) == 0)
    def _():
        st_ref[...] = jnp.zeros_like(st_ref)

    gb, n, gw = st_ref.shape
    cs = arow_ref.shape[0]
    c = zs_ref.shape[0] // cs
    hg = gw // head_dim
    for s in range(cs):
        ts = pl.ds(s * c, c)
        for g in range(gb):
            xs_, ns_, rs_ = pl.ds(g * gw, gw), pl.ds(g * n, n), pl.ds(g * hg * c, hg * c)
            _ssd_scan_group(zs_ref.at[ts, xs_], x_ref.at[ts, xs_], bm_ref.at[ts, ns_], cm_ref.at[ts, ns_],
                            acol_ref.at[ts, :], dcol_ref.at[ts, :], esc_ref.at[g], ehp_ref.at[g],
                            arow_ref.at[s, :, rs_], drow_ref.at[s, :, rs_],
                            dskip_ref.at[:, xs_], gnw_ref.at[:, xs_], o_ref.at[ts, xs_], st_ref.at[g], head_dim)


def _ssd_scan(zs, xbc, acol, dcol, arow, drow, dskip, gnorm_w, batch, seq, inner, heads, n_state):
    m = zs.shape[0]
    c = _CHUNK
    g = _SSD_GROUPS
    gb = _SSD_GROUPS_PER_STEP if g % _SSD_GROUPS_PER_STEP == 0 else 1
    hg = heads // g
    p = inner // heads
    gw = hg * p
    sw, sn = gb * gw, gb * n_state
    assert c == _LANES and gw % _LANES == 0 and n_state % _LANES == 0 and _LANES % p == 0
    assert inner % sn == 0 and (g * n_state) % sn == 0
    cs = _SSD_CHUNKS_PER_STEP if seq % (_SSD_CHUNKS_PER_STEP * c) == 0 else 1
    tr = cs * c
    nt = seq // tr
    b_blk = inner // sn
    c_blk = (inner + g * n_state) // sn

    def rows(b, t):
        return b * nt + t

    in_specs = [
        pl.BlockSpec((tr, sw), lambda b, gi, t: (rows(b, t), gi)),
        pl.BlockSpec((tr, sw), lambda b, gi, t: (rows(b, t), gi)),
        pl.BlockSpec((tr, sn), lambda b, gi, t: (rows(b, t), b_blk + gi)),
        pl.BlockSpec((tr, sn), lambda b, gi, t: (rows(b, t), c_blk + gi)),
        pl.BlockSpec((tr, _LANES), lambda b, gi, t: (rows(b, t), 0)),
        pl.BlockSpec((tr, _LANES), lambda b, gi, t: (rows(b, t), 0)),
        pl.BlockSpec((gb, 2 * _LANES, hg * c), lambda b, gi, t: (gi, 0, 0)),
        pl.BlockSpec((gb, _LANES, gw), lambda b, gi, t: (gi, 0, 0)),
        pl.BlockSpec((cs, 1, gb * hg * c), lambda b, gi, t: (rows(b, t), 0, gi)),
        pl.BlockSpec((cs, 1, gb * hg * c), lambda b, gi, t: (rows(b, t), 0, gi)),
        pl.BlockSpec((1, sw), lambda b, gi, t: (0, gi)),
        pl.BlockSpec((1, sw), lambda b, gi, t: (0, gi)),
    ]
    return pl.pallas_call(
        functools.partial(_ssd_scan_kernel, head_dim=p),
        grid=(batch, g // gb, nt),
        in_specs=in_specs,
        out_specs=pl.BlockSpec((tr, sw), lambda b, gi, t: (rows(b, t), gi)),
        out_shape=jax.ShapeDtypeStruct((m, inner), _MXU_DTYPE),
        scratch_shapes=[pltpu.VMEM((gb, n_state, gw), _F32)],
        compiler_params=_params(("parallel", "parallel", "arbitrary")),
        name="ssd_scan",
    )(zs, xbc, xbc, xbc, acol, dcol, *_head_expanders(g, hg, c, p), arow, drow, dskip, gnorm_w)


def _ssd_mixer(h, norm_w, batch, seq, layer, wt_in, conv_w, conv_b, dt_bias, a_log, d_skip, gnorm_w, w_out):
    heads = dt_bias.shape[0]
    conv_dim = conv_w.shape[1]
    inner = wt_in.shape[1] - conv_dim - heads
    n_state = (conv_dim - inner) // (2 * _SSD_GROUPS)
    n_main = inner + conv_dim
    m = h.shape[0]
    pad1 = lambda v: jnp.pad(v.astype(_F32), (0, _LANES - heads)).reshape(1, _LANES)
    zs, u = _matmul(h, wt_in, layer, wt=True, n=inner, norm_w=norm_w, emit_norm=True, act="silu",
                    out_dtype=_MXU_DTYPE)
    xbc = _proj_conv(u, wt_in, layer, inner, conv_w.astype(_F32), conv_b.reshape(1, conv_dim).astype(_F32),
                     seq, _MXU_DTYPE)
    acol, dcol, arow, drow = _ssd_gate(u, wt_in, layer, n_main, pad1(dt_bias), pad1(a_log), heads)
    arow = arow.reshape(m // _CHUNK, 1, heads * _CHUNK)
    drow = drow.reshape(m // _CHUNK, 1, heads * _CHUNK)
    dskip = jnp.repeat(d_skip.astype(_F32), inner // heads).reshape(1, inner)
    y = _ssd_scan(zs, xbc, acol, dcol, arow, drow, dskip, gnorm_w.reshape(1, inner).astype(_F32),
                  batch, seq, inner, heads, n_state)
    return _matmul_residual(y, w_out.astype(_MXU_DTYPE), layer, h)


def kernel(x, mixer_norm_w, gla_w_in, gla_w_gk_up, gla_b_gk_up, gla_o_norm_w, gla_w_out, ssd_w_in, ssd_conv_w, ssd_conv_b, ssd_dt_bias, ssd_a_log, ssd_d_skip, ssd_gnorm_w, ssd_w_out, mlp_norm_w, mlp_w_fc1, mlp_w_fc2, final_norm_w):
    batch, seq, d = x.shape
    h = x.reshape(batch * seq, d)
    w_fc1 = mlp_w_fc1.astype(_MXU_DTYPE)
    w_fc2 = mlp_w_fc2.astype(_MXU_DTYPE)
    gla_wt_in = jnp.swapaxes(gla_w_in, 1, 2).astype(_MXU_DTYPE)
    ssd_wt_in = jnp.swapaxes(ssd_w_in, 1, 2).astype(_MXU_DTYPE)
    depth = mixer_norm_w.shape[0]
    assert depth >= 1
    for i in range(depth):
        j = i // 2
        if i % 2 == 0:
            h = _gla_mixer(h, mixer_norm_w[i], batch, seq, j, gla_wt_in, gla_w_gk_up[j], gla_b_gk_up[j],
                           gla_o_norm_w[j], gla_w_out)
        else:
            h = _ssd_mixer(h, mixer_norm_w[i], batch, seq, j, ssd_wt_in, ssd_conv_w[j], ssd_conv_b[j],
                           ssd_dt_bias[j], ssd_a_log[j], ssd_d_skip[j], ssd_gnorm_w[j], ssd_w_out)
        hidden = _matmul(h, w_fc1, i, norm_w=mlp_norm_w[i], act="relu2", out_dtype=_MXU_DTYPE)
        h = _matmul_residual(hidden, w_fc2, i, h, final_norm_w if i == depth - 1 else None)
    return h.reshape(batch, seq, d)
```

```python
import functools
import math

import jax
import jax.numpy as jnp
from jax import lax
from jax.experimental import pallas as pl
from jax.experimental.pallas import tpu as pltpu

_F32 = jnp.float32
_MXU_DTYPE = jnp.bfloat16
_RMS_EPS = 1e-5
_GLA_GATE_NORMALIZER = 16.0
_SSD_GROUPS = 8
_CHUNK = 128
_LANES = 128
_GLA_DIAG = 8
_GLA_HEADS_PER_STEP = 4
_GLA_CHUNKS_PER_STEP = 4
_SSD_CHUNKS_PER_STEP = 2
_SSD_GROUPS_PER_STEP = 8
_LOG2E = math.log2(math.e)
_NEG_BIG = -1e30
_HALO = 16
_MXU_WIDTH = 256
_VMEM_LIMIT_BYTES = 60 * 1024 * 1024
_MATMUL_VMEM_BUDGET = 58 * 1024 * 1024

_NN = (((1,), (0,)), ((), ()))
_NT = (((1,), (1,)), ((), ()))
_TN = (((0,), (0,)), ((), ()))


def _params(sem, flags=None):
    return pltpu.CompilerParams(dimension_semantics=sem, vmem_limit_bytes=_VMEM_LIMIT_BYTES, flags=flags)


def _rms(x, w):
    ms = jnp.mean(x * x, axis=-1, keepdims=True)
    return (x * lax.rsqrt(ms + _RMS_EPS)) * w


def _silu(x):
    half = 0.5 * x
    return half + half * jnp.tanh(half)


def _matmul_kernel(*refs, act, act_tile0, has_norm, has_res, wt):
    it = iter(refs)
    a_ref = next(it)
    nw_ref = next(it) if has_norm else None
    w_ref = next(it)
    r_ref = next(it) if has_res else None
    o_ref = next(it)
    if has_norm:
        an_ref = next(it)

        @pl.when(pl.program_id(1) == 0)
        def _():
            an_ref[...] = _rms(a_ref[...], nw_ref[...]).astype(_MXU_DTYPE)

        a = an_ref[...]
    else:
        a = a_ref[...]
    w = w_ref[...].astype(_MXU_DTYPE)
    acc = lax.dot_general(a, w, _NT if wt else _NN, preferred_element_type=_F32)
    if act == "relu2":
        act_acc = jnp.square(jnp.maximum(acc, 0.0))
    elif act == "silu":
        act_acc = _silu(acc)
    if act is not None:
        acc = act_acc if act_tile0 == 0 else jnp.where(pl.program_id(1) >= act_tile0, act_acc, acc)
    if has_res:
        acc = r_ref[...] + acc
    o_ref[...] = acc.astype(o_ref.dtype)


def _divisor_tile(n, pref, align):
    if n <= pref:
        return n
    t = (pref // align) * align
    while n % t:
        t -= align
    return t


def _matmul_tiles(m, k, n, a_bytes, w_bytes, out_bytes, norm_copies, has_res, conv_taps=0, row_cap=None):
    mxu_bytes = jnp.dtype(_MXU_DTYPE).itemsize
    for pm, pn in ((1024, 2048), (1024, 1024), (1024, 512), (1024, 256), (512, 256), (256, 256), (128, 128)):
        pm = pm if row_cap is None else min(pm, row_cap)
        tm, tn = _divisor_tile(m, pm, 8), _divisor_tile(n, pn, _LANES)
        need = 2 * tm * k * a_bytes + 2 * k * tn * w_bytes + 2 * tm * tn * out_bytes
        need += (2 + conv_taps) * tm * _MXU_WIDTH * 4
        need += 2 * tm * tn * 4 if has_res else 0
        need += norm_copies * tm * k * mxu_bytes
        need += k * tn * mxu_bytes if w_bytes != mxu_bytes else 0
        if need <= _MATMUL_VMEM_BUDGET:
            break
    return tm, tn


def _matmul(a, w, layer, *, wt=False, n=None, norm_w=None, emit_norm=False, act=None, act_col0=0, res=None,
            out_dtype):
    m, k = a.shape
    n = w.shape[1 if wt else 2] if n is None else n
    tm, tn = _matmul_tiles(m, k, math.gcd(n, act_col0), a.dtype.itemsize, w.dtype.itemsize,
                           jnp.dtype(out_dtype).itemsize, (norm_w is not None) + emit_norm, res is not None)
    in_specs = [pl.BlockSpec((tm, k), lambda i, j: (i, 0))]
    args = [a]
    if norm_w is not None:
        in_specs.append(pl.BlockSpec((1, k), lambda i, j: (0, 0)))
        args.append(norm_w.reshape(1, k).astype(_F32))
    in_specs.append(pl.BlockSpec((None, tn, k), lambda i, j: (layer, j, 0)) if wt else
                    pl.BlockSpec((None, k, tn), lambda i, j: (layer, 0, j)))
    args.append(w)
    if res is not None:
        in_specs.append(pl.BlockSpec((tm, tn), lambda i, j: (i, j)))
        args.append(res)
    out_specs = [pl.BlockSpec((tm, tn), lambda i, j: (i, j))]
    out_shape = [jax.ShapeDtypeStruct((m, n), out_dtype)]
    scratch = []
    if emit_norm:
        out_specs.append(pl.BlockSpec((tm, k), lambda i, j: (i, 0)))
        out_shape.append(jax.ShapeDtypeStruct((m, k), _MXU_DTYPE))
    elif norm_w is not None:
        scratch.append(pltpu.VMEM((tm, k), _MXU_DTYPE))
    outs = pl.pallas_call(
        functools.partial(_matmul_kernel, act=act, act_tile0=act_col0 // tn, has_norm=norm_w is not None,
                          has_res=res is not None, wt=wt),
        grid=(m // tm, n // tn),
        in_specs=in_specs,
        out_specs=out_specs,
        out_shape=out_shape,
        scratch_shapes=scratch,
        compiler_params=_params(("parallel", "arbitrary")),
        name="matmul",
    )(*args)
    return outs if emit_norm else outs[0]


def _matmul_splitk_kernel(a_ref, w_ref, r_ref, *rest, nk):
    o_ref = rest[-1]

    @pl.when(pl.program_id(2) == 0)
    def _():
        o_ref[...] = r_ref[...]

    o_ref[...] += jnp.dot(a_ref[...], w_ref[...].astype(_MXU_DTYPE), preferred_element_type=_F32)
    if len(rest) == 2:
        @pl.when(pl.program_id(2) == nk - 1)
        def _():
            o_ref[...] = _rms(o_ref[...], rest[0][...])


def _matmul_residual(a, w, layer, res, out_norm_w=None):
    m, k = a.shape
    n = w.shape[2]
    for nk in (1, 2, 4, 8):
        tm, tn = _matmul_tiles(m, k // nk, n, a.dtype.itemsize, w.dtype.itemsize, 4, 0, True)
        if (tn == n and tm >= min(m, 1024)) if out_norm_w is not None else (
                tm * tn >= min(m, 1024) * min(n, 1024)):
            break
    if nk == 1 and out_norm_w is None:
        return _matmul(a, w, layer, res=res, out_dtype=_F32)
    tk = k // nk
    in_specs = [pl.BlockSpec((tm, tk), lambda i, j, kk: (i, kk)),
                pl.BlockSpec((None, tk, tn), lambda i, j, kk: (layer, kk, j)),
                pl.BlockSpec((tm, tn), lambda i, j, kk: (i, j))]
    args = [a, w, res]
    if out_norm_w is not None:
        assert tn == n
        in_specs.append(pl.BlockSpec((1, n), lambda i, j, kk: (0, 0)))
        args.append(out_norm_w.reshape(1, n).astype(_F32))
    return pl.pallas_call(
        functools.partial(_matmul_splitk_kernel, nk=nk),
        grid=(m // tm, n // tn, nk),
        in_specs=in_specs,
        out_specs=pl.BlockSpec((tm, tn), lambda i, j, kk: (i, j)),
        out_shape=jax.ShapeDtypeStruct((m, n), _F32),
        compiler_params=_params(("parallel", "parallel", "arbitrary")),
        name="matmul_splitk",
    )(*args)


def _proj_conv_kernel(u_ref, halo_ref, w_ref, cw_ref, cb_ref, o_ref, an_ref, *, tiles_per_seq):
    kw = cw_ref.shape[0]

    @pl.when(pl.program_id(1) == 0)
    def _():
        halo = halo_ref[...]
        an_ref[0:_HALO, :] = jnp.where(pl.program_id(0) % tiles_per_seq == 0, jnp.zeros_like(halo), halo)
        an_ref[_HALO:, :] = u_ref[...]

    wb = w_ref[...].astype(_MXU_DTYPE)
    for c0 in range(0, o_ref.shape[1], _MXU_WIDTH):
        cols = pl.ds(c0, _MXU_WIDTH)
        acc = lax.dot_general(an_ref[...], wb[c0:c0 + _MXU_WIDTH, :], _NT, preferred_element_type=_F32)
        tm = o_ref.shape[0]
        out = cb_ref[:, cols] + cw_ref[kw - 1:kw, cols] * acc[_HALO:, :]
        for s in range(1, kw):
            out = out + cw_ref[kw - 1 - s:kw - s, cols] * acc[_HALO - s:_HALO - s + tm, :]
        o_ref[:, cols] = _silu(out).astype(o_ref.dtype)


def _proj_conv(u, w, layer, col0, conv_w, conv_b, seq, out_dtype):
    m, k = u.shape
    kw, n = conv_w.shape
    tm, tn = _matmul_tiles(m, k, math.gcd(n, col0), u.dtype.itemsize, w.dtype.itemsize,
                           jnp.dtype(out_dtype).itemsize, 1, False, conv_taps=kw, row_cap=seq)
    assert seq % tm == 0 and tm % _HALO == 0 and tn % _MXU_WIDTH == 0 and kw - 1 <= _HALO
    return pl.pallas_call(
        functools.partial(_proj_conv_kernel, tiles_per_seq=seq // tm),
        grid=(m // tm, n // tn),
        in_specs=[pl.BlockSpec((tm, k), lambda i, j: (i, 0)),
                  pl.BlockSpec((_HALO, k), lambda i, j: (jnp.maximum(i * (tm // _HALO) - 1, 0), 0)),
                  pl.BlockSpec((None, tn, k), lambda i, j: (layer, col0 // tn + j, 0)),
                  pl.BlockSpec((kw, tn), lambda i, j: (0, j)),
                  pl.BlockSpec((1, tn), lambda i, j: (0, j))],
        out_specs=pl.BlockSpec((tm, tn), lambda i, j: (i, j)),
        out_shape=jax.ShapeDtypeStruct((m, n), out_dtype),
        scratch_shapes=[pltpu.VMEM((tm + _HALO, k), _MXU_DTYPE)],
        compiler_params=_params(("parallel", "arbitrary")),
        name="proj_conv",
    )(u, u, w, conv_w, conv_b)


def _softplus(x):
    return jnp.maximum(x, 0.0) + jnp.log(1.0 + jnp.exp(-jnp.abs(x)))


def _chunk_cumsum(x, chunk):
    rows = x.shape[0]
    pieces = 3
    r = lax.broadcasted_iota(jnp.int32, (chunk, pieces * chunk), 0)
    c = lax.broadcasted_iota(jnp.int32, (chunk, pieces * chunk), 1) % chunk
    tri = jnp.where(c <= r, 1.0, 0.0).astype(_MXU_DTYPE)
    out = []
    for i in range(rows // chunk):
        rem = x[i * chunk:(i + 1) * chunk, :]
        parts = []
        for _ in range(pieces):
            piece = rem.astype(_MXU_DTYPE)
            parts.append(piece)
            rem = rem - piece.astype(_F32)
        out.append(jnp.dot(tri, jnp.concatenate(parts, axis=0), preferred_element_type=_F32))
    return jnp.concatenate(out, axis=0) if len(out) > 1 else out[0]


def _row_bcast(ref, row, n):
    return jnp.broadcast_to(ref[pl.ds(row, 1), :], (n, ref.shape[1]))


def _tail_rows(w_ref, n_valid):
    row = lax.broadcasted_iota(jnp.int32, w_ref.shape, 0)
    return jnp.where(row < n_valid, w_ref[...], 0.0).astype(_MXU_DTYPE)


def _gla_gate_kernel(u_ref, wgr_ref, wup_ref, b_ref, o_ref, *, chunk, rank):
    gr = lax.dot_general(u_ref[...], _tail_rows(wgr_ref, rank), _NT, preferred_element_type=_F32)
    pre = jnp.dot(gr.astype(_MXU_DTYPE), wup_ref[...], preferred_element_type=_F32) + b_ref[...]
    o_ref[...] = _chunk_cumsum(_softplus(-pre) * (-_LOG2E / _GLA_GATE_NORMALIZER), chunk)


def _gla_gate(u, wt_in, layer, col0, rank, w_up, b_up, rows=1024):
    m, d = u.shape
    kd = w_up.shape[1]
    rows = min(rows, m)
    assert col0 % _LANES == 0 and col0 + rank == wt_in.shape[1] and rank <= _LANES
    return pl.pallas_call(
        functools.partial(_gla_gate_kernel, chunk=_CHUNK, rank=rank),
        grid=(m // rows,),
        in_specs=[pl.BlockSpec((rows, d), lambda i: (i, 0)),
                  pl.BlockSpec((None, _LANES, d), lambda i: (layer, col0 // _LANES, 0)),
                  pl.BlockSpec(w_up.shape, lambda i: (0, 0)),
                  pl.BlockSpec((1, kd), lambda i: (0, 0))],
        out_specs=pl.BlockSpec((rows, kd), lambda i: (i, 0)),
        out_shape=jax.ShapeDtypeStruct((m, kd), _F32),
        compiler_params=_params(("parallel",)),
        name="gla_gate",
    )(u, wt_in, w_up, b_up)


def _gla_levels(c, nd):
    i = lax.broadcasted_iota(jnp.int32, (c, c), 0)
    j = lax.broadcasted_iota(jnp.int32, (c, c), 1)
    x = i ^ j
    lvl = jnp.zeros((c, c), jnp.int32)
    s, level = nd, 1
    while s < c:
        lvl = jnp.where(x >= s, level, lvl)
        s, level = 2 * s, level + 1
    return jnp.where(j > i, -1, lvl)


def _gla_scan_head(q_ref, k_ref, v_ref, gs_ref, b_ref, onw_ref, lvl_ref, o_ref, st_ref, qf_ref, kf_ref, scale):
    c, dk = q_ref.shape
    qf_ref[...] = q_ref[...].astype(_F32) * scale
    kf_ref[...] = k_ref[...].astype(_F32)
    q = qf_ref[...]
    k = kf_ref[...]
    v = v_ref[...]
    bc = b_ref[...]
    st = st_ref[...]
    b_last = jnp.concatenate([_row_bcast(b_ref, c - 1, 8)] * (c // 8), axis=0)

    o = lax.dot_general((q * jnp.exp2(bc)).astype(_MXU_DTYPE), st.astype(_MXU_DTYPE), _NT,
                        preferred_element_type=_F32)
    k_dec = (k * jnp.exp2(b_last - bc)).astype(_MXU_DTYPE)

    nd = _GLA_DIAG
    lane = lax.broadcasted_iota(jnp.int32, (nd, c), 1)
    pieces = []
    for m in range(c // nd):
        qb = qf_ref[m * nd:(m + 1) * nd, :]
        bb = b_ref[m * nd:(m + 1) * nd, :]
        acc = jnp.zeros((nd, c), _F32)
        for j in range(nd):
            kj = _row_bcast(kf_ref, m * nd + j, nd)
            bj = _row_bcast(b_ref, m * nd + j, nd)
            term = qb * kj * jnp.exp2(jnp.minimum(bb - bj, 0.0))
            acc = jnp.where(lane == m * nd + j, jnp.sum(term, axis=1, keepdims=True), acc)
        pieces.append(acc)
    lvl = lvl_ref[...]
    att = jnp.where(lvl == 0, jnp.concatenate(pieces, axis=0), 0.0)

    s, level = nd, 1
    while s < c:
        prev = [jnp.zeros((s, dk), _F32)]
        end = []
        for m in range(c // s):
            if m:
                prev += [_row_bcast(b_ref, m * s - 1, 8)] * (s // 8)
            end += [_row_bcast(b_ref, (m + 1) * s - 1, 8)] * (s // 8)
        qs = (q * jnp.exp2(bc - jnp.concatenate(prev, axis=0))).astype(_MXU_DTYPE)
        ks = (k * jnp.exp2(jnp.concatenate(end, axis=0) - bc)).astype(_MXU_DTYPE)
        att = jnp.where(lvl == level, lax.dot_general(qs, ks, _NT, preferred_element_type=_F32), att)
        s, level = 2 * s, level + 1

    o = o + jnp.dot(att.astype(_MXU_DTYPE), v, preferred_element_type=_F32)
    st_ref[...] = st * jnp.exp2(b_last[:1, :]) + lax.dot_general(
        v, k_dec, _TN, preferred_element_type=_F32)

    ms = jnp.mean(o * o, axis=-1, keepdims=True)
    on = (o * lax.rsqrt(ms + _RMS_EPS)) * onw_ref[...]
    o_ref[...] = (on * gs_ref[...].astype(_F32)).astype(o_ref.dtype)


def _gla_scan_kernel(q_ref, k_ref, v_ref, gs_ref, b_ref, onw_ref, lvl_ref, o_ref, st_ref, qf_ref, kf_ref,
                     *, scale, dk, dv):
    @pl.when(pl.program_id(2) == 0)
    def _():
        st_ref[...] = jnp.zeros_like(st_ref)

    c = lvl_ref.shape[0]
    hb = st_ref.shape[0]
    for s in range(q_ref.shape[0] // c):
        rs = pl.ds(s * c, c)
        for h in range(hb):
            ks, vs = pl.ds(h * dk, dk), pl.ds(h * dv, dv)
            _gla_scan_head(q_ref.at[rs, ks], k_ref.at[rs, ks], v_ref.at[rs, vs], gs_ref.at[rs, vs],
                           b_ref.at[rs, ks], onw_ref, lvl_ref, o_ref.at[rs, vs], st_ref.at[h],
                           qf_ref.at[s * hb + h], kf_ref.at[s * hb + h], scale)


def _gla_scan(proj, bcum, o_norm_w, batch, seq, heads, dk, dv):
    m = proj.shape[0]
    c = _CHUNK
    hb = _GLA_HEADS_PER_STEP if heads % _GLA_HEADS_PER_STEP == 0 else 1
    assert c == _LANES and dk % _LANES == 0 and dv % _LANES == 0 and (2 * heads * dk) % (hb * dv) == 0
    cs = _GLA_CHUNKS_PER_STEP if seq % (_GLA_CHUNKS_PER_STEP * c) == 0 else 1
    tr = cs * c
    nt = seq // tr
    kd, vd = heads * dk, heads * dv
    wk, wv = hb * dk, hb * dv
    k_blk, v_blk, g_blk = kd // wk, (2 * kd) // wv, (2 * kd + vd) // wv

    def rows(b, h, t):
        return b * nt + t

    return pl.pallas_call(
        functools.partial(_gla_scan_kernel, scale=dk ** -0.5, dk=dk, dv=dv),
        grid=(batch, heads // hb, nt),
        in_specs=[pl.BlockSpec((tr, wk), lambda b, h, t: (rows(b, h, t), h)),
                  pl.BlockSpec((tr, wk), lambda b, h, t: (rows(b, h, t), k_blk + h)),
                  pl.BlockSpec((tr, wv), lambda b, h, t: (rows(b, h, t), v_blk + h)),
                  pl.BlockSpec((tr, wv), lambda b, h, t: (rows(b, h, t), g_blk + h)),
                  pl.BlockSpec((tr, wk), lambda b, h, t: (rows(b, h, t), h)),
                  pl.BlockSpec((1, dv), lambda b, h, t: (0, 0)),
                  pl.BlockSpec((c, c), lambda b, h, t: (0, 0))],
        out_specs=pl.BlockSpec((tr, wv), lambda b, h, t: (rows(b, h, t), h)),
        out_shape=jax.ShapeDtypeStruct((m, vd), _MXU_DTYPE),
        scratch_shapes=[pltpu.VMEM((hb, dv, dk), _F32),
                        pltpu.VMEM((cs * hb, c, dk), _F32),
                        pltpu.VMEM((cs * hb, c, dk), _F32)],
        compiler_params=_params(("parallel", "parallel", "arbitrary")),
        name="gla_scan",
    )(proj, proj, proj, proj, bcum, o_norm_w.reshape(1, dv).astype(_F32), _gla_levels(c, _GLA_DIAG))


def _gla_mixer(h, norm_w, batch, seq, layer, wt_in, w_gk_up, b_gk_up, o_norm_w, w_out):
    rank, kd = w_gk_up.shape
    vd = (wt_in.shape[1] - rank - 2 * kd) // 2
    dv = o_norm_w.shape[0]
    heads = vd // dv
    n_main = 2 * kd + 2 * vd
    w_up = jnp.pad(w_gk_up, ((0, _LANES - rank), (0, 0))).astype(_MXU_DTYPE)
    proj, u = _matmul(h, wt_in, layer, wt=True, n=n_main, norm_w=norm_w, emit_norm=True, act="silu",
                      act_col0=2 * kd + vd, out_dtype=_MXU_DTYPE)
    bcum = _gla_gate(u, wt_in, layer, n_main, rank, w_up, b_gk_up.reshape(1, kd).astype(_F32))
    o = _gla_scan(proj, bcum, o_norm_w, batch, seq, heads, kd // heads, dv)
    return _matmul_residual(o, w_out.astype(_MXU_DTYPE), layer, h)


def _ssd_gate_kernel(u_ref, wdt_ref, bias_ref, alog_ref, acol_ref, dcol_ref, arow_ref, drow_ref,
                     *, chunk, heads):
    raw = lax.dot_general(u_ref[...], _tail_rows(wdt_ref, heads), _NT, preferred_element_type=_F32)
    dt = _softplus(raw + bias_ref[...])
    a = _chunk_cumsum(dt * (-jnp.exp(alog_ref[...])) * _LOG2E, chunk)
    acol_ref[...] = a
    dcol_ref[...] = dt
    for i in range(a.shape[0] // chunk):
        arow_ref[i] = a[i * chunk:(i + 1) * chunk, :].T[:heads, :]
        drow_ref[i] = dt[i * chunk:(i + 1) * chunk, :].T[:heads, :]


def _ssd_gate(u, wt_in, layer, col0, dt_bias, a_log, heads, rows=1024):
    m, d = u.shape
    c = _CHUNK
    rows = min(rows, m)
    g = _SSD_GROUPS
    hg = heads // g
    assert col0 % _LANES == 0 and col0 + heads == wt_in.shape[1] and heads <= _LANES
    col = jax.ShapeDtypeStruct((m, _LANES), _F32)
    row = jax.ShapeDtypeStruct((m // c, heads, c), _F32)
    col_spec = pl.BlockSpec((rows, _LANES), lambda i: (i, 0))
    row_spec = pl.BlockSpec((rows // c, heads, c), lambda i: (i, 0, 0))
    return pl.pallas_call(
        functools.partial(_ssd_gate_kernel, chunk=c, heads=heads),
        grid=(m // rows,),
        in_specs=[pl.BlockSpec((rows, d), lambda i: (i, 0)),
                  pl.BlockSpec((None, _LANES, d), lambda i: (layer, col0 // _LANES, 0)),
                  pl.BlockSpec((1, _LANES), lambda i: (0, 0)),
                  pl.BlockSpec((1, _LANES), lambda i: (0, 0))],
        out_specs=[col_spec, col_spec, row_spec, row_spec],
        out_shape=[col, col, row, row],
        compiler_params=_params(("parallel",)),
        name="ssd_gate",
    )(u, wt_in, dt_bias, a_log)


def _head_expanders(groups, hg, c, head_dim):
    def one_hot(rows, width):
        row = lax.broadcasted_iota(jnp.int32, (groups, rows, hg * width), 1) % _LANES
        col = lax.broadcasted_iota(jnp.int32, (groups, rows, hg * width), 2)
        grp = lax.broadcasted_iota(jnp.int32, (groups, rows, hg * width), 0)
        return (row == grp * hg + col // width).astype(_MXU_DTYPE)
    return one_hot(2 * _LANES, c), one_hot(_LANES, head_dim)


def _expand_heads(bcasts, head_dim):
    c = bcasts[0].shape[0]
    per = _LANES // head_dim
    lane = lax.broadcasted_iota(jnp.int32, (c, _LANES), 1)
    pieces = []
    for p in range(len(bcasts) // per):
        out = bcasts[p * per]
        for i in range(1, per):
            out = jnp.where(lane >= i * head_dim, bcasts[p * per + i], out)
        pieces.append(out)
    return jnp.concatenate(pieces, axis=1)


def _ssd_scan_group(zs_ref, x_ref, bm_ref, cm_ref, acol_ref, dcol_ref, esc_ref, ehp_ref, arow_ref, drow_ref,
                    dskip_ref, gnw_ref, o_ref, st_ref, head_dim):
    c = zs_ref.shape[0]
    hg = x_ref.shape[1] // head_dim
    xs = x_ref[...].astype(_F32)
    bmx = bm_ref[...]
    cmx = cm_ref[...]

    acol = acol_ref[...]
    a_last = acol[c - 1:c, :]
    w_state = (jnp.exp2(a_last - acol) * dcol_ref[...]).astype(_MXU_DTYPE)
    a_hi = acol.astype(_MXU_DTYPE)
    a_lo = (acol - a_hi.astype(_F32)).astype(_MXU_DTYPE)
    a_sc = jnp.dot(jnp.concatenate([a_hi, a_lo], axis=1), esc_ref[...],
                   preferred_element_type=_F32)
    w_hp = jnp.dot(w_state, ehp_ref[...], preferred_element_type=_F32)

    per = _LANES // head_dim
    width = per * c
    cb = lax.dot_general(cmx, jnp.concatenate([bmx] * per, axis=0), _NT,
                         preferred_element_type=_F32)
    ri = lax.broadcasted_iota(jnp.int32, (c, width), 0)
    ci = lax.broadcasted_iota(jnp.int32, (c, width), 1) & (c - 1)
    causal = ci <= ri
    lane = lax.broadcasted_iota(jnp.int32, (c, _LANES), 1)
    y_pieces = []
    for p in range(hg // per):
        ac = a_sc[:, p * width:(p + 1) * width]
        ar = arow_ref[:, p * width:(p + 1) * width]
        dr = drow_ref[:, p * width:(p + 1) * width]
        decay = jnp.exp2(jnp.where(causal, ac - ar, _NEG_BIG))
        sc = (cb * decay * dr).astype(_MXU_DTYPE)
        xp = x_ref[:, p * _LANES:(p + 1) * _LANES]
        bd = jnp.concatenate(
            [jnp.where((lane >= i * head_dim) & (lane < (i + 1) * head_dim), xp, jnp.zeros_like(xp))
             for i in range(per)], axis=0)
        y_pieces.append(jnp.dot(sc, bd, preferred_element_type=_F32))
    y = jnp.concatenate(y_pieces, axis=1)

    st = st_ref[...]
    ea = jnp.exp2(_expand_heads([a_sc[:, h * c:(h + 1) * c] for h in range(hg)], head_dim))
    y = y + jnp.dot(cmx, st.astype(_MXU_DTYPE), preferred_element_type=_F32) * ea
    xw = (xs * w_hp).astype(_MXU_DTYPE)
    st_ref[...] = st * ea[c - 1:c, :] + lax.dot_general(bmx, xw, _TN, preferred_element_type=_F32)

    y = y + dskip_ref[...] * xs
    y = y * zs_ref[...].astype(_F32)
    ms = jnp.mean(y * y, axis=-1, keepdims=True)
    o_ref[...] = ((y * lax.rsqrt(ms + _RMS_EPS)) * gnw_ref[...]).astype(o_ref.dtype)


def _ssd_scan_kernel(zs_ref, x_ref, bm_ref, cm_ref, acol_ref, dcol_ref, esc_ref, ehp_ref, arow_ref, drow_ref,
                     dskip_ref, gnw_ref, o_ref, st_ref, *, head_dim):
    @pl.when(pl.program_id(2) == 0)
    def _():
        st_ref[...] = jnp.zeros_like(st_ref)

    gb, n, gw = st_ref.shape
    cs = arow_ref.shape[0]
    c = zs_ref.shape[0] // cs
    hg = gw // head_dim
    for s in range(cs):
        ts = pl.ds(s * c, c)
        for g in range(gb):
            xs_, ns_, rs_ = pl.ds(g * gw, gw), pl.ds(g * n, n), pl.ds(g * hg * c, hg * c)
            _ssd_scan_group(zs_ref.at[ts, xs_], x_ref.at[ts, xs_], bm_ref.at[ts, ns_], cm_ref.at[ts, ns_],
                            acol_ref.at[ts, :], dcol_ref.at[ts, :], esc_ref.at[g], ehp_ref.at[g],
                            arow_ref.at[s, :, rs_], drow_ref.at[s, :, rs_],
                            dskip_ref.at[:, xs_], gnw_ref.at[:, xs_], o_ref.at[ts, xs_], st_ref.at[g], head_dim)


def _ssd_scan(zs, xbc, acol, dcol, arow, drow, dskip, gnorm_w, batch, seq, inner, heads, n_state):
    m = zs.shape[0]
    c = _CHUNK
    g = _SSD_GROUPS
    gb = _SSD_GROUPS_PER_STEP if g % _SSD_GROUPS_PER_STEP == 0 else 1
    hg = heads // g
    p = inner // heads
    gw = hg * p
    sw, sn = gb * gw, gb * n_state
    assert c == _LANES and gw % _LANES == 0 and n_state % _LANES == 0 and _LANES % p == 0
    assert inner % sn == 0 and (g * n_state) % sn == 0
    cs = _SSD_CHUNKS_PER_STEP if seq % (_SSD_CHUNKS_PER_STEP * c) == 0 else 1
    tr = cs * c
    nt = seq // tr
    b_blk = inner // sn
    c_blk = (inner + g * n_state) // sn

    def rows(b, t):
        return b * nt + t

    in_specs = [
        pl.BlockSpec((tr, sw), lambda b, gi, t: (rows(b, t), gi)),
        pl.BlockSpec((tr, sw), lambda b, gi, t: (rows(b, t), gi)),
        pl.BlockSpec((tr, sn), lambda b, gi, t: (rows(b, t), b_blk + gi)),
        pl.BlockSpec((tr, sn), lambda b, gi, t: (rows(b, t), c_blk + gi)),
        pl.BlockSpec((tr, _LANES), lambda b, gi, t: (rows(b, t), 0)),
        pl.BlockSpec((tr, _LANES), lambda b, gi, t: (rows(b, t), 0)),
        pl.BlockSpec((gb, 2 * _LANES, hg * c), lambda b, gi, t: (gi, 0, 0)),
        pl.BlockSpec((gb, _LANES, gw), lambda b, gi, t: (gi, 0, 0)),
        pl.BlockSpec((cs, 1, gb * hg * c), lambda b, gi, t: (rows(b, t), 0, gi)),
        pl.BlockSpec((cs, 1, gb * hg * c), lambda b, gi, t: (rows(b, t), 0, gi)),
        pl.BlockSpec((1, sw), lambda b, gi, t: (0, gi)),
        pl.BlockSpec((1, sw), lambda b, gi, t: (0, gi)),
    ]
    return pl.pallas_call(
        functools.partial(_ssd_scan_kernel, head_dim=p),
        grid=(batch, g // gb, nt),
        in_specs=in_specs,
        out_specs=pl.BlockSpec((tr, sw), lambda b, gi, t: (rows(b, t), gi)),
        out_shape=jax.ShapeDtypeStruct((m, inner), _MXU_DTYPE),
        scratch_shapes=[pltpu.VMEM((gb, n_state, gw), _F32)],
        compiler_params=_params(("parallel", "parallel", "arbitrary")),
        name="ssd_scan",
    )(zs, xbc, xbc, xbc, acol, dcol, *_head_expanders(g, hg, c, p), arow, drow, dskip, gnorm_w)


def _ssd_mixer(h, norm_w, batch, seq, layer, wt_in, conv_w, conv_b, dt_bias, a_log, d_skip, gnorm_w, w_out):
    heads = dt_bias.shape[0]
    conv_dim = conv_w.shape[1]
    inner = wt_in.shape[1] - conv_dim - heads
    n_state = (conv_dim - inner) // (2 * _SSD_GROUPS)
    n_main = inner + conv_dim
    m = h.shape[0]
    pad1 = lambda v: jnp.pad(v.astype(_F32), (0, _LANES - heads)).reshape(1, _LANES)
    zs, u = _matmul(h, wt_in, layer, wt=True, n=inner, norm_w=norm_w, emit_norm=True, act="silu",
                    out_dtype=_MXU_DTYPE)
    xbc = _proj_conv(u, wt_in, layer, inner, conv_w.astype(_F32), conv_b.reshape(1, conv_dim).astype(_F32),
                     seq, _MXU_DTYPE)
    acol, dcol, arow, drow = _ssd_gate(u, wt_in, layer, n_main, pad1(dt_bias), pad1(a_log), heads)
    arow = arow.reshape(m // _CHUNK, 1, heads * _CHUNK)
    drow = drow.reshape(m // _CHUNK, 1, heads * _CHUNK)
    dskip = jnp.repeat(d_skip.astype(_F32), inner // heads).reshape(1, inner)
    y = _ssd_scan(zs, xbc, acol, dcol, arow, drow, dskip, gnorm_w.reshape(1, inner).astype(_F32),
                  batch, seq, inner, heads, n_state)
    return _matmul_residual(y, w_out.astype(_MXU_DTYPE), layer, h)


def kernel(x, mixer_norm_w, gla_w_in, gla_w_gk_up, gla_b_gk_up, gla_o_norm_w, gla_w_out, ssd_w_in, ssd_conv_w, ssd_conv_b, ssd_dt_bias, ssd_a_log, ssd_d_skip, ssd_gnorm_w, ssd_w_out, mlp_norm_w, mlp_w_fc1, mlp_w_fc2, final_norm_w):
    batch, seq, d = x.shape
    h = x.reshape(batch * seq, d)
    w_fc1 = mlp_w_fc1.astype(_MXU_DTYPE)
    w_fc2 = mlp_w_fc2.astype(_MXU_DTYPE)
    gla_wt_in = jnp.swapaxes(gla_w_in, 1, 2).astype(_MXU_DTYPE)
    ssd_wt_in = jnp.swapaxes(ssd_w_in, 1, 2).astype(_MXU_DTYPE)
    depth = mixer_norm_w.shape[0]
    assert depth >= 1
    for i in range(depth):
        j = i // 2
        if i % 2 == 0:
            h = _gla_mixer(h, mixer_norm_w[i], batch, seq, j, gla_wt_in, gla_w_gk_up[j], gla_b_gk_up[j],
                           gla_o_norm_w[j], gla_w_out)
        else:
            h = _ssd_mixer(h, mixer_norm_w[i], batch, seq, j, ssd_wt_in, ssd_conv_w[j], ssd_conv_b[j],
                           ssd_dt_bias[j], ssd_a_log[j], ssd_d_skip[j], ssd_gnorm_w[j], ssd_w_out)
        hidden = _matmul(h, w_fc1, i, norm_w=mlp_norm_w[i], act="relu2", out_dtype=_MXU_DTYPE)
        h = _matmul_residual(hidden, w_fc2, i, h, final_norm_w if i == depth - 1 else None)
    return h.reshape(batch, seq, d)
```
